```python
import math
import jax
import jax.numpy as jnp
from jax import lax
import numpy as np

D_MODEL = 2048
BATCH = 2
SEQ = 8192
DEPTH = 2

EPS = 1e-6
NEG = -1e30
FORCE = 1e30
ROPE_THETA = 10000.0

GLA_WIDTH = D_MODEL // 2
GLA_HEADS = 4
GLA_DV = GLA_WIDTH // GLA_HEADS
GLA_DK = GLA_DV // 2
GLA_RANK = 16
GLA_TAU = 16.0
LIN_CHUNK = 64

NSA_DH = 128
NSA_WIDTH = D_MODEL // 2
NSA_HEADS = NSA_WIDTH // NSA_DH
NSA_KV_GROUPS = 2
NSA_HPG = NSA_HEADS // NSA_KV_GROUPS
NSA_KV = NSA_KV_GROUPS * NSA_DH
CMP_LEN = 32
CMP_STRIDE = 16
CMP_HIDDEN = 256
SLC_BLOCK = 64
SLC_TOPK = 16
WINDOW = 512
Q_BLOCK = 128

HGRN_DK = 128
HGRN_WIDTH = D_MODEL // 2
HGRN_HEADS = HGRN_WIDTH // HGRN_DK
HGRN_DV = HGRN_WIDTH // HGRN_HEADS

SSM_HEADDIM = 64
SSM_WIDTH = D_MODEL // 2
SSM_HEADS = SSM_WIDTH // SSM_HEADDIM
SSM_GROUPS = 2
SSM_STATE = 128
SSM_CONV = 4
SSM_CHUNK = 256
SSM_CONV_CH = SSM_WIDTH + 2 * SSM_GROUPS * SSM_STATE

MIX_WIDTH = GLA_WIDTH + NSA_WIDTH

EVEN_SIZES = (GLA_HEADS * GLA_DK, GLA_HEADS * GLA_DK, GLA_WIDTH, GLA_RANK, GLA_WIDTH,
              NSA_WIDTH, NSA_KV, NSA_KV, NSA_KV, NSA_KV, NSA_KV, NSA_KV, NSA_HEADS * 3, NSA_WIDTH)
ODD_SIZES = (HGRN_HEADS * HGRN_DK, HGRN_HEADS * HGRN_DK, HGRN_WIDTH, HGRN_WIDTH,
             SSM_WIDTH, SSM_CONV_CH, SSM_HEADS)

kernel_name = 'hybrid_gla_nsa_hgrn2_ssd_trunk'


def _offsets(sizes):
    return tuple(int(v) for v in np.cumsum(sizes)[:-1])


def rms_norm(x, w):
    xf = x.astype(jnp.float32)
    y = xf * lax.rsqrt(jnp.mean(xf * xf, axis=-1, keepdims=True) + EPS)
    return (y * w.astype(jnp.float32)).astype(x.dtype)


def rope(x, pos):
    half = x.shape[-1] // 2
    inv_freq = ROPE_THETA ** (-jnp.arange(half, dtype=jnp.float32) / half)
    ang = pos.astype(jnp.float32)[:, None] * inv_freq[None, :]
    cos = jnp.cos(ang)[None, :, None, :]
    sin = jnp.sin(ang)[None, :, None, :]
    xf = x.astype(jnp.float32)
    x1, x2 = xf[..., :half], xf[..., half:]
    return jnp.concatenate([x1 * cos - x2 * sin, x1 * sin + x2 * cos], axis=-1).astype(x.dtype)


def chunked_gated_linear_attention(q, k, v, log_g):
    b, s, h, dk = q.shape
    dv = v.shape[-1]
    c = LIN_CHUNK
    n = s // c
    f32 = jnp.float32

    def to_chunks(t):
        return t.astype(f32).reshape(b, n, c, h, t.shape[-1]).transpose(1, 0, 3, 2, 4)

    qc, kc, vc, gc = to_chunks(q), to_chunks(k), to_chunks(v), to_chunks(log_g)
    bc = jnp.cumsum(gc, axis=3)
    causal = jnp.tril(jnp.ones((c, c), dtype=bool))

    def step(state, inp):
        qi, ki, vi, bi = inp
        diff = bi[:, :, :, None, :] - bi[:, :, None, :, :]
        decay = jnp.exp(jnp.where(causal[:, :, None], diff, -jnp.inf))
        attn = jnp.einsum('bhid,bhjd,bhijd->bhij', qi, ki, decay)
        o = (jnp.einsum('bhij,bhjv->bhiv', attn, vi)
             + jnp.einsum('bhid,bhdv->bhiv', qi * jnp.exp(bi), state))
        b_last = bi[:, :, -1:, :]
        k_dec = ki * jnp.exp(b_last - bi)
        state = state * jnp.exp(b_last[:, :, 0, :])[..., None] + jnp.einsum('bhjd,bhjv->bhdv', k_dec, vi)
        return state, o

    s0 = jnp.zeros((b, h, dk, dv), f32)
    _, o = lax.scan(step, s0, (qc, kc, vc, bc))
    return o.transpose(1, 0, 3, 2, 4).reshape(b, s, h, dv)


def gla_branch(q, k, v, g_lr, w_up, b_up, norm_w):
    b, s, _ = q.shape
    q = q.reshape(b, s, GLA_HEADS, GLA_DK) * (GLA_DK ** -0.5)
    k = k.reshape(b, s, GLA_HEADS, GLA_DK)
    v = v.reshape(b, s, GLA_HEADS, GLA_DV)
    log_g = jax.nn.log_sigmoid((g_lr @ w_up + b_up).astype(jnp.float32)) / GLA_TAU
    o = chunked_gated_linear_attention(q, k, v, log_g.reshape(b, s, GLA_HEADS, GLA_DK))
    return rms_norm(o, norm_w).reshape(b, s, GLA_WIDTH)


def nsa_compress(t, pe, w1, w2):
    b, s, g, dh = t.shape
    n_cmp = (s - CMP_LEN) // CMP_STRIDE + 1
    idx = jnp.arange(n_cmp)[:, None] * CMP_STRIDE + jnp.arange(CMP_LEN)[None, :]
    blocks = t[:, idx] + pe[None, None, :, None, :]
    flat = blocks.transpose(0, 1, 3, 2, 4).reshape(b, n_cmp, g, CMP_LEN * dh)
    return jax.nn.silu(flat @ w1) @ w2


def nsa_branch(q, k_cmp, v_cmp, k_slc, v_slc, k_win, v_win, gate_raw,
               pe_k, w1_k, w2_k, pe_v, w1_v, w2_v):
    b, s, _ = q.shape
    f32 = jnp.float32
    g_, j_, dh = NSA_KV_GROUPS, NSA_HPG, NSA_DH
    pos = jnp.arange(s)
    q = rope(q.reshape(b, s, NSA_HEADS, dh), pos) * (dh ** -0.5)
    q = q.reshape(b, s, g_, j_, dh)
    kc = nsa_compress(rope(k_cmp.reshape(b, s, g_, dh), pos), pe_k, w1_k, w2_k)
    vc = nsa_compress(v_cmp.reshape(b, s, g_, dh), pe_v, w1_v, w2_v)
    n_cmp = kc.shape[1]
    n_slc = s // SLC_BLOCK
    n_sel = min(SLC_TOPK, n_slc)
    ks_blocks = rope(k_slc.reshape(b, s, g_, dh), pos).reshape(b, n_slc, SLC_BLOCK, g_, dh).transpose(0, 3, 1, 2, 4)
    vs_blocks = v_slc.reshape(b, n_slc, SLC_BLOCK, g_, dh).transpose(0, 3, 1, 2, 4)
    kw_pad = jnp.pad(rope(k_win.reshape(b, s, g_, dh), pos), ((0, 0), (WINDOW, 0), (0, 0), (0, 0)))
    vw_pad = jnp.pad(v_win.reshape(b, s, g_, dh), ((0, 0), (WINDOW, 0), (0, 0), (0, 0)))
    gates = jax.nn.sigmoid(gate_raw.astype(f32)).reshape(b, s, g_, j_, 3)

    cmp_start = jnp.arange(n_cmp) * CMP_STRIDE
    cmp_end = cmp_start + CMP_LEN - 1
    sblk = jnp.arange(n_slc)
    slc_start = sblk * SLC_BLOCK
    overlap = (jnp.minimum(cmp_start[:, None] + CMP_LEN, slc_start[None, :] + SLC_BLOCK)
               - jnp.maximum(cmp_start[:, None], slc_start[None, :]))
    cmp_to_slc = (overlap > 0).astype(f32)
    gather = jax.vmap(jax.vmap(lambda kb, ib: kb[ib]))

    def block(qb):
        t0 = qb * Q_BLOCK
        t = t0 + jnp.arange(Q_BLOCK)
        qblk = lax.dynamic_slice_in_dim(q, t0, Q_BLOCK, axis=1)
        s1 = jnp.einsum('bqgjd,bngd->bgjqn', qblk, kc).astype(f32)
        m1 = cmp_end[None, :] <= t[:, None]
        p1 = jax.nn.softmax(jnp.where(m1, s1, NEG), axis=-1) * m1
        o_c = jnp.einsum('bgjqn,bngd->bqgjd', p1, vc)
        imp = jnp.einsum('bgjqn,ns->bgqs', p1, cmp_to_slc)
        blk_t = t // SLC_BLOCK
        causal_blk = sblk[None, :] <= blk_t[:, None]
        forced = (sblk[None, :] == 0) | (sblk[None, :] == blk_t[:, None]) | (sblk[None, :] == blk_t[:, None] - 1)
        score = jnp.where(forced, FORCE, jnp.where(causal_blk, imp, NEG))
        _, idx = lax.top_k(score, n_sel)
        k_sel = gather(ks_blocks, idx)
        v_sel = gather(vs_blocks, idx)
        tok = idx[..., None] * SLC_BLOCK + jnp.arange(SLC_BLOCK)
        m2 = tok <= t[None, None, :, None, None]
        s2 = jnp.einsum('bqgjd,bgqkld->bgjqkl', qblk, k_sel).astype(f32)
        s2 = jnp.where(m2[:, :, None], s2, NEG).reshape(b, g_, j_, Q_BLOCK, n_sel * SLC_BLOCK)
        p2 = jax.nn.softmax(s2, axis=-1).reshape(b, g_, j_, Q_BLOCK, n_sel, SLC_BLOCK)
        o_s = jnp.einsum('bgjqkl,bgqkld->bqgjd', p2, v_sel)
        kw = lax.dynamic_slice_in_dim(kw_pad, t0, Q_BLOCK + WINDOW, axis=1)
        vw = lax.dynamic_slice_in_dim(vw_pad, t0, Q_BLOCK + WINDOW, axis=1)
        kpos = t0 - WINDOW + jnp.arange(Q_BLOCK + WINDOW)
        m3 = (kpos[None, :] <= t[:, None]) & (kpos[None, :] > t[:, None] - WINDOW) & (kpos[None, :] >= 0)
        s3 = jnp.einsum('bqgjd,bkgd->bgjqk', qblk, kw).astype(f32)
        p3 = jax.nn.softmax(jnp.where(m3, s3, NEG), axis=-1)
        o_w = jnp.einsum('bgjqk,bkgd->bqgjd', p3, vw)
        gb = lax.dynamic_slice_in_dim(gates, t0, Q_BLOCK, axis=1)
        return gb[..., 0:1] * o_c + gb[..., 1:2] * o_s + gb[..., 2:3] * o_w

    out = lax.map(block, jnp.arange(s // Q_BLOCK))
    return out.transpose(1, 0, 2, 3, 4, 5).reshape(b, s, NSA_WIDTH)


def hgrn2_branch(q, f_raw, i, lb, norm_w):
    b, s, _ = q.shape
    q = q.reshape(b, s, HGRN_HEADS, HGRN_DK)
    z = f_raw.astype(jnp.float32).reshape(b, s, HGRN_HEADS, HGRN_DK)
    i = i.reshape(b, s, HGRN_HEADS, HGRN_DV)
    lb = lb.reshape(HGRN_HEADS, HGRN_DK)
    log_f = jnp.logaddexp(jnp.log(lb), jnp.log1p(-lb) + jax.nn.log_sigmoid(z))
    k = (1.0 - lb) * jax.nn.sigmoid(-z)
    o = chunked_gated_linear_attention(q, k, i, log_f)
    return rms_norm(o, norm_w).reshape(b, s, HGRN_WIDTH)


def causal_depthwise_conv(x, w, bias):
    ch = x.shape[-1]
    y = lax.conv_general_dilated(x, w[:, None, :], window_strides=(1,), padding=((SSM_CONV - 1, 0),),
                                 dimension_numbers=('NWC', 'WIO', 'NWC'), feature_group_count=ch)
    return y + bias


def segsum(a):
    t = a.shape[-1]
    cs = jnp.cumsum(a, axis=-1)
    d = cs[..., :, None] - cs[..., None, :]
    return jnp.where(jnp.tril(jnp.ones((t, t), dtype=bool)), d, -jnp.inf)


def ssd_chunked_scan(xdt, da, bm, cm):
    b, s, h, p = xdt.shape
    n = bm.shape[-1]
    c = math.gcd(SSM_CHUNK, s)
    nc = s // c
    x = xdt.reshape(b, nc, c, h, p)
    bm = bm.reshape(b, nc, c, h, n)
    cm = cm.reshape(b, nc, c, h, n)
    a = da.reshape(b, nc, c, h).transpose(0, 1, 3, 2)
    a_cs = jnp.cumsum(a, axis=-1)
    lmat = jnp.exp(segsum(a))
    cb = jnp.einsum('bclhn,bcshn->bchls', cm, bm)
    y_diag = jnp.einsum('bchls,bcshp->bclhp', cb * lmat, x)
    decay_states = jnp.exp(a_cs[..., -1:] - a_cs).transpose(0, 1, 3, 2)[..., None]
    states = jnp.einsum('bclhn,bclhp->bchpn', bm, x * decay_states)
    states = jnp.concatenate([jnp.zeros_like(states[:, :1]), states], axis=1)
    chunk_end = jnp.pad(a_cs[..., -1].transpose(0, 2, 1), ((0, 0), (0, 0), (1, 0)))
    chunk_decay = jnp.exp(segsum(chunk_end))
    prev = jnp.einsum('bhzc,bchpn->bzhpn', chunk_decay, states)[:, :-1]
    y_off = jnp.einsum('bclhn,bchpn->bclhp', cm, prev) * jnp.exp(a_cs).transpose(0, 1, 3, 2)[..., None]
    return (y_diag + y_off).reshape(b, s, h, p)


def mamba2_branch(xbc, dt_raw, conv_w, conv_b, dt_bias, a_log, d_skip):
    b, s, _ = xbc.shape
    f32 = jnp.float32
    xbc = jax.nn.silu(causal_depthwise_conv(xbc, conv_w, conv_b))
    xs, bm, cm = jnp.split(xbc, (SSM_WIDTH, SSM_WIDTH + SSM_GROUPS * SSM_STATE), axis=-1)
    rep = SSM_HEADS // SSM_GROUPS
    xs = xs.reshape(b, s, SSM_HEADS, SSM_HEADDIM).astype(f32)
    bm = jnp.repeat(bm.reshape(b, s, SSM_GROUPS, SSM_STATE), rep, axis=2).astype(f32)
    cm = jnp.repeat(cm.reshape(b, s, SSM_GROUPS, SSM_STATE), rep, axis=2).astype(f32)
    dt = jax.nn.softplus(dt_raw.astype(f32) + dt_bias.astype(f32))
    a = -jnp.exp(a_log.astype(f32))
    y = ssd_chunked_scan(xs * dt[..., None], dt * a, bm, cm)
    y = y + d_skip.astype(f32)[:, None] * xs
    return y.reshape(b, s, SSM_WIDTH)


def even_mixer(h, w_in, w_out, gla_w_up, gla_b_up, gla_norm_w,
               pe_k, w1_k, w2_k, pe_v, w1_v, w2_v):
    proj = h @ w_in
    (gq, gk, gv, glr, gz, nq, nkc, nvc, nks, nvs, nkw, nvw, ngate, nz) = jnp.split(proj, _offsets(EVEN_SIZES), axis=-1)
    o_gla = gla_branch(gq, gk, gv, glr, gla_w_up, gla_b_up, gla_norm_w).astype(h.dtype) * jax.nn.silu(gz)
    o_nsa = nsa_branch(nq, nkc, nvc, nks, nvs, nkw, nvw, ngate,
                       pe_k, w1_k, w2_k, pe_v, w1_v, w2_v).astype(h.dtype) * jax.nn.silu(nz)
    return jnp.concatenate([o_gla, o_nsa], axis=-1) @ w_out


def odd_mixer(h, lb, w_in, w_out, hgrn_norm_w, conv_w, conv_b, dt_bias, a_log, d_skip, ssm_norm_w):
    proj = h @ w_in
    hq, hf, hi, hz, sz, sxbc, sdt = jnp.split(proj, _offsets(ODD_SIZES), axis=-1)
    o_h = hgrn2_branch(hq, hf, hi, lb, hgrn_norm_w).astype(h.dtype) * jax.nn.silu(hz)
    y = mamba2_branch(sxbc, sdt, conv_w, conv_b, dt_bias, a_log, d_skip)
    o_s = rms_norm(y * jax.nn.silu(sz.astype(jnp.float32)), ssm_norm_w).astype(h.dtype)
    return jnp.concatenate([o_h, o_s], axis=-1) @ w_out


def setup_inputs(seed: int = 0) -> dict:
    key = jax.random.key(seed)
    keys = iter(jax.random.split(key, 40))
    f32 = jnp.float32
    ne, no = (DEPTH + 1) // 2, DEPTH // 2

    def nrm(shape, scale):
        return jax.random.normal(next(keys), shape, f32) * scale

    def gain(shape):
        return 1.0 + nrm(shape, 0.01)

    dt0 = jnp.exp(jax.random.uniform(next(keys), (no, SSM_HEADS), f32, math.log(1e-3), math.log(1e-1)))
    dt_bias = dt0 + jnp.log(-jnp.expm1(-dt0))
    a_log = jnp.log(jax.random.uniform(next(keys), (no, SSM_HEADS), f32, 1.0, 16.0))
    return {
        'x': nrm((BATCH, SEQ, D_MODEL), 1.0),
        'c': nrm((BATCH, D_MODEL), 1.0),
        'ada_w': nrm((DEPTH, D_MODEL, 3 * D_MODEL), 0.5 * D_MODEL ** -0.5),
        'ada_b': nrm((DEPTH, 3 * D_MODEL), 0.01),
        'pre_norm_w': gain((DEPTH, D_MODEL)),
        'post_norm_w': gain((DEPTH, D_MODEL)),
        'even_w_in': nrm((ne, D_MODEL, sum(EVEN_SIZES)), D_MODEL ** -0.5),
        'even_w_out': nrm((ne, MIX_WIDTH, D_MODEL), MIX_WIDTH ** -0.5),
        'gla_w_up': nrm((ne, GLA_RANK, GLA_HEADS * GLA_DK), GLA_RANK ** -0.5),
        'gla_b_up': nrm((ne, GLA_HEADS * GLA_DK), 0.1),
        'gla_norm_w': gain((ne, GLA_DV)),
        'nsa_pe_k': nrm((ne, CMP_LEN, NSA_DH), 0.02),
        'nsa_w1_k': nrm((ne, CMP_LEN * NSA_DH, CMP_HIDDEN), (CMP_LEN * NSA_DH) ** -0.5),
        'nsa_w2_k': nrm((ne, CMP_HIDDEN, NSA_DH), CMP_HIDDEN ** -0.5),
        'nsa_pe_v': nrm((ne, CMP_LEN, NSA_DH), 0.02),
        'nsa_w1_v': nrm((ne, CMP_LEN * NSA_DH, CMP_HIDDEN), (CMP_LEN * NSA_DH) ** -0.5),
        'nsa_w2_v': nrm((ne, CMP_HIDDEN, NSA_DH), CMP_HIDDEN ** -0.5),
        'odd_w_in': nrm((no, D_MODEL, sum(ODD_SIZES)), D_MODEL ** -0.5),
        'odd_w_out': nrm((no, MIX_WIDTH, D_MODEL), MIX_WIDTH ** -0.5),
        'hgrn_lb_logits': nrm((DEPTH, HGRN_HEADS * HGRN_DK), 0.1),
        'hgrn_norm_w': gain((no, HGRN_DV)),
        'ssm_conv_w': nrm((no, SSM_CONV, SSM_CONV_CH), SSM_CONV ** -0.5),
        'ssm_conv_b': nrm((no, SSM_CONV_CH), 0.01),
        'ssm_dt_bias': dt_bias,
        'ssm_a_log': a_log,
        'ssm_d': gain((no, SSM_HEADS)),
        'ssm_norm_w': gain((no, SSM_WIDTH)),
    }


def reference(x, c, ada_w, ada_b, pre_norm_w, post_norm_w,
              even_w_in, even_w_out, gla_w_up, gla_b_up, gla_norm_w,
              nsa_pe_k, nsa_w1_k, nsa_w2_k, nsa_pe_v, nsa_w1_v, nsa_w2_v,
              odd_w_in, odd_w_out, hgrn_lb_logits, hgrn_norm_w,
              ssm_conv_w, ssm_conv_b, ssm_dt_bias, ssm_a_log, ssm_d, ssm_norm_w):
    sm = jax.nn.softmax(hgrn_lb_logits.astype(jnp.float32), axis=0)
    lb_all = jnp.cumsum(sm, axis=0) - sm[0]
    c_act = jax.nn.silu(c)
    for l in range(DEPTH):
        mod = c_act @ ada_w[l] + ada_b[l]
        shift, scale, gate = jnp.split(mod, 3, axis=-1)
        h = rms_norm(x, pre_norm_w[l]) * (1.0 + scale[:, None, :]) + shift[:, None, :]
        if l % 2 == 0:
            e = l // 2
            y = even_mixer(h, even_w_in[e], even_w_out[e], gla_w_up[e], gla_b_up[e], gla_norm_w[e],
                           nsa_pe_k[e], nsa_w1_k[e], nsa_w2_k[e], nsa_pe_v[e], nsa_w1_v[e], nsa_w2_v[e])
        else:
            o = l // 2
            y = odd_mixer(h, lb_all[l], odd_w_in[o], odd_w_out[o], hgrn_norm_w[o], ssm_conv_w[o], ssm_conv_b[o],
                          ssm_dt_bias[o], ssm_a_log[o], ssm_d[o], ssm_norm_w[o])
        x = x + gate[:, None, :] * rms_norm(y, post_norm_w[l])
    return x
```

```python
import functools
import math

import jax
import jax.numpy as jnp
import numpy as np
from jax import lax
from jax.experimental import pallas as pl
from jax.experimental.pallas import tpu as pltpu

F32 = jnp.float32
BF16 = jnp.bfloat16
HIGHEST = lax.Precision.HIGHEST

D_MODEL = 2048
EPS = 1e-6
NEG = -1e30
FORCE = 1e30
ROPE_THETA = 10000.0

GLA_HEADS = 4
GLA_DK = 128
GLA_DV = 256
GLA_RANK = 16
GLA_TAU = 16.0

NSA_DH = 128
NSA_HEADS = 8
NSA_GROUPS = 2
NSA_HPG = 4
CMP_LEN = 32
CMP_STRIDE = 16
CMP_HIDDEN = 256
SLC_BLOCK = 64
SLC_SHIFT = 6
SLC_TOPK = 16
WINDOW = 512

HGRN_HEADS = 8
HGRN_DK = 128
HGRN_DV = 128

SSM_HEADDIM = 64
SSM_HEADS = 16
SSM_GROUPS = 2
SSM_STATE = 128
SSM_CONV = 4
SSM_CHUNK = 256
SSM_WIDTH = 1024
SSM_CONV_CH = SSM_WIDTH + 2 * SSM_GROUPS * SSM_STATE

LIN_CHUNK = 64
MAIN_COLS = 6656
VMEM_LIMIT = 56 * 1024 * 1024


def _cparams(sem):
    return pltpu.CompilerParams(dimension_semantics=sem, vmem_limit_bytes=VMEM_LIMIT)


def _dot(a, b):
    return jnp.dot(a, b, preferred_element_type=F32)


def _dot_nt(a, b):
    return lax.dot_general(a, b, (((1,), (1,)), ((), ())), preferred_element_type=F32)


def _dot_tn(a, b):
    return lax.dot_general(a, b, (((0,), (0,)), ((), ())), preferred_element_type=F32)


def _dot_exact(a, b):
    return jnp.dot(a, b, preferred_element_type=F32, precision=HIGHEST)


def _sigmoid(x):
    return 1.0 / (1.0 + jnp.exp(-x))


def _silu(x):
    return x * _sigmoid(x)


def _log_sigmoid(x):
    return jnp.minimum(x, 0.0) - jnp.log1p(jnp.exp(-jnp.abs(x)))


def _softplus(x):
    return jnp.maximum(x, 0.0) + jnp.log1p(jnp.exp(-jnp.abs(x)))


def _logaddexp(a, b):
    return jnp.maximum(a, b) + jnp.log1p(jnp.exp(-jnp.abs(a - b)))


def _mod_kernel(c_ref, w_ref, b_ref, o_ref):
    ca = _silu(c_ref[...])
    o_ref[0] = _dot(ca, w_ref[0]) + b_ref[0]


def _adaln_mod(c, ada_w, ada_b):
    depth, d, n3 = ada_w.shape
    b = c.shape[0]
    rows = 8
    c_pad = jnp.pad(c, ((0, rows - b), (0, 0)))
    tn = 768
    out = pl.pallas_call(
        _mod_kernel,
        out_shape=jax.ShapeDtypeStruct((depth, rows, n3), F32),
        grid=(depth, n3 // tn),
        in_specs=[
            pl.BlockSpec((rows, d), lambda l, j: (0, 0)),
            pl.BlockSpec((1, d, tn), lambda l, j: (l, 0, j)),
            pl.BlockSpec((1, 1, tn), lambda l, j: (l, 0, j)),
        ],
        out_specs=pl.BlockSpec((1, rows, tn), lambda l, j: (l, 0, j)),
        compiler_params=_cparams(("parallel", "parallel")),
        name="adaln_mod",
    )(c_pad, ada_w, ada_b.reshape(depth, 1, n3))
    return out[:, :b]


def _inproj_kernel(x_ref, mod_ref, nw_ref, w_ref, ws_ref, o_ref, os_ref, h_scr):
    d = x_ref.shape[-1]

    @pl.when(pl.program_id(2) == 0)
    def _():
        x = x_ref[0]
        var = jnp.mean(x * x, axis=-1, keepdims=True)
        y = x * lax.rsqrt(var + EPS) * nw_ref[...]
        shift = mod_ref[0, :, 0:d]
        scale = mod_ref[0, :, d:2 * d]
        hb = (y * (1.0 + scale) + shift).astype(BF16)
        h_scr[...] = hb
        os_ref[0] = _dot(hb, ws_ref[...])

    o_ref[0] = _dot(h_scr[...], w_ref[...])


def _in_proj(x, mod_l, norm_w, w_main, w_small):
    b, s, d = x.shape
    n = w_main.shape[1]
    ns = w_small.shape[1]
    tm = min(512, s)
    tn = 512
    return pl.pallas_call(
        _inproj_kernel,
        out_shape=(jax.ShapeDtypeStruct((b, s, n), F32),
                   jax.ShapeDtypeStruct((b, s, ns), F32)),
        grid=(b, s // tm, n // tn),
        in_specs=[
            pl.BlockSpec((1, tm, d), lambda bi, i, j: (bi, i, 0)),
            pl.BlockSpec((1, 1, 3 * d), lambda bi, i, j: (bi, 0, 0)),
            pl.BlockSpec((1, d), lambda bi, i, j: (0, 0)),
            pl.BlockSpec((d, tn), lambda bi, i, j: (0, j)),
            pl.BlockSpec((d, ns), lambda bi, i, j: (0, 0)),
        ],
        out_specs=(pl.BlockSpec((1, tm, tn), lambda bi, i, j: (bi, i, j)),
                   pl.BlockSpec((1, tm, ns), lambda bi, i, j: (bi, i, 0))),
        scratch_shapes=[pltpu.VMEM((tm, d), BF16)],
        compiler_params=_cparams(("parallel", "parallel", "arbitrary")),
        name="in_proj",
    )(x, mod_l.reshape(b, 1, 3 * d), norm_w.reshape(1, d), w_main, w_small)


def _outproj_kernel(a1_ref, a2_ref, w_ref, x_ref, mod_ref, nw_ref, o_ref):
    d = x_ref.shape[-1]
    half = a1_ref.shape[-1]
    y = _dot(a1_ref[0], w_ref[0:half, :]) + _dot(a2_ref[0], w_ref[half:2 * half, :])
    var = jnp.mean(y * y, axis=-1, keepdims=True)
    yn = y * lax.rsqrt(var + EPS) * nw_ref[...]
    gate = mod_ref[0, :, 2 * d:3 * d]
    o_ref[0] = x_ref[0] + gate * yn


def _out_proj(a1, a2, w_out, x, mod_l, norm_w):
    b, s, d = x.shape
    half = a1.shape[-1]
    tm = min(512, s)
    return pl.pallas_call(
        _outproj_kernel,
        out_shape=jax.ShapeDtypeStruct((b, s, d), F32),
        grid=(b, s // tm),
        in_specs=[
            pl.BlockSpec((1, tm, half), lambda bi, i: (bi, i, 0)),
            pl.BlockSpec((1, tm, half), lambda bi, i: (bi, i, 0)),
            pl.BlockSpec((2 * half, d), lambda bi, i: (0, 0)),
            pl.BlockSpec((1, tm, d), lambda bi, i: (bi, i, 0)),
            pl.BlockSpec((1, 1, 3 * d), lambda bi, i: (bi, 0, 0)),
            pl.BlockSpec((1, d), lambda bi, i: (0, 0)),
        ],
        out_specs=pl.BlockSpec((1, tm, d), lambda bi, i: (bi, i, 0)),
        compiler_params=_cparams(("parallel", "parallel")),
        name="out_proj",
    )(a1, a2, w_out, x, mod_l.reshape(b, 1, 3 * d), norm_w.reshape(1, d))


def _lin_attn_chunk(q, k, v, g, st_scr, b_scr, g_scr, consts):
    c, dk = q.shape
    row, eye, tri, parents = consts
    b = _dot_exact(tri, g)
    b_scr[...] = b
    g_scr[pl.ds(8, c), :] = g
    b_last = b_scr[pl.ds(c - 1, 1), :]

    qb16 = q.astype(BF16)
    kb16 = k.astype(BF16)
    attn = jnp.where(eye, _dot_nt(qb16, kb16), 0.0)

    for s in (32, 16, 8, 4):
        pieces = [jnp.broadcast_to(b_scr[pl.ds(p * 2 * s + s - 1, 1), :], (2 * s, dk))
                  for p in range(c // (2 * s))]
        d = b - jnp.concatenate(pieces, axis=0)
        upper = (row & s) != 0
        qs = q * jnp.exp(jnp.where(upper, d, NEG))
        ks = k * jnp.exp(jnp.where(upper, NEG, -d))
        attn = attn + jnp.where(parents[s], _dot_nt(qs.astype(BF16), ks.astype(BF16)), 0.0)

    g_dn = g_scr[pl.ds(7, c), :]
    g_up = g_scr[pl.ds(9, c), :]
    r4 = row & 3
    dq = jnp.where(r4 == 2, g, jnp.where(r4 == 3, g + g_dn, NEG))
    dk_ = jnp.where(r4 == 0, g_up, jnp.where(r4 == 1, 0.0, NEG))
    attn = attn + jnp.where(parents[2], _dot_nt((q * jnp.exp(dq)).astype(BF16),
                                                (k * jnp.exp(dk_)).astype(BF16)), 0.0)

    odd = (row & 1) != 0
    q1 = jnp.where(odd, q * jnp.exp(g), 0.0)
    k1 = jnp.where(odd, 0.0, k)
    attn = attn + jnp.where(parents[1], _dot_nt(q1.astype(BF16), k1.astype(BF16)), 0.0)

    v16 = v.astype(BF16)
    st = st_scr[...]
    o = _dot(attn.astype(BF16), v16) + _dot_nt((q * jnp.exp(b)).astype(BF16), st.astype(BF16))
    e_last = jnp.exp(b_last)
    k_dec = (k * jnp.exp(b_last - b)).astype(BF16)
    st_scr[...] = st * e_last + _dot_tn(v16, k_dec)
    return o


def _lin_attn_consts(c, dk):
    row = lax.broadcasted_iota(jnp.int32, (c, dk), 0)
    ri = lax.broadcasted_iota(jnp.int32, (c, c), 0)
    ci = lax.broadcasted_iota(jnp.int32, (c, c), 1)
    eye = ri == ci
    tri = jnp.where(ri >= ci, 1.0, 0.0).astype(F32)
    parents = {s: (ri & -(2 * s)) == (ci & -(2 * s)) for s in (32, 16, 8, 4, 2, 1)}
    return row, eye, tri, parents


def _lin_attn_finish(o, gz, nw):
    var = jnp.mean(o * o, axis=-1, keepdims=True)
    return (o * lax.rsqrt(var + EPS) * nw) * _silu(gz)


def _gla_kernel(q_ref, k_ref, v_ref, gz_ref, glr_ref, wup_ref, bup_ref, nw_ref, o_ref,
                st_scr, b_scr, g_scr):
    c = LIN_CHUNK
    ts, dk = q_ref.shape[1], q_ref.shape[2]

    @pl.when(pl.program_id(2) == 0)
    def _():
        st_scr[...] = jnp.zeros_like(st_scr)

    g_scr[...] = jnp.zeros_like(g_scr)
    consts = _lin_attn_consts(c, dk)
    wup = wup_ref[...]
    bup = bup_ref[...]
    nw = nw_ref[...]
    q_scale = dk ** -0.5

    def body(ci, carry):
        r0 = pl.multiple_of(ci * c, c)
        q = q_ref[0, pl.ds(r0, c), :] * q_scale
        k = k_ref[0, pl.ds(r0, c), :]
        v = v_ref[0, pl.ds(r0, c), :]
        z = _dot(glr_ref[0, pl.ds(r0, c), :], wup) + bup
        g = _log_sigmoid(z) * (1.0 / GLA_TAU)
        o = _lin_attn_chunk(q, k, v, g, st_scr, b_scr, g_scr, consts)
        out = _lin_attn_finish(o, gz_ref[0, pl.ds(r0, c), :], nw)
        o_ref[0, pl.ds(r0, c), :] = out.astype(o_ref.dtype)
        return carry

    lax.fori_loop(0, ts // c, body, 0)


def _hgrn_kernel(q_ref, f_ref, v_ref, gz_ref, lbl_ref, nw_ref, o_ref, st_scr, b_scr, g_scr,
                 *, layer):
    c = LIN_CHUNK
    ts, dk = q_ref.shape[1], q_ref.shape[2]

    @pl.when(pl.program_id(2) == 0)
    def _():
        st_scr[...] = jnp.zeros_like(st_scr)

    g_scr[...] = jnp.zeros_like(g_scr)
    consts = _lin_attn_consts(c, dk)
    nw = nw_ref[...]

    logits = lbl_ref[...]
    depth = logits.shape[0]
    mx = logits[0:1, :]
    for r in range(1, depth):
        mx = jnp.maximum(mx, logits[r:r + 1, :])
    ex = [jnp.exp(logits[r:r + 1, :] - mx) for r in range(depth)]
    den = ex[0]
    for r in range(1, depth):
        den = den + ex[r]
    sm = [e / den for e in ex]
    lb = sm[0]
    for r in range(1, layer + 1):
        lb = lb + sm[r]
    lb = lb - sm[0]
    log_lb = jnp.log(lb)
    log_1mlb = jnp.log1p(-lb)

    def body(ci, carry):
        r0 = pl.multiple_of(ci * c, c)
        q = q_ref[0, pl.ds(r0, c), :]
        z = f_ref[0, pl.ds(r0, c), :]
        v = v_ref[0, pl.ds(r0, c), :]
        g = _logaddexp(log_lb, log_1mlb + _log_sigmoid(z))
        k = (1.0 - lb) * (1.0 / (1.0 + jnp.exp(z)))
        o = _lin_attn_chunk(q, k, v, g, st_scr, b_scr, g_scr, consts)
        out = _lin_attn_finish(o, gz_ref[0, pl.ds(r0, c), :], nw)
        o_ref[0, pl.ds(r0, c), :] = out.astype(o_ref.dtype)
        return carry

    lax.fori_loop(0, ts // c, body, 0)


def _lin_scratch(dv, dk):
    return [pltpu.VMEM((dv, dk), F32),
            pltpu.VMEM((LIN_CHUNK, dk), F32),
            pltpu.VMEM((LIN_CHUNK + 16, dk), F32)]


def _gla_branch(proj, small, w_up, b_up, norm_w):
    b, s, _ = proj.shape
    ts = min(512, s)
    dk, dv, h = GLA_DK, GLA_DV, GLA_HEADS
    w_up_pad = jnp.pad(w_up, ((0, 128 - GLA_RANK), (0, 0)))
    return pl.pallas_call(
        _gla_kernel,
        out_shape=jax.ShapeDtypeStruct((b, s, h * dv), BF16),
        grid=(b, h, s // ts),
        in_specs=[
            pl.BlockSpec((1, ts, dk), lambda bi, hi, i: (bi, i, hi)),
            pl.BlockSpec((1, ts, dk), lambda bi, hi, i: (bi, i, h + hi)),
            pl.BlockSpec((1, ts, dv), lambda bi, hi, i: (bi, i, 4 + hi)),
            pl.BlockSpec((1, ts, dv), lambda bi, hi, i: (bi, i, 8 + hi)),
            pl.BlockSpec((1, ts, 128), lambda bi, hi, i: (bi, i, 0)),
            pl.BlockSpec((128, dk), lambda bi, hi, i: (0, hi)),
            pl.BlockSpec((1, dk), lambda bi, hi, i: (0, hi)),
            pl.BlockSpec((1, dv), lambda bi, hi, i: (0, 0)),
        ],
        out_specs=pl.BlockSpec((1, ts, dv), lambda bi, hi, i: (bi, i, hi)),
        scratch_shapes=_lin_scratch(dv, dk),
        compiler_params=_cparams(("parallel", "parallel", "arbitrary")),
        name="gla",
    )(proj, proj, proj, proj, small, w_up_pad, b_up.reshape(1, -1), norm_w.reshape(1, dv))


def _hgrn_branch(proj, lb_logits, norm_w, layer):
    b, s, _ = proj.shape
    ts = min(512, s)
    dk, dv, h = HGRN_DK, HGRN_DV, HGRN_HEADS
    depth = lb_logits.shape[0]
    return pl.pallas_call(
        functools.partial(_hgrn_kernel, layer=layer),
        out_shape=jax.ShapeDtypeStruct((b, s, h * dv), BF16),
        grid=(b, h, s // ts),
        in_specs=[
            pl.BlockSpec((1, ts, dk), lambda bi, hi, i: (bi, i, hi)),
            pl.BlockSpec((1, ts, dk), lambda bi, hi, i: (bi, i, h + hi)),
            pl.BlockSpec((1, ts, dv), lambda bi, hi, i: (bi, i, 2 * h + hi)),
            pl.BlockSpec((1, ts, dv), lambda bi, hi, i: (bi, i, 3 * h + hi)),
            pl.BlockSpec((depth, dk), lambda bi, hi, i: (0, hi)),
            pl.BlockSpec((1, dv), lambda bi, hi, i: (0, 0)),
        ],
        out_specs=pl.BlockSpec((1, ts, dv), lambda bi, hi, i: (bi, i, hi)),
        scratch_shapes=_lin_scratch(dv, dk),
        compiler_params=_cparams(("parallel", "parallel", "arbitrary")),
        name="hgrn2",
    )(proj, proj, proj, proj, lb_logits, norm_w.reshape(1, dv))


def _ssd_kernel(z_ref, x_ref, bc_ref, dt_ref, cw_ref, cb_ref, dtb_ref, alog_ref, dexp_ref, nw_ref,
                hexp_ref, o_ref, stage_scr, xbc_scr, st_scr, y_scr):
    c = x_ref.shape[1]
    nch = SSM_CONV_CH
    width = SSM_WIDTH
    n = SSM_STATE
    pairs = SSM_HEADS // 2

    @pl.when(pl.program_id(1) == 0)
    def _():
        st_scr[...] = jnp.zeros_like(st_scr)
        stage_scr[pl.ds(0, 8), :] = jnp.zeros((8, nch), F32)

    stage_scr[pl.ds(8, c), 0:width] = x_ref[0]
    stage_scr[pl.ds(8, c), width:nch] = bc_ref[0]
    acc = cb_ref[...] + cw_ref[0:1, :] * stage_scr[pl.ds(5, c), :]
    for kk in range(1, SSM_CONV):
        acc = acc + cw_ref[kk:kk + 1, :] * stage_scr[pl.ds(5 + kk, c), :]
    stage_scr[pl.ds(0, 8), :] = stage_scr[pl.ds(c, 8), :]
    xbc_scr[...] = _silu(acc)

    dt = _softplus(dt_ref[0] + dtb_ref[...])
    a = -jnp.exp(alog_ref[...])
    da = dt * a
    ri = lax.broadcasted_iota(jnp.int32, (c, c), 0)
    ci = lax.broadcasted_iota(jnp.int32, (c, c), 1)
    causal = ri >= ci
    tri = jnp.where(causal, 1.0, 0.0).astype(F32)
    a_cs = _dot_exact(tri, da)
    a_cs_t = a_cs.T
    hexp = hexp_ref[...]
    dt_x = _dot_exact(dt, hexp)
    acs_x = _dot_exact(a_cs, hexp)
    a_last_x = acs_x[c - 1:c, :]
    lane = lax.broadcasted_iota(jnp.int32, (c, 2 * SSM_HEADDIM), 1)
    first = lane < SSM_HEADDIM

    for grp in range(SSM_GROUPS):
        bm = xbc_scr[:, width + grp * n: width + (grp + 1) * n]
        cm = xbc_scr[:, width + (SSM_GROUPS + grp) * n: width + (SSM_GROUPS + grp + 1) * n]
        bm16 = bm.astype(BF16)
        cm16 = cm.astype(BF16)
        cb = _dot_nt(cm16, bm16)
        for pp in range(pairs // SSM_GROUPS):
            p = grp * (pairs // SSM_GROUPS) + pp
            lo = p * 2 * SSM_HEADDIM
            hi = lo + 2 * SSM_HEADDIM
            xs = xbc_scr[:, lo:hi]
            xdt = xs * dt_x[:, lo:hi]
            acs = acs_x[:, lo:hi]
            y = jnp.zeros((c, 2 * SSM_HEADDIM), F32)
            for hh in range(2):
                h = 2 * p + hh
                col = jnp.broadcast_to(a_cs[:, h:h + 1], (c, c))
                rw = jnp.broadcast_to(a_cs_t[h:h + 1, :], (c, c))
                lmat = jnp.exp(jnp.where(causal, col - rw, NEG))
                keep = first if hh == 0 else jnp.logical_not(first)
                xh = jnp.where(keep, xdt, 0.0).astype(BF16)
                y = y + _dot((cb * lmat).astype(BF16), xh)
            st = st_scr[p]
            y = y + _dot(cm16, st.astype(BF16)) * jnp.exp(acs)
            decay = jnp.exp(a_last_x[:, lo:hi] - acs)
            st_scr[p] = st * jnp.exp(a_last_x[:, lo:hi]) + _dot_tn(bm16, (xdt * decay).astype(BF16))
            y_scr[:, lo:hi] = y + dexp_ref[:, lo:hi] * xs

    yz = y_scr[...] * _silu(z_ref[0])
    var = jnp.mean(yz * yz, axis=-1, keepdims=True)
    o_ref[0] = (yz * lax.rsqrt(var + EPS) * nw_ref[...]).astype(o_ref.dtype)


def _ssd_branch(proj, small, conv_w, conv_b, dt_bias, a_log, d_skip, norm_w):
    b, s, _ = proj.shape
    c = math.gcd(SSM_CHUNK, s)
    nch = SSM_CONV_CH
    pad16 = lambda v: jnp.pad(v.reshape(1, -1), ((0, 0), (0, 128 - SSM_HEADS)))
    hexp = np.zeros((128, SSM_WIDTH), np.float32)
    for h in range(SSM_HEADS):
        hexp[h, h * SSM_HEADDIM:(h + 1) * SSM_HEADDIM] = 1.0
    d_exp = jnp.repeat(d_skip, SSM_HEADDIM).reshape(1, SSM_WIDTH)
    return pl.pallas_call(
        _ssd_kernel,
        out_shape=jax.ShapeDtypeStruct((b, s, SSM_WIDTH), BF16),
        grid=(b, s // c),
        in_specs=[
            pl.BlockSpec((1, c, SSM_WIDTH), lambda bi, i: (bi, i, 4)),
            pl.BlockSpec((1, c, SSM_WIDTH), lambda bi, i: (bi, i, 5)),
            pl.BlockSpec((1, c, 512), lambda bi, i: (bi, i, 12)),
            pl.BlockSpec((1, c, 128), lambda bi, i: (bi, i, 0)),
            pl.BlockSpec((SSM_CONV, nch), lambda bi, i: (0, 0)),
            pl.BlockSpec((1, nch), lambda bi, i: (0, 0)),
            pl.BlockSpec((1, 128), lambda bi, i: (0, 0)),
            pl.BlockSpec((1, 128), lambda bi, i: (0, 0)),
            pl.BlockSpec((1, SSM_WIDTH), lambda bi, i: (0, 0)),
            pl.BlockSpec((1, SSM_WIDTH), lambda bi, i: (0, 0)),
            pl.BlockSpec((128, SSM_WIDTH), lambda bi, i: (0, 0)),
        ],
        out_specs=pl.BlockSpec((1, c, SSM_WIDTH), lambda bi, i: (bi, i, 0)),
        scratch_shapes=[
            pltpu.VMEM((c + 8, nch), F32),
            pltpu.VMEM((c, nch), F32),
            pltpu.VMEM((SSM_HEADS // 2, SSM_STATE, 2 * SSM_HEADDIM), F32),
            pltpu.VMEM((c, SSM_WIDTH), F32),
        ],
        compiler_params=_cparams(("parallel", "arbitrary")),
        name="ssd",
    )(proj, proj, proj, small, conv_w, conv_b.reshape(1, nch), pad16(dt_bias), pad16(a_log),
      d_exp, norm_w.reshape(1, SSM_WIDTH), jnp.asarray(hexp))


def _rope(x, cos, sin_signed):
    return x * cos + pltpu.roll(x, NSA_DH // 2, axis=1) * sin_signed


def _nsa_prep_kernel(q_ref, kc_ref, ks_ref, vs_ref, kw_ref, vw_ref, cos_ref, sin_ref,
                     qo_ref, kco_ref, kso_ref, vso_ref, kwo_ref, vwo_ref):
    cos = cos_ref[...]
    sin = sin_ref[...]
    dh = NSA_DH
    scale = dh ** -0.5
    for h in range(NSA_HEADS):
        sl = slice(h * dh, (h + 1) * dh)
        qo_ref[0, :, sl] = (_rope(q_ref[0, :, sl], cos, sin) * scale).astype(qo_ref.dtype)
    for g in range(NSA_GROUPS):
        sl = slice(g * dh, (g + 1) * dh)
        kco_ref[0, :, sl] = _rope(kc_ref[0, :, sl], cos, sin)
        kso_ref[0, :, sl] = _rope(ks_ref[0, :, sl], cos, sin).astype(kso_ref.dtype)
        kwo_ref[0, :, sl] = _rope(kw_ref[0, :, sl], cos, sin).astype(kwo_ref.dtype)
    vso_ref[0] = vs_ref[0].astype(vso_ref.dtype)
    vwo_ref[0] = vw_ref[0].astype(vwo_ref.dtype)


def _nsa_prep(proj):
    b, s, _ = proj.shape
    ts = min(512, s)
    half = NSA_DH // 2
    inv_freq = ROPE_THETA ** (-jnp.arange(half, dtype=F32) / half)
    ang = jnp.arange(s, dtype=F32)[:, None] * inv_freq[None, :]
    cos = jnp.cos(ang)
    sin = jnp.sin(ang)
    cos_full = jnp.concatenate([cos, cos], axis=-1)
    sin_signed = jnp.concatenate([-sin, sin], axis=-1)
    kvw = NSA_GROUPS * NSA_DH
    col = lambda idx: (lambda bi, i: (bi, i, idx))
    kv_spec = lambda idx: pl.BlockSpec((1, ts, kvw), col(idx))
    kv_out = pl.BlockSpec((1, ts, kvw), lambda bi, i: (bi, i, 0))
    return pl.pallas_call(
        _nsa_prep_kernel,
        out_shape=(jax.ShapeDtypeStruct((b, s, NSA_HEADS * NSA_DH), BF16),
                   jax.ShapeDtypeStruct((b, s, kvw), F32),
                   jax.ShapeDtypeStruct((b, s, kvw), BF16),
                   jax.ShapeDtypeStruct((b, s, kvw), BF16),
                   jax.ShapeDtypeStruct((b, s, kvw), BF16),
                   jax.ShapeDtypeStruct((b, s, kvw), BF16)),
        grid=(b, s // ts),
        in_specs=[
            pl.BlockSpec((1, ts, NSA_HEADS * NSA_DH), col(3)),
            kv_spec(16), kv_spec(18), kv_spec(19), kv_spec(20), kv_spec(21),
            pl.BlockSpec((ts, NSA_DH), lambda bi, i: (i, 0)),
            pl.BlockSpec((ts, NSA_DH), lambda bi, i: (i, 0)),
        ],
        out_specs=(pl.BlockSpec((1, ts, NSA_HEADS * NSA_DH), lambda bi, i: (bi, i, 0)),
                   kv_out, kv_out, kv_out, kv_out, kv_out),
        compiler_params=_cparams(("parallel", "parallel")),
        name="nsa_prep",
    )(proj, proj, proj, proj, proj, proj, cos_full, sin_signed)


def _compress_kernel(t_ref, pe_ref, w1_ref, w2_ref, o_ref, sh_scr):
    n = o_ref.shape[2]
    dh = NSA_DH
    half = CMP_LEN // 2
    acc_a = jnp.zeros((n, CMP_HIDDEN), F32)
    acc_b = jnp.zeros((n, CMP_HIDDEN), F32)
    for l in range(half):
        t = t_ref[0, pl.ds(l, n, stride=CMP_STRIDE), :]
        acc_a = acc_a + _dot((t + pe_ref[l:l + 1, :]).astype(BF16), w1_ref[l * dh:(l + 1) * dh, :])
        acc_b = acc_b + _dot((t + pe_ref[half + l:half + l + 1, :]).astype(BF16),
                             w1_ref[(half + l) * dh:(half + l + 1) * dh, :])
    sh_scr[pl.ds(0, n), :] = acc_b
    sh_scr[pl.ds(n, 8), :] = jnp.zeros((8, CMP_HIDDEN), F32)
    hid = acc_a + sh_scr[pl.ds(1, n), :]
    o_ref[0, 0] = _dot(_silu(hid).astype(BF16), w2_ref[...]).astype(o_ref.dtype)


def _nsa_compress(src, col0, pe, w1, w2):
    b, s, _ = src.shape
    n = s // CMP_STRIDE
    base = col0 // NSA_DH
    return pl.pallas_call(
        _compress_kernel,
        out_shape=jax.ShapeDtypeStruct((b, NSA_GROUPS, n, NSA_DH), BF16),
        grid=(b, NSA_GROUPS),
        in_specs=[
            pl.BlockSpec((1, s, NSA_DH), lambda bi, g: (bi, 0, base + g)),
            pl.BlockSpec((CMP_LEN, NSA_DH), lambda bi, g: (0, 0)),
            pl.BlockSpec((CMP_LEN * NSA_DH, CMP_HIDDEN), lambda bi, g: (0, 0)),
            pl.BlockSpec((CMP_HIDDEN, NSA_DH), lambda bi, g: (0, 0)),
        ],
        out_specs=pl.BlockSpec((1, 1, n, NSA_DH), lambda bi, g: (bi, g, 0, 0)),
        scratch_shapes=[pltpu.VMEM((n + 8, CMP_HIDDEN), F32)],
        compiler_params=_cparams(("parallel", "parallel")),
        name="nsa_compress",
    )(src, pe, w1.astype(BF16), w2.astype(BF16))


def _nsa_cmp_kernel(q_ref, kc_ref, vc_ref, gate_ref, ov_ref, oc_ref, sel_ref):
    tq = q_ref.shape[1]
    ncp = kc_ref.shape[2]
    ns = sel_ref.shape[3]
    dh = NSA_DH
    t0 = pl.program_id(2) * tq
    t = t0 + lax.broadcasted_iota(jnp.int32, (tq, ncp), 0)
    nidx = lax.broadcasted_iota(jnp.int32, (tq, ncp), 1)
    valid = (nidx * CMP_STRIDE + (CMP_LEN - 1)) <= t
    kc = kc_ref[0, 0]
    vc = vc_ref[0, 0]
    sg = _sigmoid(gate_ref[0])
    p_sum = jnp.zeros((tq, ncp), F32)
    for j in range(NSA_HPG):
        qh = q_ref[0, :, j * dh:(j + 1) * dh]
        s1 = jnp.where(valid, _dot_nt(qh, kc), NEG)
        m = jnp.max(s1, axis=-1, keepdims=True)
        e = jnp.where(valid, jnp.exp(s1 - m), 0.0)
        den = jnp.sum(e, axis=-1, keepdims=True)
        p1 = e / jnp.where(den > 0.0, den, 1.0)
        o_c = _dot(p1.astype(BF16), vc)
        gate = jnp.broadcast_to(sg[:, 3 * j:3 * j + 1], (tq, dh))
        oc_ref[0, :, j * dh:(j + 1) * dh] = gate * o_c
        p_sum = p_sum + p1

    imp = _dot_exact(p_sum, ov_ref[...])
    blk = lax.broadcasted_iota(jnp.int32, (tq, ns), 1)
    blk_t = jnp.right_shift(t0 + lax.broadcasted_iota(jnp.int32, (tq, ns), 0), SLC_SHIFT)
    forced = (blk == 0) | (blk == blk_t) | (blk == blk_t - 1)
    score = jnp.where(forced, FORCE, jnp.where(blk <= blk_t, imp, NEG))
    blk_f = blk.astype(F32)
    sel = jnp.zeros((tq, ns), F32)
    for _ in range(min(SLC_TOPK, ns)):
        m = jnp.max(score, axis=-1, keepdims=True)
        first = jnp.min(jnp.where(score == m, blk_f, float(ns)), axis=-1, keepdims=True)
        pick = blk_f == first
        sel = jnp.where(pick, 1.0, sel)
        score = jnp.where(pick, -jnp.inf, score)
    sel_ref[0, 0] = sel.astype(sel_ref.dtype)


def _nsa_cmp(qr, kc, vc, small):
    b, s, _ = qr.shape
    ncp = kc.shape[2]
    ns = s // SLC_BLOCK
    tq = min(128, s)
    gw = NSA_HPG * NSA_DH
    cs = np.arange(ncp)[:, None] * CMP_STRIDE
    ss = np.arange(ns)[None, :] * SLC_BLOCK
    ov = ((np.minimum(cs + CMP_LEN, ss + SLC_BLOCK) - np.maximum(cs, ss)) > 0).astype(np.float32)
    ov[ncp - 1, :] = 0.0
    return pl.pallas_call(
        _nsa_cmp_kernel,
        out_shape=(jax.ShapeDtypeStruct((b, s, NSA_HEADS * NSA_DH), F32),
                   jax.ShapeDtypeStruct((b, NSA_GROUPS, s, ns), BF16)),
        grid=(b, NSA_GROUPS, s // tq),
        in_specs=[
            pl.BlockSpec((1, tq, gw), lambda bi, g, i: (bi, i, g)),
            pl.BlockSpec((1, 1, ncp, NSA_DH), lambda bi, g, i: (bi, g, 0, 0)),
            pl.BlockSpec((1, 1, ncp, NSA_DH), lambda bi, g, i: (bi, g, 0, 0)),
            pl.BlockSpec((1, tq, 128), lambda bi, g, i: (bi, i, 1 + g)),
            pl.BlockSpec((ncp, ns), lambda bi, g, i: (0, 0)),
        ],
        out_specs=(pl.BlockSpec((1, tq, gw), lambda bi, g, i: (bi, i, g)),
                   pl.BlockSpec((1, 1, tq, ns), lambda bi, g, i: (bi, g, i, 0))),
        compiler_params=_cparams(("parallel", "parallel", "parallel")),
        name="nsa_cmp_topk",
    )(qr, kc, vc, small, jnp.asarray(ov))


def _flash_tile(qs, k, v, mask, carry):
    m, l, acc = carry
    s = jnp.where(mask, _dot_nt(qs, k), NEG)
    m_new = jnp.maximum(m, jnp.max(s, axis=-1, keepdims=True))
    alpha = jnp.exp(m - m_new)
    p = jnp.where(mask, jnp.exp(s - m_new), 0.0)
    l = alpha * l + jnp.sum(p, axis=-1, keepdims=True)
    acc = alpha * acc + _dot(p.astype(BF16), v)
    return m_new, l, acc


def _nsa_attn_kernel(q_ref, ks_ref, vs_ref, kw_ref, vw_ref, sel_ref, oc_ref, gate_ref, nz_ref,
                     o_ref):
    tq = q_ref.shape[1]
    ns = sel_ref.shape[3]
    dh = NSA_DH
    tk = 2 * SLC_BLOCK
    hpg = NSA_HPG
    qi = pl.program_id(2)
    t0 = qi * tq
    rows = hpg * tq
    qs = jnp.concatenate([q_ref[0, :, j * dh:(j + 1) * dh] for j in range(hpg)], axis=0)
    t = t0 + (lax.broadcasted_iota(jnp.int32, (rows, tk), 0) & (tq - 1))
    lane = lax.broadcasted_iota(jnp.int32, (rows, tk), 1)
    sel = sel_ref[0, 0]
    eb = lax.broadcasted_iota(jnp.int32, (ns, tk), 0)
    el = jnp.right_shift(lax.broadcasted_iota(jnp.int32, (ns, tk), 1), SLC_SHIFT)
    init = (jnp.full((rows, 1), NEG, F32), jnp.zeros((rows, 1), F32), jnp.zeros((rows, dh), F32))
    n_tiles = (t0 + tq + tk - 1) // tk

    def slc_body(kt, carry):
        k0 = pl.multiple_of(kt * tk, tk)
        expand = jnp.where(eb == 2 * kt + el, 1.0, 0.0).astype(BF16)
        picked = _dot(sel, expand)
        picked = jnp.concatenate([picked] * hpg, axis=0)
        mask = (picked > 0.5) & ((k0 + lane) <= t)
        return _flash_tile(qs, ks_ref[0, pl.ds(k0, tk), :], vs_ref[0, pl.ds(k0, tk), :], mask, carry)

    _, l_s, acc_s = lax.fori_loop(0, n_tiles, slc_body, init)

    def win_body(kt, carry):
        k0 = pl.multiple_of(kt * tk, tk)
        kpos = k0 + lane
        mask = (kpos <= t) & (kpos > t - WINDOW)
        return _flash_tile(qs, kw_ref[0, pl.ds(k0, tk), :], vw_ref[0, pl.ds(k0, tk), :], mask, carry)

    first_w = jnp.maximum(t0 - WINDOW + 1, 0) // tk
    _, l_w, acc_w = lax.fori_loop(first_w, n_tiles, win_body, init)

    o_s = acc_s / l_s
    o_w = acc_w / l_w
    sg = _sigmoid(gate_ref[0])
    for j in range(hpg):
        g_s = jnp.broadcast_to(sg[:, 3 * j + 1:3 * j + 2], (tq, dh))
        g_w = jnp.broadcast_to(sg[:, 3 * j + 2:3 * j + 3], (tq, dh))
        sl = slice(j * dh, (j + 1) * dh)
        rs = slice(j * tq, (j + 1) * tq)
        mix = oc_ref[0, :, sl] + g_s * o_s[rs] + g_w * o_w[rs]
        o_ref[0, :, sl] = (mix * _silu(nz_ref[0, :, sl])).astype(o_ref.dtype)


def _nsa_attn(qr, ksr, vsb, kwr, vwb, sel, oc, small, proj):
    b, s, _ = qr.shape
    ns = sel.shape[3]
    tq = min(128, s)
    gw = NSA_HPG * NSA_DH
    kv_spec = pl.BlockSpec((1, s, NSA_DH), lambda bi, g, i: (bi, 0, g))
    return pl.pallas_call(
        _nsa_attn_kernel,
        out_shape=jax.ShapeDtypeStruct((b, s, NSA_HEADS * NSA_DH), BF16),
        grid=(b, NSA_GROUPS, s // tq),
        in_specs=[
            pl.BlockSpec((1, tq, gw), lambda bi, g, i: (bi, i, g)),
            kv_spec, kv_spec, kv_spec, kv_spec,
            pl.BlockSpec((1, 1, tq, ns), lambda bi, g, i: (bi, g, i, 0)),
            pl.BlockSpec((1, tq, gw), lambda bi, g, i: (bi, i, g)),
            pl.BlockSpec((1, tq, 128), lambda bi, g, i: (bi, i, 1 + g)),
            pl.BlockSpec((1, tq, gw), lambda bi, g, i: (bi, i, 11 + g)),
        ],
        out_specs=pl.BlockSpec((1, tq, gw), lambda bi, g, i: (bi, i, g)),
        compiler_params=_cparams(("parallel", "parallel", "parallel")),
        name="nsa_slc_win",
    )(qr, ksr, vsb, kwr, vwb, sel, oc, small, proj)


def _nsa_branch(proj, small, pe_k, w1_k, w2_k, pe_v, w1_v, w2_v):
    qr, kcr, ksr, vsb, kwr, vwb = _nsa_prep(proj)
    kc = _nsa_compress(kcr, 0, pe_k, w1_k, w2_k)
    vc = _nsa_compress(proj, 4352, pe_v, w1_v, w2_v)
    oc, sel = _nsa_cmp(qr, kc, vc, small)
    return _nsa_attn(qr, ksr, vsb, kwr, vwb, sel, oc, small, proj)


def _even_weights(w_in):
    main = jnp.concatenate([w_in[:, 0:2048], w_in[:, 2064:5648], w_in[:, 5672:6696]], axis=1)
    d = w_in.shape[0]
    gates = w_in[:, 5648:5672]
    per = NSA_HPG * 3
    blocks = [jnp.pad(w_in[:, 2048:2064], ((0, 0), (0, 128 - GLA_RANK)))]
    for g in range(NSA_GROUPS):
        blocks.append(jnp.pad(gates[:, g * per:(g + 1) * per], ((0, 0), (0, 128 - per))))
    small = jnp.concatenate(blocks, axis=1)
    return main.astype(BF16), small.astype(BF16)


def _odd_weights(w_in):
    main = w_in[:, 0:MAIN_COLS]
    small = jnp.pad(w_in[:, MAIN_COLS:MAIN_COLS + SSM_HEADS], ((0, 0), (0, 128 - SSM_HEADS)))
    return main.astype(BF16), small.astype(BF16)


def kernel(x, c, ada_w, ada_b, pre_norm_w, post_norm_w, even_w_in, even_w_out, gla_w_up, gla_b_up,
           gla_norm_w, nsa_pe_k, nsa_w1_k, nsa_w2_k, nsa_pe_v, nsa_w1_v, nsa_w2_v, odd_w_in,
           odd_w_out, hgrn_lb_logits, hgrn_norm_w, ssm_conv_w, ssm_conv_b, ssm_dt_bias, ssm_a_log,
           ssm_d, ssm_norm_w):
    depth = ada_w.shape[0]
    mod = _adaln_mod(c, ada_w, ada_b)
    for l in range(depth):
        if l % 2 == 0:
            e = l // 2
            w_main, w_small = _even_weights(even_w_in[e])
            proj, small = _in_proj(x, mod[l], pre_norm_w[l], w_main, w_small)
            o_a = _gla_branch(proj, small, gla_w_up[e], gla_b_up[e], gla_norm_w[e])
            o_b = _nsa_branch(proj, small, nsa_pe_k[e], nsa_w1_k[e], nsa_w2_k[e],
                              nsa_pe_v[e], nsa_w1_v[e], nsa_w2_v[e])
            w_out = even_w_out[e]
        else:
            o = l // 2
            w_main, w_small = _odd_weights(odd_w_in[o])
            proj, small = _in_proj(x, mod[l], pre_norm_w[l], w_main, w_small)
            o_a = _hgrn_branch(proj, hgrn_lb_logits, hgrn_norm_w[o], l)
            o_b = _ssd_branch(proj, small, ssm_conv_w[o], ssm_conv_b[o], ssm_dt_bias[o],
                              ssm_a_log[o], ssm_d[o], ssm_norm_w[o])
            w_out = odd_w_out[o]
        x = _out_proj(o_a, o_b, w_out.astype(BF16), x, mod[l], post_norm_w[l])
    return x
```

```python
import functools
import math

import jax
import jax.numpy as jnp
import numpy as np
from jax import lax
from jax.experimental import pallas as pl
from jax.experimental.pallas import tpu as pltpu

F32 = jnp.float32
BF16 = jnp.bfloat16
HIGHEST = lax.Precision.HIGHEST

D_MODEL = 2048
EPS = 1e-6
NEG = -1e30
FORCE = 1e30
ROPE_THETA = 10000.0

GLA_HEADS = 4
GLA_DK = 128
GLA_DV = 256
GLA_RANK = 16
GLA_TAU = 16.0

NSA_DH = 128
NSA_HEADS = 8
NSA_GROUPS = 2
NSA_HPG = 4
CMP_LEN = 32
CMP_STRIDE = 16
CMP_HIDDEN = 256
SLC_BLOCK = 64
SLC_SHIFT = 6
SLC_TOPK = 16
WINDOW = 512
NSA_TILE = 128
NSA_TILES_PER_STEP = 4

HGRN_HEADS = 8
HGRN_DK = 128
HGRN_DV = 128

SSM_HEADDIM = 64
SSM_HEADS = 16
SSM_GROUPS = 2
SSM_STATE = 128
SSM_CONV = 4
SSM_CHUNK = 256
SSM_WIDTH = 1024
SSM_CONV_CH = SSM_WIDTH + 2 * SSM_GROUPS * SSM_STATE

LIN_CHUNK = 64
MAIN_COLS = 6656
VMEM_LIMIT = 56 * 1024 * 1024


def _cparams(sem):
    return pltpu.CompilerParams(dimension_semantics=sem, vmem_limit_bytes=VMEM_LIMIT)


def _dot(a, b):
    return jnp.dot(a, b, preferred_element_type=F32)


def _dot_nt(a, b):
    return lax.dot_general(a, b, (((1,), (1,)), ((), ())), preferred_element_type=F32)


def _dot_tn(a, b):
    return lax.dot_general(a, b, (((0,), (0,)), ((), ())), preferred_element_type=F32)


def _dot_exact(a, b):
    return jnp.dot(a, b, preferred_element_type=F32, precision=HIGHEST)


def _sigmoid(x):
    return 1.0 / (1.0 + jnp.exp(-x))


def _silu(x):
    return x * _sigmoid(x)


def _log_sigmoid(x):
    return jnp.minimum(x, 0.0) - jnp.log1p(jnp.exp(-jnp.abs(x)))


def _softplus(x):
    return jnp.maximum(x, 0.0) + jnp.log1p(jnp.exp(-jnp.abs(x)))


def _logaddexp(a, b):
    return jnp.maximum(a, b) + jnp.log1p(jnp.exp(-jnp.abs(a - b)))


def _mod_kernel(c_ref, w_ref, b_ref, o_ref):
    ca = _silu(c_ref[...])
    o_ref[0] = _dot(ca, w_ref[0]) + b_ref[0]


def _adaln_mod(c, ada_w, ada_b):
    depth, d, n3 = ada_w.shape
    b = c.shape[0]
    rows = 8
    c_pad = jnp.pad(c, ((0, rows - b), (0, 0)))
    tn = 768
    out = pl.pallas_call(
        _mod_kernel,
        out_shape=jax.ShapeDtypeStruct((depth, rows, n3), F32),
        grid=(depth, n3 // tn),
        in_specs=[
            pl.BlockSpec((rows, d), lambda l, j: (0, 0)),
            pl.BlockSpec((1, d, tn), lambda l, j: (l, 0, j)),
            pl.BlockSpec((1, 1, tn), lambda l, j: (l, 0, j)),
        ],
        out_specs=pl.BlockSpec((1, rows, tn), lambda l, j: (l, 0, j)),
        compiler_params=_cparams(("parallel", "parallel")),
        name="adaln_mod",
    )(c_pad, ada_w, ada_b.reshape(depth, 1, n3))
    return out[:, :b]


def _inproj_kernel(x_ref, mod_ref, nw_ref, w_ref, ws_ref, o_ref, os_ref, h_scr):
    d = x_ref.shape[-1]

    @pl.when(pl.program_id(2) == 0)
    def _():
        x = x_ref[0]
        var = jnp.mean(x * x, axis=-1, keepdims=True)
        y = x * lax.rsqrt(var + EPS) * nw_ref[...]
        shift = mod_ref[0, :, 0:d]
        scale = mod_ref[0, :, d:2 * d]
        hb = (y * (1.0 + scale) + shift).astype(BF16)
        h_scr[...] = hb
        os_ref[0] = _dot(hb, ws_ref[...])

    o_ref[0] = _dot(h_scr[...], w_ref[...])


def _in_proj(x, mod_l, norm_w, w_main, w_small):
    b, s, d = x.shape
    n = w_main.shape[1]
    ns = w_small.shape[1]
    tm = min(1024, s)
    tn = 512
    return pl.pallas_call(
        _inproj_kernel,
        out_shape=(jax.ShapeDtypeStruct((b, s, n), F32),
                   jax.ShapeDtypeStruct((b, s, ns), F32)),
        grid=(b, s // tm, n // tn),
        in_specs=[
            pl.BlockSpec((1, tm, d), lambda bi, i, j: (bi, i, 0)),
            pl.BlockSpec((1, 1, 3 * d), lambda bi, i, j: (bi, 0, 0)),
            pl.BlockSpec((1, d), lambda bi, i, j: (0, 0)),
            pl.BlockSpec((d, tn), lambda bi, i, j: (0, j)),
            pl.BlockSpec((d, ns), lambda bi, i, j: (0, 0)),
        ],
        out_specs=(pl.BlockSpec((1, tm, tn), lambda bi, i, j: (bi, i, j)),
                   pl.BlockSpec((1, tm, ns), lambda bi, i, j: (bi, i, 0))),
        scratch_shapes=[pltpu.VMEM((tm, d), BF16)],
        compiler_params=_cparams(("parallel", "parallel", "arbitrary")),
        name="in_proj",
    )(x, mod_l.reshape(b, 1, 3 * d), norm_w.reshape(1, d), w_main, w_small)


def _outproj_kernel(a1_ref, a2_ref, w_ref, x_ref, mod_ref, nw_ref, o_ref):
    d = x_ref.shape[-1]
    half = a1_ref.shape[-1]
    y = _dot(a1_ref[0], w_ref[0:half, :]) + _dot(a2_ref[0], w_ref[half:2 * half, :])
    var = jnp.mean(y * y, axis=-1, keepdims=True)
    yn = y * lax.rsqrt(var + EPS) * nw_ref[...]
    gate = mod_ref[0, :, 2 * d:3 * d]
    o_ref[0] = x_ref[0] + gate * yn


def _out_proj(a1, a2, w_out, x, mod_l, norm_w):
    b, s, d = x.shape
    half = a1.shape[-1]
    tm = min(512, s)
    return pl.pallas_call(
        _outproj_kernel,
        out_shape=jax.ShapeDtypeStruct((b, s, d), F32),
        grid=(b, s // tm),
        in_specs=[
            pl.BlockSpec((1, tm, half), lambda bi, i: (bi, i, 0)),
            pl.BlockSpec((1, tm, half), lambda bi, i: (bi, i, 0)),
            pl.BlockSpec((2 * half, d), lambda bi, i: (0, 0)),
            pl.BlockSpec((1, tm, d), lambda bi, i: (bi, i, 0)),
            pl.BlockSpec((1, 1, 3 * d), lambda bi, i: (bi, 0, 0)),
            pl.BlockSpec((1, d), lambda bi, i: (0, 0)),
        ],
        out_specs=pl.BlockSpec((1, tm, d), lambda bi, i: (bi, i, 0)),
        compiler_params=_cparams(("parallel", "parallel")),
        name="out_proj",
    )(a1, a2, w_out, x, mod_l.reshape(b, 1, 3 * d), norm_w.reshape(1, d))


def _lin_attn_chunk(q, k, v, g, st_scr, b_scr, g_scr, consts):
    c, dk = q.shape
    row, eye, tri, parents = consts
    b = _dot_exact(tri, g)
    b_scr[...] = b
    g_scr[pl.ds(8, c), :] = g
    b_last = b_scr[pl.ds(c - 1, 1), :]

    qb16 = q.astype(BF16)
    kb16 = k.astype(BF16)
    attn = jnp.where(eye, _dot_nt(qb16, kb16), 0.0)

    for s in (32, 16, 8, 4):
        pieces = [jnp.broadcast_to(b_scr[pl.ds(p * 2 * s + s - 1, 1), :], (2 * s, dk))
                  for p in range(c // (2 * s))]
        d = b - jnp.concatenate(pieces, axis=0)
        upper = (row & s) != 0
        qs = q * jnp.exp(jnp.where(upper, d, NEG))
        ks = k * jnp.exp(jnp.where(upper, NEG, -d))
        attn = attn + jnp.where(parents[s], _dot_nt(qs.astype(BF16), ks.astype(BF16)), 0.0)

    g_dn = g_scr[pl.ds(7, c), :]
    g_up = g_scr[pl.ds(9, c), :]
    r4 = row & 3
    dq = jnp.where(r4 == 2, g, jnp.where(r4 == 3, g + g_dn, NEG))
    dk_ = jnp.where(r4 == 0, g_up, jnp.where(r4 == 1, 0.0, NEG))
    attn = attn + jnp.where(parents[2], _dot_nt((q * jnp.exp(dq)).astype(BF16),
                                                (k * jnp.exp(dk_)).astype(BF16)), 0.0)

    odd = (row & 1) != 0
    q1 = jnp.where(odd, q * jnp.exp(g), 0.0)
    k1 = jnp.where(odd, 0.0, k)
    attn = attn + jnp.where(parents[1], _dot_nt(q1.astype(BF16), k1.astype(BF16)), 0.0)

    v16 = v.astype(BF16)
    st = st_scr[...]
    o = _dot(attn.astype(BF16), v16) + _dot_nt((q * jnp.exp(b)).astype(BF16), st.astype(BF16))
    e_last = jnp.exp(b_last)
    k_dec = (k * jnp.exp(b_last - b)).astype(BF16)
    st_scr[...] = st * e_last + _dot_tn(v16, k_dec)
    return o


def _lin_attn_consts(c, dk):
    row = lax.broadcasted_iota(jnp.int32, (c, dk), 0)
    ri = lax.broadcasted_iota(jnp.int32, (c, c), 0)
    ci = lax.broadcasted_iota(jnp.int32, (c, c), 1)
    eye = ri == ci
    tri = jnp.where(ri >= ci, 1.0, 0.0).astype(F32)
    parents = {s: (ri & -(2 * s)) == (ci & -(2 * s)) for s in (32, 16, 8, 4, 2, 1)}
    return row, eye, tri, parents


def _lin_attn_finish(o, gz, nw):
    var = jnp.mean(o * o, axis=-1, keepdims=True)
    return (o * lax.rsqrt(var + EPS) * nw) * _silu(gz)


LIN_HEADS_PER_STEP = 4


def _gla_kernel(q_ref, k_ref, v_ref, gz_ref, glr_ref, wup_ref, bup_ref, nw_ref, o_ref,
                st_scr, b_scr, g_scr):
    c = LIN_CHUNK
    dk, dv = GLA_DK, GLA_DV
    ts = q_ref.shape[1]
    hp = q_ref.shape[2] // dk

    @pl.when(pl.program_id(2) == 0)
    def _():
        st_scr[...] = jnp.zeros_like(st_scr)

    g_scr[...] = jnp.zeros_like(g_scr)
    consts = _lin_attn_consts(c, dk)
    nw = nw_ref[...]
    q_scale = dk ** -0.5

    def body(ci, carry):
        r0 = pl.multiple_of(ci * c, c)
        rows = pl.ds(r0, c)
        glr = glr_ref[0, rows, :]
        for hh in range(hp):
            ks = slice(hh * dk, (hh + 1) * dk)
            vs = slice(hh * dv, (hh + 1) * dv)
            q = q_ref[0, rows, ks] * q_scale
            z = _dot(glr, wup_ref[:, ks]) + bup_ref[:, ks]
            g = _log_sigmoid(z) * (1.0 / GLA_TAU)
            o = _lin_attn_chunk(q, k_ref[0, rows, ks], v_ref[0, rows, vs], g,
                                st_scr.at[hh], b_scr.at[hh], g_scr.at[hh], consts)
            out = _lin_attn_finish(o, gz_ref[0, rows, vs], nw)
            o_ref[0, rows, vs] = out.astype(o_ref.dtype)
        return carry

    lax.fori_loop(0, ts // c, body, 0)


def _hgrn_kernel(q_ref, f_ref, v_ref, gz_ref, lbl_ref, nw_ref, o_ref, st_scr, b_scr, g_scr,
                 *, layer):
    c = LIN_CHUNK
    dk, dv = HGRN_DK, HGRN_DV
    ts = q_ref.shape[1]
    hp = q_ref.shape[2] // dk

    @pl.when(pl.program_id(2) == 0)
    def _():
        st_scr[...] = jnp.zeros_like(st_scr)

    g_scr[...] = jnp.zeros_like(g_scr)
    consts = _lin_attn_consts(c, dk)
    nw = nw_ref[...]

    logits = lbl_ref[...]
    depth = logits.shape[0]
    mx = logits[0:1, :]
    for r in range(1, depth):
        mx = jnp.maximum(mx, logits[r:r + 1, :])
    ex = [jnp.exp(logits[r:r + 1, :] - mx) for r in range(depth)]
    den = ex[0]
    for r in range(1, depth):
        den = den + ex[r]
    sm = [e / den for e in ex]
    lb_all = sm[0]
    for r in range(1, layer + 1):
        lb_all = lb_all + sm[r]
    lb_all = lb_all - sm[0]
    log_lb_all = jnp.log(lb_all)
    log_1mlb_all = jnp.log1p(-lb_all)

    def body(ci, carry):
        r0 = pl.multiple_of(ci * c, c)
        rows = pl.ds(r0, c)
        for hh in range(hp):
            ks = slice(hh * dk, (hh + 1) * dk)
            vs = slice(hh * dv, (hh + 1) * dv)
            lb = lb_all[:, ks]
            z = f_ref[0, rows, ks]
            g = _logaddexp(log_lb_all[:, ks], log_1mlb_all[:, ks] + _log_sigmoid(z))
            k = (1.0 - lb) * (1.0 / (1.0 + jnp.exp(z)))
            o = _lin_attn_chunk(q_ref[0, rows, ks], k, v_ref[0, rows, vs], g,
                                st_scr.at[hh], b_scr.at[hh], g_scr.at[hh], consts)
            out = _lin_attn_finish(o, gz_ref[0, rows, vs], nw)
            o_ref[0, rows, vs] = out.astype(o_ref.dtype)
        return carry

    lax.fori_loop(0, ts // c, body, 0)


def _lin_scratch(hp, dv, dk):
    return [pltpu.VMEM((hp, dv, dk), F32),
            pltpu.VMEM((hp, LIN_CHUNK, dk), F32),
            pltpu.VMEM((hp, LIN_CHUNK + 16, dk), F32)]


def _gla_branch(proj, small, w_up, b_up, norm_w):
    b, s, _ = proj.shape
    ts = min(512, s)
    dk, dv, h = GLA_DK, GLA_DV, GLA_HEADS
    hp = LIN_HEADS_PER_STEP
    kw, vw = hp * dk, hp * dv
    w_up_pad = jnp.pad(w_up, ((0, 128 - GLA_RANK), (0, 0)))
    return pl.pallas_call(
        _gla_kernel,
        out_shape=jax.ShapeDtypeStruct((b, s, h * dv), BF16),
        grid=(b, h // hp, s // ts),
        in_specs=[
            pl.BlockSpec((1, ts, kw), lambda bi, hi, i: (bi, i, hi)),
            pl.BlockSpec((1, ts, kw), lambda bi, hi, i: (bi, i, h // hp + hi)),
            pl.BlockSpec((1, ts, vw), lambda bi, hi, i: (bi, i, h // hp + hi)),
            pl.BlockSpec((1, ts, vw), lambda bi, hi, i: (bi, i, 2 * (h // hp) + hi)),
            pl.BlockSpec((1, ts, 128), lambda bi, hi, i: (bi, i, 0)),
            pl.BlockSpec((128, kw), lambda bi, hi, i: (0, hi)),
            pl.BlockSpec((1, kw), lambda bi, hi, i: (0, hi)),
            pl.BlockSpec((1, dv), lambda bi, hi, i: (0, 0)),
        ],
        out_specs=pl.BlockSpec((1, ts, vw), lambda bi, hi, i: (bi, i, hi)),
        scratch_shapes=_lin_scratch(hp, dv, dk),
        compiler_params=_cparams(("parallel", "parallel", "arbitrary")),
        name="gla",
    )(proj, proj, proj, proj, small, w_up_pad, b_up.reshape(1, -1), norm_w.reshape(1, dv))


def _hgrn_branch(proj, lb_logits, norm_w, layer):
    b, s, _ = proj.shape
    ts = min(512, s)
    dk, dv, h = HGRN_DK, HGRN_DV, HGRN_HEADS
    hp = LIN_HEADS_PER_STEP
    kw, vw = hp * dk, hp * dv
    ng = h // hp
    depth = lb_logits.shape[0]
    return pl.pallas_call(
        functools.partial(_hgrn_kernel, layer=layer),
        out_shape=jax.ShapeDtypeStruct((b, s, h * dv), BF16),
        grid=(b, ng, s // ts),
        in_specs=[
            pl.BlockSpec((1, ts, kw), lambda bi, hi, i: (bi, i, hi)),
            pl.BlockSpec((1, ts, kw), lambda bi, hi, i: (bi, i, ng + hi)),
            pl.BlockSpec((1, ts, vw), lambda bi, hi, i: (bi, i, 2 * ng + hi)),
            pl.BlockSpec((1, ts, vw), lambda bi, hi, i: (bi, i, 3 * ng + hi)),
            pl.BlockSpec((depth, kw), lambda bi, hi, i: (0, hi)),
            pl.BlockSpec((1, dv), lambda bi, hi, i: (0, 0)),
        ],
        out_specs=pl.BlockSpec((1, ts, vw), lambda bi, hi, i: (bi, i, hi)),
        scratch_shapes=_lin_scratch(hp, dv, dk),
        compiler_params=_cparams(("parallel", "parallel", "arbitrary")),
        name="hgrn2",
    )(proj, proj, proj, proj, lb_logits, norm_w.reshape(1, dv))


def _ssd_kernel(z_ref, x_ref, bc_ref, dt_ref, cw_ref, cb_ref, dtb_ref, alog_ref, dexp_ref, nw_ref,
                hexp_ref, o_ref, stage_scr, xbc_scr, st_scr, y_scr):
    c = x_ref.shape[1]
    nch = SSM_CONV_CH
    width = SSM_WIDTH
    n = SSM_STATE
    pairs = SSM_HEADS // 2

    @pl.when(pl.program_id(1) == 0)
    def _():
        st_scr[...] = jnp.zeros_like(st_scr)
        stage_scr[pl.ds(0, 8), :] = jnp.zeros((8, nch), F32)

    stage_scr[pl.ds(8, c), 0:width] = x_ref[0]
    stage_scr[pl.ds(8, c), width:nch] = bc_ref[0]
    acc = cb_ref[...] + cw_ref[0:1, :] * stage_scr[pl.ds(5, c), :]
    for kk in range(1, SSM_CONV):
        acc = acc + cw_ref[kk:kk + 1, :] * stage_scr[pl.ds(5 + kk, c), :]
    stage_scr[pl.ds(0, 8), :] = stage_scr[pl.ds(c, 8), :]
    xbc_scr[...] = _silu(acc)

    dt = _softplus(dt_ref[0] + dtb_ref[...])
    a = -jnp.exp(alog_ref[...])
    da = dt * a
    ri = lax.broadcasted_iota(jnp.int32, (c, c), 0)
    ci = lax.broadcasted_iota(jnp.int32, (c, c), 1)
    causal = ri >= ci
    tri = jnp.where(causal, 1.0, 0.0).astype(F32)
    a_cs = _dot_exact(tri, da)
    a_cs_t = a_cs.T
    hexp = hexp_ref[...]
    dt_x = _dot_exact(dt, hexp)
    acs_x = _dot_exact(a_cs, hexp)
    a_last_x = acs_x[c - 1:c, :]
    lane = lax.broadcasted_iota(jnp.int32, (c, 2 * SSM_HEADDIM), 1)
    first = lane < SSM_HEADDIM

    for grp in range(SSM_GROUPS):
        bm = xbc_scr[:, width + grp * n: width + (grp + 1) * n]
        cm = xbc_scr[:, width + (SSM_GROUPS + grp) * n: width + (SSM_GROUPS + grp + 1) * n]
        bm16 = bm.astype(BF16)
        cm16 = cm.astype(BF16)
        cb = _dot_nt(cm16, bm16)
        for pp in range(pairs // SSM_GROUPS):
            p = grp * (pairs // SSM_GROUPS) + pp
            lo = p * 2 * SSM_HEADDIM
            hi = lo + 2 * SSM_HEADDIM
            xs = xbc_scr[:, lo:hi]
            xdt = xs * dt_x[:, lo:hi]
            acs = acs_x[:, lo:hi]
            y = jnp.zeros((c, 2 * SSM_HEADDIM), F32)
            for hh in range(2):
                h = 2 * p + hh
                col = jnp.broadcast_to(a_cs[:, h:h + 1], (c, c))
                rw = jnp.broadcast_to(a_cs_t[h:h + 1, :], (c, c))
                lmat = jnp.exp(jnp.where(causal, col - rw, NEG))
                keep = first if hh == 0 else jnp.logical_not(first)
                xh = jnp.where(keep, xdt, 0.0).astype(BF16)
                y = y + _dot((cb * lmat).astype(BF16), xh)
            st = st_scr[p]
            y = y + _dot(cm16, st.astype(BF16)) * jnp.exp(acs)
            decay = jnp.exp(a_last_x[:, lo:hi] - acs)
            st_scr[p] = st * jnp.exp(a_last_x[:, lo:hi]) + _dot_tn(bm16, (xdt * decay).astype(BF16))
            y_scr[:, lo:hi] = y + dexp_ref[:, lo:hi] * xs

    yz = y_scr[...] * _silu(z_ref[0])
    var = jnp.mean(yz * yz, axis=-1, keepdims=True)
    o_ref[0] = (yz * lax.rsqrt(var + EPS) * nw_ref[...]).astype(o_ref.dtype)


def _ssd_branch(proj, small, conv_w, conv_b, dt_bias, a_log, d_skip, norm_w):
    b, s, _ = proj.shape
    c = math.gcd(SSM_CHUNK, s)
    nch = SSM_CONV_CH
    pad16 = lambda v: jnp.pad(v.reshape(1, -1), ((0, 0), (0, 128 - SSM_HEADS)))
    hexp = np.zeros((128, SSM_WIDTH), np.float32)
    for h in range(SSM_HEADS):
        hexp[h, h * SSM_HEADDIM:(h + 1) * SSM_HEADDIM] = 1.0
    d_exp = jnp.repeat(d_skip, SSM_HEADDIM).reshape(1, SSM_WIDTH)
    return pl.pallas_call(
        _ssd_kernel,
        out_shape=jax.ShapeDtypeStruct((b, s, SSM_WIDTH), BF16),
        grid=(b, s // c),
        in_specs=[
            pl.BlockSpec((1, c, SSM_WIDTH), lambda bi, i: (bi, i, 4)),
            pl.BlockSpec((1, c, SSM_WIDTH), lambda bi, i: (bi, i, 5)),
            pl.BlockSpec((1, c, 512), lambda bi, i: (bi, i, 12)),
            pl.BlockSpec((1, c, 128), lambda bi, i: (bi, i, 0)),
            pl.BlockSpec((SSM_CONV, nch), lambda bi, i: (0, 0)),
            pl.BlockSpec((1, nch), lambda bi, i: (0, 0)),
            pl.BlockSpec((1, 128), lambda bi, i: (0, 0)),
            pl.BlockSpec((1, 128), lambda bi, i: (0, 0)),
            pl.BlockSpec((1, SSM_WIDTH), lambda bi, i: (0, 0)),
            pl.BlockSpec((1, SSM_WIDTH), lambda bi, i: (0, 0)),
            pl.BlockSpec((128, SSM_WIDTH), lambda bi, i: (0, 0)),
        ],
        out_specs=pl.BlockSpec((1, c, SSM_WIDTH), lambda bi, i: (bi, i, 0)),
        scratch_shapes=[
            pltpu.VMEM((c + 8, nch), F32),
            pltpu.VMEM((c, nch), F32),
            pltpu.VMEM((SSM_HEADS // 2, SSM_STATE, 2 * SSM_HEADDIM), F32),
            pltpu.VMEM((c, SSM_WIDTH), F32),
        ],
        compiler_params=_cparams(("parallel", "arbitrary")),
        name="ssd",
    )(proj, proj, proj, small, conv_w, conv_b.reshape(1, nch), pad16(dt_bias), pad16(a_log),
      d_exp, norm_w.reshape(1, SSM_WIDTH), jnp.asarray(hexp))


def _rope(x, cos, sin_signed):
    return x * cos + pltpu.roll(x, NSA_DH // 2, axis=1) * sin_signed


def _nsa_prep_kernel(q_ref, kc_ref, ks_ref, vs_ref, kw_ref, vw_ref, cos_ref, sin_ref,
                     qo_ref, kco_ref, kso_ref, vso_ref, kwo_ref, vwo_ref):
    cos = cos_ref[...]
    sin = sin_ref[...]
    dh = NSA_DH
    scale = dh ** -0.5
    for h in range(NSA_HEADS):
        sl = slice(h * dh, (h + 1) * dh)
        qo_ref[0, :, sl] = (_rope(q_ref[0, :, sl], cos, sin) * scale).astype(qo_ref.dtype)
    for g in range(NSA_GROUPS):
        sl = slice(g * dh, (g + 1) * dh)
        kco_ref[0, :, sl] = _rope(kc_ref[0, :, sl], cos, sin)
        kso_ref[0, :, sl] = _rope(ks_ref[0, :, sl], cos, sin).astype(kso_ref.dtype)
        kwo_ref[0, :, sl] = _rope(kw_ref[0, :, sl], cos, sin).astype(kwo_ref.dtype)
    tk = NSA_TILE
    for g in range(NSA_GROUPS):
        for r in range(q_ref.shape[1] // tk):
            rows = slice(r * tk, (r + 1) * tk)
            sl = slice(g * dh, (g + 1) * dh)
            vso_ref[0, g, r] = vs_ref[0, rows, sl].T.astype(vso_ref.dtype)
            vwo_ref[0, g, r] = vw_ref[0, rows, sl].T.astype(vwo_ref.dtype)


def _nsa_prep(proj):
    b, s, _ = proj.shape
    ts = min(512, s)
    tk = NSA_TILE
    half = NSA_DH // 2
    inv_freq = ROPE_THETA ** (-jnp.arange(half, dtype=F32) / half)
    ang = jnp.arange(s, dtype=F32)[:, None] * inv_freq[None, :]
    cos = jnp.cos(ang)
    sin = jnp.sin(ang)
    cos_full = jnp.concatenate([cos, cos], axis=-1)
    sin_signed = jnp.concatenate([-sin, sin], axis=-1)
    kvw = NSA_GROUPS * NSA_DH
    col = lambda idx: (lambda bi, i: (bi, i, idx))
    kv_spec = lambda idx: pl.BlockSpec((1, ts, kvw), col(idx))
    kv_out = pl.BlockSpec((1, ts, kvw), lambda bi, i: (bi, i, 0))
    vt_shape = jax.ShapeDtypeStruct((b, NSA_GROUPS, s // tk, NSA_DH, tk), BF16)
    vt_out = pl.BlockSpec((1, NSA_GROUPS, ts // tk, NSA_DH, tk), lambda bi, i: (bi, 0, i, 0, 0))
    return pl.pallas_call(
        _nsa_prep_kernel,
        out_shape=(jax.ShapeDtypeStruct((b, s, NSA_HEADS * NSA_DH), BF16),
                   jax.ShapeDtypeStruct((b, s, kvw), F32),
                   jax.ShapeDtypeStruct((b, s, kvw), BF16),
                   vt_shape,
                   jax.ShapeDtypeStruct((b, s, kvw), BF16),
                   vt_shape),
        grid=(b, s // ts),
        in_specs=[
            pl.BlockSpec((1, ts, NSA_HEADS * NSA_DH), col(3)),
            kv_spec(16), kv_spec(18), kv_spec(19), kv_spec(20), kv_spec(21),
            pl.BlockSpec((ts, NSA_DH), lambda bi, i: (i, 0)),
            pl.BlockSpec((ts, NSA_DH), lambda bi, i: (i, 0)),
        ],
        out_specs=(pl.BlockSpec((1, ts, NSA_HEADS * NSA_DH), lambda bi, i: (bi, i, 0)),
                   kv_out, kv_out, vt_out, kv_out, vt_out),
        compiler_params=_cparams(("parallel", "parallel")),
        name="nsa_prep",
    )(proj, proj, proj, proj, proj, proj, cos_full, sin_signed)


def _compress_kernel(t_ref, pe_ref, w1_ref, w2_ref, o_ref, sh_scr, *, transposed):
    n = t_ref.shape[1] // CMP_STRIDE
    dh = NSA_DH
    half = CMP_LEN // 2
    acc_a = jnp.zeros((n, CMP_HIDDEN), F32)
    acc_b = jnp.zeros((n, CMP_HIDDEN), F32)
    for l in range(half):
        t = t_ref[0, pl.ds(l, n, stride=CMP_STRIDE), :]
        acc_a = acc_a + _dot((t + pe_ref[l:l + 1, :]).astype(BF16), w1_ref[l * dh:(l + 1) * dh, :])
        acc_b = acc_b + _dot((t + pe_ref[half + l:half + l + 1, :]).astype(BF16),
                             w1_ref[(half + l) * dh:(half + l + 1) * dh, :])
    sh_scr[pl.ds(0, n), :] = acc_b
    sh_scr[pl.ds(n, 8), :] = jnp.zeros((8, CMP_HIDDEN), F32)
    hid = acc_a + sh_scr[pl.ds(1, n), :]
    out = _dot(_silu(hid).astype(BF16), w2_ref[...])
    if transposed:
        tk = NSA_TILE
        for r in range(n // tk):
            o_ref[0, 0, r] = out[r * tk:(r + 1) * tk, :].T.astype(o_ref.dtype)
    else:
        o_ref[0, 0] = out.astype(o_ref.dtype)


def _nsa_compress(src, col0, pe, w1, w2, transposed):
    b, s, _ = src.shape
    n = s // CMP_STRIDE
    base = col0 // NSA_DH
    tk = NSA_TILE
    if transposed:
        out_shape = jax.ShapeDtypeStruct((b, NSA_GROUPS, n // tk, NSA_DH, tk), BF16)
        out_spec = pl.BlockSpec((1, 1, n // tk, NSA_DH, tk), lambda bi, g: (bi, g, 0, 0, 0))
    else:
        out_shape = jax.ShapeDtypeStruct((b, NSA_GROUPS, n, NSA_DH), BF16)
        out_spec = pl.BlockSpec((1, 1, n, NSA_DH), lambda bi, g: (bi, g, 0, 0))
    return pl.pallas_call(
        functools.partial(_compress_kernel, transposed=transposed),
        out_shape=out_shape,
        grid=(b, NSA_GROUPS),
        in_specs=[
            pl.BlockSpec((1, s, NSA_DH), lambda bi, g: (bi, 0, base + g)),
            pl.BlockSpec((CMP_LEN, NSA_DH), lambda bi, g: (0, 0)),
            pl.BlockSpec((CMP_LEN * NSA_DH, CMP_HIDDEN), lambda bi, g: (0, 0)),
            pl.BlockSpec((CMP_HIDDEN, NSA_DH), lambda bi, g: (0, 0)),
        ],
        out_specs=out_spec,
        scratch_shapes=[pltpu.VMEM((n + 8, CMP_HIDDEN), F32)],
        compiler_params=_cparams(("parallel", "parallel")),
        name="nsa_compress",
    )(src, pe, w1.astype(BF16), w2.astype(BF16))


def _flash_update(s_tiles, vt_tiles, m_ref, l_ref, acc_ref):
    m_old = m_ref[...]
    m_new = m_old
    for s in s_tiles:
        m_new = jnp.maximum(m_new, jnp.max(s, axis=0, keepdims=True))
    alpha = jnp.exp(m_old - m_new)
    p_tiles = [jnp.exp(s - m_new) for s in s_tiles]
    l_new = alpha * l_ref[...]
    acc = alpha * acc_ref[...]
    for p, vt in zip(p_tiles, vt_tiles):
        l_new = l_new + jnp.sum(p, axis=0, keepdims=True)
        acc = acc + _dot(vt, p.astype(BF16))
    l_ref[...] = l_new
    acc_ref[...] = acc
    m_ref[...] = m_new
    return p_tiles, alpha


def _nsa_kernel(q_ref, kc_ref, vct_ref, ovt_ref, ks_ref, vst_ref, kw_ref, vwt_ref, gate_ref, nz_ref,
                o_ref, m_scr, l_scr, acc_scr, imp_scr, cap_scr, capd_scr, capl_scr):
    tq = q_ref.shape[1]
    tk = NSA_TILE
    dh = NSA_DH
    hpg = NSA_HPG
    lanes = hpg * tq
    ns = ovt_ref.shape[1]
    qi = pl.program_id(2)
    t0 = qi * tq

    qt = jnp.concatenate([q_ref[0, :, j * dh:(j + 1) * dh].astype(F32).T for j in range(hpg)],
                         axis=1).astype(BF16)
    rowi = lax.broadcasted_iota(jnp.int32, (tk, lanes), 0)
    qpos = lax.broadcasted_iota(jnp.int32, (tk, lanes), 1) & (tq - 1)
    capd_scr[...] = jnp.where(rowi <= qpos, FORCE, NEG)
    capl_scr[...] = jnp.where(rowi > qpos, FORCE, NEG)
    for br in range(3):
        m_scr[br] = jnp.full((1, lanes), NEG, F32)
        l_scr[br] = jnp.zeros((1, lanes), F32)
        acc_scr[br] = jnp.zeros((dh, lanes), F32)
    imp_scr[...] = jnp.zeros_like(imp_scr)

    def cmp_body(c, carry):
        n0 = pl.multiple_of(c * tk, tk)
        first_end = c * (tk * CMP_STRIDE) + (CMP_LEN - 1) - t0
        cap = jnp.where(rowi * CMP_STRIDE + first_end <= qpos, FORCE, NEG)
        s = jnp.minimum(_dot(kc_ref[0, 0, pl.ds(n0, tk), :], qt), cap)
        (p,), alpha = _flash_update([s], [vct_ref[0, 0, c]], m_scr.at[0], l_scr.at[0], acc_scr.at[0])
        p_hi = p.astype(BF16)
        p_lo = (p - p_hi.astype(F32)).astype(BF16)
        ov = ovt_ref[c]
        imp_scr[...] = alpha * imp_scr[...] + _dot(ov, p_hi) + _dot(ov, p_lo)
        return carry

    n_cmp_tiles = (t0 + tq - CMP_LEN) // (tk * CMP_STRIDE) + 1
    lax.fori_loop(0, n_cmp_tiles, cmp_body, 0)

    tl = t0 + (lax.broadcasted_iota(jnp.int32, (1, lanes), 1) & (tq - 1))
    inv_l = jnp.where(tl >= CMP_LEN - 1, 1.0 / l_scr[0], 0.0)
    o_c = acc_scr[0] * inv_l
    impn = imp_scr[...] * inv_l
    imp = impn[:, 0:tq]
    for j in range(1, hpg):
        imp = imp + impn[:, j * tq:(j + 1) * tq]

    blk = lax.broadcasted_iota(jnp.int32, (ns, tq), 0)
    blk_t = jnp.right_shift(t0 + lax.broadcasted_iota(jnp.int32, (ns, tq), 1), SLC_SHIFT)
    score = jnp.where(blk == 0, FORCE,
                      jnp.where(blk == blk_t, FORCE,
                                jnp.where(blk == blk_t - 1, FORCE,
                                          jnp.where(blk <= blk_t, imp, NEG))))
    blk_f = blk.astype(F32)
    cap_sel = jnp.full((ns, tq), NEG, F32)
    for _ in range(min(SLC_TOPK, ns)):
        mx = jnp.max(score, axis=0, keepdims=True)
        first = jnp.min(jnp.where(score == mx, blk_f, float(ns)), axis=0, keepdims=True)
        pick = blk_f == first
        cap_sel = jnp.where(pick, FORCE, cap_sel)
        score = jnp.where(pick, -jnp.inf, score)
    for j in range(hpg):
        cap_scr[:, j * tq:(j + 1) * tq] = cap_sel

    bpt = tk // SLC_BLOCK

    def slc_scores(kt):
        k0 = pl.multiple_of(kt * tk, tk)
        s = _dot(ks_ref[0, pl.ds(k0, tk), :], qt)
        caps = [jnp.broadcast_to(cap_scr[pl.ds(kt * bpt + r, 1), :], (SLC_BLOCK, lanes))
                for r in range(bpt)]
        return jnp.minimum(s, jnp.concatenate(caps, axis=0))

    slc_state = (m_scr.at[1], l_scr.at[1], acc_scr.at[1])
    nb = NSA_TILES_PER_STEP

    def slc_multi(i, carry):
        kts = [i * nb + r for r in range(nb)]
        _flash_update([slc_scores(kt) for kt in kts], [vst_ref[0, 0, kt] for kt in kts], *slc_state)
        return carry

    def slc_single(kt, carry):
        _flash_update([slc_scores(kt)], [vst_ref[0, 0, kt]], *slc_state)
        return carry

    n_multi = qi // nb
    lax.fori_loop(0, n_multi, slc_multi, 0)
    lax.fori_loop(n_multi * nb, qi, slc_single, 0)
    _flash_update([jnp.minimum(slc_scores(qi), capd_scr[...])], [vst_ref[0, 0, qi]], *slc_state)

    wt = WINDOW // tk
    win_state = (m_scr.at[2], l_scr.at[2], acc_scr.at[2])

    def win_scores(kt):
        k0 = pl.multiple_of(kt * tk, tk)
        return _dot(kw_ref[0, pl.ds(k0, tk), :], qt)

    @pl.when(qi >= wt)
    def _():
        s_tiles = [jnp.minimum(win_scores(qi - wt), capl_scr[...])]
        s_tiles += [win_scores(qi - r) for r in range(wt - 1, 0, -1)]
        s_tiles += [jnp.minimum(win_scores(qi), capd_scr[...])]
        _flash_update(s_tiles, [vwt_ref[0, 0, qi - r] for r in range(wt, -1, -1)], *win_state)

    @pl.when(qi < wt)
    def _():
        def win_single(kt, carry):
            _flash_update([win_scores(kt)], [vwt_ref[0, 0, kt]], *win_state)
            return carry

        lax.fori_loop(0, qi, win_single, 0)
        _flash_update([jnp.minimum(win_scores(qi), capd_scr[...])], [vwt_ref[0, 0, qi]], *win_state)

    o_s = acc_scr[1] * (1.0 / l_scr[1])
    o_w = acc_scr[2] * (1.0 / l_scr[2])
    sgt = _sigmoid(gate_ref[0]).T
    for j in range(hpg):
        ls = slice(j * tq, (j + 1) * tq)
        mix_t = (sgt[3 * j:3 * j + 1, :] * o_c[:, ls] + sgt[3 * j + 1:3 * j + 2, :] * o_s[:, ls]
                 + sgt[3 * j + 2:3 * j + 3, :] * o_w[:, ls])
        sl = slice(j * dh, (j + 1) * dh)
        o_ref[0, :, sl] = (mix_t.T * _silu(nz_ref[0, :, sl])).astype(o_ref.dtype)


def _nsa_attention(qr, kc, vct, ksr, vst, kwr, vwt, small, proj):
    b, s, _ = qr.shape
    tq = tk = NSA_TILE
    assert s % tq == 0 and WINDOW % tk == 0 and tk % SLC_BLOCK == 0
    ncp = kc.shape[2]
    assert ncp % tk == 0
    ns = s // SLC_BLOCK
    gw = NSA_HPG * NSA_DH
    lanes = NSA_HPG * tq
    cs = np.arange(ncp)[None, :] * CMP_STRIDE
    ss = np.arange(ns)[:, None] * SLC_BLOCK
    ovt = ((np.minimum(cs + CMP_LEN, ss + SLC_BLOCK) - np.maximum(cs, ss)) > 0).astype(np.float32)
    ovt[:, ncp - 1] = 0.0
    ovt = ovt.reshape(ns, ncp // tk, tk).transpose(1, 0, 2)
    kv_spec = pl.BlockSpec((1, s, NSA_DH), lambda bi, g, i: (bi, 0, g))
    vt_spec = pl.BlockSpec((1, 1, s // tk, NSA_DH, tk), lambda bi, g, i: (bi, g, 0, 0, 0))
    return pl.pallas_call(
        _nsa_kernel,
        out_shape=jax.ShapeDtypeStruct((b, s, NSA_HEADS * NSA_DH), BF16),
        grid=(b, NSA_GROUPS, s // tq),
        in_specs=[
            pl.BlockSpec((1, tq, gw), lambda bi, g, i: (bi, i, g)),
            pl.BlockSpec((1, 1, ncp, NSA_DH), lambda bi, g, i: (bi, g, 0, 0)),
            pl.BlockSpec((1, 1, ncp // tk, NSA_DH, tk), lambda bi, g, i: (bi, g, 0, 0, 0)),
            pl.BlockSpec((ncp // tk, ns, tk), lambda bi, g, i: (0, 0, 0)),
            kv_spec, vt_spec, kv_spec, vt_spec,
            pl.BlockSpec((1, tq, 128), lambda bi, g, i: (bi, i, 1 + g)),
            pl.BlockSpec((1, tq, gw), lambda bi, g, i: (bi, i, 11 + g)),
        ],
        out_specs=pl.BlockSpec((1, tq, gw), lambda bi, g, i: (bi, i, g)),
        scratch_shapes=[
            pltpu.VMEM((3, 1, lanes), F32),
            pltpu.VMEM((3, 1, lanes), F32),
            pltpu.VMEM((3, NSA_DH, lanes), F32),
            pltpu.VMEM((ns, lanes), F32),
            pltpu.VMEM((ns, lanes), F32),
            pltpu.VMEM((tk, lanes), F32),
            pltpu.VMEM((tk, lanes), F32),
        ],
        compiler_params=_cparams(("parallel", "parallel", "arbitrary")),
        name="nsa_attention",
    )(qr, kc, vct, jnp.asarray(ovt, dtype=BF16), ksr, vst, kwr, vwt, small, proj)


def _nsa_branch(proj, small, pe_k, w1_k, w2_k, pe_v, w1_v, w2_v):
    qr, kcr, ksr, vst, kwr, vwt = _nsa_prep(proj)
    kc = _nsa_compress(kcr, 0, pe_k, w1_k, w2_k, transposed=False)
    vct = _nsa_compress(proj, 4352, pe_v, w1_v, w2_v, transposed=True)
    return _nsa_attention(qr, kc, vct, ksr, vst, kwr, vwt, small, proj)


def _even_weights(w_in):
    main = jnp.concatenate([w_in[:, 0:2048], w_in[:, 2064:5648], w_in[:, 5672:6696]], axis=1)
    d = w_in.shape[0]
    gates = w_in[:, 5648:5672]
    per = NSA_HPG * 3
    blocks = [jnp.pad(w_in[:, 2048:2064], ((0, 0), (0, 128 - GLA_RANK)))]
    for g in range(NSA_GROUPS):
        blocks.append(jnp.pad(gates[:, g * per:(g + 1) * per], ((0, 0), (0, 128 - per))))
    small = jnp.concatenate(blocks, axis=1)
    return main.astype(BF16), small.astype(BF16)


def _odd_weights(w_in):
    main = w_in[:, 0:MAIN_COLS]
    small = jnp.pad(w_in[:, MAIN_COLS:MAIN_COLS + SSM_HEADS], ((0, 0), (0, 128 - SSM_HEADS)))
    return main.astype(BF16), small.astype(BF16)


def kernel(x, c, ada_w, ada_b, pre_norm_w, post_norm_w, even_w_in, even_w_out, gla_w_up, gla_b_up,
           gla_norm_w, nsa_pe_k, nsa_w1_k, nsa_w2_k, nsa_pe_v, nsa_w1_v, nsa_w2_v, odd_w_in,
           odd_w_out, hgrn_lb_logits, hgrn_norm_w, ssm_conv_w, ssm_conv_b, ssm_dt_bias, ssm_a_log,
           ssm_d, ssm_norm_w):
    depth = ada_w.shape[0]
    mod = _adaln_mod(c, ada_w, ada_b)
    for l in range(depth):
        if l % 2 == 0:
            e = l // 2
            w_main, w_small = _even_weights(even_w_in[e])
            proj, small = _in_proj(x, mod[l], pre_norm_w[l], w_main, w_small)
            o_a = _gla_branch(proj, small, gla_w_up[e], gla_b_up[e], gla_norm_w[e])
            o_b = _nsa_branch(proj, small, nsa_pe_k[e], nsa_w1_k[e], nsa_w2_k[e],
                              nsa_pe_v[e], nsa_w1_v[e], nsa_w2_v[e])
            w_out = even_w_out[e]
        else:
            o = l // 2
            w_main, w_small = _odd_weights(odd_w_in[o])
            proj, small = _in_proj(x, mod[l], pre_norm_w[l], w_main, w_small)
            o_a = _hgrn_branch(proj, hgrn_lb_logits, hgrn_norm_w[o], l)
            o_b = _ssd_branch(proj, small, ssm_conv_w[o], ssm_conv_b[o], ssm_dt_bias[o],
                              ssm_a_log[o], ssm_d[o], ssm_norm_w[o])
            w_out = odd_w_out[o]
        x = _out_proj(o_a, o_b, w_out.astype(BF16), x, mod[l], post_norm_w[l])
    return x
```

```python
import functools
import math

import jax
import jax.numpy as jnp
import numpy as np
from jax import lax
from jax.experimental import pallas as pl
from jax.experimental.pallas import tpu as pltpu

F32 = jnp.float32
BF16 = jnp.bfloat16
HIGHEST = lax.Precision.HIGHEST

D_MODEL = 2048
EPS = 1e-6
NEG = -1e30
FORCE = 1e30
ROPE_THETA = 10000.0

GLA_HEADS = 4
GLA_DK = 128
GLA_DV = 256
GLA_RANK = 16
GLA_TAU = 16.0

NSA_DH = 128
NSA_HEADS = 8
NSA_GROUPS = 2
NSA_HPG = 4
CMP_LEN = 32
CMP_STRIDE = 16
CMP_HIDDEN = 256
SLC_BLOCK = 64
SLC_SHIFT = 6
SLC_TOPK = 16
WINDOW = 512
NSA_TILE = 128
NSA_TILES_PER_STEP = 8
NSA_STREAMS = 2

HGRN_HEADS = 8
HGRN_DK = 128
HGRN_DV = 128

SSM_HEADDIM = 64
SSM_HEADS = 16
SSM_GROUPS = 2
SSM_STATE = 128
SSM_CONV = 4
SSM_CHUNK = 256
SSM_WIDTH = 1024
SSM_CONV_CH = SSM_WIDTH + 2 * SSM_GROUPS * SSM_STATE

LIN_CHUNK = 128
MAIN_COLS = 6656
VMEM_LIMIT = 56 * 1024 * 1024


def _cparams(sem):
    return pltpu.CompilerParams(dimension_semantics=sem, vmem_limit_bytes=VMEM_LIMIT)


def _dot(a, b):
    return jnp.dot(a, b, preferred_element_type=F32)


def _dot_nt(a, b):
    return lax.dot_general(a, b, (((1,), (1,)), ((), ())), preferred_element_type=F32)


def _dot_tn(a, b):
    return lax.dot_general(a, b, (((0,), (0,)), ((), ())), preferred_element_type=F32)


def _dot_exact(a, b):
    return jnp.dot(a, b, preferred_element_type=F32, precision=HIGHEST)


def _sigmoid(x):
    return 1.0 / (1.0 + jnp.exp(-x))


def _silu(x):
    return x * _sigmoid(x)


def _log1p_exp_neg_abs(x):
    return jnp.log(1.0 + jnp.exp(-jnp.abs(x)))


def _log_sigmoid(x):
    return jnp.minimum(x, 0.0) - _log1p_exp_neg_abs(x)


def _softplus(x):
    return jnp.maximum(x, 0.0) + _log1p_exp_neg_abs(x)


def _logaddexp(a, b):
    return jnp.maximum(a, b) + _log1p_exp_neg_abs(a - b)


def _mod_kernel(c_ref, w_ref, b_ref, o_ref):
    ca = _silu(c_ref[...])
    o_ref[0] = _dot(ca, w_ref[0]) + b_ref[0]


def _adaln_mod(c, ada_w, ada_b):
    depth, d, n3 = ada_w.shape
    b = c.shape[0]
    rows = 8
    c_pad = jnp.pad(c, ((0, rows - b), (0, 0)))
    tn = 768
    out = pl.pallas_call(
        _mod_kernel,
        out_shape=jax.ShapeDtypeStruct((depth, rows, n3), F32),
        grid=(depth, n3 // tn),
        in_specs=[
            pl.BlockSpec((rows, d), lambda l, j: (0, 0)),
            pl.BlockSpec((1, d, tn), lambda l, j: (l, 0, j)),
            pl.BlockSpec((1, 1, tn), lambda l, j: (l, 0, j)),
        ],
        out_specs=pl.BlockSpec((1, rows, tn), lambda l, j: (l, 0, j)),
        compiler_params=_cparams(("parallel", "parallel")),
        name="adaln_mod",
    )(c_pad, ada_w, ada_b.reshape(depth, 1, n3))
    return out[:, :b]


def _inproj_kernel(x_ref, mod_ref, nw_ref, w_ref, ws_ref, o_ref, os_ref, h_scr):
    d = x_ref.shape[-1]

    @pl.when(pl.program_id(2) == 0)
    def _():
        x = x_ref[0]
        var = jnp.mean(x * x, axis=-1, keepdims=True)
        y = x * lax.rsqrt(var + EPS) * nw_ref[...]
        shift = mod_ref[0, :, 0:d]
        scale = mod_ref[0, :, d:2 * d]
        hb = (y * (1.0 + scale) + shift).astype(BF16)
        h_scr[...] = hb
        os_ref[0] = _dot(hb, ws_ref[...])

    o_ref[0] = _dot(h_scr[...], w_ref[...])


def _in_proj(x, mod_l, norm_w, w_main, w_small):
    b, s, d = x.shape
    n = w_main.shape[1]
    ns = w_small.shape[1]
    tm = min(1024, s)
    tn = 512
    return pl.pallas_call(
        _inproj_kernel,
        out_shape=(jax.ShapeDtypeStruct((b, s, n), F32),
                   jax.ShapeDtypeStruct((b, s, ns), F32)),
        grid=(b, s // tm, n // tn),
        in_specs=[
            pl.BlockSpec((1, tm, d), lambda bi, i, j: (bi, i, 0)),
            pl.BlockSpec((1, 1, 3 * d), lambda bi, i, j: (bi, 0, 0)),
            pl.BlockSpec((1, d), lambda bi, i, j: (0, 0)),
            pl.BlockSpec((d, tn), lambda bi, i, j: (0, j)),
            pl.BlockSpec((d, ns), lambda bi, i, j: (0, 0)),
        ],
        out_specs=(pl.BlockSpec((1, tm, tn), lambda bi, i, j: (bi, i, j)),
                   pl.BlockSpec((1, tm, ns), lambda bi, i, j: (bi, i, 0))),
        scratch_shapes=[pltpu.VMEM((tm, d), BF16)],
        compiler_params=_cparams(("parallel", "parallel", "arbitrary")),
        name="in_proj",
    )(x, mod_l.reshape(b, 1, 3 * d), norm_w.reshape(1, d), w_main, w_small)


def _outproj_kernel(a1_ref, a2_ref, w_ref, x_ref, mod_ref, nw_ref, o_ref):
    d = x_ref.shape[-1]
    half = a1_ref.shape[-1]
    y = _dot(a1_ref[0], w_ref[0:half, :]) + _dot(a2_ref[0], w_ref[half:2 * half, :])
    var = jnp.mean(y * y, axis=-1, keepdims=True)
    yn = y * lax.rsqrt(var + EPS) * nw_ref[...]
    gate = mod_ref[0, :, 2 * d:3 * d]
    o_ref[0] = x_ref[0] + gate * yn


def _out_proj(a1, a2, w_out, x, mod_l, norm_w):
    b, s, d = x.shape
    half = a1.shape[-1]
    tm = min(512, s)
    return pl.pallas_call(
        _outproj_kernel,
        out_shape=jax.ShapeDtypeStruct((b, s, d), F32),
        grid=(b, s // tm),
        in_specs=[
            pl.BlockSpec((1, tm, half), lambda bi, i: (bi, i, 0)),
            pl.BlockSpec((1, tm, half), lambda bi, i: (bi, i, 0)),
            pl.BlockSpec((2 * half, d), lambda bi, i: (0, 0)),
            pl.BlockSpec((1, tm, d), lambda bi, i: (bi, i, 0)),
            pl.BlockSpec((1, 1, 3 * d), lambda bi, i: (bi, 0, 0)),
            pl.BlockSpec((1, d), lambda bi, i: (0, 0)),
        ],
        out_specs=pl.BlockSpec((1, tm, d), lambda bi, i: (bi, i, 0)),
        compiler_params=_cparams(("parallel", "parallel")),
        name="out_proj",
    )(a1, a2, w_out, x, mod_l.reshape(b, 1, 3 * d), norm_w.reshape(1, d))


def _lin_attn_chunk(q, k, v, g, st_scr, b_scr, g_scr, consts):
    c, dk = q.shape
    row, eye, tri, pairs = consts
    b = _dot_exact(tri, g)
    b_scr[...] = b
    g_scr[pl.ds(8, c), :] = g
    b_last = b_scr[pl.ds(c - 1, 1), :]

    attn = jnp.where(eye, _dot_nt(q.astype(BF16), k.astype(BF16)), 0.0)

    def level(s, expo):
        upper = (row & s) != 0
        x = (jnp.where(upper, q, k) * jnp.exp(expo)).astype(BF16)
        return jnp.where(pairs[s], _dot_nt(x, x), 0.0)

    for s in [c >> i for i in range(1, c.bit_length() - 2)]:
        pieces = [jnp.broadcast_to(b_scr[pl.ds(p * 2 * s + s - 1, 1), :], (2 * s, dk))
                  for p in range(c // (2 * s))]
        d = b - jnp.concatenate(pieces, axis=0)
        attn = attn + level(s, jnp.where((row & s) != 0, d, -d))

    g_dn = g_scr[pl.ds(7, c), :]
    g_up = g_scr[pl.ds(9, c), :]
    r4 = row & 3
    attn = attn + level(2, jnp.where(r4 == 2, g, jnp.where(r4 == 3, g + g_dn,
                                                           jnp.where(r4 == 0, g_up, 0.0))))
    attn = attn + level(1, jnp.where((row & 1) != 0, g, 0.0))

    v16 = v.astype(BF16)
    st = st_scr[...]
    o = _dot(attn.astype(BF16), v16) + _dot_nt((q * jnp.exp(b)).astype(BF16), st.astype(BF16))
    e_last = jnp.exp(b_last)
    k_dec = (k * jnp.exp(b_last - b)).astype(BF16)
    st_scr[...] = st * e_last + _dot_tn(v16, k_dec)
    return o


def _lin_attn_consts(c, dk):
    row = lax.broadcasted_iota(jnp.int32, (c, dk), 0)
    ri = lax.broadcasted_iota(jnp.int32, (c, c), 0)
    ci = lax.broadcasted_iota(jnp.int32, (c, c), 1)
    eye = ri == ci
    tri = jnp.where(ri >= ci, 1.0, 0.0).astype(F32)
    pairs = {s: jnp.where((ri & -(2 * s)) == (ci & -(2 * s)), (ri & s) - (ci & s), 0) == s
             for s in [c >> i for i in range(1, c.bit_length())]}
    return row, eye, tri, pairs


def _lin_attn_finish(o, gz, nw):
    var = jnp.mean(o * o, axis=-1, keepdims=True)
    return (o * lax.rsqrt(var + EPS) * nw) * _silu(gz)


LIN_HEADS_PER_STEP = 4


def _gla_kernel(q_ref, k_ref, v_ref, gz_ref, glr_ref, wup_ref, bup_ref, nw_ref, o_ref,
                st_scr, b_scr, g_scr):
    c = LIN_CHUNK
    dk, dv = GLA_DK, GLA_DV
    nb, ts = q_ref.shape[0], q_ref.shape[1]
    hp = q_ref.shape[2] // dk

    @pl.when(pl.program_id(1) == 0)
    def _():
        st_scr[...] = jnp.zeros_like(st_scr)

    g_scr[...] = jnp.zeros_like(g_scr)
    consts = _lin_attn_consts(c, dk)
    nw = nw_ref[...]
    q_scale = dk ** -0.5

    def body(ci, carry):
        r0 = pl.multiple_of(ci * c, c)
        rows = pl.ds(r0, c)
        for bb in range(nb):
            glr = glr_ref[bb, rows, :]
            for hh in range(hp):
                ks = slice(hh * dk, (hh + 1) * dk)
                vs = slice(hh * dv, (hh + 1) * dv)
                ch = bb * hp + hh
                q = q_ref[bb, rows, ks] * q_scale
                z = _dot(glr, wup_ref[:, ks]) + bup_ref[:, ks]
                g = _log_sigmoid(z) * (1.0 / GLA_TAU)
                o = _lin_attn_chunk(q, k_ref[bb, rows, ks], v_ref[bb, rows, vs], g,
                                    st_scr.at[ch], b_scr.at[ch], g_scr.at[ch], consts)
                out = _lin_attn_finish(o, gz_ref[bb, rows, vs], nw)
                o_ref[bb, rows, vs] = out.astype(o_ref.dtype)
        return carry

    lax.fori_loop(0, ts // c, body, 0)


def _hgrn_kernel(q_ref, f_ref, v_ref, gz_ref, lbl_ref, nw_ref, o_ref, st_scr, b_scr, g_scr,
                 *, layer):
    c = LIN_CHUNK
    dk, dv = HGRN_DK, HGRN_DV
    nb, ts = q_ref.shape[0], q_ref.shape[1]
    hp = q_ref.shape[2] // dk

    @pl.when(pl.program_id(1) == 0)
    def _():
        st_scr[...] = jnp.zeros_like(st_scr)

    g_scr[...] = jnp.zeros_like(g_scr)
    consts = _lin_attn_consts(c, dk)
    nw = nw_ref[...]

    logits = lbl_ref[...]
    depth = logits.shape[0]
    mx = logits[0:1, :]
    for r in range(1, depth):
        mx = jnp.maximum(mx, logits[r:r + 1, :])
    ex = [jnp.exp(logits[r:r + 1, :] - mx) for r in range(depth)]
    den = ex[0]
    for r in range(1, depth):
        den = den + ex[r]
    sm = [e / den for e in ex]
    lb_all = sm[0]
    for r in range(1, layer + 1):
        lb_all = lb_all + sm[r]
    lb_all = lb_all - sm[0]
    log_lb_all = jnp.log(lb_all)
    log_1mlb_all = jnp.log1p(-lb_all)

    def body(ci, carry):
        r0 = pl.multiple_of(ci * c, c)
        rows = pl.ds(r0, c)
        for bb in range(nb):
            for hh in range(hp):
                ks = slice(hh * dk, (hh + 1) * dk)
                vs = slice(hh * dv, (hh + 1) * dv)
                ch = bb * hp + hh
                lb = lb_all[:, ks]
                z = f_ref[bb, rows, ks]
                g = _logaddexp(log_lb_all[:, ks], log_1mlb_all[:, ks] + _log_sigmoid(z))
                k = (1.0 - lb) * (1.0 / (1.0 + jnp.exp(z)))
                o = _lin_attn_chunk(q_ref[bb, rows, ks], k, v_ref[bb, rows, vs], g,
                                    st_scr.at[ch], b_scr.at[ch], g_scr.at[ch], consts)
                out = _lin_attn_finish(o, gz_ref[bb, rows, vs], nw)
                o_ref[bb, rows, vs] = out.astype(o_ref.dtype)
        return carry

    lax.fori_loop(0, ts // c, body, 0)


def _lin_scratch(chains, dv, dk):
    return [pltpu.VMEM((chains, dv, dk), F32),
            pltpu.VMEM((chains, LIN_CHUNK, dk), F32),
            pltpu.VMEM((chains, LIN_CHUNK + 16, dk), F32)]


def _gla_branch(proj, small, w_up, b_up, norm_w):
    b, s, _ = proj.shape
    ts = min(512, s)
    dk, dv, h = GLA_DK, GLA_DV, GLA_HEADS
    hp = LIN_HEADS_PER_STEP
    kw, vw = hp * dk, hp * dv
    w_up_pad = jnp.pad(w_up, ((0, 128 - GLA_RANK), (0, 0)))
    return pl.pallas_call(
        _gla_kernel,
        out_shape=jax.ShapeDtypeStruct((b, s, h * dv), BF16),
        grid=(h // hp, s // ts),
        in_specs=[
            pl.BlockSpec((b, ts, kw), lambda hi, i: (0, i, hi)),
            pl.BlockSpec((b, ts, kw), lambda hi, i: (0, i, h // hp + hi)),
            pl.BlockSpec((b, ts, vw), lambda hi, i: (0, i, h // hp + hi)),
            pl.BlockSpec((b, ts, vw), lambda hi, i: (0, i, 2 * (h // hp) + hi)),
            pl.BlockSpec((b, ts, 128), lambda hi, i: (0, i, 0)),
            pl.BlockSpec((128, kw), lambda hi, i: (0, hi)),
            pl.BlockSpec((1, kw), lambda hi, i: (0, hi)),
            pl.BlockSpec((1, dv), lambda hi, i: (0, 0)),
        ],
        out_specs=pl.BlockSpec((b, ts, vw), lambda hi, i: (0, i, hi)),
        scratch_shapes=_lin_scratch(b * hp, dv, dk),
        compiler_params=_cparams(("parallel", "arbitrary")),
        name="gla",
    )(proj, proj, proj, proj, small, w_up_pad, b_up.reshape(1, -1), norm_w.reshape(1, dv))


def _hgrn_branch(proj, lb_logits, norm_w, layer):
    b, s, _ = proj.shape
    ts = min(512, s)
    dk, dv, h = HGRN_DK, HGRN_DV, HGRN_HEADS
    hp = LIN_HEADS_PER_STEP
    kw, vw = hp * dk, hp * dv
    ng = h // hp
    depth = lb_logits.shape[0]
    return pl.pallas_call(
        functools.partial(_hgrn_kernel, layer=layer),
        out_shape=jax.ShapeDtypeStruct((b, s, h * dv), BF16),
        grid=(ng, s // ts),
        in_specs=[
            pl.BlockSpec((b, ts, kw), lambda hi, i: (0, i, hi)),
            pl.BlockSpec((b, ts, kw), lambda hi, i: (0, i, ng + hi)),
            pl.BlockSpec((b, ts, vw), lambda hi, i: (0, i, 2 * ng + hi)),
            pl.BlockSpec((b, ts, vw), lambda hi, i: (0, i, 3 * ng + hi)),
            pl.BlockSpec((depth, kw), lambda hi, i: (0, hi)),
            pl.BlockSpec((1, dv), lambda hi, i: (0, 0)),
        ],
        out_specs=pl.BlockSpec((b, ts, vw), lambda hi, i: (0, i, hi)),
        scratch_shapes=_lin_scratch(b * hp, dv, dk),
        compiler_params=_cparams(("parallel", "arbitrary")),
        name="hgrn2",
    )(proj, proj, proj, proj, lb_logits, norm_w.reshape(1, dv))


def _ssd_kernel(z_ref, x_ref, bc_ref, dt_ref, cw_ref, cb_ref, dtb_ref, alog_ref, dexp_ref, nw_ref,
                hexp_ref, o_ref, stage_scr, xbc_scr, st_scr, y_scr):
    c = x_ref.shape[1]
    nch = SSM_CONV_CH
    width = SSM_WIDTH
    n = SSM_STATE
    pairs = SSM_HEADS // 2

    @pl.when(pl.program_id(1) == 0)
    def _():
        st_scr[...] = jnp.zeros_like(st_scr)
        stage_scr[pl.ds(0, 8), :] = jnp.zeros((8, nch), F32)

    stage_scr[pl.ds(8, c), 0:width] = x_ref[0]
    stage_scr[pl.ds(8, c), width:nch] = bc_ref[0]
    acc = cb_ref[...] + cw_ref[0:1, :] * stage_scr[pl.ds(5, c), :]
    for kk in range(1, SSM_CONV):
        acc = acc + cw_ref[kk:kk + 1, :] * stage_scr[pl.ds(5 + kk, c), :]
    stage_scr[pl.ds(0, 8), :] = stage_scr[pl.ds(c, 8), :]
    xbc_scr[...] = _silu(acc)

    dt = _softplus(dt_ref[0] + dtb_ref[...])
    a = -jnp.exp(alog_ref[...])
    da = dt * a
    ri = lax.broadcasted_iota(jnp.int32, (c, c), 0)
    ci = lax.broadcasted_iota(jnp.int32, (c, c), 1)
    causal = ri >= ci
    tri = jnp.where(causal, 1.0, 0.0).astype(F32)
    a_cs = _dot_exact(tri, da)
    a_cs_t = a_cs.T
    hexp = hexp_ref[...]
    dt_x = _dot_exact(dt, hexp)
    acs_x = _dot_exact(a_cs, hexp)
    a_last_x = acs_x[c - 1:c, :]
    lane = lax.broadcasted_iota(jnp.int32, (c, 2 * SSM_HEADDIM), 1)
    first = lane < SSM_HEADDIM

    for grp in range(SSM_GROUPS):
        bm = xbc_scr[:, width + grp * n: width + (grp + 1) * n]
        cm = xbc_scr[:, width + (SSM_GROUPS + grp) * n: width + (SSM_GROUPS + grp + 1) * n]
        bm16 = bm.astype(BF16)
        cm16 = cm.astype(BF16)
        cb = _dot_nt(cm16, bm16)
        for pp in range(pairs // SSM_GROUPS):
            p = grp * (pairs // SSM_GROUPS) + pp
            lo = p * 2 * SSM_HEADDIM
            hi = lo + 2 * SSM_HEADDIM
            xs = xbc_scr[:, lo:hi]
            xdt = xs * dt_x[:, lo:hi]
            acs = acs_x[:, lo:hi]
            y = jnp.zeros((c, 2 * SSM_HEADDIM), F32)
            for hh in range(2):
                h = 2 * p + hh
                col = jnp.broadcast_to(a_cs[:, h:h + 1], (c, c))
                rw = jnp.broadcast_to(a_cs_t[h:h + 1, :], (c, c))
                lmat = jnp.exp(jnp.where(causal, col - rw, NEG))
                keep = first if hh == 0 else jnp.logical_not(first)
                xh = jnp.where(keep, xdt, 0.0).astype(BF16)
                y = y + _dot((cb * lmat).astype(BF16), xh)
            st = st_scr[p]
            y = y + _dot(cm16, st.astype(BF16)) * jnp.exp(acs)
            decay = jnp.exp(a_last_x[:, lo:hi] - acs)
            st_scr[p] = st * jnp.exp(a_last_x[:, lo:hi]) + _dot_tn(bm16, (xdt * decay).astype(BF16))
            y_scr[:, lo:hi] = y + dexp_ref[:, lo:hi] * xs

    yz = y_scr[...] * _silu(z_ref[0])
    var = jnp.mean(yz * yz, axis=-1, keepdims=True)
    o_ref[0] = (yz * lax.rsqrt(var + EPS) * nw_ref[...]).astype(o_ref.dtype)


def _ssd_branch(proj, small, conv_w, conv_b, dt_bias, a_log, d_skip, norm_w):
    b, s, _ = proj.shape
    c = math.gcd(SSM_CHUNK, s)
    nch = SSM_CONV_CH
    pad16 = lambda v: jnp.pad(v.reshape(1, -1), ((0, 0), (0, 128 - SSM_HEADS)))
    hexp = np.zeros((128, SSM_WIDTH), np.float32)
    for h in range(SSM_HEADS):
        hexp[h, h * SSM_HEADDIM:(h + 1) * SSM_HEADDIM] = 1.0
    d_exp = jnp.repeat(d_skip, SSM_HEADDIM).reshape(1, SSM_WIDTH)
    return pl.pallas_call(
        _ssd_kernel,
        out_shape=jax.ShapeDtypeStruct((b, s, SSM_WIDTH), BF16),
        grid=(b, s // c),
        in_specs=[
            pl.BlockSpec((1, c, SSM_WIDTH), lambda bi, i: (bi, i, 4)),
            pl.BlockSpec((1, c, SSM_WIDTH), lambda bi, i: (bi, i, 5)),
            pl.BlockSpec((1, c, 512), lambda bi, i: (bi, i, 12)),
            pl.BlockSpec((1, c, 128), lambda bi, i: (bi, i, 0)),
            pl.BlockSpec((SSM_CONV, nch), lambda bi, i: (0, 0)),
            pl.BlockSpec((1, nch), lambda bi, i: (0, 0)),
            pl.BlockSpec((1, 128), lambda bi, i: (0, 0)),
            pl.BlockSpec((1, 128), lambda bi, i: (0, 0)),
            pl.BlockSpec((1, SSM_WIDTH), lambda bi, i: (0, 0)),
            pl.BlockSpec((1, SSM_WIDTH), lambda bi, i: (0, 0)),
            pl.BlockSpec((128, SSM_WIDTH), lambda bi, i: (0, 0)),
        ],
        out_specs=pl.BlockSpec((1, c, SSM_WIDTH), lambda bi, i: (bi, i, 0)),
        scratch_shapes=[
            pltpu.VMEM((c + 8, nch), F32),
            pltpu.VMEM((c, nch), F32),
            pltpu.VMEM((SSM_HEADS // 2, SSM_STATE, 2 * SSM_HEADDIM), F32),
            pltpu.VMEM((c, SSM_WIDTH), F32),
        ],
        compiler_params=_cparams(("parallel", "arbitrary")),
        name="ssd",
    )(proj, proj, proj, small, conv_w, conv_b.reshape(1, nch), pad16(dt_bias), pad16(a_log),
      d_exp, norm_w.reshape(1, SSM_WIDTH), jnp.asarray(hexp))


def _rope(x, cos, sin_signed):
    return x * cos + pltpu.roll(x, NSA_DH // 2, axis=1) * sin_signed


def _nsa_prep_kernel(q_ref, kc_ref, ks_ref, vs_ref, kw_ref, vw_ref, cos_ref, sin_ref,
                     qo_ref, kco_ref, kso_ref, vso_ref, kwo_ref, vwo_ref):
    cos = cos_ref[...]
    sin = sin_ref[...]
    dh = NSA_DH
    scale = dh ** -0.5
    for h in range(NSA_HEADS):
        sl = slice(h * dh, (h + 1) * dh)
        qo_ref[0, :, sl] = (_rope(q_ref[0, :, sl], cos, sin) * scale).astype(qo_ref.dtype)
    for g in range(NSA_GROUPS):
        sl = slice(g * dh, (g + 1) * dh)
        kco_ref[0, :, sl] = _rope(kc_ref[0, :, sl], cos, sin)
        kso_ref[0, :, sl] = _rope(ks_ref[0, :, sl], cos, sin).astype(kso_ref.dtype)
        kwo_ref[0, :, sl] = _rope(kw_ref[0, :, sl], cos, sin).astype(kwo_ref.dtype)
    tk = NSA_TILE
    for g in range(NSA_GROUPS):
        for r in range(q_ref.shape[1] // tk):
            rows = slice(r * tk, (r + 1) * tk)
            sl = slice(g * dh, (g + 1) * dh)
            vso_ref[0, g, r] = vs_ref[0, rows, sl].T.astype(vso_ref.dtype)
            vwo_ref[0, g, r] = vw_ref[0, rows, sl].T.astype(vwo_ref.dtype)


def _nsa_prep(proj):
    b, s, _ = proj.shape
    ts = min(512, s)
    tk = NSA_TILE
    half = NSA_DH // 2
    inv_freq = ROPE_THETA ** (-jnp.arange(half, dtype=F32) / half)
    ang = jnp.arange(s, dtype=F32)[:, None] * inv_freq[None, :]
    cos = jnp.cos(ang)
    sin = jnp.sin(ang)
    cos_full = jnp.concatenate([cos, cos], axis=-1)
    sin_signed = jnp.concatenate([-sin, sin], axis=-1)
    kvw = NSA_GROUPS * NSA_DH
    col = lambda idx: (lambda bi, i: (bi, i, idx))
    kv_spec = lambda idx: pl.BlockSpec((1, ts, kvw), col(idx))
    kv_out = pl.BlockSpec((1, ts, kvw), lambda bi, i: (bi, i, 0))
    vt_shape = jax.ShapeDtypeStruct((b, NSA_GROUPS, s // tk, NSA_DH, tk), BF16)
    vt_out = pl.BlockSpec((1, NSA_GROUPS, ts // tk, NSA_DH, tk), lambda bi, i: (bi, 0, i, 0, 0))
    return pl.pallas_call(
        _nsa_prep_kernel,
        out_shape=(jax.ShapeDtypeStruct((b, s, NSA_HEADS * NSA_DH), BF16),
                   jax.ShapeDtypeStruct((b, s, kvw), F32),
                   jax.ShapeDtypeStruct((b, s, kvw), BF16),
                   vt_shape,
                   jax.ShapeDtypeStruct((b, s, kvw), BF16),
                   vt_shape),
        grid=(b, s // ts),
        in_specs=[
            pl.BlockSpec((1, ts, NSA_HEADS * NSA_DH), col(3)),
            kv_spec(16), kv_spec(18), kv_spec(19), kv_spec(20), kv_spec(21),
            pl.BlockSpec((ts, NSA_DH), lambda bi, i: (i, 0)),
            pl.BlockSpec((ts, NSA_DH), lambda bi, i: (i, 0)),
        ],
        out_specs=(pl.BlockSpec((1, ts, NSA_HEADS * NSA_DH), lambda bi, i: (bi, i, 0)),
                   kv_out, kv_out, vt_out, kv_out, vt_out),
        compiler_params=_cparams(("parallel", "parallel")),
        name="nsa_prep",
    )(proj, proj, proj, proj, proj, proj, cos_full, sin_signed)


def _compress_kernel(t_ref, pe_ref, w1_ref, w2_ref, o_ref, sh_scr, *, transposed):
    n = t_ref.shape[1] // CMP_STRIDE
    dh = NSA_DH
    half = CMP_LEN // 2
    acc_a = jnp.zeros((n, CMP_HIDDEN), F32)
    acc_b = jnp.zeros((n, CMP_HIDDEN), F32)
    for l in range(half):
        t = t_ref[0, pl.ds(l, n, stride=CMP_STRIDE), :]
        acc_a = acc_a + _dot((t + pe_ref[l:l + 1, :]).astype(BF16), w1_ref[l * dh:(l + 1) * dh, :])
        acc_b = acc_b + _dot((t + pe_ref[half + l:half + l + 1, :]).astype(BF16),
                             w1_ref[(half + l) * dh:(half + l + 1) * dh, :])
    sh_scr[pl.ds(0, n), :] = acc_b
    sh_scr[pl.ds(n, 8), :] = jnp.zeros((8, CMP_HIDDEN), F32)
    hid = acc_a + sh_scr[pl.ds(1, n), :]
    out = _dot(_silu(hid).astype(BF16), w2_ref[...])
    if transposed:
        tk = NSA_TILE
        for r in range(n // tk):
            o_ref[0, 0, r] = out[r * tk:(r + 1) * tk, :].T.astype(o_ref.dtype)
    else:
        o_ref[0, 0] = out.astype(o_ref.dtype)


def _nsa_compress(src, col0, pe, w1, w2, transposed):
    b, s, _ = src.shape
    n = s // CMP_STRIDE
    base = col0 // NSA_DH
    tk = NSA_TILE
    if transposed:
        out_shape = jax.ShapeDtypeStruct((b, NSA_GROUPS, n // tk, NSA_DH, tk), BF16)
        out_spec = pl.BlockSpec((1, 1, n // tk, NSA_DH, tk), lambda bi, g: (bi, g, 0, 0, 0))
    else:
        out_shape = jax.ShapeDtypeStruct((b, NSA_GROUPS, n, NSA_DH), BF16)
        out_spec = pl.BlockSpec((1, 1, n, NSA_DH), lambda bi, g: (bi, g, 0, 0))
    return pl.pallas_call(
        functools.partial(_compress_kernel, transposed=transposed),
        out_shape=out_shape,
        grid=(b, NSA_GROUPS),
        in_specs=[
            pl.BlockSpec((1, s, NSA_DH), lambda bi, g: (bi, 0, base + g)),
            pl.BlockSpec((CMP_LEN, NSA_DH), lambda bi, g: (0, 0)),
            pl.BlockSpec((CMP_LEN * NSA_DH, CMP_HIDDEN), lambda bi, g: (0, 0)),
            pl.BlockSpec((CMP_HIDDEN, NSA_DH), lambda bi, g: (0, 0)),
        ],
        out_specs=out_spec,
        scratch_shapes=[pltpu.VMEM((n + 8, CMP_HIDDEN), F32)],
        compiler_params=_cparams(("parallel", "parallel")),
        name="nsa_compress",
    )(src, pe, w1.astype(BF16), w2.astype(BF16))


def _flash_update(s_tiles, vt_tiles, m_ref, l_ref, acc_ref):
    m_old = m_ref[...]
    m_new = m_old
    for s in s_tiles:
        m_new = jnp.maximum(m_new, jnp.max(s, axis=0, keepdims=True))
    alpha = jnp.exp(m_old - m_new)
    p_tiles = [jnp.exp(s - m_new) for s in s_tiles]
    l_new = alpha * l_ref[...]
    for p in p_tiles:
        l_new = l_new + jnp.sum(p, axis=0, keepdims=True)
    l_ref[...] = l_new
    acc_ref[...] = alpha * acc_ref[...] + _pv(vt_tiles, p_tiles)
    m_ref[...] = m_new
    return p_tiles, alpha


def _pv(vt_tiles, p_tiles):
    vt = jnp.concatenate(vt_tiles, axis=1)
    p = jnp.concatenate([p.astype(BF16) for p in p_tiles], axis=0)
    return _dot(vt, p)


def _softmax_tiles(s_tiles, vt_tiles):
    m = jnp.max(s_tiles[0], axis=0, keepdims=True)
    for s in s_tiles[1:]:
        m = jnp.maximum(m, jnp.max(s, axis=0, keepdims=True))
    p_tiles = [jnp.exp(s - m) for s in s_tiles]
    l = jnp.sum(p_tiles[0], axis=0, keepdims=True)
    for p in p_tiles[1:]:
        l = l + jnp.sum(p, axis=0, keepdims=True)
    return p_tiles, l, _pv(vt_tiles, p_tiles)


def _nsa_kernel(q_ref, kc_ref, vct_ref, ovt_ref, ks_ref, vst_ref, kw_ref, vwt_ref, gate_ref, nz_ref,
                o_ref, m_scr, l_scr, acc_scr, cap_scr, capd_scr, capl_scr):
    tq = q_ref.shape[1]
    tk = NSA_TILE
    dh = NSA_DH
    hpg = NSA_HPG
    lanes = hpg * tq
    ns = ovt_ref.shape[1]
    qi = pl.program_id(2)
    t0 = qi * tq

    qt = jnp.concatenate([q_ref[0, :, j * dh:(j + 1) * dh].astype(F32).T for j in range(hpg)],
                         axis=1).astype(BF16)
    rowi = lax.broadcasted_iota(jnp.int32, (tk, lanes), 0)
    qpos = lax.broadcasted_iota(jnp.int32, (tk, lanes), 1) & (tq - 1)
    capd_scr[...] = jnp.where(rowi <= qpos, FORCE, NEG)
    capl_scr[...] = jnp.where(rowi > qpos, FORCE, NEG)

    s_tiles = []
    n_ct = kc_ref.shape[2] // tk
    for c in range(n_ct):
        first_end = c * (tk * CMP_STRIDE) + (CMP_LEN - 1) - t0
        cap = jnp.where(rowi * CMP_STRIDE + first_end <= qpos, FORCE, NEG)
        s_tiles.append(jnp.minimum(_dot(kc_ref[0, 0, c * tk:(c + 1) * tk, :], qt), cap))
    p_tiles, l_c, acc_c = _softmax_tiles(s_tiles, [vct_ref[0, 0, c] for c in range(n_ct)])
    imp_un = jnp.zeros((ns, lanes), F32)
    for c, p in enumerate(p_tiles):
        p_hi = p.astype(BF16)
        p_lo = (p - p_hi.astype(F32)).astype(BF16)
        imp_un = imp_un + _dot(ovt_ref[c], p_hi) + _dot(ovt_ref[c], p_lo)
    tl = t0 + (lax.broadcasted_iota(jnp.int32, (1, lanes), 1) & (tq - 1))
    inv_l = jnp.where(tl >= CMP_LEN - 1, 1.0 / l_c, 0.0)
    acc_scr[0] = acc_c * inv_l
    impn = imp_un * inv_l
    imp = impn[:, 0:tq]
    for j in range(1, hpg):
        imp = imp + impn[:, j * tq:(j + 1) * tq]

    blk = lax.broadcasted_iota(jnp.int32, (ns, tq), 0)
    blk_t = jnp.right_shift(t0 + lax.broadcasted_iota(jnp.int32, (ns, tq), 1), SLC_SHIFT)
    score = jnp.where(blk == 0, FORCE,
                      jnp.where(blk == blk_t, FORCE,
                                jnp.where(blk == blk_t - 1, FORCE,
                                          jnp.where(blk <= blk_t, imp, NEG))))
    blk_f = blk.astype(F32)
    cap_sel = jnp.full((ns, tq), NEG, F32)
    for _ in range(min(SLC_TOPK, ns)):
        mx = jnp.max(score, axis=0, keepdims=True)
        first = jnp.min(jnp.where(score == mx, blk_f, float(ns)), axis=0, keepdims=True)
        pick = blk_f == first
        cap_sel = jnp.where(pick, FORCE, cap_sel)
        score = jnp.where(pick, -jnp.inf, score)
    for j in range(hpg):
        cap_scr[:, j * tq:(j + 1) * tq] = cap_sel

    bpt = tk // SLC_BLOCK

    def slc_scores(kt):
        k0 = pl.multiple_of(kt * tk, tk)
        s = _dot(ks_ref[0, pl.ds(k0, tk), :], qt)
        caps = [jnp.broadcast_to(cap_scr[pl.ds(kt * bpt + r, 1), :], (SLC_BLOCK, lanes))
                for r in range(bpt)]
        return jnp.minimum(s, jnp.concatenate(caps, axis=0))

    wt = WINDOW // tk
    s_tiles, vt_tiles = [], []
    for r in range(wt, -1, -1):
        kt = qi - r
        ktc = jnp.maximum(kt, 0)
        s = _dot(kw_ref[0, pl.ds(pl.multiple_of(ktc * tk, tk), tk), :], qt)
        if r == wt:
            s = jnp.minimum(s, capl_scr[...])
        if r == 0:
            s = jnp.minimum(s, capd_scr[...])
        else:
            s = jnp.minimum(s, jnp.where(kt >= 0, FORCE, NEG))
        s_tiles.append(s)
        vt_tiles.append(vwt_ref[0, 0, ktc])
    _, l_w, acc_w = _softmax_tiles(s_tiles, vt_tiles)
    acc_scr[1] = acc_w * (1.0 / l_w)

    nst = NSA_STREAMS
    nb = NSA_TILES_PER_STEP
    per = nb // nst
    states = []
    for st in range(nst):
        m_scr[st] = jnp.full((1, lanes), NEG, F32)
        l_scr[st] = jnp.zeros((1, lanes), F32)
        acc_scr[2 + st] = jnp.zeros((dh, lanes), F32)
        states.append((m_scr.at[st], l_scr.at[st], acc_scr.at[2 + st]))
    last_tile = ks_ref.shape[1] // tk - 1

    def slc_full(i, carry):
        kts = [[i * nb + st * per + r for r in range(per)] for st in range(nst)]
        scores = [[slc_scores(kt) for kt in kts[st]] for st in range(nst)]
        for st in range(nst):
            _flash_update(scores[st], [vst_ref[0, 0, kt] for kt in kts[st]], *states[st])
        return carry

    n_full = qi // nb
    lax.fori_loop(0, n_full, slc_full, 0)
    for st in range(nst):
        s_tiles, vt_tiles = [], []
        for r in range(per):
            kt = n_full * nb + st * per + r
            ktc = jnp.minimum(kt, last_tile)
            below = jnp.where(kt < qi, FORCE, NEG)
            upto = jnp.where(kt <= qi, FORCE, NEG)
            cap = jnp.minimum(jnp.maximum(capd_scr[...], below), upto)
            s_tiles.append(jnp.minimum(slc_scores(ktc), cap))
            vt_tiles.append(vst_ref[0, 0, ktc])
        _flash_update(s_tiles, vt_tiles, *states[st])

    m_all = m_scr[0]
    for st in range(1, nst):
        m_all = jnp.maximum(m_all, m_scr[st])
    l_all = jnp.zeros((1, lanes), F32)
    acc_all = jnp.zeros((dh, lanes), F32)
    for st in range(nst):
        w = jnp.exp(m_scr[st] - m_all)
        l_all = l_all + w * l_scr[st]
        acc_all = acc_all + w * acc_scr[2 + st]

    o_c = acc_scr[0]
    o_s = acc_all * (1.0 / l_all)
    o_w = acc_scr[1]
    sgt = _sigmoid(gate_ref[0]).T
    for j in range(hpg):
        ls = slice(j * tq, (j + 1) * tq)
        mix_t = (sgt[3 * j:3 * j + 1, :] * o_c[:, ls] + sgt[3 * j + 1:3 * j + 2, :] * o_s[:, ls]
                 + sgt[3 * j + 2:3 * j + 3, :] * o_w[:, ls])
        sl = slice(j * dh, (j + 1) * dh)
        o_ref[0, :, sl] = (mix_t.T * _silu(nz_ref[0, :, sl])).astype(o_ref.dtype)


def _nsa_attention(qr, kc, vct, ksr, vst, kwr, vwt, small, proj):
    b, s, _ = qr.shape
    tq = tk = NSA_TILE
    assert s % tq == 0 and WINDOW % tk == 0 and tk % SLC_BLOCK == 0
    ncp = kc.shape[2]
    assert ncp % tk == 0
    ns = s // SLC_BLOCK
    gw = NSA_HPG * NSA_DH
    lanes = NSA_HPG * tq
    cs = np.arange(ncp)[None, :] * CMP_STRIDE
    ss = np.arange(ns)[:, None] * SLC_BLOCK
    ovt = ((np.minimum(cs + CMP_LEN, ss + SLC_BLOCK) - np.maximum(cs, ss)) > 0).astype(np.float32)
    ovt[:, ncp - 1] = 0.0
    ovt = ovt.reshape(ns, ncp // tk, tk).transpose(1, 0, 2)
    kv_spec = pl.BlockSpec((1, s, NSA_DH), lambda bi, g, i: (bi, 0, g))
    vt_spec = pl.BlockSpec((1, 1, s // tk, NSA_DH, tk), lambda bi, g, i: (bi, g, 0, 0, 0))
    return pl.pallas_call(
        _nsa_kernel,
        out_shape=jax.ShapeDtypeStruct((b, s, NSA_HEADS * NSA_DH), BF16),
        grid=(b, NSA_GROUPS, s // tq),
        in_specs=[
            pl.BlockSpec((1, tq, gw), lambda bi, g, i: (bi, i, g)),
            pl.BlockSpec((1, 1, ncp, NSA_DH), lambda bi, g, i: (bi, g, 0, 0)),
            pl.BlockSpec((1, 1, ncp // tk, NSA_DH, tk), lambda bi, g, i: (bi, g, 0, 0, 0)),
            pl.BlockSpec((ncp // tk, ns, tk), lambda bi, g, i: (0, 0, 0)),
            kv_spec, vt_spec, kv_spec, vt_spec,
            pl.BlockSpec((1, tq, 128), lambda bi, g, i: (bi, i, 1 + g)),
            pl.BlockSpec((1, tq, gw), lambda bi, g, i: (bi, i, 11 + g)),
        ],
        out_specs=pl.BlockSpec((1, tq, gw), lambda bi, g, i: (bi, i, g)),
        scratch_shapes=[
            pltpu.VMEM((NSA_STREAMS, 1, lanes), F32),
            pltpu.VMEM((NSA_STREAMS, 1, lanes), F32),
            pltpu.VMEM((2 + NSA_STREAMS, NSA_DH, lanes), F32),
            pltpu.VMEM((ns, lanes), F32),
            pltpu.VMEM((tk, lanes), F32),
            pltpu.VMEM((tk, lanes), F32),
        ],
        compiler_params=_cparams(("parallel", "parallel", "arbitrary")),
        name="nsa_attention",
    )(qr, kc, vct, jnp.asarray(ovt, dtype=BF16), ksr, vst, kwr, vwt, small, proj)


def _nsa_branch(proj, small, pe_k, w1_k, w2_k, pe_v, w1_v, w2_v):
    qr, kcr, ksr, vst, kwr, vwt = _nsa_prep(proj)
    kc = _nsa_compress(kcr, 0, pe_k, w1_k, w2_k, transposed=False)
    vct = _nsa_compress(proj, 4352, pe_v, w1_v, w2_v, transposed=True)
    return _nsa_attention(qr, kc, vct, ksr, vst, kwr, vwt, small, proj)


def _even_weights(w_in):
    main = jnp.concatenate([w_in[:, 0:2048], w_in[:, 2064:5648], w_in[:, 5672:6696]], axis=1)
    d = w_in.shape[0]
    gates = w_in[:, 5648:5672]
    per = NSA_HPG * 3
    blocks = [jnp.pad(w_in[:, 2048:2064], ((0, 0), (0, 128 - GLA_RANK)))]
    for g in range(NSA_GROUPS):
        blocks.append(jnp.pad(gates[:, g * per:(g + 1) * per], ((0, 0), (0, 128 - per))))
    small = jnp.concatenate(blocks, axis=1)
    return main.astype(BF16), small.astype(BF16)


def _odd_weights(w_in):
    main = w_in[:, 0:MAIN_COLS]
    small = jnp.pad(w_in[:, MAIN_COLS:MAIN_COLS + SSM_HEADS], ((0, 0), (0, 128 - SSM_HEADS)))
    return main.astype(BF16), small.astype(BF16)


def kernel(x, c, ada_w, ada_b, pre_norm_w, post_norm_w, even_w_in, even_w_out, gla_w_up, gla_b_up,
           gla_norm_w, nsa_pe_k, nsa_w1_k, nsa_w2_k, nsa_pe_v, nsa_w1_v, nsa_w2_v, odd_w_in,
           odd_w_out, hgrn_lb_logits, hgrn_norm_w, ssm_conv_w, ssm_conv_b, ssm_dt_bias, ssm_a_log,
           ssm_d, ssm_norm_w):
    depth = ada_w.shape[0]
    mod = _adaln_mod(c, ada_w, ada_b)
    for l in range(depth):
        if l % 2 == 0:
            e = l // 2
            w_main, w_small = _even_weights(even_w_in[e])
            proj, small = _in_proj(x, mod[l], pre_norm_w[l], w_main, w_small)
            o_a = _gla_branch(proj, small, gla_w_up[e], gla_b_up[e], gla_norm_w[e])
            o_b = _nsa_branch(proj, small, nsa_pe_k[e], nsa_w1_k[e], nsa_w2_k[e],
                              nsa_pe_v[e], nsa_w1_v[e], nsa_w2_v[e])
            w_out = even_w_out[e]
        else:
            o = l // 2
            w_main, w_small = _odd_weights(odd_w_in[o])
            proj, small = _in_proj(x, mod[l], pre_norm_w[l], w_main, w_small)
            o_a = _hgrn_branch(proj, hgrn_lb_logits, hgrn_norm_w[o], l)
            o_b = _ssd_branch(proj, small, ssm_conv_w[o], ssm_conv_b[o], ssm_dt_bias[o],
                              ssm_a_log[o], ssm_d[o], ssm_norm_w[o])
            w_out = odd_w_out[o]
        x = _out_proj(o_a, o_b, w_out.astype(BF16), x, mod[l], post_norm_w[l])
    return x
```

```python
import functools
import math

import jax
import jax.numpy as jnp
import numpy as np
from jax import lax
from jax.experimental import pallas as pl
from jax.experimental.pallas import tpu as pltpu

F32 = jnp.float32
BF16 = jnp.bfloat16

D_MODEL = 2048
EPS = 1e-6
NEG = -1e30
FORCE = 1e30
ROPE_THETA = 10000.0

GLA_HEADS = 4
GLA_DK = 128
GLA_DV = 256
GLA_RANK = 16
GLA_TAU = 16.0

NSA_DH = 128
NSA_HEADS = 8
NSA_GROUPS = 2
NSA_HPG = 4
CMP_LEN = 32
CMP_STRIDE = 16
CMP_HIDDEN = 256
SLC_BLOCK = 64
SLC_SHIFT = 6
SLC_TOPK = 16
WINDOW = 512
NSA_TILE = 128
NSA_TILES_PER_STEP = 4

HGRN_HEADS = 8
HGRN_DK = 128
HGRN_DV = 128

SSM_HEADDIM = 64
SSM_HEADS = 16
SSM_GROUPS = 2
SSM_STATE = 128
SSM_CONV = 4
SSM_CHUNK = 256
SSM_WIDTH = 1024
SSM_CONV_CH = SSM_WIDTH + 2 * SSM_GROUPS * SSM_STATE

LIN_CHUNK = 128
MAIN_COLS = 6656
VMEM_LIMIT = 56 * 1024 * 1024


def _cparams(sem):
    return pltpu.CompilerParams(dimension_semantics=sem, vmem_limit_bytes=VMEM_LIMIT)


def _dot(a, b):
    return jnp.dot(a, b, preferred_element_type=F32)


def _dot_nt(a, b):
    return lax.dot_general(a, b, (((1,), (1,)), ((), ())), preferred_element_type=F32)


def _dot_tn(a, b):
    return lax.dot_general(a, b, (((0,), (0,)), ((), ())), preferred_element_type=F32)


def _split3(x):
    hi = x.astype(BF16)
    r = x - hi.astype(F32)
    mid = r.astype(BF16)
    lo = (r - mid.astype(F32)).astype(BF16)
    return hi, mid, lo


def _sel_dot_left(sel, x):
    n = x.shape[1]
    y = _dot(sel.astype(BF16), jnp.concatenate(_split3(x), axis=1))
    return y[:, 0:n] + y[:, n:2 * n] + y[:, 2 * n:3 * n]


def _sel_dot_right(x, sel):
    m = x.shape[0]
    y = _dot(jnp.concatenate(_split3(x), axis=0), sel.astype(BF16))
    return y[0:m] + y[m:2 * m] + y[2 * m:3 * m]


def _sigmoid(x):
    return 1.0 / (1.0 + jnp.exp(-x))


def _silu(x):
    return x * _sigmoid(x)


def _log1p_exp_neg_abs(x):
    return jnp.log(1.0 + jnp.exp(-jnp.abs(x)))


def _log_sigmoid(x):
    return jnp.minimum(x, 0.0) - _log1p_exp_neg_abs(x)


def _softplus(x):
    return jnp.maximum(x, 0.0) + _log1p_exp_neg_abs(x)


def _logaddexp(a, b):
    return jnp.maximum(a, b) + _log1p_exp_neg_abs(a - b)


def _mod_kernel(c_ref, w_ref, b_ref, o_ref):
    ca = _silu(c_ref[...])
    o_ref[0] = _dot(ca, w_ref[0]) + b_ref[0]


def _adaln_mod(c, ada_w, ada_b):
    depth, d, n3 = ada_w.shape
    b = c.shape[0]
    rows = 8
    c_pad = jnp.pad(c, ((0, rows - b), (0, 0)))
    tn = 768
    out = pl.pallas_call(
        _mod_kernel,
        out_shape=jax.ShapeDtypeStruct((depth, rows, n3), F32),
        grid=(depth, n3 // tn),
        in_specs=[
            pl.BlockSpec((rows, d), lambda l, j: (0, 0)),
            pl.BlockSpec((1, d, tn), lambda l, j: (l, 0, j)),
            pl.BlockSpec((1, 1, tn), lambda l, j: (l, 0, j)),
        ],
        out_specs=pl.BlockSpec((1, rows, tn), lambda l, j: (l, 0, j)),
        compiler_params=_cparams(("parallel", "parallel")),
        name="adaln_mod",
    )(c_pad, ada_w, ada_b.reshape(depth, 1, n3))
    return out[:, :b]


def _inproj_kernel(x_ref, mod_ref, nw_ref, w_ref, ws_ref, o_ref, os_ref, h_scr):
    d = x_ref.shape[-1]

    @pl.when(pl.program_id(2) == 0)
    def _():
        x = x_ref[0]
        var = jnp.mean(x * x, axis=-1, keepdims=True)
        y = x * lax.rsqrt(var + EPS) * nw_ref[...]
        shift = mod_ref[0, :, 0:d]
        scale = mod_ref[0, :, d:2 * d]
        hb = (y * (1.0 + scale) + shift).astype(BF16)
        h_scr[...] = hb
        os_ref[0] = _dot(hb, ws_ref[...])

    o_ref[0] = _dot(h_scr[...], w_ref[...])


def _in_proj(x, mod_l, norm_w, w_main, w_small):
    b, s, d = x.shape
    n = w_main.shape[1]
    ns = w_small.shape[1]
    tm = min(1024, s)
    tn = 512
    return pl.pallas_call(
        _inproj_kernel,
        out_shape=(jax.ShapeDtypeStruct((b, s, n), F32),
                   jax.ShapeDtypeStruct((b, s, ns), F32)),
        grid=(b, s // tm, n // tn),
        in_specs=[
            pl.BlockSpec((1, tm, d), lambda bi, i, j: (bi, i, 0)),
            pl.BlockSpec((1, 1, 3 * d), lambda bi, i, j: (bi, 0, 0)),
            pl.BlockSpec((1, d), lambda bi, i, j: (0, 0)),
            pl.BlockSpec((d, tn), lambda bi, i, j: (0, j)),
            pl.BlockSpec((d, ns), lambda bi, i, j: (0, 0)),
        ],
        out_specs=(pl.BlockSpec((1, tm, tn), lambda bi, i, j: (bi, i, j)),
                   pl.BlockSpec((1, tm, ns), lambda bi, i, j: (bi, i, 0))),
        scratch_shapes=[pltpu.VMEM((tm, d), BF16)],
        compiler_params=_cparams(("parallel", "parallel", "arbitrary")),
        name="in_proj",
    )(x, mod_l.reshape(b, 1, 3 * d), norm_w.reshape(1, d), w_main, w_small)


def _outproj_kernel(a1_ref, a2_ref, w_ref, x_ref, mod_ref, nw_ref, o_ref):
    d = x_ref.shape[-1]
    half = a1_ref.shape[-1]
    y = _dot(a1_ref[0], w_ref[0:half, :]) + _dot(a2_ref[0], w_ref[half:2 * half, :])
    var = jnp.mean(y * y, axis=-1, keepdims=True)
    yn = y * lax.rsqrt(var + EPS) * nw_ref[...]
    gate = mod_ref[0, :, 2 * d:3 * d]
    o_ref[0] = x_ref[0] + gate * yn


def _out_proj(a1, a2, w_out, x, mod_l, norm_w):
    b, s, d = x.shape
    half = a1.shape[-1]
    tm = min(512, s)
    return pl.pallas_call(
        _outproj_kernel,
        out_shape=jax.ShapeDtypeStruct((b, s, d), F32),
        grid=(b, s // tm),
        in_specs=[
            pl.BlockSpec((1, tm, half), lambda bi, i: (bi, i, 0)),
            pl.BlockSpec((1, tm, half), lambda bi, i: (bi, i, 0)),
            pl.BlockSpec((2 * half, d), lambda bi, i: (0, 0)),
            pl.BlockSpec((1, tm, d), lambda bi, i: (bi, i, 0)),
            pl.BlockSpec((1, 1, 3 * d), lambda bi, i: (bi, 0, 0)),
            pl.BlockSpec((1, d), lambda bi, i: (0, 0)),
        ],
        out_specs=pl.BlockSpec((1, tm, d), lambda bi, i: (bi, i, 0)),
        compiler_params=_cparams(("parallel", "parallel")),
        name="out_proj",
    )(a1, a2, w_out, x, mod_l.reshape(b, 1, 3 * d), norm_w.reshape(1, d))


def _lin_attn_chunk(q, k, v, g, st_scr, b_scr, g_scr, consts):
    c, dk = q.shape
    row, eye, tri, pairs = consts
    b = _sel_dot_left(tri, g)
    b_scr[...] = b
    g_scr[pl.ds(8, c), :] = g
    b_last = b_scr[pl.ds(c - 1, 1), :]

    attn = jnp.where(eye, _dot_nt(q.astype(BF16), k.astype(BF16)), 0.0)

    def level(s, expo, attn):
        upper = (row & s) != 0
        x = (jnp.where(upper, q, k) * jnp.exp(expo)).astype(BF16)
        return jnp.where(pairs[s], _dot_nt(x, x), attn)

    for s in [c >> i for i in range(1, c.bit_length() - 2)]:
        pieces = [jnp.broadcast_to(b_scr[pl.ds(p * 2 * s + s - 1, 1), :], (2 * s, dk))
                  for p in range(c // (2 * s))]
        d = b - jnp.concatenate(pieces, axis=0)
        attn = level(s, jnp.where((row & s) != 0, d, -d), attn)

    g_dn = g_scr[pl.ds(7, c), :]
    g_up = g_scr[pl.ds(9, c), :]
    r4 = row & 3
    attn = level(2, jnp.where(r4 == 2, g, jnp.where(r4 == 3, g + g_dn, jnp.where(r4 == 0, g_up, 0.0))),
                 attn)
    attn = level(1, jnp.where((row & 1) != 0, g, 0.0), attn)

    v16 = v.astype(BF16)
    st = st_scr[...]
    o = _dot(attn.astype(BF16), v16) + _dot_nt((q * jnp.exp(b)).astype(BF16), st.astype(BF16))
    e_last = jnp.exp(b_last)
    k_dec = (k * jnp.exp(b_last - b)).astype(BF16)
    st_scr[...] = st * e_last + _dot_tn(v16, k_dec)
    return o


def _lin_attn_consts(c, dk):
    row = lax.broadcasted_iota(jnp.int32, (c, dk), 0)
    ri = lax.broadcasted_iota(jnp.int32, (c, c), 0)
    ci = lax.broadcasted_iota(jnp.int32, (c, c), 1)
    eye = ri == ci
    tri = jnp.where(ri >= ci, 1.0, 0.0).astype(BF16)
    pairs = {s: jnp.where((ri & -(2 * s)) == (ci & -(2 * s)), (ri & s) - (ci & s), 0) == s
             for s in [c >> i for i in range(1, c.bit_length())]}
    return row, eye, tri, pairs


def _lin_attn_finish(o, gz, nw):
    var = jnp.mean(o * o, axis=-1, keepdims=True)
    return (o * lax.rsqrt(var + EPS) * nw) * _silu(gz)


LIN_HEADS_PER_STEP = 4


def _gla_kernel(q_ref, k_ref, v_ref, gz_ref, glr_ref, wup_ref, bup_ref, nw_ref, o_ref,
                st_scr, b_scr, g_scr):
    c = LIN_CHUNK
    dk, dv = GLA_DK, GLA_DV
    nb, ts = q_ref.shape[0], q_ref.shape[1]
    hp = q_ref.shape[2] // dk

    @pl.when(pl.program_id(1) == 0)
    def _():
        st_scr[...] = jnp.zeros_like(st_scr)

    g_scr[...] = jnp.zeros_like(g_scr)
    consts = _lin_attn_consts(c, dk)
    nw = nw_ref[...]
    q_scale = dk ** -0.5

    def body(ci, carry):
        r0 = pl.multiple_of(ci * c, c)
        rows = pl.ds(r0, c)
        for bb in range(nb):
            glr = glr_ref[bb, rows, :]
            for hh in range(hp):
                ks = slice(hh * dk, (hh + 1) * dk)
                vs = slice(hh * dv, (hh + 1) * dv)
                ch = bb * hp + hh
                q = q_ref[bb, rows, ks] * q_scale
                z = _dot(glr, wup_ref[:, ks]) + bup_ref[:, ks]
                g = _log_sigmoid(z) * (1.0 / GLA_TAU)
                o = _lin_attn_chunk(q, k_ref[bb, rows, ks], v_ref[bb, rows, vs], g,
                                    st_scr.at[ch], b_scr.at[ch], g_scr.at[ch], consts)
                out = _lin_attn_finish(o, gz_ref[bb, rows, vs], nw)
                o_ref[bb, rows, vs] = out.astype(o_ref.dtype)
        return carry

    lax.fori_loop(0, ts // c, body, 0)


def _hgrn_kernel(q_ref, f_ref, v_ref, gz_ref, lbl_ref, nw_ref, o_ref, st_scr, b_scr, g_scr,
                 *, layer):
    c = LIN_CHUNK
    dk, dv = HGRN_DK, HGRN_DV
    nb, ts = q_ref.shape[0], q_ref.shape[1]
    hp = q_ref.shape[2] // dk

    @pl.when(pl.program_id(1) == 0)
    def _():
        st_scr[...] = jnp.zeros_like(st_scr)

    g_scr[...] = jnp.zeros_like(g_scr)
    consts = _lin_attn_consts(c, dk)
    nw = nw_ref[...]

    logits = lbl_ref[...]
    depth = logits.shape[0]
    mx = logits[0:1, :]
    for r in range(1, depth):
        mx = jnp.maximum(mx, logits[r:r + 1, :])
    ex = [jnp.exp(logits[r:r + 1, :] - mx) for r in range(depth)]
    den = ex[0]
    for r in range(1, depth):
        den = den + ex[r]
    sm = [e / den for e in ex]
    lb_all = sm[0]
    for r in range(1, layer + 1):
        lb_all = lb_all + sm[r]
    lb_all = lb_all - sm[0]
    log_lb_all = jnp.log(lb_all)
    log_1mlb_all = jnp.log1p(-lb_all)

    def body(ci, carry):
        r0 = pl.multiple_of(ci * c, c)
        rows = pl.ds(r0, c)
        for bb in range(nb):
            for hh in range(hp):
                ks = slice(hh * dk, (hh + 1) * dk)
                vs = slice(hh * dv, (hh + 1) * dv)
                ch = bb * hp + hh
                lb = lb_all[:, ks]
                z = f_ref[bb, rows, ks]
                g = _logaddexp(log_lb_all[:, ks], log_1mlb_all[:, ks] + _log_sigmoid(z))
                k = (1.0 - lb) * (1.0 / (1.0 + jnp.exp(z)))
                o = _lin_attn_chunk(q_ref[bb, rows, ks], k, v_ref[bb, rows, vs], g,
                                    st_scr.at[ch], b_scr.at[ch], g_scr.at[ch], consts)
                out = _lin_attn_finish(o, gz_ref[bb, rows, vs], nw)
                o_ref[bb, rows, vs] = out.astype(o_ref.dtype)
        return carry

    lax.fori_loop(0, ts // c, body, 0)


def _lin_scratch(chains, dv, dk):
    return [pltpu.VMEM((chains, dv, dk), F32),
            pltpu.VMEM((chains, LIN_CHUNK, dk), F32),
            pltpu.VMEM((chains, LIN_CHUNK + 16, dk), F32)]


def _gla_branch(proj, small, w_up, b_up, norm_w):
    b, s, _ = proj.shape
    ts = min(512, s)
    dk, dv, h = GLA_DK, GLA_DV, GLA_HEADS
    hp = LIN_HEADS_PER_STEP
    kw, vw = hp * dk, hp * dv
    w_up_pad = jnp.pad(w_up, ((0, 128 - GLA_RANK), (0, 0)))
    return pl.pallas_call(
        _gla_kernel,
        out_shape=jax.ShapeDtypeStruct((b, s, h * dv), BF16),
        grid=(h // hp, s // ts),
        in_specs=[
            pl.BlockSpec((b, ts, kw), lambda hi, i: (0, i, hi)),
            pl.BlockSpec((b, ts, kw), lambda hi, i: (0, i, h // hp + hi)),
            pl.BlockSpec((b, ts, vw), lambda hi, i: (0, i, h // hp + hi)),
            pl.BlockSpec((b, ts, vw), lambda hi, i: (0, i, 2 * (h // hp) + hi)),
            pl.BlockSpec((b, ts, 128), lambda hi, i: (0, i, 0)),
            pl.BlockSpec((128, kw), lambda hi, i: (0, hi)),
            pl.BlockSpec((1, kw), lambda hi, i: (0, hi)),
            pl.BlockSpec((1, dv), lambda hi, i: (0, 0)),
        ],
        out_specs=pl.BlockSpec((b, ts, vw), lambda hi, i: (0, i, hi)),
        scratch_shapes=_lin_scratch(b * hp, dv, dk),
        compiler_params=_cparams(("parallel", "arbitrary")),
        name="gla",
    )(proj, proj, proj, proj, small, w_up_pad, b_up.reshape(1, -1), norm_w.reshape(1, dv))


def _hgrn_branch(proj, lb_logits, norm_w, layer):
    b, s, _ = proj.shape
    ts = min(512, s)
    dk, dv, h = HGRN_DK, HGRN_DV, HGRN_HEADS
    hp = LIN_HEADS_PER_STEP
    kw, vw = hp * dk, hp * dv
    ng = h // hp
    depth = lb_logits.shape[0]
    return pl.pallas_call(
        functools.partial(_hgrn_kernel, layer=layer),
        out_shape=jax.ShapeDtypeStruct((b, s, h * dv), BF16),
        grid=(ng, s // ts),
        in_specs=[
            pl.BlockSpec((b, ts, kw), lambda hi, i: (0, i, hi)),
            pl.BlockSpec((b, ts, kw), lambda hi, i: (0, i, ng + hi)),
            pl.BlockSpec((b, ts, vw), lambda hi, i: (0, i, 2 * ng + hi)),
            pl.BlockSpec((b, ts, vw), lambda hi, i: (0, i, 3 * ng + hi)),
            pl.BlockSpec((depth, kw), lambda hi, i: (0, hi)),
            pl.BlockSpec((1, dv), lambda hi, i: (0, 0)),
        ],
        out_specs=pl.BlockSpec((b, ts, vw), lambda hi, i: (0, i, hi)),
        scratch_shapes=_lin_scratch(b * hp, dv, dk),
        compiler_params=_cparams(("parallel", "arbitrary")),
        name="hgrn2",
    )(proj, proj, proj, proj, lb_logits, norm_w.reshape(1, dv))


def _ssd_kernel(z_ref, x_ref, bc_ref, dt_ref, cw_ref, cb_ref, dtb_ref, alog_ref, dexp_ref, nw_ref,
                hexp_ref, o_ref, stage_scr, xbc_scr, st_scr, y_scr):
    c = x_ref.shape[1]
    nch = SSM_CONV_CH
    width = SSM_WIDTH
    n = SSM_STATE
    pairs = SSM_HEADS // 2

    @pl.when(pl.program_id(1) == 0)
    def _():
        st_scr[...] = jnp.zeros_like(st_scr)
        stage_scr[pl.ds(0, 8), :] = jnp.zeros((8, nch), F32)

    stage_scr[pl.ds(8, c), 0:width] = x_ref[0]
    stage_scr[pl.ds(8, c), width:nch] = bc_ref[0]
    acc = cb_ref[...] + cw_ref[0:1, :] * stage_scr[pl.ds(5, c), :]
    for kk in range(1, SSM_CONV):
        acc = acc + cw_ref[kk:kk + 1, :] * stage_scr[pl.ds(5 + kk, c), :]
    stage_scr[pl.ds(0, 8), :] = stage_scr[pl.ds(c, 8), :]
    xbc_scr[...] = _silu(acc)

    dt = _softplus(dt_ref[0] + dtb_ref[...])
    a = -jnp.exp(alog_ref[...])
    da = dt * a
    ri = lax.broadcasted_iota(jnp.int32, (c, c), 0)
    ci = lax.broadcasted_iota(jnp.int32, (c, c), 1)
    causal = ri >= ci
    tri = jnp.where(causal, 1.0, 0.0).astype(BF16)
    a_cs = _sel_dot_left(tri, da)
    a_cs_t = a_cs.T
    hexp = hexp_ref[...]
    dt_x = _sel_dot_right(dt, hexp)
    acs_x = _sel_dot_right(a_cs, hexp)
    a_last_x = acs_x[c - 1:c, :]
    lane = lax.broadcasted_iota(jnp.int32, (c, 2 * SSM_HEADDIM), 1)
    first = lane < SSM_HEADDIM

    for grp in range(SSM_GROUPS):
        bm = xbc_scr[:, width + grp * n: width + (grp + 1) * n]
        cm = xbc_scr[:, width + (SSM_GROUPS + grp) * n: width + (SSM_GROUPS + grp + 1) * n]
        bm16 = bm.astype(BF16)
        cm16 = cm.astype(BF16)
        cb = _dot_nt(cm16, bm16)
        for pp in range(pairs // SSM_GROUPS):
            p = grp * (pairs // SSM_GROUPS) + pp
            lo = p * 2 * SSM_HEADDIM
            hi = lo + 2 * SSM_HEADDIM
            xs = xbc_scr[:, lo:hi]
            xdt = xs * dt_x[:, lo:hi]
            acs = acs_x[:, lo:hi]
            y = jnp.zeros((c, 2 * SSM_HEADDIM), F32)
            for hh in range(2):
                h = 2 * p + hh
                col = jnp.broadcast_to(a_cs[:, h:h + 1], (c, c))
                rw = jnp.broadcast_to(a_cs_t[h:h + 1, :], (c, c))
                lmat = jnp.exp(jnp.where(causal, col - rw, NEG))
                keep = first if hh == 0 else jnp.logical_not(first)
                xh = jnp.where(keep, xdt, 0.0).astype(BF16)
                y = y + _dot((cb * lmat).astype(BF16), xh)
            st = st_scr[p]
            y = y + _dot(cm16, st.astype(BF16)) * jnp.exp(acs)
            decay = jnp.exp(a_last_x[:, lo:hi] - acs)
            st_scr[p] = st * jnp.exp(a_last_x[:, lo:hi]) + _dot_tn(bm16, (xdt * decay).astype(BF16))
            y_scr[:, lo:hi] = y + dexp_ref[:, lo:hi] * xs

    yz = y_scr[...] * _silu(z_ref[0])
    var = jnp.mean(yz * yz, axis=-1, keepdims=True)
    o_ref[0] = (yz * lax.rsqrt(var + EPS) * nw_ref[...]).astype(o_ref.dtype)


def _ssd_branch(proj, small, conv_w, conv_b, dt_bias, a_log, d_skip, norm_w):
    b, s, _ = proj.shape
    c = math.gcd(SSM_CHUNK, s)
    nch = SSM_CONV_CH
    pad16 = lambda v: jnp.pad(v.reshape(1, -1), ((0, 0), (0, 128 - SSM_HEADS)))
    hexp = np.zeros((128, SSM_WIDTH), np.float32)
    for h in range(SSM_HEADS):
        hexp[h, h * SSM_HEADDIM:(h + 1) * SSM_HEADDIM] = 1.0
    d_exp = jnp.repeat(d_skip, SSM_HEADDIM).reshape(1, SSM_WIDTH)
    return pl.pallas_call(
        _ssd_kernel,
        out_shape=jax.ShapeDtypeStruct((b, s, SSM_WIDTH), BF16),
        grid=(b, s // c),
        in_specs=[
            pl.BlockSpec((1, c, SSM_WIDTH), lambda bi, i: (bi, i, 4)),
            pl.BlockSpec((1, c, SSM_WIDTH), lambda bi, i: (bi, i, 5)),
            pl.BlockSpec((1, c, 512), lambda bi, i: (bi, i, 12)),
            pl.BlockSpec((1, c, 128), lambda bi, i: (bi, i, 0)),
            pl.BlockSpec((SSM_CONV, nch), lambda bi, i: (0, 0)),
            pl.BlockSpec((1, nch), lambda bi, i: (0, 0)),
            pl.BlockSpec((1, 128), lambda bi, i: (0, 0)),
            pl.BlockSpec((1, 128), lambda bi, i: (0, 0)),
            pl.BlockSpec((1, SSM_WIDTH), lambda bi, i: (0, 0)),
            pl.BlockSpec((1, SSM_WIDTH), lambda bi, i: (0, 0)),
            pl.BlockSpec((128, SSM_WIDTH), lambda bi, i: (0, 0)),
        ],
        out_specs=pl.BlockSpec((1, c, SSM_WIDTH), lambda bi, i: (bi, i, 0)),
        scratch_shapes=[
            pltpu.VMEM((c + 8, nch), F32),
            pltpu.VMEM((c, nch), F32),
            pltpu.VMEM((SSM_HEADS // 2, SSM_STATE, 2 * SSM_HEADDIM), F32),
            pltpu.VMEM((c, SSM_WIDTH), F32),
        ],
        compiler_params=_cparams(("parallel", "arbitrary")),
        name="ssd",
    )(proj, proj, proj, small, conv_w, conv_b.reshape(1, nch), pad16(dt_bias), pad16(a_log),
      d_exp, norm_w.reshape(1, SSM_WIDTH), jnp.asarray(hexp, dtype=BF16))


def _rope(x, cos, sin_signed):
    return x * cos + pltpu.roll(x, NSA_DH // 2, axis=1) * sin_signed


def _nsa_prep_kernel(q_ref, kc_ref, ks_ref, vs_ref, kw_ref, vw_ref, cos_ref, sin_ref,
                     qo_ref, kco_ref, kso_ref, vso_ref, kwo_ref, vwo_ref):
    cos = cos_ref[...]
    sin = sin_ref[...]
    dh = NSA_DH
    scale = dh ** -0.5
    for h in range(NSA_HEADS):
        sl = slice(h * dh, (h + 1) * dh)
        qo_ref[0, :, sl] = (_rope(q_ref[0, :, sl], cos, sin) * scale).astype(qo_ref.dtype)
    for g in range(NSA_GROUPS):
        sl = slice(g * dh, (g + 1) * dh)
        kco_ref[0, :, sl] = _rope(kc_ref[0, :, sl], cos, sin)
        kso_ref[0, :, sl] = _rope(ks_ref[0, :, sl], cos, sin).astype(kso_ref.dtype)
        kwo_ref[0, :, sl] = _rope(kw_ref[0, :, sl], cos, sin).astype(kwo_ref.dtype)
    tk = NSA_TILE
    for g in range(NSA_GROUPS):
        for r in range(q_ref.shape[1] // tk):
            rows = slice(r * tk, (r + 1) * tk)
            sl = slice(g * dh, (g + 1) * dh)
            vso_ref[0, g, r] = vs_ref[0, rows, sl].T.astype(vso_ref.dtype)
            vwo_ref[0, g, r] = vw_ref[0, rows, sl].T.astype(vwo_ref.dtype)


def _nsa_prep(proj):
    b, s, _ = proj.shape
    ts = min(512, s)
    tk = NSA_TILE
    half = NSA_DH // 2
    inv_freq = ROPE_THETA ** (-jnp.arange(half, dtype=F32) / half)
    ang = jnp.arange(s, dtype=F32)[:, None] * inv_freq[None, :]
    cos = jnp.cos(ang)
    sin = jnp.sin(ang)
    cos_full = jnp.concatenate([cos, cos], axis=-1)
    sin_signed = jnp.concatenate([-sin, sin], axis=-1)
    kvw = NSA_GROUPS * NSA_DH
    col = lambda idx: (lambda bi, i: (bi, i, idx))
    kv_spec = lambda idx: pl.BlockSpec((1, ts, kvw), col(idx))
    kv_out = pl.BlockSpec((1, ts, kvw), lambda bi, i: (bi, i, 0))
    vt_shape = jax.ShapeDtypeStruct((b, NSA_GROUPS, s // tk, NSA_DH, tk), BF16)
    vt_out = pl.BlockSpec((1, NSA_GROUPS, ts // tk, NSA_DH, tk), lambda bi, i: (bi, 0, i, 0, 0))
    return pl.pallas_call(
        _nsa_prep_kernel,
        out_shape=(jax.ShapeDtypeStruct((b, s, NSA_HEADS * NSA_DH), BF16),
                   jax.ShapeDtypeStruct((b, s, kvw), F32),
                   jax.ShapeDtypeStruct((b, s, kvw), BF16),
                   vt_shape,
                   jax.ShapeDtypeStruct((b, s, kvw), BF16),
                   vt_shape),
        grid=(b, s // ts),
        in_specs=[
            pl.BlockSpec((1, ts, NSA_HEADS * NSA_DH), col(3)),
            kv_spec(16), kv_spec(18), kv_spec(19), kv_spec(20), kv_spec(21),
            pl.BlockSpec((ts, NSA_DH), lambda bi, i: (i, 0)),
            pl.BlockSpec((ts, NSA_DH), lambda bi, i: (i, 0)),
        ],
        out_specs=(pl.BlockSpec((1, ts, NSA_HEADS * NSA_DH), lambda bi, i: (bi, i, 0)),
                   kv_out, kv_out, vt_out, kv_out, vt_out),
        compiler_params=_cparams(("parallel", "parallel")),
        name="nsa_prep",
    )(proj, proj, proj, proj, proj, proj, cos_full, sin_signed)


def _compress_kernel(t_ref, pe_ref, w1_ref, w2_ref, o_ref, sh_scr, *, transposed):
    n = t_ref.shape[1] // CMP_STRIDE
    dh = NSA_DH
    half = CMP_LEN // 2
    acc_a = jnp.zeros((n, CMP_HIDDEN), F32)
    acc_b = jnp.zeros((n, CMP_HIDDEN), F32)
    for l in range(half):
        t = t_ref[0, pl.ds(l, n, stride=CMP_STRIDE), :]
        acc_a = acc_a + _dot((t + pe_ref[l:l + 1, :]).astype(BF16), w1_ref[l * dh:(l + 1) * dh, :])
        acc_b = acc_b + _dot((t + pe_ref[half + l:half + l + 1, :]).astype(BF16),
                             w1_ref[(half + l) * dh:(half + l + 1) * dh, :])
    sh_scr[pl.ds(0, n), :] = acc_b
    sh_scr[pl.ds(n, 8), :] = jnp.zeros((8, CMP_HIDDEN), F32)
    hid = acc_a + sh_scr[pl.ds(1, n), :]
    out = _dot(_silu(hid).astype(BF16), w2_ref[...])
    if transposed:
        tk = NSA_TILE
        for r in range(n // tk):
            o_ref[0, 0, r] = out[r * tk:(r + 1) * tk, :].T.astype(o_ref.dtype)
    else:
        o_ref[0, 0] = out.astype(o_ref.dtype)


def _nsa_compress(src, col0, pe, w1, w2, transposed):
    b, s, _ = src.shape
    n = s // CMP_STRIDE
    base = col0 // NSA_DH
    tk = NSA_TILE
    if transposed:
        out_shape = jax.ShapeDtypeStruct((b, NSA_GROUPS, n // tk, NSA_DH, tk), BF16)
        out_spec = pl.BlockSpec((1, 1, n // tk, NSA_DH, tk), lambda bi, g: (bi, g, 0, 0, 0))
    else:
        out_shape = jax.ShapeDtypeStruct((b, NSA_GROUPS, n, NSA_DH), BF16)
        out_spec = pl.BlockSpec((1, 1, n, NSA_DH), lambda bi, g: (bi, g, 0, 0))
    return pl.pallas_call(
        functools.partial(_compress_kernel, transposed=transposed),
        out_shape=out_shape,
        grid=(b, NSA_GROUPS),
        in_specs=[
            pl.BlockSpec((1, s, NSA_DH), lambda bi, g: (bi, 0, base + g)),
            pl.BlockSpec((CMP_LEN, NSA_DH), lambda bi, g: (0, 0)),
            pl.BlockSpec((CMP_LEN * NSA_DH, CMP_HIDDEN), lambda bi, g: (0, 0)),
            pl.BlockSpec((CMP_HIDDEN, NSA_DH), lambda bi, g: (0, 0)),
        ],
        out_specs=out_spec,
        scratch_shapes=[pltpu.VMEM((n + 8, CMP_HIDDEN), F32)],
        compiler_params=_cparams(("parallel", "parallel")),
        name="nsa_compress",
    )(src, pe, w1.astype(BF16), w2.astype(BF16))


def _flash_update(s_tiles, vt_tiles, m_ref, l_ref, acc_ref):
    m_old = m_ref[...]
    m_new = m_old
    for s in s_tiles:
        m_new = jnp.maximum(m_new, jnp.max(s, axis=0, keepdims=True))
    alpha = jnp.exp(m_old - m_new)
    p_tiles = [jnp.exp(s - m_new) for s in s_tiles]
    l_new = alpha * l_ref[...]
    for p in p_tiles:
        l_new = l_new + jnp.sum(p, axis=0, keepdims=True)
    l_ref[...] = l_new
    acc_ref[...] = alpha * acc_ref[...] + _pv(vt_tiles, p_tiles)
    m_ref[...] = m_new
    return p_tiles, alpha


def _pv(vt_tiles, p_tiles):
    vt = jnp.concatenate(vt_tiles, axis=1)
    p = jnp.concatenate([p.astype(BF16) for p in p_tiles], axis=0)
    return _dot(vt, p)


def _softmax_tiles(s_tiles, vt_tiles):
    m = jnp.max(s_tiles[0], axis=0, keepdims=True)
    for s in s_tiles[1:]:
        m = jnp.maximum(m, jnp.max(s, axis=0, keepdims=True))
    p_tiles = [jnp.exp(s - m) for s in s_tiles]
    l = jnp.sum(p_tiles[0], axis=0, keepdims=True)
    for p in p_tiles[1:]:
        l = l + jnp.sum(p, axis=0, keepdims=True)
    return p_tiles, l, _pv(vt_tiles, p_tiles)


def _nsa_kernel(q_ref, kc_ref, vct_ref, ovt_ref, ks_ref, vst_ref, kw_ref, vwt_ref, gate_ref, nz_ref,
                o_ref, m_scr, l_scr, acc_scr, cap_scr, capd_scr, capl_scr, sa_scr, sb_scr):
    tq = q_ref.shape[1]
    tk = NSA_TILE
    dh = NSA_DH
    hpg = NSA_HPG
    lanes = hpg * tq
    ns = ovt_ref.shape[1]
    qi = pl.program_id(2)
    t0 = qi * tq

    qt = jnp.concatenate([q_ref[0, :, j * dh:(j + 1) * dh].astype(F32).T for j in range(hpg)],
                         axis=1).astype(BF16)
    rowi = lax.broadcasted_iota(jnp.int32, (tk, lanes), 0)
    qpos = lax.broadcasted_iota(jnp.int32, (tk, lanes), 1) & (tq - 1)
    capd_scr[...] = jnp.where(rowi <= qpos, FORCE, NEG)
    capl_scr[...] = jnp.where(rowi > qpos, FORCE, NEG)

    s_tiles = []
    n_ct = kc_ref.shape[2] // tk
    for c in range(n_ct):
        first_end = c * (tk * CMP_STRIDE) + (CMP_LEN - 1) - t0
        cap = jnp.where(rowi * CMP_STRIDE + first_end <= qpos, FORCE, NEG)
        s_tiles.append(jnp.minimum(_dot(kc_ref[0, 0, c * tk:(c + 1) * tk, :], qt), cap))
    p_tiles, l_c, acc_c = _softmax_tiles(s_tiles, [vct_ref[0, 0, c] for c in range(n_ct)])
    imp_un = jnp.zeros((ns, lanes), F32)
    for c, p in enumerate(p_tiles):
        p_hi = p.astype(BF16)
        p_lo = (p - p_hi.astype(F32)).astype(BF16)
        imp_un = imp_un + _dot(ovt_ref[c], p_hi) + _dot(ovt_ref[c], p_lo)
    tl = t0 + (lax.broadcasted_iota(jnp.int32, (1, lanes), 1) & (tq - 1))
    inv_l = jnp.where(tl >= CMP_LEN - 1, 1.0 / l_c, 0.0)
    acc_scr[0] = acc_c * inv_l
    impn = imp_un * inv_l
    imp = impn[:, 0:tq]
    for j in range(1, hpg):
        imp = imp + impn[:, j * tq:(j + 1) * tq]

    blk = lax.broadcasted_iota(jnp.int32, (ns, tq), 0)
    blk_t = jnp.right_shift(t0 + lax.broadcasted_iota(jnp.int32, (ns, tq), 1), SLC_SHIFT)
    score = jnp.where(blk == 0, FORCE,
                      jnp.where(blk == blk_t, FORCE,
                                jnp.where(blk == blk_t - 1, FORCE,
                                          jnp.where(blk <= blk_t, imp, NEG))))
    blk_f = blk.astype(F32)
    cap_sel = jnp.full((ns, tq), NEG, F32)
    for _ in range(min(SLC_TOPK, ns)):
        mx = jnp.max(score, axis=0, keepdims=True)
        first = jnp.min(jnp.where(score == mx, blk_f, float(ns)), axis=0, keepdims=True)
        pick = blk_f == first
        cap_sel = jnp.where(pick, FORCE, cap_sel)
        score = jnp.where(pick, -jnp.inf, score)
    for j in range(hpg):
        cap_scr[:, j * tq:(j + 1) * tq] = cap_sel

    bpt = tk // SLC_BLOCK

    def slc_scores(kt):
        k0 = pl.multiple_of(kt * tk, tk)
        s = _dot(ks_ref[0, pl.ds(k0, tk), :], qt)
        caps = [jnp.broadcast_to(cap_scr[pl.ds(kt * bpt + r, 1), :], (SLC_BLOCK, lanes))
                for r in range(bpt)]
        return jnp.minimum(s, jnp.concatenate(caps, axis=0))

    wt = WINDOW // tk
    s_tiles, vt_tiles = [], []
    for r in range(wt, -1, -1):
        kt = qi - r
        ktc = jnp.maximum(kt, 0)
        s = _dot(kw_ref[0, pl.ds(pl.multiple_of(ktc * tk, tk), tk), :], qt)
        if r == wt:
            s = jnp.minimum(s, capl_scr[...])
        if r == 0:
            s = jnp.minimum(s, capd_scr[...])
        else:
            s = jnp.minimum(s, jnp.where(kt >= 0, FORCE, NEG))
        s_tiles.append(s)
        vt_tiles.append(vwt_ref[0, 0, ktc])
    _, l_w, acc_w = _softmax_tiles(s_tiles, vt_tiles)
    acc_scr[1] = acc_w * (1.0 / l_w)

    nb = NSA_TILES_PER_STEP
    m_scr[...] = jnp.full((1, lanes), NEG, F32)
    l_scr[...] = jnp.zeros((1, lanes), F32)
    acc_scr[2] = jnp.zeros((dh, lanes), F32)
    state = (m_scr, l_scr, acc_scr.at[2])
    last_tile = ks_ref.shape[1] // tk - 1
    n_full = qi // nb

    def score_step(step, dst):
        for r in range(nb):
            dst[r * tk:(r + 1) * tk, :] = slc_scores(step * nb + r)

    def value_step(step, src):
        _flash_update([src[r * tk:(r + 1) * tk, :] for r in range(nb)],
                      [vst_ref[0, 0, step * nb + r] for r in range(nb)], *state)

    @pl.when(n_full > 0)
    def _():
        score_step(0, sa_scr)

    def slc_pair(j, carry):
        score_step(2 * j + 1, sb_scr)
        value_step(2 * j, sa_scr)
        score_step(jnp.minimum(2 * j + 2, n_full - 1), sa_scr)
        value_step(2 * j + 1, sb_scr)
        return carry

    lax.fori_loop(0, n_full // 2, slc_pair, 0)

    @pl.when(n_full % 2 == 1)
    def _():
        value_step(n_full - 1, sa_scr)

    s_tiles, vt_tiles = [], []
    for r in range(nb):
        kt = n_full * nb + r
        ktc = jnp.minimum(kt, last_tile)
        below = jnp.where(kt < qi, FORCE, NEG)
        upto = jnp.where(kt <= qi, FORCE, NEG)
        cap = jnp.minimum(jnp.maximum(capd_scr[...], below), upto)
        s_tiles.append(jnp.minimum(slc_scores(ktc), cap))
        vt_tiles.append(vst_ref[0, 0, ktc])
    _flash_update(s_tiles, vt_tiles, *state)

    o_c = acc_scr[0]
    o_s = acc_scr[2] * (1.0 / l_scr[...])
    o_w = acc_scr[1]
    sgt = _sigmoid(gate_ref[0]).T
    for j in range(hpg):
        ls = slice(j * tq, (j + 1) * tq)
        mix_t = (sgt[3 * j:3 * j + 1, :] * o_c[:, ls] + sgt[3 * j + 1:3 * j + 2, :] * o_s[:, ls]
                 + sgt[3 * j + 2:3 * j + 3, :] * o_w[:, ls])
        sl = slice(j * dh, (j + 1) * dh)
        o_ref[0, :, sl] = (mix_t.T * _silu(nz_ref[0, :, sl])).astype(o_ref.dtype)


def _nsa_attention(qr, kc, vct, ksr, vst, kwr, vwt, small, proj):
    b, s, _ = qr.shape
    tq = tk = NSA_TILE
    assert s % tq == 0 and WINDOW % tk == 0 and tk % SLC_BLOCK == 0
    ncp = kc.shape[2]
    assert ncp % tk == 0
    ns = s // SLC_BLOCK
    gw = NSA_HPG * NSA_DH
    lanes = NSA_HPG * tq
    cs = np.arange(ncp)[None, :] * CMP_STRIDE
    ss = np.arange(ns)[:, None] * SLC_BLOCK
    ovt = ((np.minimum(cs + CMP_LEN, ss + SLC_BLOCK) - np.maximum(cs, ss)) > 0).astype(np.float32)
    ovt[:, ncp - 1] = 0.0
    ovt = ovt.reshape(ns, ncp // tk, tk).transpose(1, 0, 2)
    kv_spec = pl.BlockSpec((1, s, NSA_DH), lambda bi, g, i: (bi, 0, g))
    vt_spec = pl.BlockSpec((1, 1, s // tk, NSA_DH, tk), lambda bi, g, i: (bi, g, 0, 0, 0))
    return pl.pallas_call(
        _nsa_kernel,
        out_shape=jax.ShapeDtypeStruct((b, s, NSA_HEADS * NSA_DH), BF16),
        grid=(b, NSA_GROUPS, s // tq),
        in_specs=[
            pl.BlockSpec((1, tq, gw), lambda bi, g, i: (bi, i, g)),
            pl.BlockSpec((1, 1, ncp, NSA_DH), lambda bi, g, i: (bi, g, 0, 0)),
            pl.BlockSpec((1, 1, ncp // tk, NSA_DH, tk), lambda bi, g, i: (bi, g, 0, 0, 0)),
            pl.BlockSpec((ncp // tk, ns, tk), lambda bi, g, i: (0, 0, 0)),
            kv_spec, vt_spec, kv_spec, vt_spec,
            pl.BlockSpec((1, tq, 128), lambda bi, g, i: (bi, i, 1 + g)),
            pl.BlockSpec((1, tq, gw), lambda bi, g, i: (bi, i, 11 + g)),
        ],
        out_specs=pl.BlockSpec((1, tq, gw), lambda bi, g, i: (bi, i, g)),
        scratch_shapes=[
            pltpu.VMEM((1, lanes), F32),
            pltpu.VMEM((1, lanes), F32),
            pltpu.VMEM((3, NSA_DH, lanes), F32),
            pltpu.VMEM((ns, lanes), F32),
            pltpu.VMEM((tk, lanes), F32),
            pltpu.VMEM((tk, lanes), F32),
            pltpu.VMEM((NSA_TILES_PER_STEP * tk, lanes), F32),
            pltpu.VMEM((NSA_TILES_PER_STEP * tk, lanes), F32),
        ],
        compiler_params=_cparams(("parallel", "parallel", "arbitrary")),
        name="nsa_attention",
    )(qr, kc, vct, jnp.asarray(ovt, dtype=BF16), ksr, vst, kwr, vwt, small, proj)


def _nsa_branch(proj, small, pe_k, w1_k, w2_k, pe_v, w1_v, w2_v):
    qr, kcr, ksr, vst, kwr, vwt = _nsa_prep(proj)
    kc = _nsa_compress(kcr, 0, pe_k, w1_k, w2_k, transposed=False)
    vct = _nsa_compress(proj, 4352, pe_v, w1_v, w2_v, transposed=True)
    return _nsa_attention(qr, kc, vct, ksr, vst, kwr, vwt, small, proj)


def _even_weights(w_in):
    main = jnp.concatenate([w_in[:, 0:2048], w_in[:, 2064:5648], w_in[:, 5672:6696]], axis=1)
    d = w_in.shape[0]
    gates = w_in[:, 5648:5672]
    per = NSA_HPG * 3
    blocks = [jnp.pad(w_in[:, 2048:2064], ((0, 0), (0, 128 - GLA_RANK)))]
    for g in range(NSA_GROUPS):
        blocks.append(jnp.pad(gates[:, g * per:(g + 1) * per], ((0, 0), (0, 128 - per))))
    small = jnp.concatenate(blocks, axis=1)
    return main.astype(BF16), small.astype(BF16)


def _odd_weights(w_in):
    main = w_in[:, 0:MAIN_COLS]
    small = jnp.pad(w_in[:, MAIN_COLS:MAIN_COLS + SSM_HEADS], ((0, 0), (0, 128 - SSM_HEADS)))
    return main.astype(BF16), small.astype(BF16)


def kernel(x, c, ada_w, ada_b, pre_norm_w, post_norm_w, even_w_in, even_w_out, gla_w_up, gla_b_up,
           gla_norm_w, nsa_pe_k, nsa_w1_k, nsa_w2_k, nsa_pe_v, nsa_w1_v, nsa_w2_v, odd_w_in,
           odd_w_out, hgrn_lb_logits, hgrn_norm_w, ssm_conv_w, ssm_conv_b, ssm_dt_bias, ssm_a_log,
           ssm_d, ssm_norm_w):
    depth = ada_w.shape[0]
    mod = _adaln_mod(c, ada_w, ada_b)
    for l in range(depth):
        if l % 2 == 0:
            e = l // 2
            w_main, w_small = _even_weights(even_w_in[e])
            proj, small = _in_proj(x, mod[l], pre_norm_w[l], w_main, w_small)
            o_a = _gla_branch(proj, small, gla_w_up[e], gla_b_up[e], gla_norm_w[e])
            o_b = _nsa_branch(proj, small, nsa_pe_k[e], nsa_w1_k[e], nsa_w2_k[e],
                              nsa_pe_v[e], nsa_w1_v[e], nsa_w2_v[e])
            w_out = even_w_out[e]
        else:
            o = l // 2
            w_main, w_small = _odd_weights(odd_w_in[o])
            proj, small = _in_proj(x, mod[l], pre_norm_w[l], w_main, w_small)
            o_a = _hgrn_branch(proj, hgrn_lb_logits, hgrn_norm_w[o], l)
            o_b = _ssd_branch(proj, small, ssm_conv_w[o], ssm_conv_b[o], ssm_dt_bias[o],
                              ssm_a_log[o], ssm_d[o], ssm_norm_w[o])
            w_out = odd_w_out[o]
        x = _out_proj(o_a, o_b, w_out.astype(BF16), x, mod[l], post_norm_w[l])
    return x
```

```python
import functools
import math

import jax
import jax.numpy as jnp
import numpy as np
from jax import lax
from jax.experimental import pallas as pl
from jax.experimental.pallas import tpu as pltpu

F32 = jnp.float32
BF16 = jnp.bfloat16

D_MODEL = 2048
EPS = 1e-6
NEG = -1e30
FORCE = 1e30
ROPE_THETA = 10000.0

GLA_HEADS = 4
GLA_DK = 128
GLA_DV = 256
GLA_RANK = 16
GLA_TAU = 16.0

NSA_DH = 128
NSA_HEADS = 8
NSA_GROUPS = 2
NSA_HPG = 4
CMP_LEN = 32
CMP_STRIDE = 16
CMP_HIDDEN = 256
SLC_BLOCK = 64
SLC_SHIFT = 6
SLC_TOPK = 16
WINDOW = 512
NSA_TILE = 128
NSA_ONES = 16
NSA_TILES_PER_STEP = 4

HGRN_HEADS = 8
HGRN_DK = 128
HGRN_DV = 128

SSM_HEADDIM = 64
SSM_HEADS = 16
SSM_GROUPS = 2
SSM_STATE = 128
SSM_CONV = 4
SSM_CHUNK = 256
SSM_WIDTH = 1024
SSM_CONV_CH = SSM_WIDTH + 2 * SSM_GROUPS * SSM_STATE

LIN_CHUNK = 128
MAIN_COLS = 6656
VMEM_LIMIT = 56 * 1024 * 1024


def _cparams(sem):
    return pltpu.CompilerParams(dimension_semantics=sem, vmem_limit_bytes=VMEM_LIMIT)


def _dot(a, b):
    return jnp.dot(a, b, preferred_element_type=F32)


def _dot_nt(a, b):
    return lax.dot_general(a, b, (((1,), (1,)), ((), ())), preferred_element_type=F32)


def _dot_tn(a, b):
    return lax.dot_general(a, b, (((0,), (0,)), ((), ())), preferred_element_type=F32)


def _split3(x):
    hi = x.astype(BF16)
    r = x - hi.astype(F32)
    mid = r.astype(BF16)
    lo = (r - mid.astype(F32)).astype(BF16)
    return hi, mid, lo


def _sel_dot_left(sel, x):
    n = x.shape[1]
    y = _dot(sel.astype(BF16), jnp.concatenate(_split3(x), axis=1))
    return y[:, 0:n] + y[:, n:2 * n] + y[:, 2 * n:3 * n]


def _sel_dot_right(x, sel):
    m = x.shape[0]
    y = _dot(jnp.concatenate(_split3(x), axis=0), sel.astype(BF16))
    return y[0:m] + y[m:2 * m] + y[2 * m:3 * m]


def _sigmoid(x):
    return 1.0 / (1.0 + jnp.exp(-x))


def _silu(x):
    return x * _sigmoid(x)


def _log1p_exp_neg_abs(x):
    return jnp.log(1.0 + jnp.exp(-jnp.abs(x)))


def _log_sigmoid(x):
    return jnp.minimum(x, 0.0) - _log1p_exp_neg_abs(x)


def _softplus(x):
    return jnp.maximum(x, 0.0) + _log1p_exp_neg_abs(x)


def _logaddexp(a, b):
    return jnp.maximum(a, b) + _log1p_exp_neg_abs(a - b)


def _mod_kernel(c_ref, w_ref, b_ref, o_ref):
    ca = _silu(c_ref[...])
    o_ref[0] = _dot(ca, w_ref[0]) + b_ref[0]


def _adaln_mod(c, ada_w, ada_b):
    depth, d, n3 = ada_w.shape
    b = c.shape[0]
    rows = 8
    c_pad = jnp.pad(c, ((0, rows - b), (0, 0)))
    tn = 768
    out = pl.pallas_call(
        _mod_kernel,
        out_shape=jax.ShapeDtypeStruct((depth, rows, n3), F32),
        grid=(depth, n3 // tn),
        in_specs=[
            pl.BlockSpec((rows, d), lambda l, j: (0, 0)),
            pl.BlockSpec((1, d, tn), lambda l, j: (l, 0, j)),
            pl.BlockSpec((1, 1, tn), lambda l, j: (l, 0, j)),
        ],
        out_specs=pl.BlockSpec((1, rows, tn), lambda l, j: (l, 0, j)),
        compiler_params=_cparams(("parallel", "parallel")),
        name="adaln_mod",
    )(c_pad, ada_w, ada_b.reshape(depth, 1, n3))
    return out[:, :b]


def _inproj_kernel(x_ref, mod_ref, nw_ref, w_ref, ws_ref, o_ref, os_ref, h_scr):
    d = x_ref.shape[-1]

    @pl.when(pl.program_id(2) == 0)
    def _():
        x = x_ref[0]
        var = jnp.mean(x * x, axis=-1, keepdims=True)
        y = x * lax.rsqrt(var + EPS) * nw_ref[...]
        shift = mod_ref[0, :, 0:d]
        scale = mod_ref[0, :, d:2 * d]
        hb = (y * (1.0 + scale) + shift).astype(BF16)
        h_scr[...] = hb
        os_ref[0] = _dot(hb, ws_ref[...])

    o_ref[0] = _dot(h_scr[...], w_ref[...])


def _in_proj(x, mod_l, norm_w, w_main, w_small):
    b, s, d = x.shape
    n = w_main.shape[1]
    ns = w_small.shape[1]
    tm = min(1024, s)
    tn = 512
    return pl.pallas_call(
        _inproj_kernel,
        out_shape=(jax.ShapeDtypeStruct((b, s, n), F32),
                   jax.ShapeDtypeStruct((b, s, ns), F32)),
        grid=(b, s // tm, n // tn),
        in_specs=[
            pl.BlockSpec((1, tm, d), lambda bi, i, j: (bi, i, 0)),
            pl.BlockSpec((1, 1, 3 * d), lambda bi, i, j: (bi, 0, 0)),
            pl.BlockSpec((1, d), lambda bi, i, j: (0, 0)),
            pl.BlockSpec((d, tn), lambda bi, i, j: (0, j)),
            pl.BlockSpec((d, ns), lambda bi, i, j: (0, 0)),
        ],
        out_specs=(pl.BlockSpec((1, tm, tn), lambda bi, i, j: (bi, i, j)),
                   pl.BlockSpec((1, tm, ns), lambda bi, i, j: (bi, i, 0))),
        scratch_shapes=[pltpu.VMEM((tm, d), BF16)],
        compiler_params=_cparams(("parallel", "parallel", "arbitrary")),
        name="in_proj",
    )(x, mod_l.reshape(b, 1, 3 * d), norm_w.reshape(1, d), w_main, w_small)


def _outproj_kernel(a1_ref, a2_ref, w_ref, x_ref, mod_ref, nw_ref, o_ref):
    d = x_ref.shape[-1]
    half = a1_ref.shape[-1]
    y = _dot(a1_ref[0], w_ref[0:half, :]) + _dot(a2_ref[0], w_ref[half:2 * half, :])
    var = jnp.mean(y * y, axis=-1, keepdims=True)
    yn = y * lax.rsqrt(var + EPS) * nw_ref[...]
    gate = mod_ref[0, :, 2 * d:3 * d]
    o_ref[0] = x_ref[0] + gate * yn


def _out_proj(a1, a2, w_out, x, mod_l, norm_w):
    b, s, d = x.shape
    half = a1.shape[-1]
    tm = min(512, s)
    return pl.pallas_call(
        _outproj_kernel,
        out_shape=jax.ShapeDtypeStruct((b, s, d), F32),
        grid=(b, s // tm),
        in_specs=[
            pl.BlockSpec((1, tm, half), lambda bi, i: (bi, i, 0)),
            pl.BlockSpec((1, tm, half), lambda bi, i: (bi, i, 0)),
            pl.BlockSpec((2 * half, d), lambda bi, i: (0, 0)),
            pl.BlockSpec((1, tm, d), lambda bi, i: (bi, i, 0)),
            pl.BlockSpec((1, 1, 3 * d), lambda bi, i: (bi, 0, 0)),
            pl.BlockSpec((1, d), lambda bi, i: (0, 0)),
        ],
        out_specs=pl.BlockSpec((1, tm, d), lambda bi, i: (bi, i, 0)),
        compiler_params=_cparams(("parallel", "parallel")),
        name="out_proj",
    )(a1, a2, w_out, x, mod_l.reshape(b, 1, 3 * d), norm_w.reshape(1, d))


def _lin_attn_chunk(q, k, v, g, st_scr, b_scr, g_scr, consts):
    c, dk = q.shape
    row, eye, tri, pairs = consts
    b = _sel_dot_left(tri, g)
    b_scr[...] = b
    g_scr[pl.ds(8, c), :] = g
    b_last = b_scr[pl.ds(c - 1, 1), :]

    attn = jnp.where(eye, _dot_nt(q.astype(BF16), k.astype(BF16)), 0.0)

    def level(s, expo, attn):
        upper = (row & s) != 0
        x = (jnp.where(upper, q, k) * jnp.exp(expo)).astype(BF16)
        return jnp.where(pairs[s], _dot_nt(x, x), attn)

    for s in [c >> i for i in range(1, c.bit_length() - 2)]:
        pieces = [jnp.broadcast_to(b_scr[pl.ds(p * 2 * s + s - 1, 1), :], (2 * s, dk))
                  for p in range(c // (2 * s))]
        d = b - jnp.concatenate(pieces, axis=0)
        attn = level(s, jnp.where((row & s) != 0, d, -d), attn)

    g_dn = g_scr[pl.ds(7, c), :]
    g_up = g_scr[pl.ds(9, c), :]
    r4 = row & 3
    attn = level(2, jnp.where(r4 == 2, g, jnp.where(r4 == 3, g + g_dn, jnp.where(r4 == 0, g_up, 0.0))),
                 attn)
    attn = level(1, jnp.where((row & 1) != 0, g, 0.0), attn)

    v16 = v.astype(BF16)
    st = st_scr[...]
    o = _dot(attn.astype(BF16), v16) + _dot_nt((q * jnp.exp(b)).astype(BF16), st.astype(BF16))
    e_last = jnp.exp(b_last)
    k_dec = (k * jnp.exp(b_last - b)).astype(BF16)
    st_scr[...] = st * e_last + _dot_tn(v16, k_dec)
    return o


def _lin_attn_consts(c, dk):
    row = lax.broadcasted_iota(jnp.int32, (c, dk), 0)
    ri = lax.broadcasted_iota(jnp.int32, (c, c), 0)
    ci = lax.broadcasted_iota(jnp.int32, (c, c), 1)
    eye = ri == ci
    tri = jnp.where(ri >= ci, 1.0, 0.0).astype(BF16)
    pairs = {s: jnp.where((ri & -(2 * s)) == (ci & -(2 * s)), (ri & s) - (ci & s), 0) == s
             for s in [c >> i for i in range(1, c.bit_length())]}
    return row, eye, tri, pairs


def _lin_attn_finish(o, gz, nw):
    var = jnp.mean(o * o, axis=-1, keepdims=True)
    return (o * lax.rsqrt(var + EPS) * nw) * _silu(gz)


LIN_HEADS_PER_STEP = 4


def _gla_kernel(q_ref, k_ref, v_ref, gz_ref, glr_ref, wup_ref, bup_ref, nw_ref, o_ref,
                st_scr, b_scr, g_scr):
    c = LIN_CHUNK
    dk, dv = GLA_DK, GLA_DV
    nb, ts = q_ref.shape[0], q_ref.shape[1]
    hp = q_ref.shape[2] // dk

    @pl.when(pl.program_id(1) == 0)
    def _():
        st_scr[...] = jnp.zeros_like(st_scr)

    g_scr[...] = jnp.zeros_like(g_scr)
    consts = _lin_attn_consts(c, dk)
    nw = nw_ref[...]
    q_scale = dk ** -0.5

    def body(ci, carry):
        r0 = pl.multiple_of(ci * c, c)
        rows = pl.ds(r0, c)
        for bb in range(nb):
            glr = glr_ref[bb, rows, :]
            for hh in range(hp):
                ks = slice(hh * dk, (hh + 1) * dk)
                vs = slice(hh * dv, (hh + 1) * dv)
                ch = bb * hp + hh
                q = q_ref[bb, rows, ks] * q_scale
                z = _dot(glr, wup_ref[:, ks]) + bup_ref[:, ks]
                g = _log_sigmoid(z) * (1.0 / GLA_TAU)
                o = _lin_attn_chunk(q, k_ref[bb, rows, ks], v_ref[bb, rows, vs], g,
                                    st_scr.at[ch], b_scr.at[ch], g_scr.at[ch], consts)
                out = _lin_attn_finish(o, gz_ref[bb, rows, vs], nw)
                o_ref[bb, rows, vs] = out.astype(o_ref.dtype)
        return carry

    lax.fori_loop(0, ts // c, body, 0)


def _hgrn_kernel(q_ref, f_ref, v_ref, gz_ref, lbl_ref, nw_ref, o_ref, st_scr, b_scr, g_scr,
                 *, layer):
    c = LIN_CHUNK
    dk, dv = HGRN_DK, HGRN_DV
    nb, ts = q_ref.shape[0], q_ref.shape[1]
    hp = q_ref.shape[2] // dk

    @pl.when(pl.program_id(1) == 0)
    def _():
        st_scr[...] = jnp.zeros_like(st_scr)

    g_scr[...] = jnp.zeros_like(g_scr)
    consts = _lin_attn_consts(c, dk)
    nw = nw_ref[...]

    logits = lbl_ref[...]
    depth = logits.shape[0]
    mx = logits[0:1, :]
    for r in range(1, depth):
        mx = jnp.maximum(mx, logits[r:r + 1, :])
    ex = [jnp.exp(logits[r:r + 1, :] - mx) for r in range(depth)]
    den = ex[0]
    for r in range(1, depth):
        den = den + ex[r]
    sm = [e / den for e in ex]
    lb_all = sm[0]
    for r in range(1, layer + 1):
        lb_all = lb_all + sm[r]
    lb_all = lb_all - sm[0]
    log_lb_all = jnp.log(lb_all)
    log_1mlb_all = jnp.log1p(-lb_all)

    def body(ci, carry):
        r0 = pl.multiple_of(ci * c, c)
        rows = pl.ds(r0, c)
        for bb in range(nb):
            for hh in range(hp):
                ks = slice(hh * dk, (hh + 1) * dk)
                vs = slice(hh * dv, (hh + 1) * dv)
                ch = bb * hp + hh
                lb = lb_all[:, ks]
                z = f_ref[bb, rows, ks]
                g = _logaddexp(log_lb_all[:, ks], log_1mlb_all[:, ks] + _log_sigmoid(z))
                k = (1.0 - lb) * (1.0 / (1.0 + jnp.exp(z)))
                o = _lin_attn_chunk(q_ref[bb, rows, ks], k, v_ref[bb, rows, vs], g,
                                    st_scr.at[ch], b_scr.at[ch], g_scr.at[ch], consts)
                out = _lin_attn_finish(o, gz_ref[bb, rows, vs], nw)
                o_ref[bb, rows, vs] = out.astype(o_ref.dtype)
        return carry

    lax.fori_loop(0, ts // c, body, 0)


def _lin_scratch(chains, dv, dk):
    return [pltpu.VMEM((chains, dv, dk), F32),
            pltpu.VMEM((chains, LIN_CHUNK, dk), F32),
            pltpu.VMEM((chains, LIN_CHUNK + 16, dk), F32)]


def _gla_branch(proj, small, w_up, b_up, norm_w):
    b, s, _ = proj.shape
    ts = min(512, s)
    dk, dv, h = GLA_DK, GLA_DV, GLA_HEADS
    hp = LIN_HEADS_PER_STEP
    kw, vw = hp * dk, hp * dv
    w_up_pad = jnp.pad(w_up, ((0, 128 - GLA_RANK), (0, 0)))
    return pl.pallas_call(
        _gla_kernel,
        out_shape=jax.ShapeDtypeStruct((b, s, h * dv), BF16),
        grid=(h // hp, s // ts),
        in_specs=[
            pl.BlockSpec((b, ts, kw), lambda hi, i: (0, i, hi)),
            pl.BlockSpec((b, ts, kw), lambda hi, i: (0, i, h // hp + hi)),
            pl.BlockSpec((b, ts, vw), lambda hi, i: (0, i, h // hp + hi)),
            pl.BlockSpec((b, ts, vw), lambda hi, i: (0, i, 2 * (h // hp) + hi)),
            pl.BlockSpec((b, ts, 128), lambda hi, i: (0, i, 0)),
            pl.BlockSpec((128, kw), lambda hi, i: (0, hi)),
            pl.BlockSpec((1, kw), lambda hi, i: (0, hi)),
            pl.BlockSpec((1, dv), lambda hi, i: (0, 0)),
        ],
        out_specs=pl.BlockSpec((b, ts, vw), lambda hi, i: (0, i, hi)),
        scratch_shapes=_lin_scratch(b * hp, dv, dk),
        compiler_params=_cparams(("parallel", "arbitrary")),
        name="gla",
    )(proj, proj, proj, proj, small, w_up_pad, b_up.reshape(1, -1), norm_w.reshape(1, dv))


def _hgrn_branch(proj, lb_logits, norm_w, layer):
    b, s, _ = proj.shape
    ts = min(512, s)
    dk, dv, h = HGRN_DK, HGRN_DV, HGRN_HEADS
    hp = LIN_HEADS_PER_STEP
    kw, vw = hp * dk, hp * dv
    ng = h // hp
    depth = lb_logits.shape[0]
    return pl.pallas_call(
        functools.partial(_hgrn_kernel, layer=layer),
        out_shape=jax.ShapeDtypeStruct((b, s, h * dv), BF16),
        grid=(ng, s // ts),
        in_specs=[
            pl.BlockSpec((b, ts, kw), lambda hi, i: (0, i, hi)),
            pl.BlockSpec((b, ts, kw), lambda hi, i: (0, i, ng + hi)),
            pl.BlockSpec((b, ts, vw), lambda hi, i: (0, i, 2 * ng + hi)),
            pl.BlockSpec((b, ts, vw), lambda hi, i: (0, i, 3 * ng + hi)),
            pl.BlockSpec((depth, kw), lambda hi, i: (0, hi)),
            pl.BlockSpec((1, dv), lambda hi, i: (0, 0)),
        ],
        out_specs=pl.BlockSpec((b, ts, vw), lambda hi, i: (0, i, hi)),
        scratch_shapes=_lin_scratch(b * hp, dv, dk),
        compiler_params=_cparams(("parallel", "arbitrary")),
        name="hgrn2",
    )(proj, proj, proj, proj, lb_logits, norm_w.reshape(1, dv))


def _ssd_kernel(z_ref, x_ref, bc_ref, dt_ref, cw_ref, cb_ref, dtb_ref, alog_ref, dexp_ref, nw_ref,
                hexp_ref, o_ref, stage_scr, xbc_scr, st_scr, y_scr):
    c = x_ref.shape[1]
    nch = SSM_CONV_CH
    width = SSM_WIDTH
    n = SSM_STATE
    pairs = SSM_HEADS // 2

    @pl.when(pl.program_id(1) == 0)
    def _():
        st_scr[...] = jnp.zeros_like(st_scr)
        stage_scr[pl.ds(0, 8), :] = jnp.zeros((8, nch), F32)

    stage_scr[pl.ds(8, c), 0:width] = x_ref[0]
    stage_scr[pl.ds(8, c), width:nch] = bc_ref[0]
    acc = cb_ref[...] + cw_ref[0:1, :] * stage_scr[pl.ds(5, c), :]
    for kk in range(1, SSM_CONV):
        acc = acc + cw_ref[kk:kk + 1, :] * stage_scr[pl.ds(5 + kk, c), :]
    stage_scr[pl.ds(0, 8), :] = stage_scr[pl.ds(c, 8), :]
    xbc_scr[...] = _silu(acc)

    dt = _softplus(dt_ref[0] + dtb_ref[...])
    a = -jnp.exp(alog_ref[...])
    da = dt * a
    ri = lax.broadcasted_iota(jnp.int32, (c, c), 0)
    ci = lax.broadcasted_iota(jnp.int32, (c, c), 1)
    causal = ri >= ci
    tri = jnp.where(causal, 1.0, 0.0).astype(BF16)
    a_cs = _sel_dot_left(tri, da)
    a_cs_t = a_cs.T
    hexp = hexp_ref[...]
    dt_x = _sel_dot_right(dt, hexp)
    acs_x = _sel_dot_right(a_cs, hexp)
    a_last_x = acs_x[c - 1:c, :]
    lane = lax.broadcasted_iota(jnp.int32, (c, 2 * SSM_HEADDIM), 1)
    first = lane < SSM_HEADDIM

    for grp in range(SSM_GROUPS):
        bm = xbc_scr[:, width + grp * n: width + (grp + 1) * n]
        cm = xbc_scr[:, width + (SSM_GROUPS + grp) * n: width + (SSM_GROUPS + grp + 1) * n]
        bm16 = bm.astype(BF16)
        cm16 = cm.astype(BF16)
        cb = _dot_nt(cm16, bm16)
        for pp in range(pairs // SSM_GROUPS):
            p = grp * (pairs // SSM_GROUPS) + pp
            lo = p * 2 * SSM_HEADDIM
            hi = lo + 2 * SSM_HEADDIM
            xs = xbc_scr[:, lo:hi]
            xdt = xs * dt_x[:, lo:hi]
            acs = acs_x[:, lo:hi]
            y = jnp.zeros((c, 2 * SSM_HEADDIM), F32)
            for hh in range(2):
                h = 2 * p + hh
                col = jnp.broadcast_to(a_cs[:, h:h + 1], (c, c))
                rw = jnp.broadcast_to(a_cs_t[h:h + 1, :], (c, c))
                lmat = jnp.exp(jnp.where(causal, col - rw, NEG))
                keep = first if hh == 0 else jnp.logical_not(first)
                xh = jnp.where(keep, xdt, 0.0).astype(BF16)
                y = y + _dot((cb * lmat).astype(BF16), xh)
            st = st_scr[p]
            y = y + _dot(cm16, st.astype(BF16)) * jnp.exp(acs)
            decay = jnp.exp(a_last_x[:, lo:hi] - acs)
            st_scr[p] = st * jnp.exp(a_last_x[:, lo:hi]) + _dot_tn(bm16, (xdt * decay).astype(BF16))
            y_scr[:, lo:hi] = y + dexp_ref[:, lo:hi] * xs

    yz = y_scr[...] * _silu(z_ref[0])
    var = jnp.mean(yz * yz, axis=-1, keepdims=True)
    o_ref[0] = (yz * lax.rsqrt(var + EPS) * nw_ref[...]).astype(o_ref.dtype)


def _ssd_branch(proj, small, conv_w, conv_b, dt_bias, a_log, d_skip, norm_w):
    b, s, _ = proj.shape
    c = math.gcd(SSM_CHUNK, s)
    nch = SSM_CONV_CH
    pad16 = lambda v: jnp.pad(v.reshape(1, -1), ((0, 0), (0, 128 - SSM_HEADS)))
    hexp = np.zeros((128, SSM_WIDTH), np.float32)
    for h in range(SSM_HEADS):
        hexp[h, h * SSM_HEADDIM:(h + 1) * SSM_HEADDIM] = 1.0
    d_exp = jnp.repeat(d_skip, SSM_HEADDIM).reshape(1, SSM_WIDTH)
    return pl.pallas_call(
        _ssd_kernel,
        out_shape=jax.ShapeDtypeStruct((b, s, SSM_WIDTH), BF16),
        grid=(b, s // c),
        in_specs=[
            pl.BlockSpec((1, c, SSM_WIDTH), lambda bi, i: (bi, i, 4)),
            pl.BlockSpec((1, c, SSM_WIDTH), lambda bi, i: (bi, i, 5)),
            pl.BlockSpec((1, c, 512), lambda bi, i: (bi, i, 12)),
            pl.BlockSpec((1, c, 128), lambda bi, i: (bi, i, 0)),
            pl.BlockSpec((SSM_CONV, nch), lambda bi, i: (0, 0)),
            pl.BlockSpec((1, nch), lambda bi, i: (0, 0)),
            pl.BlockSpec((1, 128), lambda bi, i: (0, 0)),
            pl.BlockSpec((1, 128), lambda bi, i: (0, 0)),
            pl.BlockSpec((1, SSM_WIDTH), lambda bi, i: (0, 0)),
            pl.BlockSpec((1, SSM_WIDTH), lambda bi, i: (0, 0)),
            pl.BlockSpec((128, SSM_WIDTH), lambda bi, i: (0, 0)),
        ],
        out_specs=pl.BlockSpec((1, c, SSM_WIDTH), lambda bi, i: (bi, i, 0)),
        scratch_shapes=[
            pltpu.VMEM((c + 8, nch), F32),
            pltpu.VMEM((c, nch), F32),
            pltpu.VMEM((SSM_HEADS // 2, SSM_STATE, 2 * SSM_HEADDIM), F32),
            pltpu.VMEM((c, SSM_WIDTH), F32),
        ],
        compiler_params=_cparams(("parallel", "arbitrary")),
        name="ssd",
    )(proj, proj, proj, small, conv_w, conv_b.reshape(1, nch), pad16(dt_bias), pad16(a_log),
      d_exp, norm_w.reshape(1, SSM_WIDTH), jnp.asarray(hexp, dtype=BF16))


def _rope(x, cos, sin_signed):
    return x * cos + pltpu.roll(x, NSA_DH // 2, axis=1) * sin_signed


def _nsa_prep_kernel(q_ref, kc_ref, ks_ref, vs_ref, kw_ref, vw_ref, cos_ref, sin_ref,
                     qo_ref, kco_ref, kso_ref, vso_ref, kwo_ref, vwo_ref):
    cos = cos_ref[...]
    sin = sin_ref[...]
    dh = NSA_DH
    scale = dh ** -0.5
    for h in range(NSA_HEADS):
        sl = slice(h * dh, (h + 1) * dh)
        qo_ref[0, :, sl] = (_rope(q_ref[0, :, sl], cos, sin) * scale).astype(qo_ref.dtype)
    for g in range(NSA_GROUPS):
        sl = slice(g * dh, (g + 1) * dh)
        kco_ref[0, :, sl] = _rope(kc_ref[0, :, sl], cos, sin)
        kso_ref[0, :, sl] = _rope(ks_ref[0, :, sl], cos, sin).astype(kso_ref.dtype)
        kwo_ref[0, :, sl] = _rope(kw_ref[0, :, sl], cos, sin).astype(kwo_ref.dtype)
    tk = NSA_TILE
    for g in range(NSA_GROUPS):
        for r in range(q_ref.shape[1] // tk):
            rows = slice(r * tk, (r + 1) * tk)
            sl = slice(g * dh, (g + 1) * dh)
            vso_ref[0, g, r, 0:dh, :] = vs_ref[0, rows, sl].T.astype(vso_ref.dtype)
            vwo_ref[0, g, r, 0:dh, :] = vw_ref[0, rows, sl].T.astype(vwo_ref.dtype)
            ones = jnp.ones((NSA_ONES, tk), vso_ref.dtype)
            vso_ref[0, g, r, dh:dh + NSA_ONES, :] = ones
            vwo_ref[0, g, r, dh:dh + NSA_ONES, :] = ones


def _nsa_prep(proj):
    b, s, _ = proj.shape
    ts = min(512, s)
    tk = NSA_TILE
    half = NSA_DH // 2
    inv_freq = ROPE_THETA ** (-jnp.arange(half, dtype=F32) / half)
    ang = jnp.arange(s, dtype=F32)[:, None] * inv_freq[None, :]
    cos = jnp.cos(ang)
    sin = jnp.sin(ang)
    cos_full = jnp.concatenate([cos, cos], axis=-1)
    sin_signed = jnp.concatenate([-sin, sin], axis=-1)
    kvw = NSA_GROUPS * NSA_DH
    col = lambda idx: (lambda bi, i: (bi, i, idx))
    kv_spec = lambda idx: pl.BlockSpec((1, ts, kvw), col(idx))
    kv_out = pl.BlockSpec((1, ts, kvw), lambda bi, i: (bi, i, 0))
    vrows = NSA_DH + NSA_ONES
    vt_shape = jax.ShapeDtypeStruct((b, NSA_GROUPS, s // tk, vrows, tk), BF16)
    vt_out = pl.BlockSpec((1, NSA_GROUPS, ts // tk, vrows, tk), lambda bi, i: (bi, 0, i, 0, 0))
    return pl.pallas_call(
        _nsa_prep_kernel,
        out_shape=(jax.ShapeDtypeStruct((b, s, NSA_HEADS * NSA_DH), BF16),
                   jax.ShapeDtypeStruct((b, s, kvw), F32),
                   jax.ShapeDtypeStruct((b, s, kvw), BF16),
                   vt_shape,
                   jax.ShapeDtypeStruct((b, s, kvw), BF16),
                   vt_shape),
        grid=(b, s // ts),
        in_specs=[
            pl.BlockSpec((1, ts, NSA_HEADS * NSA_DH), col(3)),
            kv_spec(16), kv_spec(18), kv_spec(19), kv_spec(20), kv_spec(21),
            pl.BlockSpec((ts, NSA_DH), lambda bi, i: (i, 0)),
            pl.BlockSpec((ts, NSA_DH), lambda bi, i: (i, 0)),
        ],
        out_specs=(pl.BlockSpec((1, ts, NSA_HEADS * NSA_DH), lambda bi, i: (bi, i, 0)),
                   kv_out, kv_out, vt_out, kv_out, vt_out),
        compiler_params=_cparams(("parallel", "parallel")),
        name="nsa_prep",
    )(proj, proj, proj, proj, proj, proj, cos_full, sin_signed)


def _compress_kernel(t_ref, pe_ref, w1_ref, w2_ref, o_ref, sh_scr, *, transposed):
    n = t_ref.shape[1] // CMP_STRIDE
    dh = NSA_DH
    half = CMP_LEN // 2
    acc_a = jnp.zeros((n, CMP_HIDDEN), F32)
    acc_b = jnp.zeros((n, CMP_HIDDEN), F32)
    for l in range(half):
        t = t_ref[0, pl.ds(l, n, stride=CMP_STRIDE), :]
        acc_a = acc_a + _dot((t + pe_ref[l:l + 1, :]).astype(BF16), w1_ref[l * dh:(l + 1) * dh, :])
        acc_b = acc_b + _dot((t + pe_ref[half + l:half + l + 1, :]).astype(BF16),
                             w1_ref[(half + l) * dh:(half + l + 1) * dh, :])
    sh_scr[pl.ds(0, n), :] = acc_b
    sh_scr[pl.ds(n, 8), :] = jnp.zeros((8, CMP_HIDDEN), F32)
    hid = acc_a + sh_scr[pl.ds(1, n), :]
    out = _dot(_silu(hid).astype(BF16), w2_ref[...])
    if transposed:
        tk = NSA_TILE
        for r in range(n // tk):
            o_ref[0, 0, r] = out[r * tk:(r + 1) * tk, :].T.astype(o_ref.dtype)
    else:
        o_ref[0, 0] = out.astype(o_ref.dtype)


def _nsa_compress(src, col0, pe, w1, w2, transposed):
    b, s, _ = src.shape
    n = s // CMP_STRIDE
    base = col0 // NSA_DH
    tk = NSA_TILE
    if transposed:
        out_shape = jax.ShapeDtypeStruct((b, NSA_GROUPS, n // tk, NSA_DH, tk), BF16)
        out_spec = pl.BlockSpec((1, 1, n // tk, NSA_DH, tk), lambda bi, g: (bi, g, 0, 0, 0))
    else:
        out_shape = jax.ShapeDtypeStruct((b, NSA_GROUPS, n, NSA_DH), BF16)
        out_spec = pl.BlockSpec((1, 1, n, NSA_DH), lambda bi, g: (bi, g, 0, 0))
    return pl.pallas_call(
        functools.partial(_compress_kernel, transposed=transposed),
        out_shape=out_shape,
        grid=(b, NSA_GROUPS),
        in_specs=[
            pl.BlockSpec((1, s, NSA_DH), lambda bi, g: (bi, 0, base + g)),
            pl.BlockSpec((CMP_LEN, NSA_DH), lambda bi, g: (0, 0)),
            pl.BlockSpec((CMP_LEN * NSA_DH, CMP_HIDDEN), lambda bi, g: (0, 0)),
            pl.BlockSpec((CMP_HIDDEN, NSA_DH), lambda bi, g: (0, 0)),
        ],
        out_specs=out_spec,
        scratch_shapes=[pltpu.VMEM((n + 8, CMP_HIDDEN), F32)],
        compiler_params=_cparams(("parallel", "parallel")),
        name="nsa_compress",
    )(src, pe, w1.astype(BF16), w2.astype(BF16))


def _weights_bf16(s_tiles, m):
    return [jnp.exp((s - m).astype(BF16)) for s in s_tiles]


def _flash_update(s_tiles, vt_tiles, m_ref, acc_ref):
    m_old = m_ref[...]
    m_new = m_old
    for s in s_tiles:
        m_new = jnp.maximum(m_new, jnp.max(s, axis=0, keepdims=True))
    alpha = jnp.exp(m_old - m_new)
    acc_ref[...] = alpha * acc_ref[...] + _pv(vt_tiles, _weights_bf16(s_tiles, m_new))
    m_ref[...] = m_new


def _pv(vt_tiles, p_tiles):
    vt = jnp.concatenate(vt_tiles, axis=1)
    p = jnp.concatenate([p.astype(BF16) for p in p_tiles], axis=0)
    return _dot(vt, p)


def _softmax_tiles(s_tiles, vt_tiles):
    m = jnp.max(s_tiles[0], axis=0, keepdims=True)
    for s in s_tiles[1:]:
        m = jnp.maximum(m, jnp.max(s, axis=0, keepdims=True))
    p_tiles = [jnp.exp(s - m) for s in s_tiles]
    l = jnp.sum(p_tiles[0], axis=0, keepdims=True)
    for p in p_tiles[1:]:
        l = l + jnp.sum(p, axis=0, keepdims=True)
    return p_tiles, l, _pv(vt_tiles, p_tiles)


def _softmax_tiles_aug(s_tiles, vt_tiles):
    m = jnp.max(s_tiles[0], axis=0, keepdims=True)
    for s in s_tiles[1:]:
        m = jnp.maximum(m, jnp.max(s, axis=0, keepdims=True))
    acc = _pv(vt_tiles, _weights_bf16(s_tiles, m))
    return acc[0:NSA_DH] * (1.0 / acc[NSA_DH:NSA_DH + 1])


def _nsa_kernel(q_ref, kc_ref, vct_ref, ovt_ref, ks_ref, vst_ref, kw_ref, vwt_ref, gate_ref, nz_ref,
                o_ref, m_scr, accs_scr, acc_scr, cap_scr, capd_scr, capl_scr, sa_scr, sb_scr):
    tq = q_ref.shape[1]
    tk = NSA_TILE
    dh = NSA_DH
    hpg = NSA_HPG
    lanes = hpg * tq
    ns = ovt_ref.shape[1]
    qi = pl.program_id(2)
    t0 = qi * tq

    qt = jnp.concatenate([q_ref[0, :, j * dh:(j + 1) * dh].astype(F32).T for j in range(hpg)],
                         axis=1).astype(BF16)
    rowi = lax.broadcasted_iota(jnp.int32, (tk, lanes), 0)
    qpos = lax.broadcasted_iota(jnp.int32, (tk, lanes), 1) & (tq - 1)
    capd_scr[...] = jnp.where(rowi <= qpos, FORCE, NEG)
    capl_scr[...] = jnp.where(rowi > qpos, FORCE, NEG)

    s_tiles = []
    n_ct = kc_ref.shape[2] // tk
    for c in range(n_ct):
        first_end = c * (tk * CMP_STRIDE) + (CMP_LEN - 1) - t0
        cap = jnp.where(rowi * CMP_STRIDE + first_end <= qpos, FORCE, NEG)
        s_tiles.append(jnp.minimum(_dot(kc_ref[0, 0, c * tk:(c + 1) * tk, :], qt), cap))
    p_tiles, l_c, acc_c = _softmax_tiles(s_tiles, [vct_ref[0, 0, c] for c in range(n_ct)])
    imp_un = jnp.zeros((ns, lanes), F32)
    for c, p in enumerate(p_tiles):
        p_hi = p.astype(BF16)
        p_lo = (p - p_hi.astype(F32)).astype(BF16)
        imp_un = imp_un + _dot(ovt_ref[c], p_hi) + _dot(ovt_ref[c], p_lo)
    tl = t0 + (lax.broadcasted_iota(jnp.int32, (1, lanes), 1) & (tq - 1))
    inv_l = jnp.where(tl >= CMP_LEN - 1, 1.0 / l_c, 0.0)
    acc_scr[0] = acc_c * inv_l
    impn = imp_un * inv_l
    imp = impn[:, 0:tq]
    for j in range(1, hpg):
        imp = imp + impn[:, j * tq:(j + 1) * tq]

    blk = lax.broadcasted_iota(jnp.int32, (ns, tq), 0)
    blk_t = jnp.right_shift(t0 + lax.broadcasted_iota(jnp.int32, (ns, tq), 1), SLC_SHIFT)
    score = jnp.where(blk == 0, FORCE,
                      jnp.where(blk == blk_t, FORCE,
                                jnp.where(blk == blk_t - 1, FORCE,
                                          jnp.where(blk <= blk_t, imp, NEG))))
    blk_f = blk.astype(F32)
    cap_sel = jnp.full((ns, tq), NEG, F32)
    for _ in range(min(SLC_TOPK, ns)):
        mx = jnp.max(score, axis=0, keepdims=True)
        first = jnp.min(jnp.where(score == mx, blk_f, float(ns)), axis=0, keepdims=True)
        pick = blk_f == first
        cap_sel = jnp.where(pick, FORCE, cap_sel)
        score = jnp.where(pick, -jnp.inf, score)
    for j in range(hpg):
        cap_scr[:, j * tq:(j + 1) * tq] = cap_sel

    bpt = tk // SLC_BLOCK

    def slc_scores(kt):
        k0 = pl.multiple_of(kt * tk, tk)
        s = _dot(ks_ref[0, pl.ds(k0, tk), :], qt)
        caps = [jnp.broadcast_to(cap_scr[pl.ds(kt * bpt + r, 1), :], (SLC_BLOCK, lanes))
                for r in range(bpt)]
        return jnp.minimum(s, jnp.concatenate(caps, axis=0))

    wt = WINDOW // tk
    s_tiles, vt_tiles = [], []
    for r in range(wt, -1, -1):
        kt = qi - r
        ktc = jnp.maximum(kt, 0)
        s = _dot(kw_ref[0, pl.ds(pl.multiple_of(ktc * tk, tk), tk), :], qt)
        if r == wt:
            s = jnp.minimum(s, capl_scr[...])
        if r == 0:
            s = jnp.minimum(s, capd_scr[...])
        else:
            s = jnp.minimum(s, jnp.where(kt >= 0, FORCE, NEG))
        s_tiles.append(s)
        vt_tiles.append(vwt_ref[0, 0, ktc])
    acc_scr[1] = _softmax_tiles_aug(s_tiles, vt_tiles)

    nb = NSA_TILES_PER_STEP
    m_scr[...] = jnp.full((1, lanes), NEG, F32)
    accs_scr[...] = jnp.zeros_like(accs_scr)
    state = (m_scr, accs_scr)
    last_tile = ks_ref.shape[1] // tk - 1
    n_full = qi // nb

    def score_step(step, dst):
        for r in range(nb):
            dst[r * tk:(r + 1) * tk, :] = slc_scores(step * nb + r)

    def value_step(step, src):
        _flash_update([src[r * tk:(r + 1) * tk, :] for r in range(nb)],
                      [vst_ref[0, 0, step * nb + r] for r in range(nb)], *state)

    @pl.when(n_full > 0)
    def _():
        score_step(0, sa_scr)

    def slc_pair(j, carry):
        score_step(2 * j + 1, sb_scr)
        value_step(2 * j, sa_scr)
        score_step(jnp.minimum(2 * j + 2, n_full - 1), sa_scr)
        value_step(2 * j + 1, sb_scr)
        return carry

    lax.fori_loop(0, n_full // 2, slc_pair, 0)

    @pl.when(n_full % 2 == 1)
    def _():
        value_step(n_full - 1, sa_scr)

    s_tiles, vt_tiles = [], []
    for r in range(nb):
        kt = n_full * nb + r
        ktc = jnp.minimum(kt, last_tile)
        below = jnp.where(kt < qi, FORCE, NEG)
        upto = jnp.where(kt <= qi, FORCE, NEG)
        cap = jnp.minimum(jnp.maximum(capd_scr[...], below), upto)
        s_tiles.append(jnp.minimum(slc_scores(ktc), cap))
        vt_tiles.append(vst_ref[0, 0, ktc])
    _flash_update(s_tiles, vt_tiles, *state)

    o_c = acc_scr[0]
    o_s = accs_scr[0:dh, :] * (1.0 / accs_scr[dh:dh + 1, :])
    o_w = acc_scr[1]
    sgt = _sigmoid(gate_ref[0]).T
    for j in range(hpg):
        ls = slice(j * tq, (j + 1) * tq)
        mix_t = (sgt[3 * j:3 * j + 1, :] * o_c[:, ls] + sgt[3 * j + 1:3 * j + 2, :] * o_s[:, ls]
                 + sgt[3 * j + 2:3 * j + 3, :] * o_w[:, ls])
        sl = slice(j * dh, (j + 1) * dh)
        o_ref[0, :, sl] = (mix_t.T * _silu(nz_ref[0, :, sl])).astype(o_ref.dtype)


def _nsa_attention(qr, kc, vct, ksr, vst, kwr, vwt, small, proj):
    b, s, _ = qr.shape
    tq = tk = NSA_TILE
    assert s % tq == 0 and WINDOW % tk == 0 and tk % SLC_BLOCK == 0
    ncp = kc.shape[2]
    assert ncp % tk == 0
    ns = s // SLC_BLOCK
    gw = NSA_HPG * NSA_DH
    lanes = NSA_HPG * tq
    cs = np.arange(ncp)[None, :] * CMP_STRIDE
    ss = np.arange(ns)[:, None] * SLC_BLOCK
    ovt = ((np.minimum(cs + CMP_LEN, ss + SLC_BLOCK) - np.maximum(cs, ss)) > 0).astype(np.float32)
    ovt[:, ncp - 1] = 0.0
    ovt = ovt.reshape(ns, ncp // tk, tk).transpose(1, 0, 2)
    kv_spec = pl.BlockSpec((1, s, NSA_DH), lambda bi, g, i: (bi, 0, g))
    vrows = NSA_DH + NSA_ONES
    vt_spec = pl.BlockSpec((1, 1, s // tk, vrows, tk), lambda bi, g, i: (bi, g, 0, 0, 0))
    return pl.pallas_call(
        _nsa_kernel,
        out_shape=jax.ShapeDtypeStruct((b, s, NSA_HEADS * NSA_DH), BF16),
        grid=(b, NSA_GROUPS, s // tq),
        in_specs=[
            pl.BlockSpec((1, tq, gw), lambda bi, g, i: (bi, i, g)),
            pl.BlockSpec((1, 1, ncp, NSA_DH), lambda bi, g, i: (bi, g, 0, 0)),
            pl.BlockSpec((1, 1, ncp // tk, NSA_DH, tk), lambda bi, g, i: (bi, g, 0, 0, 0)),
            pl.BlockSpec((ncp // tk, ns, tk), lambda bi, g, i: (0, 0, 0)),
            kv_spec, vt_spec, kv_spec, vt_spec,
            pl.BlockSpec((1, tq, 128), lambda bi, g, i: (bi, i, 1 + g)),
            pl.BlockSpec((1, tq, gw), lambda bi, g, i: (bi, i, 11 + g)),
        ],
        out_specs=pl.BlockSpec((1, tq, gw), lambda bi, g, i: (bi, i, g)),
        scratch_shapes=[
            pltpu.VMEM((1, lanes), F32),
            pltpu.VMEM((vrows, lanes), F32),
            pltpu.VMEM((2, NSA_DH, lanes), F32),
            pltpu.VMEM((ns, lanes), F32),
            pltpu.VMEM((tk, lanes), F32),
            pltpu.VMEM((tk, lanes), F32),
            pltpu.VMEM((NSA_TILES_PER_STEP * tk, lanes), F32),
            pltpu.VMEM((NSA_TILES_PER_STEP * tk, lanes), F32),
        ],
        compiler_params=_cparams(("parallel", "parallel", "arbitrary")),
        name="nsa_attention",
    )(qr, kc, vct, jnp.asarray(ovt, dtype=BF16), ksr, vst, kwr, vwt, small, proj)


def _nsa_branch(proj, small, pe_k, w1_k, w2_k, pe_v, w1_v, w2_v):
    qr, kcr, ksr, vst, kwr, vwt = _nsa_prep(proj)
    kc = _nsa_compress(kcr, 0, pe_k, w1_k, w2_k, transposed=False)
    vct = _nsa_compress(proj, 4352, pe_v, w1_v, w2_v, transposed=True)
    return _nsa_attention(qr, kc, vct, ksr, vst, kwr, vwt, small, proj)


def _even_weights(w_in):
    main = jnp.concatenate([w_in[:, 0:2048], w_in[:, 2064:5648], w_in[:, 5672:6696]], axis=1)
    d = w_in.shape[0]
    gates = w_in[:, 5648:5672]
    per = NSA_HPG * 3
    blocks = [jnp.pad(w_in[:, 2048:2064], ((0, 0), (0, 128 - GLA_RANK)))]
    for g in range(NSA_GROUPS):
        blocks.append(jnp.pad(gates[:, g * per:(g + 1) * per], ((0, 0), (0, 128 - per))))
    small = jnp.concatenate(blocks, axis=1)
    return main.astype(BF16), small.astype(BF16)


def _odd_weights(w_in):
    main = w_in[:, 0:MAIN_COLS]
    small = jnp.pad(w_in[:, MAIN_COLS:MAIN_COLS + SSM_HEADS], ((0, 0), (0, 128 - SSM_HEADS)))
    return main.astype(BF16), small.astype(BF16)


def kernel(x, c, ada_w, ada_b, pre_norm_w, post_norm_w, even_w_in, even_w_out, gla_w_up, gla_b_up,
           gla_norm_w, nsa_pe_k, nsa_w1_k, nsa_w2_k, nsa_pe_v, nsa_w1_v, nsa_w2_v, odd_w_in,
           odd_w_out, hgrn_lb_logits, hgrn_norm_w, ssm_conv_w, ssm_conv_b, ssm_dt_bias, ssm_a_log,
           ssm_d, ssm_norm_w):
    depth = ada_w.shape[0]
    mod = _adaln_mod(c, ada_w, ada_b)
    for l in range(depth):
        if l % 2 == 0:
            e = l // 2
            w_main, w_small = _even_weights(even_w_in[e])
            proj, small = _in_proj(x, mod[l], pre_norm_w[l], w_main, w_small)
            o_a = _gla_branch(proj, small, gla_w_up[e], gla_b_up[e], gla_norm_w[e])
            o_b = _nsa_branch(proj, small, nsa_pe_k[e], nsa_w1_k[e], nsa_w2_k[e],
                              nsa_pe_v[e], nsa_w1_v[e], nsa_w2_v[e])
            w_out = even_w_out[e]
        else:
            o = l // 2
            w_main, w_small = _odd_weights(odd_w_in[o])
            proj, small = _in_proj(x, mod[l], pre_norm_w[l], w_main, w_small)
            o_a = _hgrn_branch(proj, hgrn_lb_logits, hgrn_norm_w[o], l)
            o_b = _ssd_branch(proj, small, ssm_conv_w[o], ssm_conv_b[o], ssm_dt_bias[o],
                              ssm_a_log[o], ssm_d[o], ssm_norm_w[o])
            w_out = odd_w_out[o]
        x = _out_proj(o_a, o_b, w_out.astype(BF16), x, mod[l], post_norm_w[l])
    return x
```

```python
import functools
import math

import jax
import jax.numpy as jnp
import numpy as np
from jax import lax
from jax.experimental import pallas as pl
from jax.experimental.pallas import tpu as pltpu

F32 = jnp.float32
BF16 = jnp.bfloat16

D_MODEL = 2048
EPS = 1e-6
NEG = -1e30
FORCE = 1e30
ROPE_THETA = 10000.0

GLA_HEADS = 4
GLA_DK = 128
GLA_DV = 256
GLA_RANK = 16
GLA_TAU = 16.0

NSA_DH = 128
NSA_HEADS = 8
NSA_GROUPS = 2
NSA_HPG = 4
CMP_LEN = 32
CMP_STRIDE = 16
CMP_HIDDEN = 256
SLC_BLOCK = 64
SLC_SHIFT = 6
SLC_TOPK = 16
WINDOW = 512
NSA_TILE = 128
NSA_QUERY_TILE = 256
NSA_ONES = 16
NSA_TILES_PER_STEP = 4

HGRN_HEADS = 8
HGRN_DK = 128
HGRN_DV = 128

SSM_HEADDIM = 64
SSM_HEADS = 16
SSM_GROUPS = 2
SSM_STATE = 128
SSM_CONV = 4
SSM_CHUNK = 256
SSM_WIDTH = 1024
SSM_CONV_CH = SSM_WIDTH + 2 * SSM_GROUPS * SSM_STATE

LIN_CHUNK = 128
MAIN_COLS = 6656
VMEM_LIMIT = 56 * 1024 * 1024


def _cparams(sem):
    return pltpu.CompilerParams(dimension_semantics=sem, vmem_limit_bytes=VMEM_LIMIT)


def _dot(a, b):
    return jnp.dot(a, b, preferred_element_type=F32)


def _dot_nt(a, b):
    return lax.dot_general(a, b, (((1,), (1,)), ((), ())), preferred_element_type=F32)


def _dot_tn(a, b):
    return lax.dot_general(a, b, (((0,), (0,)), ((), ())), preferred_element_type=F32)


def _split3(x):
    hi = x.astype(BF16)
    r = x - hi.astype(F32)
    mid = r.astype(BF16)
    lo = (r - mid.astype(F32)).astype(BF16)
    return hi, mid, lo


def _sel_dot_left(sel, x):
    n = x.shape[1]
    y = _dot(sel.astype(BF16), jnp.concatenate(_split3(x), axis=1))
    return y[:, 0:n] + y[:, n:2 * n] + y[:, 2 * n:3 * n]


def _sel_dot_right(x, sel):
    m = x.shape[0]
    y = _dot(jnp.concatenate(_split3(x), axis=0), sel.astype(BF16))
    return y[0:m] + y[m:2 * m] + y[2 * m:3 * m]


def _sigmoid(x):
    return 1.0 / (1.0 + jnp.exp(-x))


def _silu(x):
    return x * _sigmoid(x)


def _log1p_exp_neg_abs(x):
    return jnp.log(1.0 + jnp.exp(-jnp.abs(x)))


def _log_sigmoid(x):
    return jnp.minimum(x, 0.0) - _log1p_exp_neg_abs(x)


def _softplus(x):
    return jnp.maximum(x, 0.0) + _log1p_exp_neg_abs(x)


def _logaddexp(a, b):
    return jnp.maximum(a, b) + _log1p_exp_neg_abs(a - b)


def _mod_kernel(c_ref, w_ref, b_ref, o_ref):
    ca = _silu(c_ref[...])
    o_ref[0] = _dot(ca, w_ref[0]) + b_ref[0]


def _adaln_mod(c, ada_w, ada_b):
    depth, d, n3 = ada_w.shape
    b = c.shape[0]
    rows = 8
    c_pad = jnp.pad(c, ((0, rows - b), (0, 0)))
    tn = 768
    out = pl.pallas_call(
        _mod_kernel,
        out_shape=jax.ShapeDtypeStruct((depth, rows, n3), F32),
        grid=(depth, n3 // tn),
        in_specs=[
            pl.BlockSpec((rows, d), lambda l, j: (0, 0)),
            pl.BlockSpec((1, d, tn), lambda l, j: (l, 0, j)),
            pl.BlockSpec((1, 1, tn), lambda l, j: (l, 0, j)),
        ],
        out_specs=pl.BlockSpec((1, rows, tn), lambda l, j: (l, 0, j)),
        compiler_params=_cparams(("parallel", "parallel")),
        name="adaln_mod",
    )(c_pad, ada_w, ada_b.reshape(depth, 1, n3))
    return out[:, :b]


def _inproj_kernel(x_ref, mod_ref, nw_ref, w_ref, ws_ref, o_ref, os_ref, h_scr):
    d = x_ref.shape[-1]

    @pl.when(pl.program_id(2) == 0)
    def _():
        x = x_ref[0]
        var = jnp.mean(x * x, axis=-1, keepdims=True)
        y = x * lax.rsqrt(var + EPS) * nw_ref[...]
        shift = mod_ref[0, :, 0:d]
        scale = mod_ref[0, :, d:2 * d]
        hb = (y * (1.0 + scale) + shift).astype(BF16)
        h_scr[...] = hb
        os_ref[0] = _dot(hb, ws_ref[...])

    o_ref[0] = _dot(h_scr[...], w_ref[...])


def _in_proj(x, mod_l, norm_w, w_main, w_small):
    b, s, d = x.shape
    n = w_main.shape[1]
    ns = w_small.shape[1]
    tm = min(1024, s)
    tn = 512
    return pl.pallas_call(
        _inproj_kernel,
        out_shape=(jax.ShapeDtypeStruct((b, s, n), F32),
                   jax.ShapeDtypeStruct((b, s, ns), F32)),
        grid=(b, s // tm, n // tn),
        in_specs=[
            pl.BlockSpec((1, tm, d), lambda bi, i, j: (bi, i, 0)),
            pl.BlockSpec((1, 1, 3 * d), lambda bi, i, j: (bi, 0, 0)),
            pl.BlockSpec((1, d), lambda bi, i, j: (0, 0)),
            pl.BlockSpec((d, tn), lambda bi, i, j: (0, j)),
            pl.BlockSpec((d, ns), lambda bi, i, j: (0, 0)),
        ],
        out_specs=(pl.BlockSpec((1, tm, tn), lambda bi, i, j: (bi, i, j)),
                   pl.BlockSpec((1, tm, ns), lambda bi, i, j: (bi, i, 0))),
        scratch_shapes=[pltpu.VMEM((tm, d), BF16)],
        compiler_params=_cparams(("parallel", "parallel", "arbitrary")),
        name="in_proj",
    )(x, mod_l.reshape(b, 1, 3 * d), norm_w.reshape(1, d), w_main, w_small)


def _outproj_kernel(a1_ref, a2_ref, w_ref, x_ref, mod_ref, nw_ref, o_ref):
    d = x_ref.shape[-1]
    half = a1_ref.shape[-1]
    y = _dot(a1_ref[0], w_ref[0:half, :]) + _dot(a2_ref[0], w_ref[half:2 * half, :])
    var = jnp.mean(y * y, axis=-1, keepdims=True)
    yn = y * lax.rsqrt(var + EPS) * nw_ref[...]
    gate = mod_ref[0, :, 2 * d:3 * d]
    o_ref[0] = x_ref[0] + gate * yn


def _out_proj(a1, a2, w_out, x, mod_l, norm_w):
    b, s, d = x.shape
    half = a1.shape[-1]
    tm = min(512, s)
    return pl.pallas_call(
        _outproj_kernel,
        out_shape=jax.ShapeDtypeStruct((b, s, d), F32),
        grid=(b, s // tm),
        in_specs=[
            pl.BlockSpec((1, tm, half), lambda bi, i: (bi, i, 0)),
            pl.BlockSpec((1, tm, half), lambda bi, i: (bi, i, 0)),
            pl.BlockSpec((2 * half, d), lambda bi, i: (0, 0)),
            pl.BlockSpec((1, tm, d), lambda bi, i: (bi, i, 0)),
            pl.BlockSpec((1, 1, 3 * d), lambda bi, i: (bi, 0, 0)),
            pl.BlockSpec((1, d), lambda bi, i: (0, 0)),
        ],
        out_specs=pl.BlockSpec((1, tm, d), lambda bi, i: (bi, i, 0)),
        compiler_params=_cparams(("parallel", "parallel")),
        name="out_proj",
    )(a1, a2, w_out, x, mod_l.reshape(b, 1, 3 * d), norm_w.reshape(1, d))


def _cumsum_chains(tri, gs):
    n = gs[0].shape[1]
    terms = []
    for g in gs:
        terms.extend(_split3(g))
    y = _dot(tri, jnp.concatenate(terms, axis=1))
    return [y[:, (3 * i) * n:(3 * i + 1) * n] + y[:, (3 * i + 1) * n:(3 * i + 2) * n]
            + y[:, (3 * i + 2) * n:(3 * i + 3) * n] for i in range(len(gs))]


def _lin_attn_chunk(q, k, v, g, b, st_scr, b_scr, g_scr, consts):
    c, dk = q.shape
    row, eye, tri, pairs = consts
    b_scr[...] = b
    g_scr[pl.ds(8, c), :] = g
    b_last = b_scr[pl.ds(c - 1, 1), :]

    def level_operand(s, expo):
        upper = (row & s) != 0
        return (jnp.where(upper, q, k) * jnp.exp(expo)).astype(BF16)

    items = []
    for s in [c >> i for i in range(1, c.bit_length() - 2)]:
        pieces = [jnp.broadcast_to(b_scr[pl.ds(p * 2 * s + s - 1, 1), :], (2 * s, dk))
                  for p in range(c // (2 * s))]
        d = b - jnp.concatenate(pieces, axis=0)
        x = level_operand(s, jnp.where((row & s) != 0, d, -d))
        items.append((x, x, pairs[s]))
    g_dn = g_scr[pl.ds(7, c), :]
    g_up = g_scr[pl.ds(9, c), :]
    r4 = row & 3
    x = level_operand(2, jnp.where(r4 == 2, g, jnp.where(r4 == 3, g + g_dn, jnp.where(r4 == 0, g_up, 0.0))))
    items.append((x, x, pairs[2]))
    x = level_operand(1, jnp.where((row & 1) != 0, g, 0.0))
    items.append((x, x, pairs[1]))
    items.append((q.astype(BF16), k.astype(BF16), eye))

    attn = jnp.zeros((c, c), F32)
    for i in range(0, len(items), 2):
        grp = items[i:i + 2]
        prod = _dot_nt(jnp.concatenate([it[0] for it in grp], axis=0),
                       jnp.concatenate([it[1] for it in grp], axis=0))
        for n, it in enumerate(grp):
            attn = jnp.where(it[2], prod[n * c:(n + 1) * c, n * c:(n + 1) * c], attn)

    v16 = v.astype(BF16)
    st = st_scr[...]
    o = _dot(jnp.concatenate([attn.astype(BF16), (q * jnp.exp(b)).astype(BF16)], axis=1),
             jnp.concatenate([v16, st.astype(BF16)], axis=0))
    k_dec = (k * jnp.exp(b_last - b)).astype(BF16)
    e_col = jnp.broadcast_to(jnp.exp(b_last), (dk, dk)).T
    e_col = jnp.concatenate([e_col] * (v.shape[1] // dk), axis=1)
    st_scr[...] = st * e_col + _dot_tn(k_dec, v16)
    return o


def _lin_attn_consts(c, dk):
    row = lax.broadcasted_iota(jnp.int32, (c, dk), 0)
    ri = lax.broadcasted_iota(jnp.int32, (c, c), 0)
    ci = lax.broadcasted_iota(jnp.int32, (c, c), 1)
    eye = ri == ci
    tri = jnp.where(ri >= ci, 1.0, 0.0).astype(BF16)
    pairs = {s: jnp.where((ri & -(2 * s)) == (ci & -(2 * s)), (ri & s) - (ci & s), 0) == s
             for s in [c >> i for i in range(1, c.bit_length())]}
    return row, eye, tri, pairs


def _lin_attn_finish(o, gz, nw):
    var = jnp.mean(o * o, axis=-1, keepdims=True)
    return (o * lax.rsqrt(var + EPS) * nw) * _silu(gz)


LIN_HEADS_PER_STEP = 4


def _gla_kernel(q_ref, k_ref, v_ref, gz_ref, glr_ref, wup_ref, bup_ref, nw_ref, o_ref,
                st_scr, b_scr, g_scr):
    c = LIN_CHUNK
    dk, dv = GLA_DK, GLA_DV
    nb, ts = q_ref.shape[0], q_ref.shape[1]
    hp = q_ref.shape[2] // dk

    @pl.when(pl.program_id(1) == 0)
    def _():
        st_scr[...] = jnp.zeros_like(st_scr)

    g_scr[...] = jnp.zeros_like(g_scr)
    consts = _lin_attn_consts(c, dk)
    nw = nw_ref[...]
    q_scale = dk ** -0.5

    def body(ci, carry):
        r0 = pl.multiple_of(ci * c, c)
        rows = pl.ds(r0, c)
        gs = []
        for bb in range(nb):
            z = _dot(glr_ref[bb, rows, :], wup_ref[...]) + bup_ref[...]
            g_all = _log_sigmoid(z) * (1.0 / GLA_TAU)
            gs.extend(g_all[:, hh * dk:(hh + 1) * dk] for hh in range(hp))
        bs = _cumsum_chains(consts[2], gs)
        for bb in range(nb):
            for hh in range(hp):
                ks = slice(hh * dk, (hh + 1) * dk)
                vs = slice(hh * dv, (hh + 1) * dv)
                ch = bb * hp + hh
                q = q_ref[bb, rows, ks] * q_scale
                o = _lin_attn_chunk(q, k_ref[bb, rows, ks], v_ref[bb, rows, vs], gs[ch], bs[ch],
                                    st_scr.at[ch], b_scr.at[ch], g_scr.at[ch], consts)
                out = _lin_attn_finish(o, gz_ref[bb, rows, vs], nw)
                o_ref[bb, rows, vs] = out.astype(o_ref.dtype)
        return carry

    lax.fori_loop(0, ts // c, body, 0)


def _hgrn_kernel(q_ref, f_ref, v_ref, gz_ref, lbl_ref, nw_ref, o_ref, st_scr, b_scr, g_scr,
                 *, layer):
    c = LIN_CHUNK
    dk, dv = HGRN_DK, HGRN_DV
    nb, ts = q_ref.shape[0], q_ref.shape[1]
    hp = q_ref.shape[2] // dk

    @pl.when(pl.program_id(1) == 0)
    def _():
        st_scr[...] = jnp.zeros_like(st_scr)

    g_scr[...] = jnp.zeros_like(g_scr)
    consts = _lin_attn_consts(c, dk)
    nw = nw_ref[...]

    logits = lbl_ref[...]
    depth = logits.shape[0]
    mx = logits[0:1, :]
    for r in range(1, depth):
        mx = jnp.maximum(mx, logits[r:r + 1, :])
    ex = [jnp.exp(logits[r:r + 1, :] - mx) for r in range(depth)]
    den = ex[0]
    for r in range(1, depth):
        den = den + ex[r]
    sm = [e / den for e in ex]
    lb_all = sm[0]
    for r in range(1, layer + 1):
        lb_all = lb_all + sm[r]
    lb_all = lb_all - sm[0]
    log_lb_all = jnp.log(lb_all)
    log_1mlb_all = jnp.log1p(-lb_all)

    def body(ci, carry):
        r0 = pl.multiple_of(ci * c, c)
        rows = pl.ds(r0, c)
        gs, kk = [], []
        for bb in range(nb):
            for hh in range(hp):
                ks = slice(hh * dk, (hh + 1) * dk)
                z = f_ref[bb, rows, ks]
                gs.append(_logaddexp(log_lb_all[:, ks], log_1mlb_all[:, ks] + _log_sigmoid(z)))
                kk.append((1.0 - lb_all[:, ks]) * (1.0 / (1.0 + jnp.exp(z))))
        bs = _cumsum_chains(consts[2], gs)
        for bb in range(nb):
            for hh in range(hp):
                ks = slice(hh * dk, (hh + 1) * dk)
                vs = slice(hh * dv, (hh + 1) * dv)
                ch = bb * hp + hh
                o = _lin_attn_chunk(q_ref[bb, rows, ks], kk[ch], v_ref[bb, rows, vs], gs[ch], bs[ch],
                                    st_scr.at[ch], b_scr.at[ch], g_scr.at[ch], consts)
                out = _lin_attn_finish(o, gz_ref[bb, rows, vs], nw)
                o_ref[bb, rows, vs] = out.astype(o_ref.dtype)
        return carry

    lax.fori_loop(0, ts // c, body, 0)


def _lin_scratch(chains, dv, dk):
    return [pltpu.VMEM((chains, dk, dv), F32),
            pltpu.VMEM((chains, LIN_CHUNK, dk), F32),
            pltpu.VMEM((chains, LIN_CHUNK + 16, dk), F32)]


def _gla_branch(proj, small, w_up, b_up, norm_w):
    b, s, _ = proj.shape
    ts = min(512, s)
    dk, dv, h = GLA_DK, GLA_DV, GLA_HEADS
    hp = LIN_HEADS_PER_STEP
    kw, vw = hp * dk, hp * dv
    w_up_pad = jnp.pad(w_up, ((0, 128 - GLA_RANK), (0, 0)))
    return pl.pallas_call(
        _gla_kernel,
        out_shape=jax.ShapeDtypeStruct((b, s, h * dv), BF16),
        grid=(h // hp, s // ts),
        in_specs=[
            pl.BlockSpec((b, ts, kw), lambda hi, i: (0, i, hi)),
            pl.BlockSpec((b, ts, kw), lambda hi, i: (0, i, h // hp + hi)),
            pl.BlockSpec((b, ts, vw), lambda hi, i: (0, i, h // hp + hi)),
            pl.BlockSpec((b, ts, vw), lambda hi, i: (0, i, 2 * (h // hp) + hi)),
            pl.BlockSpec((b, ts, 128), lambda hi, i: (0, i, 0)),
            pl.BlockSpec((128, kw), lambda hi, i: (0, hi)),
            pl.BlockSpec((1, kw), lambda hi, i: (0, hi)),
            pl.BlockSpec((1, dv), lambda hi, i: (0, 0)),
        ],
        out_specs=pl.BlockSpec((b, ts, vw), lambda hi, i: (0, i, hi)),
        scratch_shapes=_lin_scratch(b * hp, dv, dk),
        compiler_params=_cparams(("parallel", "arbitrary")),
        name="gla",
    )(proj, proj, proj, proj, small, w_up_pad, b_up.reshape(1, -1), norm_w.reshape(1, dv))


def _hgrn_branch(proj, lb_logits, norm_w, layer):
    b, s, _ = proj.shape
    ts = min(512, s)
    dk, dv, h = HGRN_DK, HGRN_DV, HGRN_HEADS
    hp = LIN_HEADS_PER_STEP
    kw, vw = hp * dk, hp * dv
    ng = h // hp
    depth = lb_logits.shape[0]
    return pl.pallas_call(
        functools.partial(_hgrn_kernel, layer=layer),
        out_shape=jax.ShapeDtypeStruct((b, s, h * dv), BF16),
        grid=(ng, s // ts),
        in_specs=[
            pl.BlockSpec((b, ts, kw), lambda hi, i: (0, i, hi)),
            pl.BlockSpec((b, ts, kw), lambda hi, i: (0, i, ng + hi)),
            pl.BlockSpec((b, ts, vw), lambda hi, i: (0, i, 2 * ng + hi)),
            pl.BlockSpec((b, ts, vw), lambda hi, i: (0, i, 3 * ng + hi)),
            pl.BlockSpec((depth, kw), lambda hi, i: (0, hi)),
            pl.BlockSpec((1, dv), lambda hi, i: (0, 0)),
        ],
        out_specs=pl.BlockSpec((b, ts, vw), lambda hi, i: (0, i, hi)),
        scratch_shapes=_lin_scratch(b * hp, dv, dk),
        compiler_params=_cparams(("parallel", "arbitrary")),
        name="hgrn2",
    )(proj, proj, proj, proj, lb_logits, norm_w.reshape(1, dv))


def _ssd_kernel(z_ref, x_ref, bc_ref, dt_ref, cw_ref, cb_ref, dtb_ref, alog_ref, dexp_ref, nw_ref,
                hexp_ref, o_ref, stage_scr, xbc_scr, st_scr, y_scr):
    c = x_ref.shape[1]
    nch = SSM_CONV_CH
    width = SSM_WIDTH
    n = SSM_STATE
    pairs = SSM_HEADS // 2

    @pl.when(pl.program_id(1) == 0)
    def _():
        st_scr[...] = jnp.zeros_like(st_scr)
        stage_scr[pl.ds(0, 8), :] = jnp.zeros((8, nch), F32)

    stage_scr[pl.ds(8, c), 0:width] = x_ref[0]
    stage_scr[pl.ds(8, c), width:nch] = bc_ref[0]
    acc = cb_ref[...] + cw_ref[0:1, :] * stage_scr[pl.ds(5, c), :]
    for kk in range(1, SSM_CONV):
        acc = acc + cw_ref[kk:kk + 1, :] * stage_scr[pl.ds(5 + kk, c), :]
    stage_scr[pl.ds(0, 8), :] = stage_scr[pl.ds(c, 8), :]
    xbc_scr[...] = _silu(acc)

    dt = _softplus(dt_ref[0] + dtb_ref[...])
    a = -jnp.exp(alog_ref[...])
    da = dt * a
    ri = lax.broadcasted_iota(jnp.int32, (c, c), 0)
    ci = lax.broadcasted_iota(jnp.int32, (c, c), 1)
    causal = ri >= ci
    tri = jnp.where(causal, 1.0, 0.0).astype(BF16)
    a_cs = _sel_dot_left(tri, da)
    a_cs_t = a_cs.T
    hexp = hexp_ref[...]
    dt_x = _sel_dot_right(dt, hexp)
    acs_x = _sel_dot_right(a_cs, hexp)
    a_last_x = acs_x[c - 1:c, :]
    lane = lax.broadcasted_iota(jnp.int32, (c, 2 * SSM_HEADDIM), 1)
    first = lane < SSM_HEADDIM

    for grp in range(SSM_GROUPS):
        bm = xbc_scr[:, width + grp * n: width + (grp + 1) * n]
        cm = xbc_scr[:, width + (SSM_GROUPS + grp) * n: width + (SSM_GROUPS + grp + 1) * n]
        bm16 = bm.astype(BF16)
        cm16 = cm.astype(BF16)
        cb = _dot_nt(cm16, bm16)
        for pp in range(pairs // SSM_GROUPS):
            p = grp * (pairs // SSM_GROUPS) + pp
            lo = p * 2 * SSM_HEADDIM
            hi = lo + 2 * SSM_HEADDIM
            xs = xbc_scr[:, lo:hi]
            xdt = xs * dt_x[:, lo:hi]
            acs = acs_x[:, lo:hi]
            y = jnp.zeros((c, 2 * SSM_HEADDIM), F32)
            for hh in range(2):
                h = 2 * p + hh
                col = jnp.broadcast_to(a_cs[:, h:h + 1], (c, c))
                rw = jnp.broadcast_to(a_cs_t[h:h + 1, :], (c, c))
                lmat = jnp.exp(jnp.where(causal, col - rw, NEG))
                keep = first if hh == 0 else jnp.logical_not(first)
                xh = jnp.where(keep, xdt, 0.0).astype(BF16)
                y = y + _dot((cb * lmat).astype(BF16), xh)
            st = st_scr[p]
            y = y + _dot(cm16, st.astype(BF16)) * jnp.exp(acs)
            decay = jnp.exp(a_last_x[:, lo:hi] - acs)
            st_scr[p] = st * jnp.exp(a_last_x[:, lo:hi]) + _dot_tn(bm16, (xdt * decay).astype(BF16))
            y_scr[:, lo:hi] = y + dexp_ref[:, lo:hi] * xs

    yz = y_scr[...] * _silu(z_ref[0])
    var = jnp.mean(yz * yz, axis=-1, keepdims=True)
    o_ref[0] = (yz * lax.rsqrt(var + EPS) * nw_ref[...]).astype(o_ref.dtype)


def _ssd_branch(proj, small, conv_w, conv_b, dt_bias, a_log, d_skip, norm_w):
    b, s, _ = proj.shape
    c = math.gcd(SSM_CHUNK, s)
    nch = SSM_CONV_CH
    pad16 = lambda v: jnp.pad(v.reshape(1, -1), ((0, 0), (0, 128 - SSM_HEADS)))
    hexp = np.zeros((128, SSM_WIDTH), np.float32)
    for h in range(SSM_HEADS):
        hexp[h, h * SSM_HEADDIM:(h + 1) * SSM_HEADDIM] = 1.0
    d_exp = jnp.repeat(d_skip, SSM_HEADDIM).reshape(1, SSM_WIDTH)
    return pl.pallas_call(
        _ssd_kernel,
        out_shape=jax.ShapeDtypeStruct((b, s, SSM_WIDTH), BF16),
        grid=(b, s // c),
        in_specs=[
            pl.BlockSpec((1, c, SSM_WIDTH), lambda bi, i: (bi, i, 4)),
            pl.BlockSpec((1, c, SSM_WIDTH), lambda bi, i: (bi, i, 5)),
            pl.BlockSpec((1, c, 512), lambda bi, i: (bi, i, 12)),
            pl.BlockSpec((1, c, 128), lambda bi, i: (bi, i, 0)),
            pl.BlockSpec((SSM_CONV, nch), lambda bi, i: (0, 0)),
            pl.BlockSpec((1, nch), lambda bi, i: (0, 0)),
            pl.BlockSpec((1, 128), lambda bi, i: (0, 0)),
            pl.BlockSpec((1, 128), lambda bi, i: (0, 0)),
            pl.BlockSpec((1, SSM_WIDTH), lambda bi, i: (0, 0)),
            pl.BlockSpec((1, SSM_WIDTH), lambda bi, i: (0, 0)),
            pl.BlockSpec((128, SSM_WIDTH), lambda bi, i: (0, 0)),
        ],
        out_specs=pl.BlockSpec((1, c, SSM_WIDTH), lambda bi, i: (bi, i, 0)),
        scratch_shapes=[
            pltpu.VMEM((c + 8, nch), F32),
            pltpu.VMEM((c, nch), F32),
            pltpu.VMEM((SSM_HEADS // 2, SSM_STATE, 2 * SSM_HEADDIM), F32),
            pltpu.VMEM((c, SSM_WIDTH), F32),
        ],
        compiler_params=_cparams(("parallel", "arbitrary")),
        name="ssd",
    )(proj, proj, proj, small, conv_w, conv_b.reshape(1, nch), pad16(dt_bias), pad16(a_log),
      d_exp, norm_w.reshape(1, SSM_WIDTH), jnp.asarray(hexp, dtype=BF16))


def _rope(x, cos, sin_signed):
    return x * cos + pltpu.roll(x, NSA_DH // 2, axis=1) * sin_signed


def _nsa_prep_kernel(q_ref, kc_ref, ks_ref, vs_ref, kw_ref, vw_ref, cos_ref, sin_ref,
                     qo_ref, kco_ref, kso_ref, vso_ref, kwo_ref, vwo_ref):
    cos = cos_ref[...]
    sin = sin_ref[...]
    dh = NSA_DH
    scale = dh ** -0.5
    for h in range(NSA_HEADS):
        sl = slice(h * dh, (h + 1) * dh)
        qo_ref[0, :, sl] = (_rope(q_ref[0, :, sl], cos, sin) * scale).astype(qo_ref.dtype)
    for g in range(NSA_GROUPS):
        sl = slice(g * dh, (g + 1) * dh)
        kco_ref[0, :, sl] = _rope(kc_ref[0, :, sl], cos, sin)
        kso_ref[0, :, sl] = _rope(ks_ref[0, :, sl], cos, sin).astype(kso_ref.dtype)
        kwo_ref[0, :, sl] = _rope(kw_ref[0, :, sl], cos, sin).astype(kwo_ref.dtype)
    tk = NSA_TILE
    for g in range(NSA_GROUPS):
        for r in range(q_ref.shape[1] // tk):
            rows = slice(r * tk, (r + 1) * tk)
            sl = slice(g * dh, (g + 1) * dh)
            vso_ref[0, g, r, 0:dh, :] = vs_ref[0, rows, sl].T.astype(vso_ref.dtype)
            vwo_ref[0, g, r, 0:dh, :] = vw_ref[0, rows, sl].T.astype(vwo_ref.dtype)
            ones = jnp.ones((NSA_ONES, tk), vso_ref.dtype)
            vso_ref[0, g, r, dh:dh + NSA_ONES, :] = ones
            vwo_ref[0, g, r, dh:dh + NSA_ONES, :] = ones


def _nsa_prep(proj):
    b, s, _ = proj.shape
    ts = min(512, s)
    tk = NSA_TILE
    half = NSA_DH // 2
    inv_freq = ROPE_THETA ** (-jnp.arange(half, dtype=F32) / half)
    ang = jnp.arange(s, dtype=F32)[:, None] * inv_freq[None, :]
    cos = jnp.cos(ang)
    sin = jnp.sin(ang)
    cos_full = jnp.concatenate([cos, cos], axis=-1)
    sin_signed = jnp.concatenate([-sin, sin], axis=-1)
    kvw = NSA_GROUPS * NSA_DH
    col = lambda idx: (lambda bi, i: (bi, i, idx))
    kv_spec = lambda idx: pl.BlockSpec((1, ts, kvw), col(idx))
    kv_out = pl.BlockSpec((1, ts, kvw), lambda bi, i: (bi, i, 0))
    vrows = NSA_DH + NSA_ONES
    vt_shape = jax.ShapeDtypeStruct((b, NSA_GROUPS, s // tk, vrows, tk), BF16)
    vt_out = pl.BlockSpec((1, NSA_GROUPS, ts // tk, vrows, tk), lambda bi, i: (bi, 0, i, 0, 0))
    return pl.pallas_call(
        _nsa_prep_kernel,
        out_shape=(jax.ShapeDtypeStruct((b, s, NSA_HEADS * NSA_DH), BF16),
                   jax.ShapeDtypeStruct((b, s, kvw), F32),
                   jax.ShapeDtypeStruct((b, s, kvw), BF16),
                   vt_shape,
                   jax.ShapeDtypeStruct((b, s, kvw), BF16),
                   vt_shape),
        grid=(b, s // ts),
        in_specs=[
            pl.BlockSpec((1, ts, NSA_HEADS * NSA_DH), col(3)),
            kv_spec(16), kv_spec(18), kv_spec(19), kv_spec(20), kv_spec(21),
            pl.BlockSpec((ts, NSA_DH), lambda bi, i: (i, 0)),
            pl.BlockSpec((ts, NSA_DH), lambda bi, i: (i, 0)),
        ],
        out_specs=(pl.BlockSpec((1, ts, NSA_HEADS * NSA_DH), lambda bi, i: (bi, i, 0)),
                   kv_out, kv_out, vt_out, kv_out, vt_out),
        compiler_params=_cparams(("parallel", "parallel")),
        name="nsa_prep",
    )(proj, proj, proj, proj, proj, proj, cos_full, sin_signed)


def _compress_kernel(t_ref, pe_ref, w1_ref, w2_ref, o_ref, sh_scr, *, transposed):
    n = t_ref.shape[1] // CMP_STRIDE
    dh = NSA_DH
    half = CMP_LEN // 2
    acc_a = jnp.zeros((n, CMP_HIDDEN), F32)
    acc_b = jnp.zeros((n, CMP_HIDDEN), F32)
    for l in range(half):
        t = t_ref[0, pl.ds(l, n, stride=CMP_STRIDE), :]
        acc_a = acc_a + _dot((t + pe_ref[l:l + 1, :]).astype(BF16), w1_ref[l * dh:(l + 1) * dh, :])
        acc_b = acc_b + _dot((t + pe_ref[half + l:half + l + 1, :]).astype(BF16),
                             w1_ref[(half + l) * dh:(half + l + 1) * dh, :])
    sh_scr[pl.ds(0, n), :] = acc_b
    sh_scr[pl.ds(n, 8), :] = jnp.zeros((8, CMP_HIDDEN), F32)
    hid = acc_a + sh_scr[pl.ds(1, n), :]
    out = _dot(_silu(hid).astype(BF16), w2_ref[...])
    if transposed:
        tk = NSA_TILE
        for r in range(n // tk):
            o_ref[0, 0, r] = out[r * tk:(r + 1) * tk, :].T.astype(o_ref.dtype)
    else:
        o_ref[0, 0] = out.astype(o_ref.dtype)


def _nsa_compress(src, col0, pe, w1, w2, transposed):
    b, s, _ = src.shape
    n = s // CMP_STRIDE
    base = col0 // NSA_DH
    tk = NSA_TILE
    if transposed:
        out_shape = jax.ShapeDtypeStruct((b, NSA_GROUPS, n // tk, NSA_DH, tk), BF16)
        out_spec = pl.BlockSpec((1, 1, n // tk, NSA_DH, tk), lambda bi, g: (bi, g, 0, 0, 0))
    else:
        out_shape = jax.ShapeDtypeStruct((b, NSA_GROUPS, n, NSA_DH), BF16)
        out_spec = pl.BlockSpec((1, 1, n, NSA_DH), lambda bi, g: (bi, g, 0, 0))
    return pl.pallas_call(
        functools.partial(_compress_kernel, transposed=transposed),
        out_shape=out_shape,
        grid=(b, NSA_GROUPS),
        in_specs=[
            pl.BlockSpec((1, s, NSA_DH), lambda bi, g: (bi, 0, base + g)),
            pl.BlockSpec((CMP_LEN, NSA_DH), lambda bi, g: (0, 0)),
            pl.BlockSpec((CMP_LEN * NSA_DH, CMP_HIDDEN), lambda bi, g: (0, 0)),
            pl.BlockSpec((CMP_HIDDEN, NSA_DH), lambda bi, g: (0, 0)),
        ],
        out_specs=out_spec,
        scratch_shapes=[pltpu.VMEM((n + 8, CMP_HIDDEN), F32)],
        compiler_params=_cparams(("parallel", "parallel")),
        name="nsa_compress",
    )(src, pe, w1.astype(BF16), w2.astype(BF16))


def _weights_bf16(s_tiles, m):
    return [jnp.exp((s - m).astype(BF16)) for s in s_tiles]


def _flash_update(s_tiles, vt_tiles, m_ref, acc_ref):
    m_old = m_ref[...]
    m_new = m_old
    for s in s_tiles:
        m_new = jnp.maximum(m_new, jnp.max(s, axis=0, keepdims=True))
    alpha = jnp.exp(m_old - m_new)
    acc_ref[...] = alpha * acc_ref[...] + _pv(vt_tiles, _weights_bf16(s_tiles, m_new))
    m_ref[...] = m_new


def _pv(vt_tiles, p_tiles):
    vt = jnp.concatenate(vt_tiles, axis=1)
    p = jnp.concatenate([p.astype(BF16) for p in p_tiles], axis=0)
    return _dot(vt, p)


def _softmax_tiles(s_tiles, vt_tiles):
    m = jnp.max(s_tiles[0], axis=0, keepdims=True)
    for s in s_tiles[1:]:
        m = jnp.maximum(m, jnp.max(s, axis=0, keepdims=True))
    p_tiles = [jnp.exp(s - m) for s in s_tiles]
    l = jnp.sum(p_tiles[0], axis=0, keepdims=True)
    for p in p_tiles[1:]:
        l = l + jnp.sum(p, axis=0, keepdims=True)
    return p_tiles, l, _pv(vt_tiles, p_tiles)


def _softmax_tiles_aug(s_tiles, vt_tiles):
    m = jnp.max(s_tiles[0], axis=0, keepdims=True)
    for s in s_tiles[1:]:
        m = jnp.maximum(m, jnp.max(s, axis=0, keepdims=True))
    acc = _pv(vt_tiles, _weights_bf16(s_tiles, m))
    return acc[0:NSA_DH] * (1.0 / acc[NSA_DH:NSA_DH + 1])


def _nsa_kernel(q_ref, kc_ref, vct_ref, ovt_ref, ks_ref, vst_ref, kw_ref, vwt_ref, gate_ref, nz_ref,
                o_ref, m_scr, accs_scr, acc_scr, cap_scr, capd_scr, capl_scr, sa_scr, sb_scr):
    tq = q_ref.shape[1]
    tk = NSA_TILE
    dh = NSA_DH
    hpg = NSA_HPG
    lanes = hpg * tq
    ns = ovt_ref.shape[1]
    qtiles = tq // tk
    qi = pl.program_id(2)
    t0 = qi * tq
    kt0 = qi * qtiles

    qt = jnp.concatenate([q_ref[0, :, j * dh:(j + 1) * dh].astype(F32).T for j in range(hpg)],
                         axis=1).astype(BF16)
    rowi = lax.broadcasted_iota(jnp.int32, (tk, lanes), 0)
    qpos = lax.broadcasted_iota(jnp.int32, (tk, lanes), 1) & (tq - 1)
    for d in range(qtiles):
        capd_scr[d] = jnp.where(rowi + d * tk <= qpos, FORCE, NEG)
        capl_scr[d] = jnp.where(rowi + d * tk > qpos, FORCE, NEG)

    s_tiles = []
    n_ct = kc_ref.shape[2] // tk
    for c in range(n_ct):
        first_end = c * (tk * CMP_STRIDE) + (CMP_LEN - 1) - t0
        cap = jnp.where(rowi * CMP_STRIDE + first_end <= qpos, FORCE, NEG)
        s_tiles.append(jnp.minimum(_dot(kc_ref[0, 0, c * tk:(c + 1) * tk, :], qt), cap))
    p_tiles, l_c, acc_c = _softmax_tiles(s_tiles, [vct_ref[0, 0, c] for c in range(n_ct)])
    imp_un = jnp.zeros((ns, lanes), F32)
    for c, p in enumerate(p_tiles):
        p_hi = p.astype(BF16)
        p_lo = (p - p_hi.astype(F32)).astype(BF16)
        imp_un = imp_un + _dot(ovt_ref[c], p_hi) + _dot(ovt_ref[c], p_lo)
    tl = t0 + (lax.broadcasted_iota(jnp.int32, (1, lanes), 1) & (tq - 1))
    inv_l = jnp.where(tl >= CMP_LEN - 1, 1.0 / l_c, 0.0)
    acc_scr[0] = acc_c * inv_l
    impn = imp_un * inv_l
    imp = impn[:, 0:tq]
    for j in range(1, hpg):
        imp = imp + impn[:, j * tq:(j + 1) * tq]

    blk = lax.broadcasted_iota(jnp.int32, (ns, tq), 0)
    blk_t = jnp.right_shift(t0 + lax.broadcasted_iota(jnp.int32, (ns, tq), 1), SLC_SHIFT)
    score = jnp.where(blk == 0, FORCE,
                      jnp.where(blk == blk_t, FORCE,
                                jnp.where(blk == blk_t - 1, FORCE,
                                          jnp.where(blk <= blk_t, imp, NEG))))
    blk_f = blk.astype(F32)
    cap_sel = jnp.full((ns, tq), NEG, F32)
    for _ in range(min(SLC_TOPK, ns)):
        mx = jnp.max(score, axis=0, keepdims=True)
        first = jnp.min(jnp.where(score == mx, blk_f, float(ns)), axis=0, keepdims=True)
        pick = blk_f == first
        cap_sel = jnp.where(pick, FORCE, cap_sel)
        score = jnp.where(pick, -jnp.inf, score)
    for j in range(hpg):
        cap_scr[:, j * tq:(j + 1) * tq] = cap_sel

    bpt = tk // SLC_BLOCK

    def slc_scores(kt):
        k0 = pl.multiple_of(kt * tk, tk)
        s = _dot(ks_ref[0, pl.ds(k0, tk), :], qt)
        caps = [jnp.broadcast_to(cap_scr[pl.ds(kt * bpt + r, 1), :], (SLC_BLOCK, lanes))
                for r in range(bpt)]
        return jnp.minimum(s, jnp.concatenate(caps, axis=0))

    wt = WINDOW // tk
    s_tiles, vt_tiles = [], []
    for d in range(-wt, qtiles):
        kt = kt0 + d
        ktc = jnp.maximum(kt, 0)
        s = _dot(kw_ref[0, pl.ds(pl.multiple_of(ktc * tk, tk), tk), :], qt)
        if d + wt < qtiles:
            s = jnp.minimum(s, capl_scr[d + wt])
        if d >= 0:
            s = jnp.minimum(s, capd_scr[d])
        else:
            s = jnp.minimum(s, jnp.where(kt >= 0, FORCE, NEG))
        s_tiles.append(s)
        vt_tiles.append(vwt_ref[0, 0, ktc])
    acc_scr[1] = _softmax_tiles_aug(s_tiles, vt_tiles)

    nb = NSA_TILES_PER_STEP
    m_scr[...] = jnp.full((1, lanes), NEG, F32)
    accs_scr[...] = jnp.zeros_like(accs_scr)
    state = (m_scr, accs_scr)
    last_tile = ks_ref.shape[1] // tk - 1
    n_full = kt0 // nb

    def score_step(step, dst):
        for r in range(nb):
            dst[r * tk:(r + 1) * tk, :] = slc_scores(step * nb + r)

    def value_step(step, src):
        _flash_update([src[r * tk:(r + 1) * tk, :] for r in range(nb)],
                      [vst_ref[0, 0, step * nb + r] for r in range(nb)], *state)

    @pl.when(n_full > 0)
    def _():
        score_step(0, sa_scr)

    def slc_pair(j, carry):
        score_step(2 * j + 1, sb_scr)
        value_step(2 * j, sa_scr)
        score_step(jnp.minimum(2 * j + 2, n_full - 1), sa_scr)
        value_step(2 * j + 1, sb_scr)
        return carry

    lax.fori_loop(0, n_full // 2, slc_pair, 0)

    @pl.when(n_full % 2 == 1)
    def _():
        value_step(n_full - 1, sa_scr)

    s_tiles, vt_tiles = [], []
    for r in range(nb):
        kt = n_full * nb + r
        ktc = jnp.minimum(kt, last_tile)
        cap = jnp.where(kt < kt0 + qtiles, FORCE, NEG)
        for d in range(qtiles):
            cap = jnp.minimum(cap, jnp.maximum(capd_scr[d], jnp.where(kt == kt0 + d, NEG, FORCE)))
        s_tiles.append(jnp.minimum(slc_scores(ktc), cap))
        vt_tiles.append(vst_ref[0, 0, ktc])
    _flash_update(s_tiles, vt_tiles, *state)

    o_c = acc_scr[0]
    o_s = accs_scr[0:dh, :] * (1.0 / accs_scr[dh:dh + 1, :])
    o_w = acc_scr[1]
    sgt = _sigmoid(gate_ref[0]).T
    for j in range(hpg):
        ls = slice(j * tq, (j + 1) * tq)
        mix_t = (sgt[3 * j:3 * j + 1, :] * o_c[:, ls] + sgt[3 * j + 1:3 * j + 2, :] * o_s[:, ls]
                 + sgt[3 * j + 2:3 * j + 3, :] * o_w[:, ls])
        sl = slice(j * dh, (j + 1) * dh)
        o_ref[0, :, sl] = (mix_t.T * _silu(nz_ref[0, :, sl])).astype(o_ref.dtype)


def _nsa_attention(qr, kc, vct, ksr, vst, kwr, vwt, small, proj):
    b, s, _ = qr.shape
    tk = NSA_TILE
    tq = min(NSA_QUERY_TILE, s)
    qtiles = tq // tk
    assert s % tq == 0 and tq % tk == 0 and WINDOW % tk == 0 and tk % SLC_BLOCK == 0
    assert NSA_TILES_PER_STEP % qtiles == 0 and WINDOW // tk >= qtiles
    ncp = kc.shape[2]
    assert ncp % tk == 0
    ns = s // SLC_BLOCK
    gw = NSA_HPG * NSA_DH
    lanes = NSA_HPG * tq
    cs = np.arange(ncp)[None, :] * CMP_STRIDE
    ss = np.arange(ns)[:, None] * SLC_BLOCK
    ovt = ((np.minimum(cs + CMP_LEN, ss + SLC_BLOCK) - np.maximum(cs, ss)) > 0).astype(np.float32)
    ovt[:, ncp - 1] = 0.0
    ovt = ovt.reshape(ns, ncp // tk, tk).transpose(1, 0, 2)
    kv_spec = pl.BlockSpec((1, s, NSA_DH), lambda bi, g, i: (bi, 0, g))
    vrows = NSA_DH + NSA_ONES
    vt_spec = pl.BlockSpec((1, 1, s // tk, vrows, tk), lambda bi, g, i: (bi, g, 0, 0, 0))
    return pl.pallas_call(
        _nsa_kernel,
        out_shape=jax.ShapeDtypeStruct((b, s, NSA_HEADS * NSA_DH), BF16),
        grid=(b, NSA_GROUPS, s // tq),
        in_specs=[
            pl.BlockSpec((1, tq, gw), lambda bi, g, i: (bi, i, g)),
            pl.BlockSpec((1, 1, ncp, NSA_DH), lambda bi, g, i: (bi, g, 0, 0)),
            pl.BlockSpec((1, 1, ncp // tk, NSA_DH, tk), lambda bi, g, i: (bi, g, 0, 0, 0)),
            pl.BlockSpec((ncp // tk, ns, tk), lambda bi, g, i: (0, 0, 0)),
            kv_spec, vt_spec, kv_spec, vt_spec,
            pl.BlockSpec((1, tq, 128), lambda bi, g, i: (bi, i, 1 + g)),
            pl.BlockSpec((1, tq, gw), lambda bi, g, i: (bi, i, 11 + g)),
        ],
        out_specs=pl.BlockSpec((1, tq, gw), lambda bi, g, i: (bi, i, g)),
        scratch_shapes=[
            pltpu.VMEM((1, lanes), F32),
            pltpu.VMEM((vrows, lanes), F32),
            pltpu.VMEM((2, NSA_DH, lanes), F32),
            pltpu.VMEM((ns, lanes), F32),
            pltpu.VMEM((qtiles, tk, lanes), F32),
            pltpu.VMEM((qtiles, tk, lanes), F32),
            pltpu.VMEM((NSA_TILES_PER_STEP * tk, lanes), F32),
            pltpu.VMEM((NSA_TILES_PER_STEP * tk, lanes), F32),
        ],
        compiler_params=_cparams(("parallel", "parallel", "arbitrary")),
        name="nsa_attention",
    )(qr, kc, vct, jnp.asarray(ovt, dtype=BF16), ksr, vst, kwr, vwt, small, proj)


def _nsa_branch(proj, small, pe_k, w1_k, w2_k, pe_v, w1_v, w2_v):
    qr, kcr, ksr, vst, kwr, vwt = _nsa_prep(proj)
    kc = _nsa_compress(kcr, 0, pe_k, w1_k, w2_k, transposed=False)
    vct = _nsa_compress(proj, 4352, pe_v, w1_v, w2_v, transposed=True)
    return _nsa_attention(qr, kc, vct, ksr, vst, kwr, vwt, small, proj)


def _even_weights(w_in):
    main = jnp.concatenate([w_in[:, 0:2048], w_in[:, 2064:5648], w_in[:, 5672:6696]], axis=1)
    d = w_in.shape[0]
    gates = w_in[:, 5648:5672]
    per = NSA_HPG * 3
    blocks = [jnp.pad(w_in[:, 2048:2064], ((0, 0), (0, 128 - GLA_RANK)))]
    for g in range(NSA_GROUPS):
        blocks.append(jnp.pad(gates[:, g * per:(g + 1) * per], ((0, 0), (0, 128 - per))))
    small = jnp.concatenate(blocks, axis=1)
    return main.astype(BF16), small.astype(BF16)


def _odd_weights(w_in):
    main = w_in[:, 0:MAIN_COLS]
    small = jnp.pad(w_in[:, MAIN_COLS:MAIN_COLS + SSM_HEADS], ((0, 0), (0, 128 - SSM_HEADS)))
    return main.astype(BF16), small.astype(BF16)


def kernel(x, c, ada_w, ada_b, pre_norm_w, post_norm_w, even_w_in, even_w_out, gla_w_up, gla_b_up,
           gla_norm_w, nsa_pe_k, nsa_w1_k, nsa_w2_k, nsa_pe_v, nsa_w1_v, nsa_w2_v, odd_w_in,
           odd_w_out, hgrn_lb_logits, hgrn_norm_w, ssm_conv_w, ssm_conv_b, ssm_dt_bias, ssm_a_log,
           ssm_d, ssm_norm_w):
    depth = ada_w.shape[0]
    mod = _adaln_mod(c, ada_w, ada_b)
    for l in range(depth):
        if l % 2 == 0:
            e = l // 2
            w_main, w_small = _even_weights(even_w_in[e])
            proj, small = _in_proj(x, mod[l], pre_norm_w[l], w_main, w_small)
            o_a = _gla_branch(proj, small, gla_w_up[e], gla_b_up[e], gla_norm_w[e])
            o_b = _nsa_branch(proj, small, nsa_pe_k[e], nsa_w1_k[e], nsa_w2_k[e],
                              nsa_pe_v[e], nsa_w1_v[e], nsa_w2_v[e])
            w_out = even_w_out[e]
        else:
            o = l // 2
            w_main, w_small = _odd_weights(odd_w_in[o])
            proj, small = _in_proj(x, mod[l], pre_norm_w[l], w_main, w_small)
            o_a = _hgrn_branch(proj, hgrn_lb_logits, hgrn_norm_w[o], l)
            o_b = _ssd_branch(proj, small, ssm_conv_w[o], ssm_conv_b[o], ssm_dt_bias[o],
                              ssm_a_log[o], ssm_d[o], ssm_norm_w[o])
            w_out = odd_w_out[o]
        x = _out_proj(o_a, o_b, w_out.astype(BF16), x, mod[l], post_norm_w[l])
    return x
```

```python
import functools
import math

import jax
import jax.numpy as jnp
import numpy as np
from jax import lax
from jax.experimental import pallas as pl
from jax.experimental.pallas import tpu as pltpu

F32 = jnp.float32
BF16 = jnp.bfloat16

D_MODEL = 2048
EPS = 1e-6
NEG = -1e30
FORCE = 1e30
ROPE_THETA = 10000.0

GLA_HEADS = 4
GLA_DK = 128
GLA_DV = 256
GLA_RANK = 16
GLA_TAU = 16.0

NSA_DH = 128
NSA_HEADS = 8
NSA_GROUPS = 2
NSA_HPG = 4
CMP_LEN = 32
CMP_STRIDE = 16
CMP_HIDDEN = 256
SLC_BLOCK = 64
SLC_SHIFT = 6
SLC_TOPK = 16
WINDOW = 512
NSA_TILE = 128
NSA_QUERY_TILE = 256
NSA_ONES = 16
NSA_TILES_PER_STEP = 4

HGRN_HEADS = 8
HGRN_DK = 128
HGRN_DV = 128

SSM_HEADDIM = 64
SSM_HEADS = 16
SSM_GROUPS = 2
SSM_STATE = 128
SSM_CONV = 4
SSM_CHUNK = 256
SSM_WIDTH = 1024
SSM_CONV_CH = SSM_WIDTH + 2 * SSM_GROUPS * SSM_STATE

LIN_CHUNK = 128
MAIN_COLS = 6656
VMEM_LIMIT = 56 * 1024 * 1024


def _cparams(sem):
    return pltpu.CompilerParams(dimension_semantics=sem, vmem_limit_bytes=VMEM_LIMIT)


def _dot(a, b):
    return jnp.dot(a, b, preferred_element_type=F32)


def _dot_nt(a, b):
    return lax.dot_general(a, b, (((1,), (1,)), ((), ())), preferred_element_type=F32)


def _dot_tn(a, b):
    return lax.dot_general(a, b, (((0,), (0,)), ((), ())), preferred_element_type=F32)


def _split3(x):
    hi = x.astype(BF16)
    r = x - hi.astype(F32)
    mid = r.astype(BF16)
    lo = (r - mid.astype(F32)).astype(BF16)
    return hi, mid, lo


def _sel_dot_left(sel, x):
    n = x.shape[1]
    y = _dot(sel.astype(BF16), jnp.concatenate(_split3(x), axis=1))
    return y[:, 0:n] + y[:, n:2 * n] + y[:, 2 * n:3 * n]


def _sel_dot_right(x, sel):
    m = x.shape[0]
    y = _dot(jnp.concatenate(_split3(x), axis=0), sel.astype(BF16))
    return y[0:m] + y[m:2 * m] + y[2 * m:3 * m]


def _sigmoid(x):
    return 1.0 / (1.0 + jnp.exp(-x))


def _silu(x):
    return x * _sigmoid(x)


def _log1p_exp_neg_abs(x):
    return jnp.log(1.0 + jnp.exp(-jnp.abs(x)))


def _log_sigmoid(x):
    return jnp.minimum(x, 0.0) - _log1p_exp_neg_abs(x)


def _softplus(x):
    return jnp.maximum(x, 0.0) + _log1p_exp_neg_abs(x)


def _logaddexp(a, b):
    return jnp.maximum(a, b) + _log1p_exp_neg_abs(a - b)


def _mod_kernel(c_ref, w_ref, b_ref, o_ref):
    ca = _silu(c_ref[...])
    o_ref[0] = _dot(ca, w_ref[0]) + b_ref[0]


def _adaln_mod(c, ada_w, ada_b):
    depth, d, n3 = ada_w.shape
    b = c.shape[0]
    rows = 8
    c_pad = jnp.pad(c, ((0, rows - b), (0, 0)))
    tn = 768
    out = pl.pallas_call(
        _mod_kernel,
        out_shape=jax.ShapeDtypeStruct((depth, rows, n3), F32),
        grid=(depth, n3 // tn),
        in_specs=[
            pl.BlockSpec((rows, d), lambda l, j: (0, 0)),
            pl.BlockSpec((1, d, tn), lambda l, j: (l, 0, j)),
            pl.BlockSpec((1, 1, tn), lambda l, j: (l, 0, j)),
        ],
        out_specs=pl.BlockSpec((1, rows, tn), lambda l, j: (l, 0, j)),
        compiler_params=_cparams(("parallel", "parallel")),
        name="adaln_mod",
    )(c_pad, ada_w, ada_b.reshape(depth, 1, n3))
    return out[:, :b]


def _inproj_kernel(x_ref, mod_ref, nw_ref, w_ref, ws_ref, o_ref, os_ref, h_scr):
    d = x_ref.shape[-1]

    @pl.when(pl.program_id(2) == 0)
    def _():
        x = x_ref[0]
        var = jnp.mean(x * x, axis=-1, keepdims=True)
        y = x * lax.rsqrt(var + EPS) * nw_ref[...]
        shift = mod_ref[0, :, 0:d]
        scale = mod_ref[0, :, d:2 * d]
        hb = (y * (1.0 + scale) + shift).astype(BF16)
        h_scr[...] = hb
        os_ref[0] = _dot(hb, ws_ref[...])

    o_ref[0] = _dot(h_scr[...], w_ref[...])


def _in_proj(x, mod_l, norm_w, w_main, w_small):
    b, s, d = x.shape
    n = w_main.shape[1]
    ns = w_small.shape[1]
    tm = min(1024, s)
    tn = 512
    return pl.pallas_call(
        _inproj_kernel,
        out_shape=(jax.ShapeDtypeStruct((b, s, n), F32),
                   jax.ShapeDtypeStruct((b, s, ns), F32)),
        grid=(b, s // tm, n // tn),
        in_specs=[
            pl.BlockSpec((1, tm, d), lambda bi, i, j: (bi, i, 0)),
            pl.BlockSpec((1, 1, 3 * d), lambda bi, i, j: (bi, 0, 0)),
            pl.BlockSpec((1, d), lambda bi, i, j: (0, 0)),
            pl.BlockSpec((d, tn), lambda bi, i, j: (0, j)),
            pl.BlockSpec((d, ns), lambda bi, i, j: (0, 0)),
        ],
        out_specs=(pl.BlockSpec((1, tm, tn), lambda bi, i, j: (bi, i, j)),
                   pl.BlockSpec((1, tm, ns), lambda bi, i, j: (bi, i, 0))),
        scratch_shapes=[pltpu.VMEM((tm, d), BF16)],
        compiler_params=_cparams(("parallel", "parallel", "arbitrary")),
        name="in_proj",
    )(x, mod_l.reshape(b, 1, 3 * d), norm_w.reshape(1, d), w_main, w_small)


def _outproj_kernel(a1_ref, a2_ref, w_ref, x_ref, mod_ref, nw_ref, o_ref):
    d = x_ref.shape[-1]
    half = a1_ref.shape[-1]
    y = _dot(a1_ref[0], w_ref[0:half, :]) + _dot(a2_ref[0], w_ref[half:2 * half, :])
    var = jnp.mean(y * y, axis=-1, keepdims=True)
    yn = y * lax.rsqrt(var + EPS) * nw_ref[...]
    gate = mod_ref[0, :, 2 * d:3 * d]
    o_ref[0] = x_ref[0] + gate * yn


def _out_proj(a1, a2, w_out, x, mod_l, norm_w):
    b, s, d = x.shape
    half = a1.shape[-1]
    tm = min(512, s)
    return pl.pallas_call(
        _outproj_kernel,
        out_shape=jax.ShapeDtypeStruct((b, s, d), F32),
        grid=(b, s // tm),
        in_specs=[
            pl.BlockSpec((1, tm, half), lambda bi, i: (bi, i, 0)),
            pl.BlockSpec((1, tm, half), lambda bi, i: (bi, i, 0)),
            pl.BlockSpec((2 * half, d), lambda bi, i: (0, 0)),
            pl.BlockSpec((1, tm, d), lambda bi, i: (bi, i, 0)),
            pl.BlockSpec((1, 1, 3 * d), lambda bi, i: (bi, 0, 0)),
            pl.BlockSpec((1, d), lambda bi, i: (0, 0)),
        ],
        out_specs=pl.BlockSpec((1, tm, d), lambda bi, i: (bi, i, 0)),
        compiler_params=_cparams(("parallel", "parallel")),
        name="out_proj",
    )(a1, a2, w_out, x, mod_l.reshape(b, 1, 3 * d), norm_w.reshape(1, d))


def _cumsum_chains(tri, gs):
    n = gs[0].shape[1]
    terms = []
    for g in gs:
        terms.extend(_split3(g))
    y = _dot(tri, jnp.concatenate(terms, axis=1))
    return [y[:, (3 * i) * n:(3 * i + 1) * n] + y[:, (3 * i + 1) * n:(3 * i + 2) * n]
            + y[:, (3 * i + 2) * n:(3 * i + 3) * n] for i in range(len(gs))]


def _lin_attn_chunk(q, k, v, g, b, st_scr, b_scr, g_scr, consts):
    c, dk = q.shape
    row, eye, tri, pairs = consts
    b_scr[...] = b
    g_scr[pl.ds(8, c), :] = g
    b_last = b_scr[pl.ds(c - 1, 1), :]

    def level_operand(s, expo):
        upper = (row & s) != 0
        return (jnp.where(upper, q, k) * jnp.exp(expo)).astype(BF16)

    items = []
    for s in [c >> i for i in range(1, c.bit_length() - 2)]:
        pieces = [jnp.broadcast_to(b_scr[pl.ds(p * 2 * s + s - 1, 1), :], (2 * s, dk))
                  for p in range(c // (2 * s))]
        d = b - jnp.concatenate(pieces, axis=0)
        x = level_operand(s, jnp.where((row & s) != 0, d, -d))
        items.append((x, x, pairs[s]))
    g_dn = g_scr[pl.ds(7, c), :]
    g_up = g_scr[pl.ds(9, c), :]
    r4 = row & 3
    x = level_operand(2, jnp.where(r4 == 2, g, jnp.where(r4 == 3, g + g_dn, jnp.where(r4 == 0, g_up, 0.0))))
    items.append((x, x, pairs[2]))
    x = level_operand(1, jnp.where((row & 1) != 0, g, 0.0))
    items.append((x, x, pairs[1]))
    items.append((q.astype(BF16), k.astype(BF16), eye))

    attn = jnp.zeros((c, c), F32)
    for i in range(0, len(items), 2):
        grp = items[i:i + 2]
        prod = _dot_nt(jnp.concatenate([it[0] for it in grp], axis=0),
                       jnp.concatenate([it[1] for it in grp], axis=0))
        for n, it in enumerate(grp):
            attn = jnp.where(it[2], prod[n * c:(n + 1) * c, n * c:(n + 1) * c], attn)

    v16 = v.astype(BF16)
    st = st_scr[...]
    o = _dot(jnp.concatenate([attn.astype(BF16), (q * jnp.exp(b)).astype(BF16)], axis=1),
             jnp.concatenate([v16, st.astype(BF16)], axis=0))
    k_dec = (k * jnp.exp(b_last - b)).astype(BF16)
    e_col = jnp.broadcast_to(jnp.exp(b_last), (dk, dk)).T
    e_col = jnp.concatenate([e_col] * (v.shape[1] // dk), axis=1)
    st_scr[...] = st * e_col + _dot_tn(k_dec, v16)
    return o


def _lin_attn_consts(c, dk):
    row = lax.broadcasted_iota(jnp.int32, (c, dk), 0)
    ri = lax.broadcasted_iota(jnp.int32, (c, c), 0)
    ci = lax.broadcasted_iota(jnp.int32, (c, c), 1)
    eye = ri == ci
    tri = jnp.where(ri >= ci, 1.0, 0.0).astype(BF16)
    pairs = {s: jnp.where((ri & -(2 * s)) == (ci & -(2 * s)), (ri & s) - (ci & s), 0) == s
             for s in [c >> i for i in range(1, c.bit_length())]}
    return row, eye, tri, pairs


def _lin_attn_finish(o, gz, nw):
    var = jnp.mean(o * o, axis=-1, keepdims=True)
    return (o * lax.rsqrt(var + EPS) * nw) * _silu(gz)


LIN_HEADS_PER_STEP = 4


def _gla_kernel(q_ref, k_ref, v_ref, gz_ref, glr_ref, wup_ref, bup_ref, nw_ref, o_ref,
                st_scr, b_scr, g_scr):
    c = LIN_CHUNK
    dk, dv = GLA_DK, GLA_DV
    nb, ts = q_ref.shape[0], q_ref.shape[1]
    hp = q_ref.shape[2] // dk

    @pl.when(pl.program_id(1) == 0)
    def _():
        st_scr[...] = jnp.zeros_like(st_scr)

    g_scr[...] = jnp.zeros_like(g_scr)
    consts = _lin_attn_consts(c, dk)
    nw = nw_ref[...]
    q_scale = dk ** -0.5

    def body(ci, carry):
        r0 = pl.multiple_of(ci * c, c)
        rows = pl.ds(r0, c)
        gs = []
        for bb in range(nb):
            z = _dot(glr_ref[bb, rows, :], wup_ref[...]) + bup_ref[...]
            g_all = _log_sigmoid(z) * (1.0 / GLA_TAU)
            gs.extend(g_all[:, hh * dk:(hh + 1) * dk] for hh in range(hp))
        bs = _cumsum_chains(consts[2], gs)
        for bb in range(nb):
            for hh in range(hp):
                ks = slice(hh * dk, (hh + 1) * dk)
                vs = slice(hh * dv, (hh + 1) * dv)
                ch = bb * hp + hh
                q = q_ref[bb, rows, ks] * q_scale
                o = _lin_attn_chunk(q, k_ref[bb, rows, ks], v_ref[bb, rows, vs], gs[ch], bs[ch],
                                    st_scr.at[ch], b_scr.at[ch], g_scr.at[ch], consts)
                out = _lin_attn_finish(o, gz_ref[bb, rows, vs], nw)
                o_ref[bb, rows, vs] = out.astype(o_ref.dtype)
        return carry

    lax.fori_loop(0, ts // c, body, 0)


def _hgrn_kernel(q_ref, f_ref, v_ref, gz_ref, lbl_ref, nw_ref, o_ref, st_scr, b_scr, g_scr,
                 *, layer):
    c = LIN_CHUNK
    dk, dv = HGRN_DK, HGRN_DV
    nb, ts = q_ref.shape[0], q_ref.shape[1]
    hp = q_ref.shape[2] // dk

    @pl.when(pl.program_id(1) == 0)
    def _():
        st_scr[...] = jnp.zeros_like(st_scr)

    g_scr[...] = jnp.zeros_like(g_scr)
    consts = _lin_attn_consts(c, dk)
    nw = nw_ref[...]

    logits = lbl_ref[...]
    depth = logits.shape[0]
    mx = logits[0:1, :]
    for r in range(1, depth):
        mx = jnp.maximum(mx, logits[r:r + 1, :])
    ex = [jnp.exp(logits[r:r + 1, :] - mx) for r in range(depth)]
    den = ex[0]
    for r in range(1, depth):
        den = den + ex[r]
    sm = [e / den for e in ex]
    lb_all = sm[0]
    for r in range(1, layer + 1):
        lb_all = lb_all + sm[r]
    lb_all = lb_all - sm[0]
    log_lb_all = jnp.log(lb_all)
    log_1mlb_all = jnp.log1p(-lb_all)

    def body(ci, carry):
        r0 = pl.multiple_of(ci * c, c)
        rows = pl.ds(r0, c)
        gs, kk = [], []
        for bb in range(nb):
            for hh in range(hp):
                ks = slice(hh * dk, (hh + 1) * dk)
                z = f_ref[bb, rows, ks]
                gs.append(_logaddexp(log_lb_all[:, ks], log_1mlb_all[:, ks] + _log_sigmoid(z)))
                kk.append((1.0 - lb_all[:, ks]) * (1.0 / (1.0 + jnp.exp(z))))
        bs = _cumsum_chains(consts[2], gs)
        for bb in range(nb):
            for hh in range(hp):
                ks = slice(hh * dk, (hh + 1) * dk)
                vs = slice(hh * dv, (hh + 1) * dv)
                ch = bb * hp + hh
                o = _lin_attn_chunk(q_ref[bb, rows, ks], kk[ch], v_ref[bb, rows, vs], gs[ch], bs[ch],
                                    st_scr.at[ch], b_scr.at[ch], g_scr.at[ch], consts)
                out = _lin_attn_finish(o, gz_ref[bb, rows, vs], nw)
                o_ref[bb, rows, vs] = out.astype(o_ref.dtype)
        return carry

    lax.fori_loop(0, ts // c, body, 0)


def _lin_scratch(chains, dv, dk):
    return [pltpu.VMEM((chains, dk, dv), F32),
            pltpu.VMEM((chains, LIN_CHUNK, dk), F32),
            pltpu.VMEM((chains, LIN_CHUNK + 16, dk), F32)]


def _gla_branch(proj, small, w_up, b_up, norm_w):
    b, s, _ = proj.shape
    ts = min(512, s)
    dk, dv, h = GLA_DK, GLA_DV, GLA_HEADS
    hp = LIN_HEADS_PER_STEP
    kw, vw = hp * dk, hp * dv
    w_up_pad = jnp.pad(w_up, ((0, 128 - GLA_RANK), (0, 0)))
    return pl.pallas_call(
        _gla_kernel,
        out_shape=jax.ShapeDtypeStruct((b, s, h * dv), BF16),
        grid=(h // hp, s // ts),
        in_specs=[
            pl.BlockSpec((b, ts, kw), lambda hi, i: (0, i, hi)),
            pl.BlockSpec((b, ts, kw), lambda hi, i: (0, i, h // hp + hi)),
            pl.BlockSpec((b, ts, vw), lambda hi, i: (0, i, h // hp + hi)),
            pl.BlockSpec((b, ts, vw), lambda hi, i: (0, i, 2 * (h // hp) + hi)),
            pl.BlockSpec((b, ts, 128), lambda hi, i: (0, i, 0)),
            pl.BlockSpec((128, kw), lambda hi, i: (0, hi)),
            pl.BlockSpec((1, kw), lambda hi, i: (0, hi)),
            pl.BlockSpec((1, dv), lambda hi, i: (0, 0)),
        ],
        out_specs=pl.BlockSpec((b, ts, vw), lambda hi, i: (0, i, hi)),
        scratch_shapes=_lin_scratch(b * hp, dv, dk),
        compiler_params=_cparams(("parallel", "arbitrary")),
        name="gla",
    )(proj, proj, proj, proj, small, w_up_pad, b_up.reshape(1, -1), norm_w.reshape(1, dv))


def _hgrn_branch(proj, lb_logits, norm_w, layer):
    b, s, _ = proj.shape
    ts = min(512, s)
    dk, dv, h = HGRN_DK, HGRN_DV, HGRN_HEADS
    hp = LIN_HEADS_PER_STEP
    kw, vw = hp * dk, hp * dv
    ng = h // hp
    depth = lb_logits.shape[0]
    return pl.pallas_call(
        functools.partial(_hgrn_kernel, layer=layer),
        out_shape=jax.ShapeDtypeStruct((b, s, h * dv), BF16),
        grid=(ng, s // ts),
        in_specs=[
            pl.BlockSpec((b, ts, kw), lambda hi, i: (0, i, hi)),
            pl.BlockSpec((b, ts, kw), lambda hi, i: (0, i, ng + hi)),
            pl.BlockSpec((b, ts, vw), lambda hi, i: (0, i, 2 * ng + hi)),
            pl.BlockSpec((b, ts, vw), lambda hi, i: (0, i, 3 * ng + hi)),
            pl.BlockSpec((depth, kw), lambda hi, i: (0, hi)),
            pl.BlockSpec((1, dv), lambda hi, i: (0, 0)),
        ],
        out_specs=pl.BlockSpec((b, ts, vw), lambda hi, i: (0, i, hi)),
        scratch_shapes=_lin_scratch(b * hp, dv, dk),
        compiler_params=_cparams(("parallel", "arbitrary")),
        name="hgrn2",
    )(proj, proj, proj, proj, lb_logits, norm_w.reshape(1, dv))


def _ssd_kernel(z_ref, x_ref, bc_ref, dt_ref, cw_ref, cb_ref, dtb_ref, alog_ref, dexp_ref, nw_ref,
                hexp_ref, o_ref, stage_scr, xbc_scr, st_scr, y_scr):
    c = x_ref.shape[1]
    nch = SSM_CONV_CH
    width = SSM_WIDTH
    n = SSM_STATE
    pairs = SSM_HEADS // 2

    @pl.when(pl.program_id(1) == 0)
    def _():
        st_scr[...] = jnp.zeros_like(st_scr)
        stage_scr[pl.ds(0, 8), :] = jnp.zeros((8, nch), F32)

    stage_scr[pl.ds(8, c), 0:width] = x_ref[0]
    stage_scr[pl.ds(8, c), width:nch] = bc_ref[0]
    acc = cb_ref[...] + cw_ref[0:1, :] * stage_scr[pl.ds(5, c), :]
    for kk in range(1, SSM_CONV):
        acc = acc + cw_ref[kk:kk + 1, :] * stage_scr[pl.ds(5 + kk, c), :]
    stage_scr[pl.ds(0, 8), :] = stage_scr[pl.ds(c, 8), :]
    xbc_scr[...] = _silu(acc)

    dt = _softplus(dt_ref[0] + dtb_ref[...])
    a = -jnp.exp(alog_ref[...])
    da = dt * a
    ri = lax.broadcasted_iota(jnp.int32, (c, c), 0)
    ci = lax.broadcasted_iota(jnp.int32, (c, c), 1)
    causal = ri >= ci
    tri = jnp.where(causal, 1.0, 0.0).astype(BF16)
    a_cs = _sel_dot_left(tri, da)
    a_cs_t = a_cs.T
    hexp = hexp_ref[...]
    dt_x = _sel_dot_right(dt, hexp)
    acs_x = _sel_dot_right(a_cs, hexp)
    a_last_x = acs_x[c - 1:c, :]
    lane = lax.broadcasted_iota(jnp.int32, (c, 2 * SSM_HEADDIM), 1)
    first = lane < SSM_HEADDIM

    for grp in range(SSM_GROUPS):
        bm = xbc_scr[:, width + grp * n: width + (grp + 1) * n]
        cm = xbc_scr[:, width + (SSM_GROUPS + grp) * n: width + (SSM_GROUPS + grp + 1) * n]
        bm16 = bm.astype(BF16)
        cm16 = cm.astype(BF16)
        cb = _dot_nt(cm16, bm16)
        for pp in range(pairs // SSM_GROUPS):
            p = grp * (pairs // SSM_GROUPS) + pp
            lo = p * 2 * SSM_HEADDIM
            hi = lo + 2 * SSM_HEADDIM
            xs = xbc_scr[:, lo:hi]
            xdt = xs * dt_x[:, lo:hi]
            acs = acs_x[:, lo:hi]
            y = jnp.zeros((c, 2 * SSM_HEADDIM), F32)
            for hh in range(2):
                h = 2 * p + hh
                col = jnp.broadcast_to(a_cs[:, h:h + 1], (c, c))
                rw = jnp.broadcast_to(a_cs_t[h:h + 1, :], (c, c))
                lmat = jnp.exp(jnp.where(causal, col - rw, NEG))
                keep = first if hh == 0 else jnp.logical_not(first)
                xh = jnp.where(keep, xdt, 0.0).astype(BF16)
                y = y + _dot((cb * lmat).astype(BF16), xh)
            st = st_scr[p]
            y = y + _dot(cm16, st.astype(BF16)) * jnp.exp(acs)
            decay = jnp.exp(a_last_x[:, lo:hi] - acs)
            st_scr[p] = st * jnp.exp(a_last_x[:, lo:hi]) + _dot_tn(bm16, (xdt * decay).astype(BF16))
            y_scr[:, lo:hi] = y + dexp_ref[:, lo:hi] * xs

    yz = y_scr[...] * _silu(z_ref[0])
    var = jnp.mean(yz * yz, axis=-1, keepdims=True)
    o_ref[0] = (yz * lax.rsqrt(var + EPS) * nw_ref[...]).astype(o_ref.dtype)


def _ssd_branch(proj, small, conv_w, conv_b, dt_bias, a_log, d_skip, norm_w):
    b, s, _ = proj.shape
    c = math.gcd(SSM_CHUNK, s)
    nch = SSM_CONV_CH
    pad16 = lambda v: jnp.pad(v.reshape(1, -1), ((0, 0), (0, 128 - SSM_HEADS)))
    hexp = np.zeros((128, SSM_WIDTH), np.float32)
    for h in range(SSM_HEADS):
        hexp[h, h * SSM_HEADDIM:(h + 1) * SSM_HEADDIM] = 1.0
    d_exp = jnp.repeat(d_skip, SSM_HEADDIM).reshape(1, SSM_WIDTH)
    return pl.pallas_call(
        _ssd_kernel,
        out_shape=jax.ShapeDtypeStruct((b, s, SSM_WIDTH), BF16),
        grid=(b, s // c),
        in_specs=[
            pl.BlockSpec((1, c, SSM_WIDTH), lambda bi, i: (bi, i, 4)),
            pl.BlockSpec((1, c, SSM_WIDTH), lambda bi, i: (bi, i, 5)),
            pl.BlockSpec((1, c, 512), lambda bi, i: (bi, i, 12)),
            pl.BlockSpec((1, c, 128), lambda bi, i: (bi, i, 0)),
            pl.BlockSpec((SSM_CONV, nch), lambda bi, i: (0, 0)),
            pl.BlockSpec((1, nch), lambda bi, i: (0, 0)),
            pl.BlockSpec((1, 128), lambda bi, i: (0, 0)),
            pl.BlockSpec((1, 128), lambda bi, i: (0, 0)),
            pl.BlockSpec((1, SSM_WIDTH), lambda bi, i: (0, 0)),
            pl.BlockSpec((1, SSM_WIDTH), lambda bi, i: (0, 0)),
            pl.BlockSpec((128, SSM_WIDTH), lambda bi, i: (0, 0)),
        ],
        out_specs=pl.BlockSpec((1, c, SSM_WIDTH), lambda bi, i: (bi, i, 0)),
        scratch_shapes=[
            pltpu.VMEM((c + 8, nch), F32),
            pltpu.VMEM((c, nch), F32),
            pltpu.VMEM((SSM_HEADS // 2, SSM_STATE, 2 * SSM_HEADDIM), F32),
            pltpu.VMEM((c, SSM_WIDTH), F32),
        ],
        compiler_params=_cparams(("parallel", "arbitrary")),
        name="ssd",
    )(proj, proj, proj, small, conv_w, conv_b.reshape(1, nch), pad16(dt_bias), pad16(a_log),
      d_exp, norm_w.reshape(1, SSM_WIDTH), jnp.asarray(hexp, dtype=BF16))


def _rope(x, cos, sin_signed):
    return x * cos + pltpu.roll(x, NSA_DH // 2, axis=1) * sin_signed


def _nsa_prep_kernel(q_ref, kc_ref, ks_ref, vs_ref, kw_ref, vw_ref, cos_ref, sin_ref,
                     qo_ref, kco_ref, kso_ref, vso_ref, kwo_ref, vwo_ref):
    cos = cos_ref[...]
    sin = sin_ref[...]
    dh = NSA_DH
    scale = dh ** -0.5
    for h in range(NSA_HEADS):
        sl = slice(h * dh, (h + 1) * dh)
        qo_ref[0, :, sl] = (_rope(q_ref[0, :, sl], cos, sin) * scale).astype(qo_ref.dtype)
    for g in range(NSA_GROUPS):
        sl = slice(g * dh, (g + 1) * dh)
        kco_ref[0, :, sl] = _rope(kc_ref[0, :, sl], cos, sin)
        kso_ref[0, :, sl] = _rope(ks_ref[0, :, sl], cos, sin).astype(kso_ref.dtype)
        kwo_ref[0, :, sl] = _rope(kw_ref[0, :, sl], cos, sin).astype(kwo_ref.dtype)
    tk = NSA_TILE
    for g in range(NSA_GROUPS):
        for r in range(q_ref.shape[1] // tk):
            rows = slice(r * tk, (r + 1) * tk)
            sl = slice(g * dh, (g + 1) * dh)
            vso_ref[0, g, r, 0:dh, :] = vs_ref[0, rows, sl].T.astype(vso_ref.dtype)
            vwo_ref[0, g, r, 0:dh, :] = vw_ref[0, rows, sl].T.astype(vwo_ref.dtype)
            ones = jnp.ones((NSA_ONES, tk), vso_ref.dtype)
            vso_ref[0, g, r, dh:dh + NSA_ONES, :] = ones
            vwo_ref[0, g, r, dh:dh + NSA_ONES, :] = ones


def _nsa_prep(proj):
    b, s, _ = proj.shape
    ts = min(512, s)
    tk = NSA_TILE
    half = NSA_DH // 2
    inv_freq = ROPE_THETA ** (-jnp.arange(half, dtype=F32) / half)
    ang = jnp.arange(s, dtype=F32)[:, None] * inv_freq[None, :]
    cos = jnp.cos(ang)
    sin = jnp.sin(ang)
    cos_full = jnp.concatenate([cos, cos], axis=-1)
    sin_signed = jnp.concatenate([-sin, sin], axis=-1)
    kvw = NSA_GROUPS * NSA_DH
    col = lambda idx: (lambda bi, i: (bi, i, idx))
    kv_spec = lambda idx: pl.BlockSpec((1, ts, kvw), col(idx))
    kv_out = pl.BlockSpec((1, ts, kvw), lambda bi, i: (bi, i, 0))
    vrows = NSA_DH + NSA_ONES
    vt_shape = jax.ShapeDtypeStruct((b, NSA_GROUPS, s // tk, vrows, tk), BF16)
    vt_out = pl.BlockSpec((1, NSA_GROUPS, ts // tk, vrows, tk), lambda bi, i: (bi, 0, i, 0, 0))
    return pl.pallas_call(
        _nsa_prep_kernel,
        out_shape=(jax.ShapeDtypeStruct((b, s, NSA_HEADS * NSA_DH), BF16),
                   jax.ShapeDtypeStruct((b, s, kvw), F32),
                   jax.ShapeDtypeStruct((b, s, kvw), BF16),
                   vt_shape,
                   jax.ShapeDtypeStruct((b, s, kvw), BF16),
                   vt_shape),
        grid=(b, s // ts),
        in_specs=[
            pl.BlockSpec((1, ts, NSA_HEADS * NSA_DH), col(3)),
            kv_spec(16), kv_spec(18), kv_spec(19), kv_spec(20), kv_spec(21),
            pl.BlockSpec((ts, NSA_DH), lambda bi, i: (i, 0)),
            pl.BlockSpec((ts, NSA_DH), lambda bi, i: (i, 0)),
        ],
        out_specs=(pl.BlockSpec((1, ts, NSA_HEADS * NSA_DH), lambda bi, i: (bi, i, 0)),
                   kv_out, kv_out, vt_out, kv_out, vt_out),
        compiler_params=_cparams(("parallel", "parallel")),
        name="nsa_prep",
    )(proj, proj, proj, proj, proj, proj, cos_full, sin_signed)


def _compress_kernel(t_ref, pe_ref, w1_ref, w2_ref, o_ref, sh_scr, *, transposed):
    n = t_ref.shape[1] // CMP_STRIDE
    dh = NSA_DH
    half = CMP_LEN // 2
    acc_a = jnp.zeros((n, CMP_HIDDEN), F32)
    acc_b = jnp.zeros((n, CMP_HIDDEN), F32)
    for l in range(half):
        t = t_ref[0, pl.ds(l, n, stride=CMP_STRIDE), :]
        acc_a = acc_a + _dot((t + pe_ref[l:l + 1, :]).astype(BF16), w1_ref[l * dh:(l + 1) * dh, :])
        acc_b = acc_b + _dot((t + pe_ref[half + l:half + l + 1, :]).astype(BF16),
                             w1_ref[(half + l) * dh:(half + l + 1) * dh, :])
    sh_scr[pl.ds(0, n), :] = acc_b
    sh_scr[pl.ds(n, 8), :] = jnp.zeros((8, CMP_HIDDEN), F32)
    hid = acc_a + sh_scr[pl.ds(1, n), :]
    out = _dot(_silu(hid).astype(BF16), w2_ref[...])
    if transposed:
        tk = NSA_TILE
        for r in range(n // tk):
            o_ref[0, 0, r] = out[r * tk:(r + 1) * tk, :].T.astype(o_ref.dtype)
    else:
        o_ref[0, 0] = out.astype(o_ref.dtype)


def _nsa_compress(src, col0, pe, w1, w2, transposed):
    b, s, _ = src.shape
    n = s // CMP_STRIDE
    base = col0 // NSA_DH
    tk = NSA_TILE
    if transposed:
        out_shape = jax.ShapeDtypeStruct((b, NSA_GROUPS, n // tk, NSA_DH, tk), BF16)
        out_spec = pl.BlockSpec((1, 1, n // tk, NSA_DH, tk), lambda bi, g: (bi, g, 0, 0, 0))
    else:
        out_shape = jax.ShapeDtypeStruct((b, NSA_GROUPS, n, NSA_DH), BF16)
        out_spec = pl.BlockSpec((1, 1, n, NSA_DH), lambda bi, g: (bi, g, 0, 0))
    return pl.pallas_call(
        functools.partial(_compress_kernel, transposed=transposed),
        out_shape=out_shape,
        grid=(b, NSA_GROUPS),
        in_specs=[
            pl.BlockSpec((1, s, NSA_DH), lambda bi, g: (bi, 0, base + g)),
            pl.BlockSpec((CMP_LEN, NSA_DH), lambda bi, g: (0, 0)),
            pl.BlockSpec((CMP_LEN * NSA_DH, CMP_HIDDEN), lambda bi, g: (0, 0)),
            pl.BlockSpec((CMP_HIDDEN, NSA_DH), lambda bi, g: (0, 0)),
        ],
        out_specs=out_spec,
        scratch_shapes=[pltpu.VMEM((n + 8, CMP_HIDDEN), F32)],
        compiler_params=_cparams(("parallel", "parallel")),
        name="nsa_compress",
    )(src, pe, w1.astype(BF16), w2.astype(BF16))


def _weights_bf16(s_tiles, m):
    return [jnp.exp((s - m).astype(BF16)) for s in s_tiles]


def _flash_update(s_tiles, vt_tiles, m_ref, acc_ref):
    m_old = m_ref[...]
    m_new = m_old
    for s in s_tiles:
        m_new = jnp.maximum(m_new, jnp.max(s, axis=0, keepdims=True))
    alpha = jnp.exp(m_old - m_new)
    acc_ref[...] = alpha * acc_ref[...] + _pv(vt_tiles, _weights_bf16(s_tiles, m_new))
    m_ref[...] = m_new


def _pv(vt_tiles, p_tiles):
    vt = jnp.concatenate(vt_tiles, axis=1)
    p = jnp.concatenate([p.astype(BF16) for p in p_tiles], axis=0)
    return _dot(vt, p)


def _softmax_tiles(s_tiles, vt_tiles):
    m = jnp.max(s_tiles[0], axis=0, keepdims=True)
    for s in s_tiles[1:]:
        m = jnp.maximum(m, jnp.max(s, axis=0, keepdims=True))
    p_tiles = [jnp.exp(s - m) for s in s_tiles]
    l = jnp.sum(p_tiles[0], axis=0, keepdims=True)
    for p in p_tiles[1:]:
        l = l + jnp.sum(p, axis=0, keepdims=True)
    return p_tiles, l, _pv(vt_tiles, p_tiles)


def _softmax_tiles_aug(s_tiles, vt_tiles):
    m = jnp.max(s_tiles[0], axis=0, keepdims=True)
    for s in s_tiles[1:]:
        m = jnp.maximum(m, jnp.max(s, axis=0, keepdims=True))
    acc = _pv(vt_tiles, _weights_bf16(s_tiles, m))
    return acc[0:NSA_DH] * (1.0 / acc[NSA_DH:NSA_DH + 1])


def _nsa_kernel(q_ref, kc_ref, vct_ref, ovt_ref, ks_ref, vst_ref, kw_ref, vwt_ref, gate_ref, nz_ref,
                o_ref, m_scr, accs_scr, acc_scr, cap_scr, capd_scr, capl_scr, sa_scr, sb_scr):
    tq = q_ref.shape[1]
    tk = NSA_TILE
    dh = NSA_DH
    hpg = NSA_HPG
    lanes = hpg * tq
    ns = ovt_ref.shape[0]
    qtiles = tq // tk
    qi = pl.program_id(2)
    t0 = qi * tq
    kt0 = qi * qtiles

    qt = jnp.concatenate([q_ref[0, :, j * dh:(j + 1) * dh].astype(F32).T for j in range(hpg)],
                         axis=1).astype(BF16)
    rowi = lax.broadcasted_iota(jnp.int32, (tk, lanes), 0)
    qpos = lax.broadcasted_iota(jnp.int32, (tk, lanes), 1) & (tq - 1)
    for d in range(qtiles):
        capd_scr[d] = jnp.where(rowi + d * tk <= qpos, FORCE, NEG)
        capl_scr[d] = jnp.where(rowi + d * tk > qpos, FORCE, NEG)

    n_ct = kc_ref.shape[2] // tk
    s_all = _dot(kc_ref[0, 0], qt)
    s_tiles = []
    for c in range(n_ct):
        first_end = c * (tk * CMP_STRIDE) + (CMP_LEN - 1) - t0
        cap = jnp.where(rowi * CMP_STRIDE + first_end <= qpos, FORCE, NEG)
        s_tiles.append(jnp.minimum(s_all[c * tk:(c + 1) * tk, :], cap))
    p_tiles, l_c, acc_c = _softmax_tiles(s_tiles, [vct_ref[0, 0, c] for c in range(n_ct)])
    p_hi = [p.astype(BF16) for p in p_tiles]
    p_lo = [(p - h.astype(F32)).astype(BF16) for p, h in zip(p_tiles, p_hi)]
    imp_2 = _dot(ovt_ref[...], jnp.concatenate([jnp.concatenate(p_hi, axis=0),
                                                 jnp.concatenate(p_lo, axis=0)], axis=1))
    imp_un = imp_2[:, 0:lanes] + imp_2[:, lanes:2 * lanes]
    tl = t0 + (lax.broadcasted_iota(jnp.int32, (1, lanes), 1) & (tq - 1))
    inv_l = jnp.where(tl >= CMP_LEN - 1, 1.0 / l_c, 0.0)
    acc_scr[0] = acc_c * inv_l
    impn = imp_un * inv_l
    imp = impn[:, 0:tq]
    for j in range(1, hpg):
        imp = imp + impn[:, j * tq:(j + 1) * tq]

    blk = lax.broadcasted_iota(jnp.int32, (ns, tq), 0)
    blk_t = jnp.right_shift(t0 + lax.broadcasted_iota(jnp.int32, (ns, tq), 1), SLC_SHIFT)
    score = jnp.where(blk == 0, FORCE,
                      jnp.where(blk == blk_t, FORCE,
                                jnp.where(blk == blk_t - 1, FORCE,
                                          jnp.where(blk <= blk_t, imp, NEG))))
    blk_f = blk.astype(F32)
    cap_sel = jnp.full((ns, tq), NEG, F32)
    for _ in range(min(SLC_TOPK, ns)):
        mx = jnp.max(score, axis=0, keepdims=True)
        first = jnp.min(jnp.where(score == mx, blk_f, float(ns)), axis=0, keepdims=True)
        pick = blk_f == first
        cap_sel = jnp.where(pick, FORCE, cap_sel)
        score = jnp.where(pick, -jnp.inf, score)
    for j in range(hpg):
        cap_scr[:, j * tq:(j + 1) * tq] = cap_sel

    bpt = tk // SLC_BLOCK

    def slc_scores(step):
        k0 = pl.multiple_of(step * (nb * tk), nb * tk)
        s = _dot(ks_ref[0, pl.ds(k0, nb * tk), :], qt)
        out = []
        for r in range(nb):
            caps = [jnp.broadcast_to(cap_scr[pl.ds((step * nb + r) * bpt + i, 1), :], (SLC_BLOCK, lanes))
                    for i in range(bpt)]
            out.append(jnp.minimum(s[r * tk:(r + 1) * tk, :], jnp.concatenate(caps, axis=0)))
        return out

    wt = WINDOW // tk
    s_tiles, vt_tiles = [], []
    for d in range(-wt, qtiles):
        kt = kt0 + d
        ktc = jnp.maximum(kt, 0)
        s = _dot(kw_ref[0, pl.ds(pl.multiple_of(ktc * tk, tk), tk), :], qt)
        if d + wt < qtiles:
            s = jnp.minimum(s, capl_scr[d + wt])
        if d >= 0:
            s = jnp.minimum(s, capd_scr[d])
        else:
            s = jnp.minimum(s, jnp.where(kt >= 0, FORCE, NEG))
        s_tiles.append(s)
        vt_tiles.append(vwt_ref[0, 0, ktc])
    acc_scr[1] = _softmax_tiles_aug(s_tiles, vt_tiles)

    nb = NSA_TILES_PER_STEP
    m_scr[...] = jnp.full((1, lanes), NEG, F32)
    accs_scr[...] = jnp.zeros_like(accs_scr)
    state = (m_scr, accs_scr)
    n_full = kt0 // nb

    def score_step(step, dst):
        for r, s in enumerate(slc_scores(step)):
            dst[r * tk:(r + 1) * tk, :] = s

    def value_step(step, src):
        _flash_update([src[r * tk:(r + 1) * tk, :] for r in range(nb)],
                      [vst_ref[0, 0, step * nb + r] for r in range(nb)], *state)

    @pl.when(n_full > 0)
    def _():
        score_step(0, sa_scr)

    def slc_pair(j, carry):
        score_step(2 * j + 1, sb_scr)
        value_step(2 * j, sa_scr)
        score_step(jnp.minimum(2 * j + 2, n_full - 1), sa_scr)
        value_step(2 * j + 1, sb_scr)
        return carry

    lax.fori_loop(0, n_full // 2, slc_pair, 0)

    @pl.when(n_full % 2 == 1)
    def _():
        value_step(n_full - 1, sa_scr)

    s_tiles, vt_tiles = [], []
    for r, s in enumerate(slc_scores(n_full)):
        kt = n_full * nb + r
        cap = jnp.where(kt < kt0 + qtiles, FORCE, NEG)
        for d in range(qtiles):
            cap = jnp.minimum(cap, jnp.maximum(capd_scr[d], jnp.where(kt == kt0 + d, NEG, FORCE)))
        s_tiles.append(jnp.minimum(s, cap))
        vt_tiles.append(vst_ref[0, 0, kt])
    _flash_update(s_tiles, vt_tiles, *state)

    o_c = acc_scr[0]
    o_s = accs_scr[0:dh, :] * (1.0 / accs_scr[dh:dh + 1, :])
    o_w = acc_scr[1]
    sgt = _sigmoid(gate_ref[0]).T
    for j in range(hpg):
        ls = slice(j * tq, (j + 1) * tq)
        mix_t = (sgt[3 * j:3 * j + 1, :] * o_c[:, ls] + sgt[3 * j + 1:3 * j + 2, :] * o_s[:, ls]
                 + sgt[3 * j + 2:3 * j + 3, :] * o_w[:, ls])
        sl = slice(j * dh, (j + 1) * dh)
        o_ref[0, :, sl] = (mix_t.T * _silu(nz_ref[0, :, sl])).astype(o_ref.dtype)


def _nsa_attention(qr, kc, vct, ksr, vst, kwr, vwt, small, proj):
    b, s, _ = qr.shape
    tk = NSA_TILE
    tq = min(NSA_QUERY_TILE, s)
    qtiles = tq // tk
    assert s % tq == 0 and tq % tk == 0 and WINDOW % tk == 0 and tk % SLC_BLOCK == 0
    assert NSA_TILES_PER_STEP % qtiles == 0 and WINDOW // tk >= qtiles
    assert (s // tk) % NSA_TILES_PER_STEP == 0
    ncp = kc.shape[2]
    assert ncp % tk == 0
    ns = s // SLC_BLOCK
    gw = NSA_HPG * NSA_DH
    lanes = NSA_HPG * tq
    cs = np.arange(ncp)[None, :] * CMP_STRIDE
    ss = np.arange(ns)[:, None] * SLC_BLOCK
    ovt = ((np.minimum(cs + CMP_LEN, ss + SLC_BLOCK) - np.maximum(cs, ss)) > 0).astype(np.float32)
    ovt[:, ncp - 1] = 0.0
    kv_spec = pl.BlockSpec((1, s, NSA_DH), lambda bi, g, i: (bi, 0, g))
    vrows = NSA_DH + NSA_ONES
    vt_spec = pl.BlockSpec((1, 1, s // tk, vrows, tk), lambda bi, g, i: (bi, g, 0, 0, 0))
    return pl.pallas_call(
        _nsa_kernel,
        out_shape=jax.ShapeDtypeStruct((b, s, NSA_HEADS * NSA_DH), BF16),
        grid=(b, NSA_GROUPS, s // tq),
        in_specs=[
            pl.BlockSpec((1, tq, gw), lambda bi, g, i: (bi, i, g)),
            pl.BlockSpec((1, 1, ncp, NSA_DH), lambda bi, g, i: (bi, g, 0, 0)),
            pl.BlockSpec((1, 1, ncp // tk, NSA_DH, tk), lambda bi, g, i: (bi, g, 0, 0, 0)),
            pl.BlockSpec((ns, ncp), lambda bi, g, i: (0, 0)),
            kv_spec, vt_spec, kv_spec, vt_spec,
            pl.BlockSpec((1, tq, 128), lambda bi, g, i: (bi, i, 1 + g)),
            pl.BlockSpec((1, tq, gw), lambda bi, g, i: (bi, i, 11 + g)),
        ],
        out_specs=pl.BlockSpec((1, tq, gw), lambda bi, g, i: (bi, i, g)),
        scratch_shapes=[
            pltpu.VMEM((1, lanes), F32),
            pltpu.VMEM((vrows, lanes), F32),
            pltpu.VMEM((2, NSA_DH, lanes), F32),
            pltpu.VMEM((ns, lanes), F32),
            pltpu.VMEM((qtiles, tk, lanes), F32),
            pltpu.VMEM((qtiles, tk, lanes), F32),
            pltpu.VMEM((NSA_TILES_PER_STEP * tk, lanes), F32),
            pltpu.VMEM((NSA_TILES_PER_STEP * tk, lanes), F32),
        ],
        compiler_params=_cparams(("parallel", "parallel", "arbitrary")),
        name="nsa_attention",
    )(qr, kc, vct, jnp.asarray(ovt, dtype=BF16), ksr, vst, kwr, vwt, small, proj)


def _nsa_branch(proj, small, pe_k, w1_k, w2_k, pe_v, w1_v, w2_v):
    qr, kcr, ksr, vst, kwr, vwt = _nsa_prep(proj)
    kc = _nsa_compress(kcr, 0, pe_k, w1_k, w2_k, transposed=False)
    vct = _nsa_compress(proj, 4352, pe_v, w1_v, w2_v, transposed=True)
    return _nsa_attention(qr, kc, vct, ksr, vst, kwr, vwt, small, proj)


WPREP_TILE = 512
WPREP_LANES = 128


def _wprep_kernel(*refs, regions):
    o_ref = refs[-1]
    j = pl.program_id(0)
    src = jnp.concatenate([r[...] for r in refs[:-1]], axis=1)
    for j0, j1, shift in regions:
        @pl.when(jnp.logical_and(j >= j0, j < j1))
        def _(shift=shift):
            o_ref[...] = src[:, shift:shift + WPREP_TILE].astype(o_ref.dtype)


def _repack_weight(w_in, regions):
    d = w_in.shape[0]
    nblk = WPREP_TILE // WPREP_LANES
    n_tiles = regions[-1][1]
    in_specs = [pl.BlockSpec((d, WPREP_LANES), (lambda j, t=t: (0, j * nblk + t))) for t in range(nblk + 1)]
    return pl.pallas_call(
        functools.partial(_wprep_kernel, regions=regions),
        out_shape=jax.ShapeDtypeStruct((d, n_tiles * WPREP_TILE), BF16),
        grid=(n_tiles,),
        in_specs=in_specs,
        out_specs=pl.BlockSpec((d, WPREP_TILE), lambda j: (0, j)),
        compiler_params=_cparams(("parallel",)),
        name="weight_repack",
    )(*([w_in] * (nblk + 1)))


def _even_weights(w_in):
    main = _repack_weight(w_in, ((0, 4, 0), (4, 11, GLA_RANK), (11, 13, GLA_RANK + NSA_HEADS * 3)))
    gates = w_in[:, 5648:5672]
    per = NSA_HPG * 3
    blocks = [jnp.pad(w_in[:, 2048:2064], ((0, 0), (0, 128 - GLA_RANK)))]
    for g in range(NSA_GROUPS):
        blocks.append(jnp.pad(gates[:, g * per:(g + 1) * per], ((0, 0), (0, 128 - per))))
    small = jnp.concatenate(blocks, axis=1)
    return main, small.astype(BF16)


def _odd_weights(w_in):
    main = _repack_weight(w_in, ((0, MAIN_COLS // WPREP_TILE, 0),))
    small = jnp.pad(w_in[:, MAIN_COLS:MAIN_COLS + SSM_HEADS], ((0, 0), (0, 128 - SSM_HEADS)))
    return main, small.astype(BF16)


def kernel(x, c, ada_w, ada_b, pre_norm_w, post_norm_w, even_w_in, even_w_out, gla_w_up, gla_b_up,
           gla_norm_w, nsa_pe_k, nsa_w1_k, nsa_w2_k, nsa_pe_v, nsa_w1_v, nsa_w2_v, odd_w_in,
           odd_w_out, hgrn_lb_logits, hgrn_norm_w, ssm_conv_w, ssm_conv_b, ssm_dt_bias, ssm_a_log,
           ssm_d, ssm_norm_w):
    depth = ada_w.shape[0]
    mod = _adaln_mod(c, ada_w, ada_b)
    for l in range(depth):
        if l % 2 == 0:
            e = l // 2
            w_main, w_small = _even_weights(even_w_in[e])
            proj, small = _in_proj(x, mod[l], pre_norm_w[l], w_main, w_small)
            o_a = _gla_branch(proj, small, gla_w_up[e], gla_b_up[e], gla_norm_w[e])
            o_b = _nsa_branch(proj, small, nsa_pe_k[e], nsa_w1_k[e], nsa_w2_k[e],
                              nsa_pe_v[e], nsa_w1_v[e], nsa_w2_v[e])
            w_out = even_w_out[e]
        else:
            o = l // 2
            w_main, w_small = _odd_weights(odd_w_in[o])
            proj, small = _in_proj(x, mod[l], pre_norm_w[l], w_main, w_small)
            o_a = _hgrn_branch(proj, hgrn_lb_logits, hgrn_norm_w[o], l)
            o_b = _ssd_branch(proj, small, ssm_conv_w[o], ssm_conv_b[o], ssm_dt_bias[o],
                              ssm_a_log[o], ssm_d[o], ssm_norm_w[o])
            w_out = odd_w_out[o]
        x = _out_proj(o_a, o_b, w_out.astype(BF16), x, mod[l], post_norm_w[l])
    return x
```

```python
import functools
import math

import jax
import jax.numpy as jnp
import numpy as np
from jax import lax
from jax.experimental import pallas as pl
from jax.experimental.pallas import tpu as pltpu

F32 = jnp.float32
BF16 = jnp.bfloat16

D_MODEL = 2048
EPS = 1e-6
NEG = -1e30
FORCE = 1e30
ROPE_THETA = 10000.0

GLA_HEADS = 4
GLA_DK = 128
GLA_DV = 256
GLA_RANK = 16
GLA_TAU = 16.0

NSA_DH = 128
NSA_HEADS = 8
NSA_GROUPS = 2
NSA_HPG = 4
CMP_LEN = 32
CMP_STRIDE = 16
CMP_HIDDEN = 256
SLC_BLOCK = 64
SLC_SHIFT = 6
SLC_TOPK = 16
WINDOW = 512
NSA_TILE = 128
NSA_QUERY_TILE = 256
NSA_ONES = 16
NSA_TILES_PER_STEP = 4

HGRN_HEADS = 8
HGRN_DK = 128
HGRN_DV = 128

SSM_HEADDIM = 64
SSM_HEADS = 16
SSM_GROUPS = 2
SSM_STATE = 128
SSM_CONV = 4
SSM_CHUNK = 256
SSM_WIDTH = 1024
SSM_CONV_CH = SSM_WIDTH + 2 * SSM_GROUPS * SSM_STATE

LIN_CHUNK = 128
MAIN_COLS = 6656
VMEM_LIMIT = 56 * 1024 * 1024


def _cparams(sem):
    return pltpu.CompilerParams(dimension_semantics=sem, vmem_limit_bytes=VMEM_LIMIT)


def _dot(a, b):
    return jnp.dot(a, b, preferred_element_type=F32)


def _dot_nt(a, b):
    return lax.dot_general(a, b, (((1,), (1,)), ((), ())), preferred_element_type=F32)


def _dot_tn(a, b):
    return lax.dot_general(a, b, (((0,), (0,)), ((), ())), preferred_element_type=F32)


def _split3(x):
    hi = x.astype(BF16)
    r = x - hi.astype(F32)
    mid = r.astype(BF16)
    lo = (r - mid.astype(F32)).astype(BF16)
    return hi, mid, lo


def _sel_dot_left(sel, x):
    n = x.shape[1]
    y = _dot(sel.astype(BF16), jnp.concatenate(_split3(x), axis=1))
    return y[:, 0:n] + y[:, n:2 * n] + y[:, 2 * n:3 * n]


def _sel_dot_right(x, sel):
    m = x.shape[0]
    y = _dot(jnp.concatenate(_split3(x), axis=0), sel.astype(BF16))
    return y[0:m] + y[m:2 * m] + y[2 * m:3 * m]


def _sigmoid(x):
    return 1.0 / (1.0 + jnp.exp(-x))


def _silu(x):
    return x * _sigmoid(x)


def _log1p_exp_neg_abs(x):
    return jnp.log(1.0 + jnp.exp(-jnp.abs(x)))


def _log_sigmoid(x):
    return jnp.minimum(x, 0.0) - _log1p_exp_neg_abs(x)


def _softplus(x):
    return jnp.maximum(x, 0.0) + _log1p_exp_neg_abs(x)


def _logaddexp(a, b):
    return jnp.maximum(a, b) + _log1p_exp_neg_abs(a - b)


def _mod_kernel(c_ref, w_ref, b_ref, o_ref):
    ca = _silu(c_ref[...])
    o_ref[0] = _dot(ca, w_ref[0]) + b_ref[0]


def _adaln_mod(c, ada_w, ada_b):
    depth, d, n3 = ada_w.shape
    b = c.shape[0]
    rows = 8
    c_pad = jnp.pad(c, ((0, rows - b), (0, 0)))
    tn = 768
    out = pl.pallas_call(
        _mod_kernel,
        out_shape=jax.ShapeDtypeStruct((depth, rows, n3), F32),
        grid=(depth, n3 // tn),
        in_specs=[
            pl.BlockSpec((rows, d), lambda l, j: (0, 0)),
            pl.BlockSpec((1, d, tn), lambda l, j: (l, 0, j)),
            pl.BlockSpec((1, 1, tn), lambda l, j: (l, 0, j)),
        ],
        out_specs=pl.BlockSpec((1, rows, tn), lambda l, j: (l, 0, j)),
        compiler_params=_cparams(("parallel", "parallel")),
        name="adaln_mod",
    )(c_pad, ada_w, ada_b.reshape(depth, 1, n3))
    return out[:, :b]


def _inproj_kernel(x_ref, mod_ref, nw_ref, w_ref, ws_ref, o_ref, os_ref, h_scr):
    d = x_ref.shape[-1]

    @pl.when(pl.program_id(2) == 0)
    def _():
        x = x_ref[0]
        var = jnp.mean(x * x, axis=-1, keepdims=True)
        y = x * lax.rsqrt(var + EPS) * nw_ref[...]
        shift = mod_ref[0, :, 0:d]
        scale = mod_ref[0, :, d:2 * d]
        hb = (y * (1.0 + scale) + shift).astype(BF16)
        h_scr[...] = hb
        os_ref[0] = _dot_nt(hb, ws_ref[...])

    o_ref[0] = _dot_nt(h_scr[...], w_ref[...])


def _in_proj(x, mod_l, norm_w, w_main, w_small):
    b, s, d = x.shape
    n = w_main.shape[0]
    ns = w_small.shape[0]
    tm = min(1024, s)
    tn = 512
    return pl.pallas_call(
        _inproj_kernel,
        out_shape=(jax.ShapeDtypeStruct((b, s, n), F32),
                   jax.ShapeDtypeStruct((b, s, ns), F32)),
        grid=(b, s // tm, n // tn),
        in_specs=[
            pl.BlockSpec((1, tm, d), lambda bi, i, j: (bi, i, 0)),
            pl.BlockSpec((1, 1, 3 * d), lambda bi, i, j: (bi, 0, 0)),
            pl.BlockSpec((1, d), lambda bi, i, j: (0, 0)),
            pl.BlockSpec((tn, d), lambda bi, i, j: (j, 0)),
            pl.BlockSpec((ns, d), lambda bi, i, j: (0, 0)),
        ],
        out_specs=(pl.BlockSpec((1, tm, tn), lambda bi, i, j: (bi, i, j)),
                   pl.BlockSpec((1, tm, ns), lambda bi, i, j: (bi, i, 0))),
        scratch_shapes=[pltpu.VMEM((tm, d), BF16)],
        compiler_params=_cparams(("parallel", "parallel", "arbitrary")),
        name="in_proj",
    )(x, mod_l.reshape(b, 1, 3 * d), norm_w.reshape(1, d), w_main, w_small)


def _outproj_kernel(a1_ref, a2_ref, w_ref, x_ref, mod_ref, nw_ref, o_ref):
    d = x_ref.shape[-1]
    half = a1_ref.shape[-1]
    y = _dot(a1_ref[0], w_ref[0:half, :]) + _dot(a2_ref[0], w_ref[half:2 * half, :])
    var = jnp.mean(y * y, axis=-1, keepdims=True)
    yn = y * lax.rsqrt(var + EPS) * nw_ref[...]
    gate = mod_ref[0, :, 2 * d:3 * d]
    o_ref[0] = x_ref[0] + gate * yn


def _out_proj(a1, a2, w_out, x, mod_l, norm_w):
    b, s, d = x.shape
    half = a1.shape[-1]
    tm = min(512, s)
    return pl.pallas_call(
        _outproj_kernel,
        out_shape=jax.ShapeDtypeStruct((b, s, d), F32),
        grid=(b, s // tm),
        in_specs=[
            pl.BlockSpec((1, tm, half), lambda bi, i: (bi, i, 0)),
            pl.BlockSpec((1, tm, half), lambda bi, i: (bi, i, 0)),
            pl.BlockSpec((2 * half, d), lambda bi, i: (0, 0)),
            pl.BlockSpec((1, tm, d), lambda bi, i: (bi, i, 0)),
            pl.BlockSpec((1, 1, 3 * d), lambda bi, i: (bi, 0, 0)),
            pl.BlockSpec((1, d), lambda bi, i: (0, 0)),
        ],
        out_specs=pl.BlockSpec((1, tm, d), lambda bi, i: (bi, i, 0)),
        compiler_params=_cparams(("parallel", "parallel")),
        name="out_proj",
    )(a1, a2, w_out, x, mod_l.reshape(b, 1, 3 * d), norm_w.reshape(1, d))


def _cumsum_chains(tri, gs):
    n = gs[0].shape[1]
    terms = []
    for g in gs:
        terms.extend(_split3(g))
    y = _dot(tri, jnp.concatenate(terms, axis=1))
    return [y[:, (3 * i) * n:(3 * i + 1) * n] + y[:, (3 * i + 1) * n:(3 * i + 2) * n]
            + y[:, (3 * i + 2) * n:(3 * i + 3) * n] for i in range(len(gs))]


def _lin_attn_chunk(q, k, v, g, b, st_scr, b_scr, g_scr, consts):
    c, dk = q.shape
    row, eye, tri, pairs = consts
    b_scr[...] = b
    g_scr[pl.ds(8, c), :] = g
    b_last = b_scr[pl.ds(c - 1, 1), :]

    def level_operand(s, expo):
        upper = (row & s) != 0
        return (jnp.where(upper, q, k) * jnp.exp(expo)).astype(BF16)

    items = []
    for s in [c >> i for i in range(1, c.bit_length() - 2)]:
        pieces = [jnp.broadcast_to(b_scr[pl.ds(p * 2 * s + s - 1, 1), :], (2 * s, dk))
                  for p in range(c // (2 * s))]
        d = b - jnp.concatenate(pieces, axis=0)
        x = level_operand(s, jnp.where((row & s) != 0, d, -d))
        items.append((x, x, pairs[s]))
    g_dn = g_scr[pl.ds(7, c), :]
    g_up = g_scr[pl.ds(9, c), :]
    r4 = row & 3
    x = level_operand(2, jnp.where(r4 == 2, g, jnp.where(r4 == 3, g + g_dn, jnp.where(r4 == 0, g_up, 0.0))))
    items.append((x, x, pairs[2]))
    x = level_operand(1, jnp.where((row & 1) != 0, g, 0.0))
    items.append((x, x, pairs[1]))
    items.append((q.astype(BF16), k.astype(BF16), eye))

    attn = jnp.zeros((c, c), F32)
    for i in range(0, len(items), 2):
        grp = items[i:i + 2]
        prod = _dot_nt(jnp.concatenate([it[0] for it in grp], axis=0),
                       jnp.concatenate([it[1] for it in grp], axis=0))
        for n, it in enumerate(grp):
            attn = jnp.where(it[2], prod[n * c:(n + 1) * c, n * c:(n + 1) * c], attn)

    v16 = v.astype(BF16)
    st = st_scr[...]
    o = _dot(jnp.concatenate([attn.astype(BF16), (q * jnp.exp(b)).astype(BF16)], axis=1),
             jnp.concatenate([v16, st.astype(BF16)], axis=0))
    k_dec = (k * jnp.exp(b_last - b)).astype(BF16)
    e_col = jnp.broadcast_to(jnp.exp(b_last), (dk, dk)).T
    e_col = jnp.concatenate([e_col] * (v.shape[1] // dk), axis=1)
    st_scr[...] = st * e_col + _dot_tn(k_dec, v16)
    return o


def _lin_attn_consts(c, dk):
    row = lax.broadcasted_iota(jnp.int32, (c, dk), 0)
    ri = lax.broadcasted_iota(jnp.int32, (c, c), 0)
    ci = lax.broadcasted_iota(jnp.int32, (c, c), 1)
    eye = ri == ci
    tri = jnp.where(ri >= ci, 1.0, 0.0).astype(BF16)
    pairs = {s: jnp.where((ri & -(2 * s)) == (ci & -(2 * s)), (ri & s) - (ci & s), 0) == s
             for s in [c >> i for i in range(1, c.bit_length())]}
    return row, eye, tri, pairs


def _lin_attn_finish(o, gz, nw):
    var = jnp.mean(o * o, axis=-1, keepdims=True)
    return (o * lax.rsqrt(var + EPS) * nw) * _silu(gz)


LIN_HEADS_PER_STEP = 4


def _gla_kernel(q_ref, k_ref, v_ref, gz_ref, glr_ref, wup_ref, bup_ref, nw_ref, o_ref,
                st_scr, b_scr, g_scr):
    c = LIN_CHUNK
    dk, dv = GLA_DK, GLA_DV
    nb, ts = q_ref.shape[0], q_ref.shape[1]
    hp = q_ref.shape[2] // dk

    @pl.when(pl.program_id(1) == 0)
    def _():
        st_scr[...] = jnp.zeros_like(st_scr)

    g_scr[...] = jnp.zeros_like(g_scr)
    consts = _lin_attn_consts(c, dk)
    nw = nw_ref[...]
    q_scale = dk ** -0.5

    def body(ci, carry):
        r0 = pl.multiple_of(ci * c, c)
        rows = pl.ds(r0, c)
        gs = []
        for bb in range(nb):
            z = _dot(glr_ref[bb, rows, :], wup_ref[...]) + bup_ref[...]
            g_all = _log_sigmoid(z) * (1.0 / GLA_TAU)
            gs.extend(g_all[:, hh * dk:(hh + 1) * dk] for hh in range(hp))
        bs = _cumsum_chains(consts[2], gs)
        for bb in range(nb):
            for hh in range(hp):
                ks = slice(hh * dk, (hh + 1) * dk)
                vs = slice(hh * dv, (hh + 1) * dv)
                ch = bb * hp + hh
                q = q_ref[bb, rows, ks] * q_scale
                o = _lin_attn_chunk(q, k_ref[bb, rows, ks], v_ref[bb, rows, vs], gs[ch], bs[ch],
                                    st_scr.at[ch], b_scr.at[ch], g_scr.at[ch], consts)
                out = _lin_attn_finish(o, gz_ref[bb, rows, vs], nw)
                o_ref[bb, rows, vs] = out.astype(o_ref.dtype)
        return carry

    lax.fori_loop(0, ts // c, body, 0)


def _hgrn_kernel(q_ref, f_ref, v_ref, gz_ref, lbl_ref, nw_ref, o_ref, st_scr, b_scr, g_scr,
                 *, layer):
    c = LIN_CHUNK
    dk, dv = HGRN_DK, HGRN_DV
    nb, ts = q_ref.shape[0], q_ref.shape[1]
    hp = q_ref.shape[2] // dk

    @pl.when(pl.program_id(1) == 0)
    def _():
        st_scr[...] = jnp.zeros_like(st_scr)

    g_scr[...] = jnp.zeros_like(g_scr)
    consts = _lin_attn_consts(c, dk)
    nw = nw_ref[...]

    logits = lbl_ref[...]
    depth = logits.shape[0]
    mx = logits[0:1, :]
    for r in range(1, depth):
        mx = jnp.maximum(mx, logits[r:r + 1, :])
    ex = [jnp.exp(logits[r:r + 1, :] - mx) for r in range(depth)]
    den = ex[0]
    for r in range(1, depth):
        den = den + ex[r]
    sm = [e / den for e in ex]
    lb_all = sm[0]
    for r in range(1, layer + 1):
        lb_all = lb_all + sm[r]
    lb_all = lb_all - sm[0]
    log_lb_all = jnp.log(lb_all)
    log_1mlb_all = jnp.log1p(-lb_all)

    def body(ci, carry):
        r0 = pl.multiple_of(ci * c, c)
        rows = pl.ds(r0, c)
        gs, kk = [], []
        for bb in range(nb):
            for hh in range(hp):
                ks = slice(hh * dk, (hh + 1) * dk)
                z = f_ref[bb, rows, ks]
                gs.append(_logaddexp(log_lb_all[:, ks], log_1mlb_all[:, ks] + _log_sigmoid(z)))
                kk.append((1.0 - lb_all[:, ks]) * (1.0 / (1.0 + jnp.exp(z))))
        bs = _cumsum_chains(consts[2], gs)
        for bb in range(nb):
            for hh in range(hp):
                ks = slice(hh * dk, (hh + 1) * dk)
                vs = slice(hh * dv, (hh + 1) * dv)
                ch = bb * hp + hh
                o = _lin_attn_chunk(q_ref[bb, rows, ks], kk[ch], v_ref[bb, rows, vs], gs[ch], bs[ch],
                                    st_scr.at[ch], b_scr.at[ch], g_scr.at[ch], consts)
                out = _lin_attn_finish(o, gz_ref[bb, rows, vs], nw)
                o_ref[bb, rows, vs] = out.astype(o_ref.dtype)
        return carry

    lax.fori_loop(0, ts // c, body, 0)


def _lin_scratch(chains, dv, dk):
    return [pltpu.VMEM((chains, dk, dv), F32),
            pltpu.VMEM((chains, LIN_CHUNK, dk), F32),
            pltpu.VMEM((chains, LIN_CHUNK + 16, dk), F32)]


def _gla_branch(proj, small, w_up, b_up, norm_w):
    b, s, _ = proj.shape
    ts = min(512, s)
    dk, dv, h = GLA_DK, GLA_DV, GLA_HEADS
    hp = LIN_HEADS_PER_STEP
    kw, vw = hp * dk, hp * dv
    w_up_pad = jnp.pad(w_up, ((0, 128 - GLA_RANK), (0, 0)))
    return pl.pallas_call(
        _gla_kernel,
        out_shape=jax.ShapeDtypeStruct((b, s, h * dv), BF16),
        grid=(h // hp, s // ts),
        in_specs=[
            pl.BlockSpec((b, ts, kw), lambda hi, i: (0, i, hi)),
            pl.BlockSpec((b, ts, kw), lambda hi, i: (0, i, h // hp + hi)),
            pl.BlockSpec((b, ts, vw), lambda hi, i: (0, i, h // hp + hi)),
            pl.BlockSpec((b, ts, vw), lambda hi, i: (0, i, 2 * (h // hp) + hi)),
            pl.BlockSpec((b, ts, 128), lambda hi, i: (0, i, 0)),
            pl.BlockSpec((128, kw), lambda hi, i: (0, hi)),
            pl.BlockSpec((1, kw), lambda hi, i: (0, hi)),
            pl.BlockSpec((1, dv), lambda hi, i: (0, 0)),
        ],
        out_specs=pl.BlockSpec((b, ts, vw), lambda hi, i: (0, i, hi)),
        scratch_shapes=_lin_scratch(b * hp, dv, dk),
        compiler_params=_cparams(("parallel", "arbitrary")),
        name="gla",
    )(proj, proj, proj, proj, small, w_up_pad, b_up.reshape(1, -1), norm_w.reshape(1, dv))


def _hgrn_branch(proj, lb_logits, norm_w, layer):
    b, s, _ = proj.shape
    ts = min(512, s)
    dk, dv, h = HGRN_DK, HGRN_DV, HGRN_HEADS
    hp = LIN_HEADS_PER_STEP
    kw, vw = hp * dk, hp * dv
    ng = h // hp
    depth = lb_logits.shape[0]
    return pl.pallas_call(
        functools.partial(_hgrn_kernel, layer=layer),
        out_shape=jax.ShapeDtypeStruct((b, s, h * dv), BF16),
        grid=(ng, s // ts),
        in_specs=[
            pl.BlockSpec((b, ts, kw), lambda hi, i: (0, i, hi)),
            pl.BlockSpec((b, ts, kw), lambda hi, i: (0, i, ng + hi)),
            pl.BlockSpec((b, ts, vw), lambda hi, i: (0, i, 2 * ng + hi)),
            pl.BlockSpec((b, ts, vw), lambda hi, i: (0, i, 3 * ng + hi)),
            pl.BlockSpec((depth, kw), lambda hi, i: (0, hi)),
            pl.BlockSpec((1, dv), lambda hi, i: (0, 0)),
        ],
        out_specs=pl.BlockSpec((b, ts, vw), lambda hi, i: (0, i, hi)),
        scratch_shapes=_lin_scratch(b * hp, dv, dk),
        compiler_params=_cparams(("parallel", "arbitrary")),
        name="hgrn2",
    )(proj, proj, proj, proj, lb_logits, norm_w.reshape(1, dv))


def _ssd_kernel(z_ref, x_ref, bc_ref, dt_ref, cw_ref, cb_ref, dtb_ref, alog_ref, dexp_ref, nw_ref,
                hexp_ref, o_ref, stage_scr, xbc_scr, st_scr, y_scr):
    c = x_ref.shape[1]
    nch = SSM_CONV_CH
    width = SSM_WIDTH
    n = SSM_STATE
    pairs = SSM_HEADS // 2

    @pl.when(pl.program_id(1) == 0)
    def _():
        st_scr[...] = jnp.zeros_like(st_scr)
        stage_scr[pl.ds(0, 8), :] = jnp.zeros((8, nch), F32)

    stage_scr[pl.ds(8, c), 0:width] = x_ref[0]
    stage_scr[pl.ds(8, c), width:nch] = bc_ref[0]
    acc = cb_ref[...] + cw_ref[0:1, :] * stage_scr[pl.ds(5, c), :]
    for kk in range(1, SSM_CONV):
        acc = acc + cw_ref[kk:kk + 1, :] * stage_scr[pl.ds(5 + kk, c), :]
    stage_scr[pl.ds(0, 8), :] = stage_scr[pl.ds(c, 8), :]
    xbc_scr[...] = _silu(acc)

    dt = _softplus(dt_ref[0] + dtb_ref[...])
    a = -jnp.exp(alog_ref[...])
    da = dt * a
    ri = lax.broadcasted_iota(jnp.int32, (c, c), 0)
    ci = lax.broadcasted_iota(jnp.int32, (c, c), 1)
    causal = ri >= ci
    tri = jnp.where(causal, 1.0, 0.0).astype(BF16)
    a_cs = _sel_dot_left(tri, da)
    a_cs_t = a_cs.T
    hexp = hexp_ref[...]
    dt_x = _sel_dot_right(dt, hexp)
    acs_x = _sel_dot_right(a_cs, hexp)
    a_last_x = acs_x[c - 1:c, :]
    lane = lax.broadcasted_iota(jnp.int32, (c, 2 * SSM_HEADDIM), 1)
    first = lane < SSM_HEADDIM

    for grp in range(SSM_GROUPS):
        bm = xbc_scr[:, width + grp * n: width + (grp + 1) * n]
        cm = xbc_scr[:, width + (SSM_GROUPS + grp) * n: width + (SSM_GROUPS + grp + 1) * n]
        bm16 = bm.astype(BF16)
        cm16 = cm.astype(BF16)
        cb = _dot_nt(cm16, bm16)
        for pp in range(pairs // SSM_GROUPS):
            p = grp * (pairs // SSM_GROUPS) + pp
            lo = p * 2 * SSM_HEADDIM
            hi = lo + 2 * SSM_HEADDIM
            xs = xbc_scr[:, lo:hi]
            xdt = xs * dt_x[:, lo:hi]
            acs = acs_x[:, lo:hi]
            y = jnp.zeros((c, 2 * SSM_HEADDIM), F32)
            for hh in range(2):
                h = 2 * p + hh
                col = jnp.broadcast_to(a_cs[:, h:h + 1], (c, c))
                rw = jnp.broadcast_to(a_cs_t[h:h + 1, :], (c, c))
                lmat = jnp.exp(jnp.where(causal, col - rw, NEG))
                keep = first if hh == 0 else jnp.logical_not(first)
                xh = jnp.where(keep, xdt, 0.0).astype(BF16)
                y = y + _dot((cb * lmat).astype(BF16), xh)
            st = st_scr[p]
            y = y + _dot(cm16, st.astype(BF16)) * jnp.exp(acs)
            decay = jnp.exp(a_last_x[:, lo:hi] - acs)
            st_scr[p] = st * jnp.exp(a_last_x[:, lo:hi]) + _dot_tn(bm16, (xdt * decay).astype(BF16))
            y_scr[:, lo:hi] = y + dexp_ref[:, lo:hi] * xs

    yz = y_scr[...] * _silu(z_ref[0])
    var = jnp.mean(yz * yz, axis=-1, keepdims=True)
    o_ref[0] = (yz * lax.rsqrt(var + EPS) * nw_ref[...]).astype(o_ref.dtype)


def _ssd_branch(proj, small, conv_w, conv_b, dt_bias, a_log, d_skip, norm_w):
    b, s, _ = proj.shape
    c = math.gcd(SSM_CHUNK, s)
    nch = SSM_CONV_CH
    pad16 = lambda v: jnp.pad(v.reshape(1, -1), ((0, 0), (0, 128 - SSM_HEADS)))
    hexp = np.zeros((128, SSM_WIDTH), np.float32)
    for h in range(SSM_HEADS):
        hexp[h, h * SSM_HEADDIM:(h + 1) * SSM_HEADDIM] = 1.0
    d_exp = jnp.repeat(d_skip, SSM_HEADDIM).reshape(1, SSM_WIDTH)
    return pl.pallas_call(
        _ssd_kernel,
        out_shape=jax.ShapeDtypeStruct((b, s, SSM_WIDTH), BF16),
        grid=(b, s // c),
        in_specs=[
            pl.BlockSpec((1, c, SSM_WIDTH), lambda bi, i: (bi, i, 4)),
            pl.BlockSpec((1, c, SSM_WIDTH), lambda bi, i: (bi, i, 5)),
            pl.BlockSpec((1, c, 512), lambda bi, i: (bi, i, 12)),
            pl.BlockSpec((1, c, 128), lambda bi, i: (bi, i, 0)),
            pl.BlockSpec((SSM_CONV, nch), lambda bi, i: (0, 0)),
            pl.BlockSpec((1, nch), lambda bi, i: (0, 0)),
            pl.BlockSpec((1, 128), lambda bi, i: (0, 0)),
            pl.BlockSpec((1, 128), lambda bi, i: (0, 0)),
            pl.BlockSpec((1, SSM_WIDTH), lambda bi, i: (0, 0)),
            pl.BlockSpec((1, SSM_WIDTH), lambda bi, i: (0, 0)),
            pl.BlockSpec((128, SSM_WIDTH), lambda bi, i: (0, 0)),
        ],
        out_specs=pl.BlockSpec((1, c, SSM_WIDTH), lambda bi, i: (bi, i, 0)),
        scratch_shapes=[
            pltpu.VMEM((c + 8, nch), F32),
            pltpu.VMEM((c, nch), F32),
            pltpu.VMEM((SSM_HEADS // 2, SSM_STATE, 2 * SSM_HEADDIM), F32),
            pltpu.VMEM((c, SSM_WIDTH), F32),
        ],
        compiler_params=_cparams(("parallel", "arbitrary")),
        name="ssd",
    )(proj, proj, proj, small, conv_w, conv_b.reshape(1, nch), pad16(dt_bias), pad16(a_log),
      d_exp, norm_w.reshape(1, SSM_WIDTH), jnp.asarray(hexp, dtype=BF16))


def _rope(x, cos, sin_signed):
    return x * cos + pltpu.roll(x, NSA_DH // 2, axis=1) * sin_signed


def _nsa_prep_kernel(q_ref, kc_ref, ks_ref, vs_ref, kw_ref, vw_ref, cos_ref, sin_ref,
                     qo_ref, kco_ref, kso_ref, vso_ref, kwo_ref, vwo_ref):
    cos = cos_ref[...]
    sin = sin_ref[...]
    dh = NSA_DH
    scale = dh ** -0.5
    for h in range(NSA_HEADS):
        sl = slice(h * dh, (h + 1) * dh)
        qo_ref[0, :, sl] = (_rope(q_ref[0, :, sl], cos, sin) * scale).astype(qo_ref.dtype)
    for g in range(NSA_GROUPS):
        sl = slice(g * dh, (g + 1) * dh)
        kco_ref[0, :, sl] = _rope(kc_ref[0, :, sl], cos, sin)
        kso_ref[0, :, sl] = _rope(ks_ref[0, :, sl], cos, sin).astype(kso_ref.dtype)
        kwo_ref[0, :, sl] = _rope(kw_ref[0, :, sl], cos, sin).astype(kwo_ref.dtype)
    tk = NSA_TILE
    for g in range(NSA_GROUPS):
        for r in range(q_ref.shape[1] // tk):
            rows = slice(r * tk, (r + 1) * tk)
            sl = slice(g * dh, (g + 1) * dh)
            vso_ref[0, g, r, 0:dh, :] = vs_ref[0, rows, sl].T.astype(vso_ref.dtype)
            vwo_ref[0, g, r, 0:dh, :] = vw_ref[0, rows, sl].T.astype(vwo_ref.dtype)
            ones = jnp.ones((NSA_ONES, tk), vso_ref.dtype)
            vso_ref[0, g, r, dh:dh + NSA_ONES, :] = ones
            vwo_ref[0, g, r, dh:dh + NSA_ONES, :] = ones


def _nsa_prep(proj):
    b, s, _ = proj.shape
    ts = min(512, s)
    tk = NSA_TILE
    half = NSA_DH // 2
    inv_freq = ROPE_THETA ** (-jnp.arange(half, dtype=F32) / half)
    ang = jnp.arange(s, dtype=F32)[:, None] * inv_freq[None, :]
    cos = jnp.cos(ang)
    sin = jnp.sin(ang)
    cos_full = jnp.concatenate([cos, cos], axis=-1)
    sin_signed = jnp.concatenate([-sin, sin], axis=-1)
    kvw = NSA_GROUPS * NSA_DH
    col = lambda idx: (lambda bi, i: (bi, i, idx))
    kv_spec = lambda idx: pl.BlockSpec((1, ts, kvw), col(idx))
    kv_out = pl.BlockSpec((1, ts, kvw), lambda bi, i: (bi, i, 0))
    vrows = NSA_DH + NSA_ONES
    vt_shape = jax.ShapeDtypeStruct((b, NSA_GROUPS, s // tk, vrows, tk), BF16)
    vt_out = pl.BlockSpec((1, NSA_GROUPS, ts // tk, vrows, tk), lambda bi, i: (bi, 0, i, 0, 0))
    return pl.pallas_call(
        _nsa_prep_kernel,
        out_shape=(jax.ShapeDtypeStruct((b, s, NSA_HEADS * NSA_DH), BF16),
                   jax.ShapeDtypeStruct((b, s, kvw), F32),
                   jax.ShapeDtypeStruct((b, s, kvw), BF16),
                   vt_shape,
                   jax.ShapeDtypeStruct((b, s, kvw), BF16),
                   vt_shape),
        grid=(b, s // ts),
        in_specs=[
            pl.BlockSpec((1, ts, NSA_HEADS * NSA_DH), col(3)),
            kv_spec(16), kv_spec(18), kv_spec(19), kv_spec(20), kv_spec(21),
            pl.BlockSpec((ts, NSA_DH), lambda bi, i: (i, 0)),
            pl.BlockSpec((ts, NSA_DH), lambda bi, i: (i, 0)),
        ],
        out_specs=(pl.BlockSpec((1, ts, NSA_HEADS * NSA_DH), lambda bi, i: (bi, i, 0)),
                   kv_out, kv_out, vt_out, kv_out, vt_out),
        compiler_params=_cparams(("parallel", "parallel")),
        name="nsa_prep",
    )(proj, proj, proj, proj, proj, proj, cos_full, sin_signed)


def _compress_kernel(t_ref, pe_ref, w1_ref, w2_ref, o_ref, sh_scr, *, transposed):
    n = t_ref.shape[1] // CMP_STRIDE
    dh = NSA_DH
    half = CMP_LEN // 2
    acc_a = jnp.zeros((n, CMP_HIDDEN), F32)
    acc_b = jnp.zeros((n, CMP_HIDDEN), F32)
    for l in range(half):
        t = t_ref[0, pl.ds(l, n, stride=CMP_STRIDE), :]
        acc_a = acc_a + _dot((t + pe_ref[l:l + 1, :]).astype(BF16), w1_ref[l * dh:(l + 1) * dh, :])
        acc_b = acc_b + _dot((t + pe_ref[half + l:half + l + 1, :]).astype(BF16),
                             w1_ref[(half + l) * dh:(half + l + 1) * dh, :])
    sh_scr[pl.ds(0, n), :] = acc_b
    sh_scr[pl.ds(n, 8), :] = jnp.zeros((8, CMP_HIDDEN), F32)
    hid = acc_a + sh_scr[pl.ds(1, n), :]
    out = _dot(_silu(hid).astype(BF16), w2_ref[...])
    if transposed:
        tk = NSA_TILE
        for r in range(n // tk):
            o_ref[0, 0, r] = out[r * tk:(r + 1) * tk, :].T.astype(o_ref.dtype)
    else:
        o_ref[0, 0] = out.astype(o_ref.dtype)


def _nsa_compress(src, col0, pe, w1, w2, transposed):
    b, s, _ = src.shape
    n = s // CMP_STRIDE
    base = col0 // NSA_DH
    tk = NSA_TILE
    if transposed:
        out_shape = jax.ShapeDtypeStruct((b, NSA_GROUPS, n // tk, NSA_DH, tk), BF16)
        out_spec = pl.BlockSpec((1, 1, n // tk, NSA_DH, tk), lambda bi, g: (bi, g, 0, 0, 0))
    else:
        out_shape = jax.ShapeDtypeStruct((b, NSA_GROUPS, n, NSA_DH), BF16)
        out_spec = pl.BlockSpec((1, 1, n, NSA_DH), lambda bi, g: (bi, g, 0, 0))
    return pl.pallas_call(
        functools.partial(_compress_kernel, transposed=transposed),
        out_shape=out_shape,
        grid=(b, NSA_GROUPS),
        in_specs=[
            pl.BlockSpec((1, s, NSA_DH), lambda bi, g: (bi, 0, base + g)),
            pl.BlockSpec((CMP_LEN, NSA_DH), lambda bi, g: (0, 0)),
            pl.BlockSpec((CMP_LEN * NSA_DH, CMP_HIDDEN), lambda bi, g: (0, 0)),
            pl.BlockSpec((CMP_HIDDEN, NSA_DH), lambda bi, g: (0, 0)),
        ],
        out_specs=out_spec,
        scratch_shapes=[pltpu.VMEM((n + 8, CMP_HIDDEN), F32)],
        compiler_params=_cparams(("parallel", "parallel")),
        name="nsa_compress",
    )(src, pe, w1.astype(BF16), w2.astype(BF16))


def _weights_bf16(s_tiles, m):
    return [jnp.exp((s - m).astype(BF16)) for s in s_tiles]


def _flash_update(s_tiles, vt_tiles, m_ref, acc_ref):
    m_old = m_ref[...]
    m_new = m_old
    for s in s_tiles:
        m_new = jnp.maximum(m_new, jnp.max(s, axis=0, keepdims=True))
    alpha = jnp.exp(m_old - m_new)
    acc_ref[...] = alpha * acc_ref[...] + _pv(vt_tiles, _weights_bf16(s_tiles, m_new))
    m_ref[...] = m_new


def _pv(vt_tiles, p_tiles):
    vt = jnp.concatenate(vt_tiles, axis=1)
    p = jnp.concatenate([p.astype(BF16) for p in p_tiles], axis=0)
    return _dot(vt, p)


def _softmax_tiles(s_tiles, vt_tiles):
    m = jnp.max(s_tiles[0], axis=0, keepdims=True)
    for s in s_tiles[1:]:
        m = jnp.maximum(m, jnp.max(s, axis=0, keepdims=True))
    p_tiles = [jnp.exp(s - m) for s in s_tiles]
    l = jnp.sum(p_tiles[0], axis=0, keepdims=True)
    for p in p_tiles[1:]:
        l = l + jnp.sum(p, axis=0, keepdims=True)
    return p_tiles, l, _pv(vt_tiles, p_tiles)


def _softmax_tiles_aug(s_tiles, vt_tiles):
    m = jnp.max(s_tiles[0], axis=0, keepdims=True)
    for s in s_tiles[1:]:
        m = jnp.maximum(m, jnp.max(s, axis=0, keepdims=True))
    acc = _pv(vt_tiles, _weights_bf16(s_tiles, m))
    return acc[0:NSA_DH] * (1.0 / acc[NSA_DH:NSA_DH + 1])


def _nsa_kernel(q_ref, kc_ref, vct_ref, ovt_ref, ks_ref, vst_ref, kw_ref, vwt_ref, gate_ref, nz_ref,
                o_ref, m_scr, accs_scr, acc_scr, cap_scr, capd_scr, capl_scr, sa_scr, sb_scr):
    tq = q_ref.shape[1]
    tk = NSA_TILE
    dh = NSA_DH
    hpg = NSA_HPG
    lanes = hpg * tq
    ns = ovt_ref.shape[0]
    qtiles = tq // tk
    qi = pl.program_id(2)
    t0 = qi * tq
    kt0 = qi * qtiles

    qt = jnp.concatenate([q_ref[0, :, j * dh:(j + 1) * dh].astype(F32).T for j in range(hpg)],
                         axis=1).astype(BF16)
    rowi = lax.broadcasted_iota(jnp.int32, (tk, lanes), 0)
    qpos = lax.broadcasted_iota(jnp.int32, (tk, lanes), 1) & (tq - 1)
    for d in range(qtiles):
        capd_scr[d] = jnp.where(rowi + d * tk <= qpos, FORCE, NEG)
        capl_scr[d] = jnp.where(rowi + d * tk > qpos, FORCE, NEG)

    n_ct = kc_ref.shape[2] // tk
    s_all = _dot(kc_ref[0, 0], qt)
    s_tiles = []
    for c in range(n_ct):
        first_end = c * (tk * CMP_STRIDE) + (CMP_LEN - 1) - t0
        cap = jnp.where(rowi * CMP_STRIDE + first_end <= qpos, FORCE, NEG)
        s_tiles.append(jnp.minimum(s_all[c * tk:(c + 1) * tk, :], cap))
    p_tiles, l_c, acc_c = _softmax_tiles(s_tiles, [vct_ref[0, 0, c] for c in range(n_ct)])
    p_hi = [p.astype(BF16) for p in p_tiles]
    p_lo = [(p - h.astype(F32)).astype(BF16) for p, h in zip(p_tiles, p_hi)]
    imp_2 = _dot(ovt_ref[...], jnp.concatenate([jnp.concatenate(p_hi, axis=0),
                                                 jnp.concatenate(p_lo, axis=0)], axis=1))
    imp_un = imp_2[:, 0:lanes] + imp_2[:, lanes:2 * lanes]
    tl = t0 + (lax.broadcasted_iota(jnp.int32, (1, lanes), 1) & (tq - 1))
    inv_l = jnp.where(tl >= CMP_LEN - 1, 1.0 / l_c, 0.0)
    acc_scr[0] = acc_c * inv_l
    impn = imp_un * inv_l
    imp = impn[:, 0:tq]
    for j in range(1, hpg):
        imp = imp + impn[:, j * tq:(j + 1) * tq]

    blk = lax.broadcasted_iota(jnp.int32, (ns, tq), 0)
    blk_t = jnp.right_shift(t0 + lax.broadcasted_iota(jnp.int32, (ns, tq), 1), SLC_SHIFT)
    score = jnp.where(blk == 0, FORCE,
                      jnp.where(blk == blk_t, FORCE,
                                jnp.where(blk == blk_t - 1, FORCE,
                                          jnp.where(blk <= blk_t, imp, NEG))))
    blk_f = blk.astype(F32)
    cap_sel = jnp.full((ns, tq), NEG, F32)
    for _ in range(min(SLC_TOPK, ns)):
        mx = jnp.max(score, axis=0, keepdims=True)
        first = jnp.min(jnp.where(score == mx, blk_f, float(ns)), axis=0, keepdims=True)
        pick = blk_f == first
        cap_sel = jnp.where(pick, FORCE, cap_sel)
        score = jnp.where(pick, -jnp.inf, score)
    for j in range(hpg):
        cap_scr[:, j * tq:(j + 1) * tq] = cap_sel

    bpt = tk // SLC_BLOCK

    def slc_scores(step):
        k0 = pl.multiple_of(step * (nb * tk), nb * tk)
        s = _dot(ks_ref[0, pl.ds(k0, nb * tk), :], qt)
        out = []
        for r in range(nb):
            caps = [jnp.broadcast_to(cap_scr[pl.ds((step * nb + r) * bpt + i, 1), :], (SLC_BLOCK, lanes))
                    for i in range(bpt)]
            out.append(jnp.minimum(s[r * tk:(r + 1) * tk, :], jnp.concatenate(caps, axis=0)))
        return out

    wt = WINDOW // tk
    s_tiles, vt_tiles = [], []
    for d in range(-wt, qtiles):
        kt = kt0 + d
        ktc = jnp.maximum(kt, 0)
        s = _dot(kw_ref[0, pl.ds(pl.multiple_of(ktc * tk, tk), tk), :], qt)
        if d + wt < qtiles:
            s = jnp.minimum(s, capl_scr[d + wt])
        if d >= 0:
            s = jnp.minimum(s, capd_scr[d])
        else:
            s = jnp.minimum(s, jnp.where(kt >= 0, FORCE, NEG))
        s_tiles.append(s)
        vt_tiles.append(vwt_ref[0, 0, ktc])
    acc_scr[1] = _softmax_tiles_aug(s_tiles, vt_tiles)

    nb = NSA_TILES_PER_STEP
    m_scr[...] = jnp.full((1, lanes), NEG, F32)
    accs_scr[...] = jnp.zeros_like(accs_scr)
    state = (m_scr, accs_scr)
    n_full = kt0 // nb

    def score_step(step, dst):
        for r, s in enumerate(slc_scores(step)):
            dst[r * tk:(r + 1) * tk, :] = s

    def value_step(step, src):
        _flash_update([src[r * tk:(r + 1) * tk, :] for r in range(nb)],
                      [vst_ref[0, 0, step * nb + r] for r in range(nb)], *state)

    @pl.when(n_full > 0)
    def _():
        score_step(0, sa_scr)

    def slc_pair(j, carry):
        score_step(2 * j + 1, sb_scr)
        value_step(2 * j, sa_scr)
        score_step(jnp.minimum(2 * j + 2, n_full - 1), sa_scr)
        value_step(2 * j + 1, sb_scr)
        return carry

    lax.fori_loop(0, n_full // 2, slc_pair, 0)

    @pl.when(n_full % 2 == 1)
    def _():
        value_step(n_full - 1, sa_scr)

    s_tiles, vt_tiles = [], []
    for r, s in enumerate(slc_scores(n_full)):
        kt = n_full * nb + r
        cap = jnp.where(kt < kt0 + qtiles, FORCE, NEG)
        for d in range(qtiles):
            cap = jnp.minimum(cap, jnp.maximum(capd_scr[d], jnp.where(kt == kt0 + d, NEG, FORCE)))
        s_tiles.append(jnp.minimum(s, cap))
        vt_tiles.append(vst_ref[0, 0, kt])
    _flash_update(s_tiles, vt_tiles, *state)

    o_c = acc_scr[0]
    o_s = accs_scr[0:dh, :] * (1.0 / accs_scr[dh:dh + 1, :])
    o_w = acc_scr[1]
    sgt = _sigmoid(gate_ref[0]).T
    for j in range(hpg):
        ls = slice(j * tq, (j + 1) * tq)
        mix_t = (sgt[3 * j:3 * j + 1, :] * o_c[:, ls] + sgt[3 * j + 1:3 * j + 2, :] * o_s[:, ls]
                 + sgt[3 * j + 2:3 * j + 3, :] * o_w[:, ls])
        sl = slice(j * dh, (j + 1) * dh)
        o_ref[0, :, sl] = (mix_t.T * _silu(nz_ref[0, :, sl])).astype(o_ref.dtype)


def _nsa_attention(qr, kc, vct, ksr, vst, kwr, vwt, small, proj):
    b, s, _ = qr.shape
    tk = NSA_TILE
    tq = min(NSA_QUERY_TILE, s)
    qtiles = tq // tk
    assert s % tq == 0 and tq % tk == 0 and WINDOW % tk == 0 and tk % SLC_BLOCK == 0
    assert NSA_TILES_PER_STEP % qtiles == 0 and WINDOW // tk >= qtiles
    assert (s // tk) % NSA_TILES_PER_STEP == 0
    ncp = kc.shape[2]
    assert ncp % tk == 0
    ns = s // SLC_BLOCK
    gw = NSA_HPG * NSA_DH
    lanes = NSA_HPG * tq
    cs = np.arange(ncp)[None, :] * CMP_STRIDE
    ss = np.arange(ns)[:, None] * SLC_BLOCK
    ovt = ((np.minimum(cs + CMP_LEN, ss + SLC_BLOCK) - np.maximum(cs, ss)) > 0).astype(np.float32)
    ovt[:, ncp - 1] = 0.0
    kv_spec = pl.BlockSpec((1, s, NSA_DH), lambda bi, g, i: (bi, 0, g))
    vrows = NSA_DH + NSA_ONES
    vt_spec = pl.BlockSpec((1, 1, s // tk, vrows, tk), lambda bi, g, i: (bi, g, 0, 0, 0))
    return pl.pallas_call(
        _nsa_kernel,
        out_shape=jax.ShapeDtypeStruct((b, s, NSA_HEADS * NSA_DH), BF16),
        grid=(b, NSA_GROUPS, s // tq),
        in_specs=[
            pl.BlockSpec((1, tq, gw), lambda bi, g, i: (bi, i, g)),
            pl.BlockSpec((1, 1, ncp, NSA_DH), lambda bi, g, i: (bi, g, 0, 0)),
            pl.BlockSpec((1, 1, ncp // tk, NSA_DH, tk), lambda bi, g, i: (bi, g, 0, 0, 0)),
            pl.BlockSpec((ns, ncp), lambda bi, g, i: (0, 0)),
            kv_spec, vt_spec, kv_spec, vt_spec,
            pl.BlockSpec((1, tq, 128), lambda bi, g, i: (bi, i, 1 + g)),
            pl.BlockSpec((1, tq, gw), lambda bi, g, i: (bi, i, 11 + g)),
        ],
        out_specs=pl.BlockSpec((1, tq, gw), lambda bi, g, i: (bi, i, g)),
        scratch_shapes=[
            pltpu.VMEM((1, lanes), F32),
            pltpu.VMEM((vrows, lanes), F32),
            pltpu.VMEM((2, NSA_DH, lanes), F32),
            pltpu.VMEM((ns, lanes), F32),
            pltpu.VMEM((qtiles, tk, lanes), F32),
            pltpu.VMEM((qtiles, tk, lanes), F32),
            pltpu.VMEM((NSA_TILES_PER_STEP * tk, lanes), F32),
            pltpu.VMEM((NSA_TILES_PER_STEP * tk, lanes), F32),
        ],
        compiler_params=_cparams(("parallel", "parallel", "arbitrary")),
        name="nsa_attention",
    )(qr, kc, vct, jnp.asarray(ovt, dtype=BF16), ksr, vst, kwr, vwt, small, proj)


def _nsa_branch(proj, small, pe_k, w1_k, w2_k, pe_v, w1_v, w2_v):
    qr, kcr, ksr, vst, kwr, vwt = _nsa_prep(proj)
    kc = _nsa_compress(kcr, 0, pe_k, w1_k, w2_k, transposed=False)
    vct = _nsa_compress(proj, 4352, pe_v, w1_v, w2_v, transposed=True)
    return _nsa_attention(qr, kc, vct, ksr, vst, kwr, vwt, small, proj)


def _pad_rows(w, rows):
    return jnp.pad(w, ((0, rows - w.shape[0]), (0, 0)))


def _even_weights(w_in):
    wt = w_in.T
    main = jnp.concatenate([wt[0:2048], wt[2064:5648], wt[5672:6696]], axis=0)
    per = NSA_HPG * 3
    blocks = [_pad_rows(wt[2048:2048 + GLA_RANK], 128)]
    for g in range(NSA_GROUPS):
        blocks.append(_pad_rows(wt[5648 + g * per:5648 + (g + 1) * per], 128))
    return main.astype(BF16), jnp.concatenate(blocks, axis=0).astype(BF16)


def _odd_weights(w_in):
    wt = w_in.T
    small = _pad_rows(wt[MAIN_COLS:MAIN_COLS + SSM_HEADS], 128)
    return wt[0:MAIN_COLS].astype(BF16), small.astype(BF16)


def kernel(x, c, ada_w, ada_b, pre_norm_w, post_norm_w, even_w_in, even_w_out, gla_w_up, gla_b_up,
           gla_norm_w, nsa_pe_k, nsa_w1_k, nsa_w2_k, nsa_pe_v, nsa_w1_v, nsa_w2_v, odd_w_in,
           odd_w_out, hgrn_lb_logits, hgrn_norm_w, ssm_conv_w, ssm_conv_b, ssm_dt_bias, ssm_a_log,
           ssm_d, ssm_norm_w):
    depth = ada_w.shape[0]
    mod = _adaln_mod(c, ada_w, ada_b)
    for l in range(depth):
        if l % 2 == 0:
            e = l // 2
            w_main, w_small = _even_weights(even_w_in[e])
            proj, small = _in_proj(x, mod[l], pre_norm_w[l], w_main, w_small)
            o_a = _gla_branch(proj, small, gla_w_up[e], gla_b_up[e], gla_norm_w[e])
            o_b = _nsa_branch(proj, small, nsa_pe_k[e], nsa_w1_k[e], nsa_w2_k[e],
                              nsa_pe_v[e], nsa_w1_v[e], nsa_w2_v[e])
            w_out = even_w_out[e]
        else:
            o = l // 2
            w_main, w_small = _odd_weights(odd_w_in[o])
            proj, small = _in_proj(x, mod[l], pre_norm_w[l], w_main, w_small)
            o_a = _hgrn_branch(proj, hgrn_lb_logits, hgrn_norm_w[o], l)
            o_b = _ssd_branch(proj, small, ssm_conv_w[o], ssm_conv_b[o], ssm_dt_bias[o],
                              ssm_a_log[o], ssm_d[o], ssm_norm_w[o])
            w_out = odd_w_out[o]
        x = _out_proj(o_a, o_b, w_out.astype(BF16), x, mod[l], post_norm_w[l])
    return x
```

```python
import functools
import math

import jax
import jax.numpy as jnp
import numpy as np
from jax import lax
from jax.experimental import pallas as pl
from jax.experimental.pallas import tpu as pltpu

F32 = jnp.float32
BF16 = jnp.bfloat16

D_MODEL = 2048
EPS = 1e-6
NEG = -1e30
FORCE = 1e30
ROPE_THETA = 10000.0

GLA_HEADS = 4
GLA_DK = 128
GLA_DV = 256
GLA_RANK = 16
GLA_TAU = 16.0

NSA_DH = 128
NSA_HEADS = 8
NSA_GROUPS = 2
NSA_HPG = 4
CMP_LEN = 32
CMP_STRIDE = 16
CMP_HIDDEN = 256
SLC_BLOCK = 64
SLC_SHIFT = 6
SLC_TOPK = 16
WINDOW = 512
NSA_TILE = 128
NSA_QUERY_TILE = 256
NSA_ONES = 16
NSA_TILES_PER_STEP = 4

HGRN_HEADS = 8
HGRN_DK = 128
HGRN_DV = 128

SSM_HEADDIM = 64
SSM_HEADS = 16
SSM_GROUPS = 2
SSM_STATE = 128
SSM_CONV = 4
SSM_CHUNK = 256
SSM_WIDTH = 1024
SSM_CONV_CH = SSM_WIDTH + 2 * SSM_GROUPS * SSM_STATE

LIN_CHUNK = 128
MAIN_COLS = 6656
VMEM_LIMIT = 56 * 1024 * 1024


def _cparams(sem):
    return pltpu.CompilerParams(dimension_semantics=sem, vmem_limit_bytes=VMEM_LIMIT)


def _dot(a, b):
    return jnp.dot(a, b, preferred_element_type=F32)


def _dot_nt(a, b):
    return lax.dot_general(a, b, (((1,), (1,)), ((), ())), preferred_element_type=F32)


def _dot_tn(a, b):
    return lax.dot_general(a, b, (((0,), (0,)), ((), ())), preferred_element_type=F32)


def _split3(x):
    hi = x.astype(BF16)
    r = x - hi.astype(F32)
    mid = r.astype(BF16)
    lo = (r - mid.astype(F32)).astype(BF16)
    return hi, mid, lo


def _sel_dot_left(sel, x):
    n = x.shape[1]
    y = _dot(sel.astype(BF16), jnp.concatenate(_split3(x), axis=1))
    return y[:, 0:n] + y[:, n:2 * n] + y[:, 2 * n:3 * n]


def _sel_dot_right(x, sel):
    m = x.shape[0]
    y = _dot(jnp.concatenate(_split3(x), axis=0), sel.astype(BF16))
    return y[0:m] + y[m:2 * m] + y[2 * m:3 * m]


def _sigmoid(x):
    return 1.0 / (1.0 + jnp.exp(-x))


def _silu(x):
    return x * _sigmoid(x)


def _log1p_exp_neg_abs(x):
    return jnp.log(1.0 + jnp.exp(-jnp.abs(x)))


def _log_sigmoid(x):
    return jnp.minimum(x, 0.0) - _log1p_exp_neg_abs(x)


def _softplus(x):
    return jnp.maximum(x, 0.0) + _log1p_exp_neg_abs(x)


def _logaddexp(a, b):
    return jnp.maximum(a, b) + _log1p_exp_neg_abs(a - b)


def _mod_kernel(c_ref, w_ref, b_ref, o_ref):
    ca = _silu(c_ref[...])
    o_ref[0] = _dot(ca, w_ref[0]) + b_ref[0]


def _adaln_mod(c, ada_w, ada_b):
    depth, d, n3 = ada_w.shape
    b = c.shape[0]
    rows = 8
    c_pad = jnp.pad(c, ((0, rows - b), (0, 0)))
    tn = 768
    out = pl.pallas_call(
        _mod_kernel,
        out_shape=jax.ShapeDtypeStruct((depth, rows, n3), F32),
        grid=(depth, n3 // tn),
        in_specs=[
            pl.BlockSpec((rows, d), lambda l, j: (0, 0)),
            pl.BlockSpec((1, d, tn), lambda l, j: (l, 0, j)),
            pl.BlockSpec((1, 1, tn), lambda l, j: (l, 0, j)),
        ],
        out_specs=pl.BlockSpec((1, rows, tn), lambda l, j: (l, 0, j)),
        compiler_params=_cparams(("parallel", "parallel")),
        name="adaln_mod",
    )(c_pad, ada_w, ada_b.reshape(depth, 1, n3))
    return out[:, :b]


def _inproj_kernel(x_ref, mod_ref, nw_ref, w_ref, ws_ref, o_ref, os_ref, h_scr):
    d = x_ref.shape[-1]

    @pl.when(pl.program_id(2) == 0)
    def _():
        x = x_ref[0]
        var = jnp.mean(x * x, axis=-1, keepdims=True)
        y = x * lax.rsqrt(var + EPS) * nw_ref[...]
        shift = mod_ref[0, :, 0:d]
        scale = mod_ref[0, :, d:2 * d]
        hb = (y * (1.0 + scale) + shift).astype(BF16)
        h_scr[...] = hb
        os_ref[0] = _dot_nt(hb, ws_ref[...])

    o_ref[0] = _dot_nt(h_scr[...], w_ref[...])


def _in_proj(x, mod_l, norm_w, w_main, w_small):
    b, s, d = x.shape
    n = MAIN_COLS
    ns = w_small.shape[0]
    tm = min(1024, s)
    tn = 512
    return pl.pallas_call(
        _inproj_kernel,
        out_shape=(jax.ShapeDtypeStruct((b, s, n), F32),
                   jax.ShapeDtypeStruct((b, s, ns), F32)),
        grid=(b, s // tm, n // tn),
        in_specs=[
            pl.BlockSpec((1, tm, d), lambda bi, i, j: (bi, i, 0)),
            pl.BlockSpec((1, 1, 3 * d), lambda bi, i, j: (bi, 0, 0)),
            pl.BlockSpec((1, d), lambda bi, i, j: (0, 0)),
            pl.BlockSpec((tn, d), lambda bi, i, j: (j, 0)),
            pl.BlockSpec((ns, d), lambda bi, i, j: (0, 0)),
        ],
        out_specs=(pl.BlockSpec((1, tm, tn), lambda bi, i, j: (bi, i, j)),
                   pl.BlockSpec((1, tm, ns), lambda bi, i, j: (bi, i, 0))),
        scratch_shapes=[pltpu.VMEM((tm, d), BF16)],
        compiler_params=_cparams(("parallel", "parallel", "arbitrary")),
        name="in_proj",
    )(x, mod_l.reshape(b, 1, 3 * d), norm_w.reshape(1, d), w_main, w_small)


def _outproj_kernel(a1_ref, a2_ref, w_ref, x_ref, mod_ref, nw_ref, o_ref):
    d = x_ref.shape[-1]
    half = a1_ref.shape[-1]
    y = _dot(a1_ref[0], w_ref[0:half, :]) + _dot(a2_ref[0], w_ref[half:2 * half, :])
    var = jnp.mean(y * y, axis=-1, keepdims=True)
    yn = y * lax.rsqrt(var + EPS) * nw_ref[...]
    gate = mod_ref[0, :, 2 * d:3 * d]
    o_ref[0] = x_ref[0] + gate * yn


def _out_proj(a1, a2, w_out, x, mod_l, norm_w):
    b, s, d = x.shape
    half = a1.shape[-1]
    tm = min(512, s)
    return pl.pallas_call(
        _outproj_kernel,
        out_shape=jax.ShapeDtypeStruct((b, s, d), F32),
        grid=(b, s // tm),
        in_specs=[
            pl.BlockSpec((1, tm, half), lambda bi, i: (bi, i, 0)),
            pl.BlockSpec((1, tm, half), lambda bi, i: (bi, i, 0)),
            pl.BlockSpec((2 * half, d), lambda bi, i: (0, 0)),
            pl.BlockSpec((1, tm, d), lambda bi, i: (bi, i, 0)),
            pl.BlockSpec((1, 1, 3 * d), lambda bi, i: (bi, 0, 0)),
            pl.BlockSpec((1, d), lambda bi, i: (0, 0)),
        ],
        out_specs=pl.BlockSpec((1, tm, d), lambda bi, i: (bi, i, 0)),
        compiler_params=_cparams(("parallel", "parallel")),
        name="out_proj",
    )(a1, a2, w_out, x, mod_l.reshape(b, 1, 3 * d), norm_w.reshape(1, d))


def _cumsum_chains(tri, gs):
    n = gs[0].shape[1]
    terms = []
    for g in gs:
        terms.extend(_split3(g))
    y = _dot(tri, jnp.concatenate(terms, axis=1))
    return [y[:, (3 * i) * n:(3 * i + 1) * n] + y[:, (3 * i + 1) * n:(3 * i + 2) * n]
            + y[:, (3 * i + 2) * n:(3 * i + 3) * n] for i in range(len(gs))]


def _lin_attn_chunk(q, k, v, g, b, st_scr, b_scr, g_scr, consts):
    c, dk = q.shape
    row, eye, tri, pairs = consts
    b_scr[...] = b
    g_scr[pl.ds(8, c), :] = g
    b_last = b_scr[pl.ds(c - 1, 1), :]

    def level_operand(s, expo):
        upper = (row & s) != 0
        return (jnp.where(upper, q, k) * jnp.exp(expo)).astype(BF16)

    items = []
    for s in [c >> i for i in range(1, c.bit_length() - 2)]:
        pieces = [jnp.broadcast_to(b_scr[pl.ds(p * 2 * s + s - 1, 1), :], (2 * s, dk))
                  for p in range(c // (2 * s))]
        d = b - jnp.concatenate(pieces, axis=0)
        x = level_operand(s, jnp.where((row & s) != 0, d, -d))
        items.append((x, x, pairs[s]))
    g_dn = g_scr[pl.ds(7, c), :]
    g_up = g_scr[pl.ds(9, c), :]
    r4 = row & 3
    x = level_operand(2, jnp.where(r4 == 2, g, jnp.where(r4 == 3, g + g_dn, jnp.where(r4 == 0, g_up, 0.0))))
    items.append((x, x, pairs[2]))
    x = level_operand(1, jnp.where((row & 1) != 0, g, 0.0))
    items.append((x, x, pairs[1]))
    items.append((q.astype(BF16), k.astype(BF16), eye))

    attn = jnp.zeros((c, c), F32)
    for i in range(0, len(items), 2):
        grp = items[i:i + 2]
        prod = _dot_nt(jnp.concatenate([it[0] for it in grp], axis=0),
                       jnp.concatenate([it[1] for it in grp], axis=0))
        for n, it in enumerate(grp):
            attn = jnp.where(it[2], prod[n * c:(n + 1) * c, n * c:(n + 1) * c], attn)

    v16 = v.astype(BF16)
    st = st_scr[...]
    o = _dot(jnp.concatenate([attn.astype(BF16), (q * jnp.exp(b)).astype(BF16)], axis=1),
             jnp.concatenate([v16, st.astype(BF16)], axis=0))
    k_dec = (k * jnp.exp(b_last - b)).astype(BF16)
    e_col = jnp.broadcast_to(jnp.exp(b_last), (dk, dk)).T
    e_col = jnp.concatenate([e_col] * (v.shape[1] // dk), axis=1)
    st_scr[...] = st * e_col + _dot_tn(k_dec, v16)
    return o


def _lin_attn_consts(c, dk):
    row = lax.broadcasted_iota(jnp.int32, (c, dk), 0)
    ri = lax.broadcasted_iota(jnp.int32, (c, c), 0)
    ci = lax.broadcasted_iota(jnp.int32, (c, c), 1)
    eye = ri == ci
    tri = jnp.where(ri >= ci, 1.0, 0.0).astype(BF16)
    pairs = {s: jnp.where((ri & -(2 * s)) == (ci & -(2 * s)), (ri & s) - (ci & s), 0) == s
             for s in [c >> i for i in range(1, c.bit_length())]}
    return row, eye, tri, pairs


def _lin_attn_finish(o, gz, nw):
    var = jnp.mean(o * o, axis=-1, keepdims=True)
    return (o * lax.rsqrt(var + EPS) * nw) * _silu(gz)


LIN_HEADS_PER_STEP = 4


def _gla_kernel(q_ref, k_ref, v_ref, gz_ref, glr_ref, wup_ref, bup_ref, nw_ref, o_ref,
                st_scr, b_scr, g_scr):
    c = LIN_CHUNK
    dk, dv = GLA_DK, GLA_DV
    nb, ts = q_ref.shape[0], q_ref.shape[1]
    hp = q_ref.shape[2] // dk

    @pl.when(pl.program_id(1) == 0)
    def _():
        st_scr[...] = jnp.zeros_like(st_scr)

    g_scr[...] = jnp.zeros_like(g_scr)
    consts = _lin_attn_consts(c, dk)
    nw = nw_ref[...]
    q_scale = dk ** -0.5

    def body(ci, carry):
        r0 = pl.multiple_of(ci * c, c)
        rows = pl.ds(r0, c)
        gs = []
        for bb in range(nb):
            z = _dot(glr_ref[bb, rows, :], wup_ref[...]) + bup_ref[...]
            g_all = _log_sigmoid(z) * (1.0 / GLA_TAU)
            gs.extend(g_all[:, hh * dk:(hh + 1) * dk] for hh in range(hp))
        bs = _cumsum_chains(consts[2], gs)
        for bb in range(nb):
            for hh in range(hp):
                ks = slice(hh * dk, (hh + 1) * dk)
                vs = slice(hh * dv, (hh + 1) * dv)
                ch = bb * hp + hh
                q = q_ref[bb, rows, ks] * q_scale
                o = _lin_attn_chunk(q, k_ref[bb, rows, ks], v_ref[bb, rows, vs], gs[ch], bs[ch],
                                    st_scr.at[ch], b_scr.at[ch], g_scr.at[ch], consts)
                out = _lin_attn_finish(o, gz_ref[bb, rows, vs], nw)
                o_ref[bb, rows, vs] = out.astype(o_ref.dtype)
        return carry

    lax.fori_loop(0, ts // c, body, 0)


def _hgrn_kernel(q_ref, f_ref, v_ref, gz_ref, lbl_ref, nw_ref, o_ref, st_scr, b_scr, g_scr,
                 *, layer):
    c = LIN_CHUNK
    dk, dv = HGRN_DK, HGRN_DV
    nb, ts = q_ref.shape[0], q_ref.shape[1]
    hp = q_ref.shape[2] // dk

    @pl.when(pl.program_id(1) == 0)
    def _():
        st_scr[...] = jnp.zeros_like(st_scr)

    g_scr[...] = jnp.zeros_like(g_scr)
    consts = _lin_attn_consts(c, dk)
    nw = nw_ref[...]

    logits = lbl_ref[...]
    depth = logits.shape[0]
    mx = logits[0:1, :]
    for r in range(1, depth):
        mx = jnp.maximum(mx, logits[r:r + 1, :])
    ex = [jnp.exp(logits[r:r + 1, :] - mx) for r in range(depth)]
    den = ex[0]
    for r in range(1, depth):
        den = den + ex[r]
    sm = [e / den for e in ex]
    lb_all = sm[0]
    for r in range(1, layer + 1):
        lb_all = lb_all + sm[r]
    lb_all = lb_all - sm[0]
    log_lb_all = jnp.log(lb_all)
    log_1mlb_all = jnp.log1p(-lb_all)

    def body(ci, carry):
        r0 = pl.multiple_of(ci * c, c)
        rows = pl.ds(r0, c)
        gs, kk = [], []
        for bb in range(nb):
            for hh in range(hp):
                ks = slice(hh * dk, (hh + 1) * dk)
                z = f_ref[bb, rows, ks]
                gs.append(_logaddexp(log_lb_all[:, ks], log_1mlb_all[:, ks] + _log_sigmoid(z)))
                kk.append((1.0 - lb_all[:, ks]) * (1.0 / (1.0 + jnp.exp(z))))
        bs = _cumsum_chains(consts[2], gs)
        for bb in range(nb):
            for hh in range(hp):
                ks = slice(hh * dk, (hh + 1) * dk)
                vs = slice(hh * dv, (hh + 1) * dv)
                ch = bb * hp + hh
                o = _lin_attn_chunk(q_ref[bb, rows, ks], kk[ch], v_ref[bb, rows, vs], gs[ch], bs[ch],
                                    st_scr.at[ch], b_scr.at[ch], g_scr.at[ch], consts)
                out = _lin_attn_finish(o, gz_ref[bb, rows, vs], nw)
                o_ref[bb, rows, vs] = out.astype(o_ref.dtype)
        return carry

    lax.fori_loop(0, ts // c, body, 0)


def _lin_scratch(chains, dv, dk):
    return [pltpu.VMEM((chains, dk, dv), F32),
            pltpu.VMEM((chains, LIN_CHUNK, dk), F32),
            pltpu.VMEM((chains, LIN_CHUNK + 16, dk), F32)]


def _gla_branch(proj, small, w_up, b_up, norm_w):
    b, s, _ = proj.shape
    ts = min(512, s)
    dk, dv, h = GLA_DK, GLA_DV, GLA_HEADS
    hp = LIN_HEADS_PER_STEP
    kw, vw = hp * dk, hp * dv
    w_up_pad = jnp.pad(w_up, ((0, 128 - GLA_RANK), (0, 0)))
    return pl.pallas_call(
        _gla_kernel,
        out_shape=jax.ShapeDtypeStruct((b, s, h * dv), BF16),
        grid=(h // hp, s // ts),
        in_specs=[
            pl.BlockSpec((b, ts, kw), lambda hi, i: (0, i, hi)),
            pl.BlockSpec((b, ts, kw), lambda hi, i: (0, i, h // hp + hi)),
            pl.BlockSpec((b, ts, vw), lambda hi, i: (0, i, h // hp + hi)),
            pl.BlockSpec((b, ts, vw), lambda hi, i: (0, i, 2 * (h // hp) + hi)),
            pl.BlockSpec((b, ts, 128), lambda hi, i: (0, i, 0)),
            pl.BlockSpec((128, kw), lambda hi, i: (0, hi)),
            pl.BlockSpec((1, kw), lambda hi, i: (0, hi)),
            pl.BlockSpec((1, dv), lambda hi, i: (0, 0)),
        ],
        out_specs=pl.BlockSpec((b, ts, vw), lambda hi, i: (0, i, hi)),
        scratch_shapes=_lin_scratch(b * hp, dv, dk),
        compiler_params=_cparams(("parallel", "arbitrary")),
        name="gla",
    )(proj, proj, proj, proj, small, w_up_pad, b_up.reshape(1, -1), norm_w.reshape(1, dv))


def _hgrn_branch(proj, lb_logits, norm_w, layer):
    b, s, _ = proj.shape
    ts = min(512, s)
    dk, dv, h = HGRN_DK, HGRN_DV, HGRN_HEADS
    hp = LIN_HEADS_PER_STEP
    kw, vw = hp * dk, hp * dv
    ng = h // hp
    depth = lb_logits.shape[0]
    return pl.pallas_call(
        functools.partial(_hgrn_kernel, layer=layer),
        out_shape=jax.ShapeDtypeStruct((b, s, h * dv), BF16),
        grid=(ng, s // ts),
        in_specs=[
            pl.BlockSpec((b, ts, kw), lambda hi, i: (0, i, hi)),
            pl.BlockSpec((b, ts, kw), lambda hi, i: (0, i, ng + hi)),
            pl.BlockSpec((b, ts, vw), lambda hi, i: (0, i, 2 * ng + hi)),
            pl.BlockSpec((b, ts, vw), lambda hi, i: (0, i, 3 * ng + hi)),
            pl.BlockSpec((depth, kw), lambda hi, i: (0, hi)),
            pl.BlockSpec((1, dv), lambda hi, i: (0, 0)),
        ],
        out_specs=pl.BlockSpec((b, ts, vw), lambda hi, i: (0, i, hi)),
        scratch_shapes=_lin_scratch(b * hp, dv, dk),
        compiler_params=_cparams(("parallel", "arbitrary")),
        name="hgrn2",
    )(proj, proj, proj, proj, lb_logits, norm_w.reshape(1, dv))


def _ssd_kernel(z_ref, x_ref, bc_ref, dt_ref, cw_ref, cb_ref, dtb_ref, alog_ref, dexp_ref, nw_ref,
                hexp_ref, o_ref, stage_scr, xbc_scr, st_scr, y_scr):
    c = x_ref.shape[1]
    nch = SSM_CONV_CH
    width = SSM_WIDTH
    n = SSM_STATE
    pairs = SSM_HEADS // 2

    @pl.when(pl.program_id(1) == 0)
    def _():
        st_scr[...] = jnp.zeros_like(st_scr)
        stage_scr[pl.ds(0, 8), :] = jnp.zeros((8, nch), F32)

    stage_scr[pl.ds(8, c), 0:width] = x_ref[0]
    stage_scr[pl.ds(8, c), width:nch] = bc_ref[0]
    acc = cb_ref[...] + cw_ref[0:1, :] * stage_scr[pl.ds(5, c), :]
    for kk in range(1, SSM_CONV):
        acc = acc + cw_ref[kk:kk + 1, :] * stage_scr[pl.ds(5 + kk, c), :]
    stage_scr[pl.ds(0, 8), :] = stage_scr[pl.ds(c, 8), :]
    xbc_scr[...] = _silu(acc)

    dt = _softplus(dt_ref[0] + dtb_ref[...])
    a = -jnp.exp(alog_ref[...])
    da = dt * a
    ri = lax.broadcasted_iota(jnp.int32, (c, c), 0)
    ci = lax.broadcasted_iota(jnp.int32, (c, c), 1)
    causal = ri >= ci
    tri = jnp.where(causal, 1.0, 0.0).astype(BF16)
    a_cs = _sel_dot_left(tri, da)
    a_cs_t = a_cs.T
    hexp = hexp_ref[...]
    dt_x = _sel_dot_right(dt, hexp)
    acs_x = _sel_dot_right(a_cs, hexp)
    a_last_x = acs_x[c - 1:c, :]
    lane = lax.broadcasted_iota(jnp.int32, (c, 2 * SSM_HEADDIM), 1)
    first = lane < SSM_HEADDIM

    for grp in range(SSM_GROUPS):
        bm = xbc_scr[:, width + grp * n: width + (grp + 1) * n]
        cm = xbc_scr[:, width + (SSM_GROUPS + grp) * n: width + (SSM_GROUPS + grp + 1) * n]
        bm16 = bm.astype(BF16)
        cm16 = cm.astype(BF16)
        cb = _dot_nt(cm16, bm16)
        for pp in range(pairs // SSM_GROUPS):
            p = grp * (pairs // SSM_GROUPS) + pp
            lo = p * 2 * SSM_HEADDIM
            hi = lo + 2 * SSM_HEADDIM
            xs = xbc_scr[:, lo:hi]
            xdt = xs * dt_x[:, lo:hi]
            acs = acs_x[:, lo:hi]
            y = jnp.zeros((c, 2 * SSM_HEADDIM), F32)
            for hh in range(2):
                h = 2 * p + hh
                col = jnp.broadcast_to(a_cs[:, h:h + 1], (c, c))
                rw = jnp.broadcast_to(a_cs_t[h:h + 1, :], (c, c))
                lmat = jnp.exp(jnp.where(causal, col - rw, NEG))
                keep = first if hh == 0 else jnp.logical_not(first)
                xh = jnp.where(keep, xdt, 0.0).astype(BF16)
                y = y + _dot((cb * lmat).astype(BF16), xh)
            st = st_scr[p]
            y = y + _dot(cm16, st.astype(BF16)) * jnp.exp(acs)
            decay = jnp.exp(a_last_x[:, lo:hi] - acs)
            st_scr[p] = st * jnp.exp(a_last_x[:, lo:hi]) + _dot_tn(bm16, (xdt * decay).astype(BF16))
            y_scr[:, lo:hi] = y + dexp_ref[:, lo:hi] * xs

    yz = y_scr[...] * _silu(z_ref[0])
    var = jnp.mean(yz * yz, axis=-1, keepdims=True)
    o_ref[0] = (yz * lax.rsqrt(var + EPS) * nw_ref[...]).astype(o_ref.dtype)


def _ssd_branch(proj, small, conv_w, conv_b, dt_bias, a_log, d_skip, norm_w):
    b, s, _ = proj.shape
    c = math.gcd(SSM_CHUNK, s)
    nch = SSM_CONV_CH
    pad16 = lambda v: jnp.pad(v.reshape(1, -1), ((0, 0), (0, 128 - SSM_HEADS)))
    hexp = np.zeros((128, SSM_WIDTH), np.float32)
    for h in range(SSM_HEADS):
        hexp[h, h * SSM_HEADDIM:(h + 1) * SSM_HEADDIM] = 1.0
    d_exp = jnp.repeat(d_skip, SSM_HEADDIM).reshape(1, SSM_WIDTH)
    return pl.pallas_call(
        _ssd_kernel,
        out_shape=jax.ShapeDtypeStruct((b, s, SSM_WIDTH), BF16),
        grid=(b, s // c),
        in_specs=[
            pl.BlockSpec((1, c, SSM_WIDTH), lambda bi, i: (bi, i, 4)),
            pl.BlockSpec((1, c, SSM_WIDTH), lambda bi, i: (bi, i, 5)),
            pl.BlockSpec((1, c, 512), lambda bi, i: (bi, i, 12)),
            pl.BlockSpec((1, c, 128), lambda bi, i: (bi, i, 0)),
            pl.BlockSpec((SSM_CONV, nch), lambda bi, i: (0, 0)),
            pl.BlockSpec((1, nch), lambda bi, i: (0, 0)),
            pl.BlockSpec((1, 128), lambda bi, i: (0, 0)),
            pl.BlockSpec((1, 128), lambda bi, i: (0, 0)),
            pl.BlockSpec((1, SSM_WIDTH), lambda bi, i: (0, 0)),
            pl.BlockSpec((1, SSM_WIDTH), lambda bi, i: (0, 0)),
            pl.BlockSpec((128, SSM_WIDTH), lambda bi, i: (0, 0)),
        ],
        out_specs=pl.BlockSpec((1, c, SSM_WIDTH), lambda bi, i: (bi, i, 0)),
        scratch_shapes=[
            pltpu.VMEM((c + 8, nch), F32),
            pltpu.VMEM((c, nch), F32),
            pltpu.VMEM((SSM_HEADS // 2, SSM_STATE, 2 * SSM_HEADDIM), F32),
            pltpu.VMEM((c, SSM_WIDTH), F32),
        ],
        compiler_params=_cparams(("parallel", "arbitrary")),
        name="ssd",
    )(proj, proj, proj, small, conv_w, conv_b.reshape(1, nch), pad16(dt_bias), pad16(a_log),
      d_exp, norm_w.reshape(1, SSM_WIDTH), jnp.asarray(hexp, dtype=BF16))


def _rope(x, cos, sin_signed):
    return x * cos + pltpu.roll(x, NSA_DH // 2, axis=1) * sin_signed


def _nsa_prep_kernel(q_ref, kc_ref, ks_ref, vs_ref, kw_ref, vw_ref, cos_ref, sin_ref,
                     qo_ref, kco_ref, kso_ref, vso_ref, kwo_ref, vwo_ref):
    cos = cos_ref[...]
    sin = sin_ref[...]
    dh = NSA_DH
    scale = dh ** -0.5
    for h in range(NSA_HEADS):
        sl = slice(h * dh, (h + 1) * dh)
        qo_ref[0, :, sl] = (_rope(q_ref[0, :, sl], cos, sin) * scale).astype(qo_ref.dtype)
    for g in range(NSA_GROUPS):
        sl = slice(g * dh, (g + 1) * dh)
        kco_ref[0, :, sl] = _rope(kc_ref[0, :, sl], cos, sin)
        kso_ref[0, :, sl] = _rope(ks_ref[0, :, sl], cos, sin).astype(kso_ref.dtype)
        kwo_ref[0, :, sl] = _rope(kw_ref[0, :, sl], cos, sin).astype(kwo_ref.dtype)
    tk = NSA_TILE
    for g in range(NSA_GROUPS):
        for r in range(q_ref.shape[1] // tk):
            rows = slice(r * tk, (r + 1) * tk)
            sl = slice(g * dh, (g + 1) * dh)
            vso_ref[0, g, r, 0:dh, :] = vs_ref[0, rows, sl].T.astype(vso_ref.dtype)
            vwo_ref[0, g, r, 0:dh, :] = vw_ref[0, rows, sl].T.astype(vwo_ref.dtype)
            ones = jnp.ones((NSA_ONES, tk), vso_ref.dtype)
            vso_ref[0, g, r, dh:dh + NSA_ONES, :] = ones
            vwo_ref[0, g, r, dh:dh + NSA_ONES, :] = ones


def _nsa_prep(proj):
    b, s, _ = proj.shape
    ts = min(512, s)
    tk = NSA_TILE
    half = NSA_DH // 2
    inv_freq = ROPE_THETA ** (-np.arange(half, dtype=np.float64) / half)
    ang = np.arange(s, dtype=np.float64)[:, None] * inv_freq[None, :]
    cos_full = jnp.asarray(np.concatenate([np.cos(ang), np.cos(ang)], axis=-1), dtype=F32)
    sin_signed = jnp.asarray(np.concatenate([-np.sin(ang), np.sin(ang)], axis=-1), dtype=F32)
    kvw = NSA_GROUPS * NSA_DH
    col = lambda idx: (lambda bi, i: (bi, i, idx))
    kv_spec = lambda idx: pl.BlockSpec((1, ts, kvw), col(idx))
    kv_out = pl.BlockSpec((1, ts, kvw), lambda bi, i: (bi, i, 0))
    vrows = NSA_DH + NSA_ONES
    vt_shape = jax.ShapeDtypeStruct((b, NSA_GROUPS, s // tk, vrows, tk), BF16)
    vt_out = pl.BlockSpec((1, NSA_GROUPS, ts // tk, vrows, tk), lambda bi, i: (bi, 0, i, 0, 0))
    return pl.pallas_call(
        _nsa_prep_kernel,
        out_shape=(jax.ShapeDtypeStruct((b, s, NSA_HEADS * NSA_DH), BF16),
                   jax.ShapeDtypeStruct((b, s, kvw), F32),
                   jax.ShapeDtypeStruct((b, s, kvw), BF16),
                   vt_shape,
                   jax.ShapeDtypeStruct((b, s, kvw), BF16),
                   vt_shape),
        grid=(b, s // ts),
        in_specs=[
            pl.BlockSpec((1, ts, NSA_HEADS * NSA_DH), col(3)),
            kv_spec(16), kv_spec(18), kv_spec(19), kv_spec(20), kv_spec(21),
            pl.BlockSpec((ts, NSA_DH), lambda bi, i: (i, 0)),
            pl.BlockSpec((ts, NSA_DH), lambda bi, i: (i, 0)),
        ],
        out_specs=(pl.BlockSpec((1, ts, NSA_HEADS * NSA_DH), lambda bi, i: (bi, i, 0)),
                   kv_out, kv_out, vt_out, kv_out, vt_out),
        compiler_params=_cparams(("parallel", "parallel")),
        name="nsa_prep",
    )(proj, proj, proj, proj, proj, proj, cos_full, sin_signed)


def _compress_kernel(t_ref, pe_ref, w1_ref, w2_ref, o_ref, sh_scr, *, transposed):
    n = t_ref.shape[1] // CMP_STRIDE
    dh = NSA_DH
    half = CMP_LEN // 2
    acc_a = jnp.zeros((n, CMP_HIDDEN), F32)
    acc_b = jnp.zeros((n, CMP_HIDDEN), F32)
    for l in range(half):
        t = t_ref[0, pl.ds(l, n, stride=CMP_STRIDE), :]
        acc_a = acc_a + _dot((t + pe_ref[l:l + 1, :]).astype(BF16), w1_ref[l * dh:(l + 1) * dh, :])
        acc_b = acc_b + _dot((t + pe_ref[half + l:half + l + 1, :]).astype(BF16),
                             w1_ref[(half + l) * dh:(half + l + 1) * dh, :])
    sh_scr[pl.ds(0, n), :] = acc_b
    sh_scr[pl.ds(n, 8), :] = jnp.zeros((8, CMP_HIDDEN), F32)
    hid = acc_a + sh_scr[pl.ds(1, n), :]
    out = _dot(_silu(hid).astype(BF16), w2_ref[...])
    if transposed:
        tk = NSA_TILE
        for r in range(n // tk):
            o_ref[0, 0, r] = out[r * tk:(r + 1) * tk, :].T.astype(o_ref.dtype)
    else:
        o_ref[0, 0] = out.astype(o_ref.dtype)


def _nsa_compress(src, col0, pe, w1, w2, transposed):
    b, s, _ = src.shape
    n = s // CMP_STRIDE
    base = col0 // NSA_DH
    tk = NSA_TILE
    if transposed:
        out_shape = jax.ShapeDtypeStruct((b, NSA_GROUPS, n // tk, NSA_DH, tk), BF16)
        out_spec = pl.BlockSpec((1, 1, n // tk, NSA_DH, tk), lambda bi, g: (bi, g, 0, 0, 0))
    else:
        out_shape = jax.ShapeDtypeStruct((b, NSA_GROUPS, n, NSA_DH), BF16)
        out_spec = pl.BlockSpec((1, 1, n, NSA_DH), lambda bi, g: (bi, g, 0, 0))
    return pl.pallas_call(
        functools.partial(_compress_kernel, transposed=transposed),
        out_shape=out_shape,
        grid=(b, NSA_GROUPS),
        in_specs=[
            pl.BlockSpec((1, s, NSA_DH), lambda bi, g: (bi, 0, base + g)),
            pl.BlockSpec((CMP_LEN, NSA_DH), lambda bi, g: (0, 0)),
            pl.BlockSpec((CMP_LEN * NSA_DH, CMP_HIDDEN), lambda bi, g: (0, 0)),
            pl.BlockSpec((CMP_HIDDEN, NSA_DH), lambda bi, g: (0, 0)),
        ],
        out_specs=out_spec,
        scratch_shapes=[pltpu.VMEM((n + 8, CMP_HIDDEN), F32)],
        compiler_params=_cparams(("parallel", "parallel")),
        name="nsa_compress",
    )(src, pe, w1.astype(BF16), w2.astype(BF16))


def _weights_bf16(s_tiles, m):
    return [jnp.exp((s - m).astype(BF16)) for s in s_tiles]


def _flash_update(s_tiles, vt_tiles, m_ref, acc_ref):
    m_old = m_ref[...]
    m_new = m_old
    for s in s_tiles:
        m_new = jnp.maximum(m_new, jnp.max(s, axis=0, keepdims=True))
    alpha = jnp.exp(m_old - m_new)
    acc_ref[...] = alpha * acc_ref[...] + _pv(vt_tiles, _weights_bf16(s_tiles, m_new))
    m_ref[...] = m_new


def _pv(vt_tiles, p_tiles):
    vt = jnp.concatenate(vt_tiles, axis=1)
    p = jnp.concatenate([p.astype(BF16) for p in p_tiles], axis=0)
    return _dot(vt, p)


def _softmax_tiles(s_tiles, vt_tiles):
    m = jnp.max(s_tiles[0], axis=0, keepdims=True)
    for s in s_tiles[1:]:
        m = jnp.maximum(m, jnp.max(s, axis=0, keepdims=True))
    p_tiles = [jnp.exp(s - m) for s in s_tiles]
    l = jnp.sum(p_tiles[0], axis=0, keepdims=True)
    for p in p_tiles[1:]:
        l = l + jnp.sum(p, axis=0, keepdims=True)
    return p_tiles, l, _pv(vt_tiles, p_tiles)


def _softmax_tiles_aug(s_tiles, vt_tiles):
    m = jnp.max(s_tiles[0], axis=0, keepdims=True)
    for s in s_tiles[1:]:
        m = jnp.maximum(m, jnp.max(s, axis=0, keepdims=True))
    acc = _pv(vt_tiles, _weights_bf16(s_tiles, m))
    return acc[0:NSA_DH] * (1.0 / acc[NSA_DH:NSA_DH + 1])


def _nsa_kernel(q_ref, kc_ref, vct_ref, ovt_ref, ks_ref, vst_ref, kw_ref, vwt_ref, gate_ref, nz_ref,
                o_ref, m_scr, accs_scr, acc_scr, cap_scr, capd_scr, capl_scr, sa_scr, sb_scr):
    tq = q_ref.shape[1]
    tk = NSA_TILE
    dh = NSA_DH
    hpg = NSA_HPG
    lanes = hpg * tq
    ns = ovt_ref.shape[0]
    qtiles = tq // tk
    qi = pl.program_id(2)
    t0 = qi * tq
    kt0 = qi * qtiles

    qt = jnp.concatenate([q_ref[0, :, j * dh:(j + 1) * dh].astype(F32).T for j in range(hpg)],
                         axis=1).astype(BF16)
    rowi = lax.broadcasted_iota(jnp.int32, (tk, lanes), 0)
    qpos = lax.broadcasted_iota(jnp.int32, (tk, lanes), 1) & (tq - 1)
    for d in range(qtiles):
        capd_scr[d] = jnp.where(rowi + d * tk <= qpos, FORCE, NEG)
        capl_scr[d] = jnp.where(rowi + d * tk > qpos, FORCE, NEG)

    n_ct = kc_ref.shape[2] // tk
    s_all = _dot(kc_ref[0, 0], qt)
    s_tiles = []
    for c in range(n_ct):
        first_end = c * (tk * CMP_STRIDE) + (CMP_LEN - 1) - t0
        cap = jnp.where(rowi * CMP_STRIDE + first_end <= qpos, FORCE, NEG)
        s_tiles.append(jnp.minimum(s_all[c * tk:(c + 1) * tk, :], cap))
    p_tiles, l_c, acc_c = _softmax_tiles(s_tiles, [vct_ref[0, 0, c] for c in range(n_ct)])
    p_hi = [p.astype(BF16) for p in p_tiles]
    p_lo = [(p - h.astype(F32)).astype(BF16) for p, h in zip(p_tiles, p_hi)]
    imp_2 = _dot(ovt_ref[...], jnp.concatenate([jnp.concatenate(p_hi, axis=0),
                                                 jnp.concatenate(p_lo, axis=0)], axis=1))
    imp_un = imp_2[:, 0:lanes] + imp_2[:, lanes:2 * lanes]
    tl = t0 + (lax.broadcasted_iota(jnp.int32, (1, lanes), 1) & (tq - 1))
    inv_l = jnp.where(tl >= CMP_LEN - 1, 1.0 / l_c, 0.0)
    acc_scr[0] = acc_c * inv_l
    impn = imp_un * inv_l
    imp = impn[:, 0:tq]
    for j in range(1, hpg):
        imp = imp + impn[:, j * tq:(j + 1) * tq]

    blk = lax.broadcasted_iota(jnp.int32, (ns, tq), 0)
    blk_t = jnp.right_shift(t0 + lax.broadcasted_iota(jnp.int32, (ns, tq), 1), SLC_SHIFT)
    score = jnp.where(blk == 0, FORCE,
                      jnp.where(blk == blk_t, FORCE,
                                jnp.where(blk == blk_t - 1, FORCE,
                                          jnp.where(blk <= blk_t, imp, NEG))))
    blk_f = blk.astype(F32)
    cap_sel = jnp.full((ns, tq), NEG, F32)
    for _ in range(min(SLC_TOPK, ns)):
        mx = jnp.max(score, axis=0, keepdims=True)
        first = jnp.min(jnp.where(score == mx, blk_f, float(ns)), axis=0, keepdims=True)
        pick = blk_f == first
        cap_sel = jnp.where(pick, FORCE, cap_sel)
        score = jnp.where(pick, -jnp.inf, score)
    for j in range(hpg):
        cap_scr[:, j * tq:(j + 1) * tq] = cap_sel

    bpt = tk // SLC_BLOCK

    def slc_scores(step):
        k0 = pl.multiple_of(step * (nb * tk), nb * tk)
        s = _dot(ks_ref[0, pl.ds(k0, nb * tk), :], qt)
        out = []
        for r in range(nb):
            caps = [jnp.broadcast_to(cap_scr[pl.ds((step * nb + r) * bpt + i, 1), :], (SLC_BLOCK, lanes))
                    for i in range(bpt)]
            out.append(jnp.minimum(s[r * tk:(r + 1) * tk, :], jnp.concatenate(caps, axis=0)))
        return out

    wt = WINDOW // tk
    s_tiles, vt_tiles = [], []
    for d in range(-wt, qtiles):
        kt = kt0 + d
        ktc = jnp.maximum(kt, 0)
        s = _dot(kw_ref[0, pl.ds(pl.multiple_of(ktc * tk, tk), tk), :], qt)
        if d + wt < qtiles:
            s = jnp.minimum(s, capl_scr[d + wt])
        if d >= 0:
            s = jnp.minimum(s, capd_scr[d])
        else:
            s = jnp.minimum(s, jnp.where(kt >= 0, FORCE, NEG))
        s_tiles.append(s)
        vt_tiles.append(vwt_ref[0, 0, ktc])
    acc_scr[1] = _softmax_tiles_aug(s_tiles, vt_tiles)

    nb = NSA_TILES_PER_STEP
    m_scr[...] = jnp.full((1, lanes), NEG, F32)
    accs_scr[...] = jnp.zeros_like(accs_scr)
    state = (m_scr, accs_scr)
    n_full = kt0 // nb

    def score_step(step, dst):
        for r, s in enumerate(slc_scores(step)):
            dst[r * tk:(r + 1) * tk, :] = s

    def value_step(step, src):
        _flash_update([src[r * tk:(r + 1) * tk, :] for r in range(nb)],
                      [vst_ref[0, 0, step * nb + r] for r in range(nb)], *state)

    @pl.when(n_full > 0)
    def _():
        score_step(0, sa_scr)

    def slc_pair(j, carry):
        score_step(2 * j + 1, sb_scr)
        value_step(2 * j, sa_scr)
        score_step(jnp.minimum(2 * j + 2, n_full - 1), sa_scr)
        value_step(2 * j + 1, sb_scr)
        return carry

    lax.fori_loop(0, n_full // 2, slc_pair, 0)

    @pl.when(n_full % 2 == 1)
    def _():
        value_step(n_full - 1, sa_scr)

    s_tiles, vt_tiles = [], []
    for r, s in enumerate(slc_scores(n_full)):
        kt = n_full * nb + r
        cap = jnp.where(kt < kt0 + qtiles, FORCE, NEG)
        for d in range(qtiles):
            cap = jnp.minimum(cap, jnp.maximum(capd_scr[d], jnp.where(kt == kt0 + d, NEG, FORCE)))
        s_tiles.append(jnp.minimum(s, cap))
        vt_tiles.append(vst_ref[0, 0, kt])
    _flash_update(s_tiles, vt_tiles, *state)

    o_c = acc_scr[0]
    o_s = accs_scr[0:dh, :] * (1.0 / accs_scr[dh:dh + 1, :])
    o_w = acc_scr[1]
    sgt = _sigmoid(gate_ref[0]).T
    for j in range(hpg):
        ls = slice(j * tq, (j + 1) * tq)
        mix_t = (sgt[3 * j:3 * j + 1, :] * o_c[:, ls] + sgt[3 * j + 1:3 * j + 2, :] * o_s[:, ls]
                 + sgt[3 * j + 2:3 * j + 3, :] * o_w[:, ls])
        sl = slice(j * dh, (j + 1) * dh)
        o_ref[0, :, sl] = (mix_t.T * _silu(nz_ref[0, :, sl])).astype(o_ref.dtype)


def _nsa_attention(qr, kc, vct, ksr, vst, kwr, vwt, small, proj):
    b, s, _ = qr.shape
    tk = NSA_TILE
    tq = min(NSA_QUERY_TILE, s)
    qtiles = tq // tk
    assert s % tq == 0 and tq % tk == 0 and WINDOW % tk == 0 and tk % SLC_BLOCK == 0
    assert NSA_TILES_PER_STEP % qtiles == 0 and WINDOW // tk >= qtiles
    assert (s // tk) % NSA_TILES_PER_STEP == 0
    ncp = kc.shape[2]
    assert ncp % tk == 0
    ns = s // SLC_BLOCK
    gw = NSA_HPG * NSA_DH
    lanes = NSA_HPG * tq
    cs = np.arange(ncp)[None, :] * CMP_STRIDE
    ss = np.arange(ns)[:, None] * SLC_BLOCK
    ovt = ((np.minimum(cs + CMP_LEN, ss + SLC_BLOCK) - np.maximum(cs, ss)) > 0).astype(np.float32)
    ovt[:, ncp - 1] = 0.0
    kv_spec = pl.BlockSpec((1, s, NSA_DH), lambda bi, g, i: (bi, 0, g))
    vrows = NSA_DH + NSA_ONES
    vt_spec = pl.BlockSpec((1, 1, s // tk, vrows, tk), lambda bi, g, i: (bi, g, 0, 0, 0))
    return pl.pallas_call(
        _nsa_kernel,
        out_shape=jax.ShapeDtypeStruct((b, s, NSA_HEADS * NSA_DH), BF16),
        grid=(b, NSA_GROUPS, s // tq),
        in_specs=[
            pl.BlockSpec((1, tq, gw), lambda bi, g, i: (bi, i, g)),
            pl.BlockSpec((1, 1, ncp, NSA_DH), lambda bi, g, i: (bi, g, 0, 0)),
            pl.BlockSpec((1, 1, ncp // tk, NSA_DH, tk), lambda bi, g, i: (bi, g, 0, 0, 0)),
            pl.BlockSpec((ns, ncp), lambda bi, g, i: (0, 0)),
            kv_spec, vt_spec, kv_spec, vt_spec,
            pl.BlockSpec((1, tq, 128), lambda bi, g, i: (bi, i, 1 + g)),
            pl.BlockSpec((1, tq, gw), lambda bi, g, i: (bi, i, 11 + g)),
        ],
        out_specs=pl.BlockSpec((1, tq, gw), lambda bi, g, i: (bi, i, g)),
        scratch_shapes=[
            pltpu.VMEM((1, lanes), F32),
            pltpu.VMEM((vrows, lanes), F32),
            pltpu.VMEM((2, NSA_DH, lanes), F32),
            pltpu.VMEM((ns, lanes), F32),
            pltpu.VMEM((qtiles, tk, lanes), F32),
            pltpu.VMEM((qtiles, tk, lanes), F32),
            pltpu.VMEM((NSA_TILES_PER_STEP * tk, lanes), F32),
            pltpu.VMEM((NSA_TILES_PER_STEP * tk, lanes), F32),
        ],
        compiler_params=_cparams(("parallel", "parallel", "arbitrary")),
        name="nsa_attention",
    )(qr, kc, vct, jnp.asarray(ovt, dtype=BF16), ksr, vst, kwr, vwt, small, proj)


def _nsa_branch(proj, small, pe_k, w1_k, w2_k, pe_v, w1_v, w2_v):
    qr, kcr, ksr, vst, kwr, vwt = _nsa_prep(proj)
    kc = _nsa_compress(kcr, 0, pe_k, w1_k, w2_k, transposed=False)
    vct = _nsa_compress(proj, 4352, pe_v, w1_v, w2_v, transposed=True)
    return _nsa_attention(qr, kc, vct, ksr, vst, kwr, vwt, small, proj)


def _pad_rows(w, rows):
    return jnp.pad(w, ((0, rows - w.shape[0]), (0, 0)))


def _even_weights(w_in):
    wt = w_in.T
    main = jnp.concatenate([wt[0:2048], wt[2064:5648], wt[5672:6696]], axis=0)
    per = NSA_HPG * 3
    blocks = [_pad_rows(wt[2048:2048 + GLA_RANK], 128)]
    for g in range(NSA_GROUPS):
        blocks.append(_pad_rows(wt[5648 + g * per:5648 + (g + 1) * per], 128))
    return main.astype(BF16), jnp.concatenate(blocks, axis=0).astype(BF16)


def _odd_weights(w_in):
    wt = w_in.T.astype(BF16)
    return wt, _pad_rows(wt[MAIN_COLS:MAIN_COLS + SSM_HEADS], 128)


def kernel(x, c, ada_w, ada_b, pre_norm_w, post_norm_w, even_w_in, even_w_out, gla_w_up, gla_b_up,
           gla_norm_w, nsa_pe_k, nsa_w1_k, nsa_w2_k, nsa_pe_v, nsa_w1_v, nsa_w2_v, odd_w_in,
           odd_w_out, hgrn_lb_logits, hgrn_norm_w, ssm_conv_w, ssm_conv_b, ssm_dt_bias, ssm_a_log,
           ssm_d, ssm_norm_w):
    depth = ada_w.shape[0]
    mod = _adaln_mod(c, ada_w, ada_b)
    for l in range(depth):
        if l % 2 == 0:
            e = l // 2
            w_main, w_small = _even_weights(even_w_in[e])
            proj, small = _in_proj(x, mod[l], pre_norm_w[l], w_main, w_small)
            o_a = _gla_branch(proj, small, gla_w_up[e], gla_b_up[e], gla_norm_w[e])
            o_b = _nsa_branch(proj, small, nsa_pe_k[e], nsa_w1_k[e], nsa_w2_k[e],
                              nsa_pe_v[e], nsa_w1_v[e], nsa_w2_v[e])
            w_out = even_w_out[e]
        else:
            o = l // 2
            w_main, w_small = _odd_weights(odd_w_in[o])
            proj, small = _in_proj(x, mod[l], pre_norm_w[l], w_main, w_small)
            o_a = _hgrn_branch(proj, hgrn_lb_logits, hgrn_norm_w[o], l)
            o_b = _ssd_branch(proj, small, ssm_conv_w[o], ssm_conv_b[o], ssm_dt_bias[o],
                              ssm_a_log[o], ssm_d[o], ssm_norm_w[o])
            w_out = odd_w_out[o]
        x = _out_proj(o_a, o_b, w_out.astype(BF16), x, mod[l], post_norm_w[l])
    return x
```

```python
import functools
import math

import jax
import jax.numpy as jnp
import numpy as np
from jax import lax
from jax.experimental import pallas as pl
from jax.experimental.pallas import tpu as pltpu

F32 = jnp.float32
BF16 = jnp.bfloat16

D_MODEL = 2048
EPS = 1e-6
NEG = -1e30
FORCE = 1e30
ROPE_THETA = 10000.0

GLA_HEADS = 4
GLA_DK = 128
GLA_DV = 256
GLA_RANK = 16
GLA_TAU = 16.0

NSA_DH = 128
NSA_HEADS = 8
NSA_GROUPS = 2
NSA_HPG = 4
CMP_LEN = 32
CMP_STRIDE = 16
CMP_HIDDEN = 256
SLC_BLOCK = 64
SLC_SHIFT = 6
SLC_TOPK = 16
WINDOW = 512
NSA_TILE = 128
NSA_QUERY_TILE = 256
NSA_ONES = 16
NSA_TILES_PER_STEP = 4

HGRN_HEADS = 8
HGRN_DK = 128
HGRN_DV = 128

SSM_HEADDIM = 64
SSM_HEADS = 16
SSM_GROUPS = 2
SSM_STATE = 128
SSM_CONV = 4
SSM_CHUNK = 256
SSM_WIDTH = 1024
SSM_CONV_CH = SSM_WIDTH + 2 * SSM_GROUPS * SSM_STATE

LIN_CHUNK = 128
MAIN_COLS = 6656
VMEM_LIMIT = 56 * 1024 * 1024


def _cparams(sem):
    return pltpu.CompilerParams(dimension_semantics=sem, vmem_limit_bytes=VMEM_LIMIT)


def _dot(a, b):
    return jnp.dot(a, b, preferred_element_type=F32)


def _dot_nt(a, b):
    return lax.dot_general(a, b, (((1,), (1,)), ((), ())), preferred_element_type=F32)


def _dot_tn(a, b):
    return lax.dot_general(a, b, (((0,), (0,)), ((), ())), preferred_element_type=F32)


def _split3(x):
    hi = x.astype(BF16)
    r = x - hi.astype(F32)
    mid = r.astype(BF16)
    lo = (r - mid.astype(F32)).astype(BF16)
    return hi, mid, lo


def _sel_dot_left(sel, x):
    n = x.shape[1]
    y = _dot(sel.astype(BF16), jnp.concatenate(_split3(x), axis=1))
    return y[:, 0:n] + y[:, n:2 * n] + y[:, 2 * n:3 * n]


def _sel_dot_right(x, sel):
    m = x.shape[0]
    y = _dot(jnp.concatenate(_split3(x), axis=0), sel.astype(BF16))
    return y[0:m] + y[m:2 * m] + y[2 * m:3 * m]


def _sigmoid(x):
    return 1.0 / (1.0 + jnp.exp(-x))


def _silu(x):
    return x * _sigmoid(x)


def _log1p_exp_neg_abs(x):
    return jnp.log(1.0 + jnp.exp(-jnp.abs(x)))


def _log_sigmoid(x):
    return jnp.minimum(x, 0.0) - _log1p_exp_neg_abs(x)


def _softplus(x):
    return jnp.maximum(x, 0.0) + _log1p_exp_neg_abs(x)


def _logaddexp(a, b):
    return jnp.maximum(a, b) + _log1p_exp_neg_abs(a - b)


def _mod_kernel(c_ref, w_ref, b_ref, o_ref):
    ca = _silu(c_ref[...])
    o_ref[0] = _dot(ca, w_ref[0]) + b_ref[0]


def _adaln_mod(c, ada_w, ada_b):
    depth, d, n3 = ada_w.shape
    b = c.shape[0]
    rows = 8
    c_pad = jnp.pad(c, ((0, rows - b), (0, 0)))
    tn = 768
    out = pl.pallas_call(
        _mod_kernel,
        out_shape=jax.ShapeDtypeStruct((depth, rows, n3), F32),
        grid=(depth, n3 // tn),
        in_specs=[
            pl.BlockSpec((rows, d), lambda l, j: (0, 0)),
            pl.BlockSpec((1, d, tn), lambda l, j: (l, 0, j)),
            pl.BlockSpec((1, 1, tn), lambda l, j: (l, 0, j)),
        ],
        out_specs=pl.BlockSpec((1, rows, tn), lambda l, j: (l, 0, j)),
        compiler_params=_cparams(("parallel", "parallel")),
        name="adaln_mod",
    )(c_pad, ada_w, ada_b.reshape(depth, 1, n3))
    return out[:, :b]


def _inproj_kernel(x_ref, mod_ref, nw_ref, w_ref, ws_ref, o_ref, os_ref, h_scr):
    d = x_ref.shape[-1]

    @pl.when(pl.program_id(2) == 0)
    def _():
        x = x_ref[0]
        var = jnp.mean(x * x, axis=-1, keepdims=True)
        y = x * lax.rsqrt(var + EPS) * nw_ref[...]
        shift = mod_ref[0, :, 0:d]
        scale = mod_ref[0, :, d:2 * d]
        hb = (y * (1.0 + scale) + shift).astype(BF16)
        h_scr[...] = hb
        os_ref[0] = _dot_nt(hb, ws_ref[...])

    o_ref[0] = _dot_nt(h_scr[...], w_ref[...])


def _in_proj(x, mod_l, norm_w, w_main, w_small):
    b, s, d = x.shape
    n = MAIN_COLS
    ns = w_small.shape[0]
    tm = min(1024, s)
    tn = 512
    return pl.pallas_call(
        _inproj_kernel,
        out_shape=(jax.ShapeDtypeStruct((b, s, n), F32),
                   jax.ShapeDtypeStruct((b, s, ns), F32)),
        grid=(b, s // tm, n // tn),
        in_specs=[
            pl.BlockSpec((1, tm, d), lambda bi, i, j: (bi, i, 0)),
            pl.BlockSpec((1, 1, 3 * d), lambda bi, i, j: (bi, 0, 0)),
            pl.BlockSpec((1, d), lambda bi, i, j: (0, 0)),
            pl.BlockSpec((tn, d), lambda bi, i, j: (j, 0)),
            pl.BlockSpec((ns, d), lambda bi, i, j: (0, 0)),
        ],
        out_specs=(pl.BlockSpec((1, tm, tn), lambda bi, i, j: (bi, i, j)),
                   pl.BlockSpec((1, tm, ns), lambda bi, i, j: (bi, i, 0))),
        scratch_shapes=[pltpu.VMEM((tm, d), BF16)],
        compiler_params=_cparams(("parallel", "parallel", "arbitrary")),
        name="in_proj",
    )(x, mod_l.reshape(b, 1, 3 * d), norm_w.reshape(1, d), w_main, w_small)


def _outproj_kernel(a1_ref, a2_ref, w_ref, x_ref, mod_ref, nw_ref, o_ref):
    d = x_ref.shape[-1]
    half = a1_ref.shape[-1]
    y = _dot(a1_ref[0], w_ref[0:half, :]) + _dot(a2_ref[0], w_ref[half:2 * half, :])
    var = jnp.mean(y * y, axis=-1, keepdims=True)
    yn = y * lax.rsqrt(var + EPS) * nw_ref[...]
    gate = mod_ref[0, :, 2 * d:3 * d]
    o_ref[0] = x_ref[0] + gate * yn


def _out_proj(a1, a2, w_out, x, mod_l, norm_w):
    b, s, d = x.shape
    half = a1.shape[-1]
    tm = min(512, s)
    return pl.pallas_call(
        _outproj_kernel,
        out_shape=jax.ShapeDtypeStruct((b, s, d), F32),
        grid=(b, s // tm),
        in_specs=[
            pl.BlockSpec((1, tm, half), lambda bi, i: (bi, i, 0)),
            pl.BlockSpec((1, tm, half), lambda bi, i: (bi, i, 0)),
            pl.BlockSpec((2 * half, d), lambda bi, i: (0, 0)),
            pl.BlockSpec((1, tm, d), lambda bi, i: (bi, i, 0)),
            pl.BlockSpec((1, 1, 3 * d), lambda bi, i: (bi, 0, 0)),
            pl.BlockSpec((1, d), lambda bi, i: (0, 0)),
        ],
        out_specs=pl.BlockSpec((1, tm, d), lambda bi, i: (bi, i, 0)),
        compiler_params=_cparams(("parallel", "parallel")),
        name="out_proj",
    )(a1, a2, w_out, x, mod_l.reshape(b, 1, 3 * d), norm_w.reshape(1, d))


def _cumsum_chains(tri, gs):
    n = gs[0].shape[1]
    terms = []
    for g in gs:
        terms.extend(_split3(g))
    y = _dot(tri, jnp.concatenate(terms, axis=1))
    return [y[:, (3 * i) * n:(3 * i + 1) * n] + y[:, (3 * i + 1) * n:(3 * i + 2) * n]
            + y[:, (3 * i + 2) * n:(3 * i + 3) * n] for i in range(len(gs))]


def _lin_attn_chunk(q, k, v, g, b, st_scr, b_scr, g_scr, consts):
    c, dk = q.shape
    row, eye, tri, pairs = consts
    b_scr[...] = b
    g_scr[pl.ds(8, c), :] = g
    b_last = b_scr[pl.ds(c - 1, 1), :]

    def level_operand(s, expo):
        upper = (row & s) != 0
        return (jnp.where(upper, q, k) * jnp.exp(expo)).astype(BF16)

    items = []
    for s in [c >> i for i in range(1, c.bit_length() - 2)]:
        pieces = [jnp.broadcast_to(b_scr[pl.ds(p * 2 * s + s - 1, 1), :], (2 * s, dk))
                  for p in range(c // (2 * s))]
        d = b - jnp.concatenate(pieces, axis=0)
        x = level_operand(s, jnp.where((row & s) != 0, d, -d))
        items.append((x, x, pairs[s]))
    g_dn = g_scr[pl.ds(7, c), :]
    g_up = g_scr[pl.ds(9, c), :]
    r4 = row & 3
    x = level_operand(2, jnp.where(r4 == 2, g, jnp.where(r4 == 3, g + g_dn, jnp.where(r4 == 0, g_up, 0.0))))
    items.append((x, x, pairs[2]))
    x = level_operand(1, jnp.where((row & 1) != 0, g, 0.0))
    items.append((x, x, pairs[1]))
    items.append((q.astype(BF16), k.astype(BF16), eye))

    attn = jnp.zeros((c, c), F32)
    for i in range(0, len(items), 2):
        grp = items[i:i + 2]
        prod = _dot_nt(jnp.concatenate([it[0] for it in grp], axis=0),
                       jnp.concatenate([it[1] for it in grp], axis=0))
        for n, it in enumerate(grp):
            attn = jnp.where(it[2], prod[n * c:(n + 1) * c, n * c:(n + 1) * c], attn)

    v16 = v.astype(BF16)
    st = st_scr[...]
    o = _dot(jnp.concatenate([attn.astype(BF16), (q * jnp.exp(b)).astype(BF16)], axis=1),
             jnp.concatenate([v16, st.astype(BF16)], axis=0))
    k_dec = (k * jnp.exp(b_last - b)).astype(BF16)
    e_col = jnp.broadcast_to(jnp.exp(b_last), (dk, dk)).T
    e_col = jnp.concatenate([e_col] * (v.shape[1] // dk), axis=1)
    st_scr[...] = st * e_col + _dot_tn(k_dec, v16)
    return o


def _lin_attn_consts(c, dk):
    row = lax.broadcasted_iota(jnp.int32, (c, dk), 0)
    ri = lax.broadcasted_iota(jnp.int32, (c, c), 0)
    ci = lax.broadcasted_iota(jnp.int32, (c, c), 1)
    eye = ri == ci
    tri = jnp.where(ri >= ci, 1.0, 0.0).astype(BF16)
    pairs = {s: jnp.where((ri & -(2 * s)) == (ci & -(2 * s)), (ri & s) - (ci & s), 0) == s
             for s in [c >> i for i in range(1, c.bit_length())]}
    return row, eye, tri, pairs


def _lin_attn_finish(o, gz, nw):
    var = jnp.mean(o * o, axis=-1, keepdims=True)
    return (o * lax.rsqrt(var + EPS) * nw) * _silu(gz)


LIN_HEADS_PER_STEP = 4


def _gla_kernel(q_ref, k_ref, v_ref, gz_ref, glr_ref, wup_ref, bup_ref, nw_ref, o_ref,
                st_scr, b_scr, g_scr):
    c = LIN_CHUNK
    dk, dv = GLA_DK, GLA_DV
    nb, ts = q_ref.shape[0], q_ref.shape[1]
    hp = q_ref.shape[2] // dk

    @pl.when(pl.program_id(1) == 0)
    def _():
        st_scr[...] = jnp.zeros_like(st_scr)

    g_scr[...] = jnp.zeros_like(g_scr)
    consts = _lin_attn_consts(c, dk)
    nw = nw_ref[...]
    q_scale = dk ** -0.5

    def body(ci, carry):
        r0 = pl.multiple_of(ci * c, c)
        rows = pl.ds(r0, c)
        gs = []
        for bb in range(nb):
            z = _dot(glr_ref[bb, rows, :], wup_ref[...]) + bup_ref[...]
            g_all = _log_sigmoid(z) * (1.0 / GLA_TAU)
            gs.extend(g_all[:, hh * dk:(hh + 1) * dk] for hh in range(hp))
        bs = _cumsum_chains(consts[2], gs)
        for bb in range(nb):
            for hh in range(hp):
                ks = slice(hh * dk, (hh + 1) * dk)
                vs = slice(hh * dv, (hh + 1) * dv)
                ch = bb * hp + hh
                q = q_ref[bb, rows, ks] * q_scale
                o = _lin_attn_chunk(q, k_ref[bb, rows, ks], v_ref[bb, rows, vs], gs[ch], bs[ch],
                                    st_scr.at[ch], b_scr.at[ch], g_scr.at[ch], consts)
                out = _lin_attn_finish(o, gz_ref[bb, rows, vs], nw)
                o_ref[bb, rows, vs] = out.astype(o_ref.dtype)
        return carry

    lax.fori_loop(0, ts // c, body, 0)


def _hgrn_kernel(q_ref, f_ref, v_ref, gz_ref, lbl_ref, nw_ref, o_ref, st_scr, b_scr, g_scr,
                 *, layer):
    c = LIN_CHUNK
    dk, dv = HGRN_DK, HGRN_DV
    nb, ts = q_ref.shape[0], q_ref.shape[1]
    hp = q_ref.shape[2] // dk

    @pl.when(pl.program_id(1) == 0)
    def _():
        st_scr[...] = jnp.zeros_like(st_scr)

    g_scr[...] = jnp.zeros_like(g_scr)
    consts = _lin_attn_consts(c, dk)
    nw = nw_ref[...]

    logits = lbl_ref[...]
    depth = logits.shape[0]
    mx = logits[0:1, :]
    for r in range(1, depth):
        mx = jnp.maximum(mx, logits[r:r + 1, :])
    ex = [jnp.exp(logits[r:r + 1, :] - mx) for r in range(depth)]
    den = ex[0]
    for r in range(1, depth):
        den = den + ex[r]
    sm = [e / den for e in ex]
    lb_all = sm[0]
    for r in range(1, layer + 1):
        lb_all = lb_all + sm[r]
    lb_all = lb_all - sm[0]
    log_lb_all = jnp.log(lb_all)
    log_1mlb_all = jnp.log1p(-lb_all)

    def body(ci, carry):
        r0 = pl.multiple_of(ci * c, c)
        rows = pl.ds(r0, c)
        gs, kk = [], []
        for bb in range(nb):
            for hh in range(hp):
                ks = slice(hh * dk, (hh + 1) * dk)
                z = f_ref[bb, rows, ks]
                y = jnp.exp(-jnp.abs(z))
                r = 1.0 / (1.0 + y)
                log_sig = jnp.minimum(z, 0.0) - jnp.log(1.0 + y)
                gs.append(_logaddexp(log_lb_all[:, ks], log_1mlb_all[:, ks] + log_sig))
                kk.append((1.0 - lb_all[:, ks]) * jnp.where(z > 0.0, y * r, r))
        bs = _cumsum_chains(consts[2], gs)
        for bb in range(nb):
            for hh in range(hp):
                ks = slice(hh * dk, (hh + 1) * dk)
                vs = slice(hh * dv, (hh + 1) * dv)
                ch = bb * hp + hh
                o = _lin_attn_chunk(q_ref[bb, rows, ks], kk[ch], v_ref[bb, rows, vs], gs[ch], bs[ch],
                                    st_scr.at[ch], b_scr.at[ch], g_scr.at[ch], consts)
                out = _lin_attn_finish(o, gz_ref[bb, rows, vs], nw)
                o_ref[bb, rows, vs] = out.astype(o_ref.dtype)
        return carry

    lax.fori_loop(0, ts // c, body, 0)


def _lin_scratch(chains, dv, dk):
    return [pltpu.VMEM((chains, dk, dv), F32),
            pltpu.VMEM((chains, LIN_CHUNK, dk), F32),
            pltpu.VMEM((chains, LIN_CHUNK + 16, dk), F32)]


def _gla_branch(proj, small, w_up, b_up, norm_w):
    b, s, _ = proj.shape
    ts = min(512, s)
    dk, dv, h = GLA_DK, GLA_DV, GLA_HEADS
    hp = LIN_HEADS_PER_STEP
    kw, vw = hp * dk, hp * dv
    w_up_pad = jnp.pad(w_up, ((0, 128 - GLA_RANK), (0, 0)))
    return pl.pallas_call(
        _gla_kernel,
        out_shape=jax.ShapeDtypeStruct((b, s, h * dv), BF16),
        grid=(h // hp, s // ts),
        in_specs=[
            pl.BlockSpec((b, ts, kw), lambda hi, i: (0, i, hi)),
            pl.BlockSpec((b, ts, kw), lambda hi, i: (0, i, h // hp + hi)),
            pl.BlockSpec((b, ts, vw), lambda hi, i: (0, i, h // hp + hi)),
            pl.BlockSpec((b, ts, vw), lambda hi, i: (0, i, 2 * (h // hp) + hi)),
            pl.BlockSpec((b, ts, 128), lambda hi, i: (0, i, 0)),
            pl.BlockSpec((128, kw), lambda hi, i: (0, hi)),
            pl.BlockSpec((1, kw), lambda hi, i: (0, hi)),
            pl.BlockSpec((1, dv), lambda hi, i: (0, 0)),
        ],
        out_specs=pl.BlockSpec((b, ts, vw), lambda hi, i: (0, i, hi)),
        scratch_shapes=_lin_scratch(b * hp, dv, dk),
        compiler_params=_cparams(("parallel", "arbitrary")),
        name="gla",
    )(proj, proj, proj, proj, small, w_up_pad, b_up.reshape(1, -1), norm_w.reshape(1, dv))


def _hgrn_branch(proj, lb_logits, norm_w, layer):
    b, s, _ = proj.shape
    ts = min(512, s)
    dk, dv, h = HGRN_DK, HGRN_DV, HGRN_HEADS
    hp = LIN_HEADS_PER_STEP
    kw, vw = hp * dk, hp * dv
    ng = h // hp
    depth = lb_logits.shape[0]
    return pl.pallas_call(
        functools.partial(_hgrn_kernel, layer=layer),
        out_shape=jax.ShapeDtypeStruct((b, s, h * dv), BF16),
        grid=(ng, s // ts),
        in_specs=[
            pl.BlockSpec((b, ts, kw), lambda hi, i: (0, i, hi)),
            pl.BlockSpec((b, ts, kw), lambda hi, i: (0, i, ng + hi)),
            pl.BlockSpec((b, ts, vw), lambda hi, i: (0, i, 2 * ng + hi)),
            pl.BlockSpec((b, ts, vw), lambda hi, i: (0, i, 3 * ng + hi)),
            pl.BlockSpec((depth, kw), lambda hi, i: (0, hi)),
            pl.BlockSpec((1, dv), lambda hi, i: (0, 0)),
        ],
        out_specs=pl.BlockSpec((b, ts, vw), lambda hi, i: (0, i, hi)),
        scratch_shapes=_lin_scratch(b * hp, dv, dk),
        compiler_params=_cparams(("parallel", "arbitrary")),
        name="hgrn2",
    )(proj, proj, proj, proj, lb_logits, norm_w.reshape(1, dv))


def _ssd_kernel(z_ref, x_ref, bc_ref, dt_ref, cw_ref, cb_ref, dtb_ref, alog_ref, dexp_ref, nw_ref,
                hexp_ref, o_ref, stage_scr, xbc_scr, st_scr, y_scr):
    c = x_ref.shape[1]
    nch = SSM_CONV_CH
    width = SSM_WIDTH
    n = SSM_STATE
    pairs = SSM_HEADS // 2

    @pl.when(pl.program_id(1) == 0)
    def _():
        st_scr[...] = jnp.zeros_like(st_scr)
        stage_scr[pl.ds(0, 8), :] = jnp.zeros((8, nch), F32)

    stage_scr[pl.ds(8, c), 0:width] = x_ref[0]
    stage_scr[pl.ds(8, c), width:nch] = bc_ref[0]
    acc = cb_ref[...] + cw_ref[0:1, :] * stage_scr[pl.ds(5, c), :]
    for kk in range(1, SSM_CONV):
        acc = acc + cw_ref[kk:kk + 1, :] * stage_scr[pl.ds(5 + kk, c), :]
    stage_scr[pl.ds(0, 8), :] = stage_scr[pl.ds(c, 8), :]
    xbc_scr[...] = _silu(acc)

    dt = _softplus(dt_ref[0] + dtb_ref[...])
    a = -jnp.exp(alog_ref[...])
    da = dt * a
    ri = lax.broadcasted_iota(jnp.int32, (c, c), 0)
    ci = lax.broadcasted_iota(jnp.int32, (c, c), 1)
    causal = ri >= ci
    tri = jnp.where(causal, 1.0, 0.0).astype(BF16)
    a_cs = _sel_dot_left(tri, da)
    a_cs_t = a_cs.T
    hexp = hexp_ref[...]
    dt_x = _sel_dot_right(dt, hexp)
    acs_x = _sel_dot_right(a_cs, hexp)
    a_last_x = acs_x[c - 1:c, :]
    lane = lax.broadcasted_iota(jnp.int32, (c, 2 * SSM_HEADDIM), 1)
    first = lane < SSM_HEADDIM

    for grp in range(SSM_GROUPS):
        bm = xbc_scr[:, width + grp * n: width + (grp + 1) * n]
        cm = xbc_scr[:, width + (SSM_GROUPS + grp) * n: width + (SSM_GROUPS + grp + 1) * n]
        bm16 = bm.astype(BF16)
        cm16 = cm.astype(BF16)
        cb = _dot_nt(cm16, bm16)
        for pp in range(pairs // SSM_GROUPS):
            p = grp * (pairs // SSM_GROUPS) + pp
            lo = p * 2 * SSM_HEADDIM
            hi = lo + 2 * SSM_HEADDIM
            xs = xbc_scr[:, lo:hi]
            xdt = xs * dt_x[:, lo:hi]
            acs = acs_x[:, lo:hi]
            y = jnp.zeros((c, 2 * SSM_HEADDIM), F32)
            for hh in range(2):
                h = 2 * p + hh
                col = jnp.broadcast_to(a_cs[:, h:h + 1], (c, c))
                rw = jnp.broadcast_to(a_cs_t[h:h + 1, :], (c, c))
                lmat = jnp.exp(jnp.where(causal, col - rw, NEG))
                keep = first if hh == 0 else jnp.logical_not(first)
                xh = jnp.where(keep, xdt, 0.0).astype(BF16)
                y = y + _dot((cb * lmat).astype(BF16), xh)
            st = st_scr[p]
            y = y + _dot(cm16, st.astype(BF16)) * jnp.exp(acs)
            decay = jnp.exp(a_last_x[:, lo:hi] - acs)
            st_scr[p] = st * jnp.exp(a_last_x[:, lo:hi]) + _dot_tn(bm16, (xdt * decay).astype(BF16))
            y_scr[:, lo:hi] = y + dexp_ref[:, lo:hi] * xs

    yz = y_scr[...] * _silu(z_ref[0])
    var = jnp.mean(yz * yz, axis=-1, keepdims=True)
    o_ref[0] = (yz * lax.rsqrt(var + EPS) * nw_ref[...]).astype(o_ref.dtype)


def _ssd_branch(proj, small, conv_w, conv_b, dt_bias, a_log, d_skip, norm_w):
    b, s, _ = proj.shape
    c = math.gcd(SSM_CHUNK, s)
    nch = SSM_CONV_CH
    pad16 = lambda v: jnp.pad(v.reshape(1, -1), ((0, 0), (0, 128 - SSM_HEADS)))
    hexp = np.zeros((128, SSM_WIDTH), np.float32)
    for h in range(SSM_HEADS):
        hexp[h, h * SSM_HEADDIM:(h + 1) * SSM_HEADDIM] = 1.0
    d_exp = jnp.repeat(d_skip, SSM_HEADDIM).reshape(1, SSM_WIDTH)
    return pl.pallas_call(
        _ssd_kernel,
        out_shape=jax.ShapeDtypeStruct((b, s, SSM_WIDTH), BF16),
        grid=(b, s // c),
        in_specs=[
            pl.BlockSpec((1, c, SSM_WIDTH), lambda bi, i: (bi, i, 4)),
            pl.BlockSpec((1, c, SSM_WIDTH), lambda bi, i: (bi, i, 5)),
            pl.BlockSpec((1, c, 512), lambda bi, i: (bi, i, 12)),
            pl.BlockSpec((1, c, 128), lambda bi, i: (bi, i, 0)),
            pl.BlockSpec((SSM_CONV, nch), lambda bi, i: (0, 0)),
            pl.BlockSpec((1, nch), lambda bi, i: (0, 0)),
            pl.BlockSpec((1, 128), lambda bi, i: (0, 0)),
            pl.BlockSpec((1, 128), lambda bi, i: (0, 0)),
            pl.BlockSpec((1, SSM_WIDTH), lambda bi, i: (0, 0)),
            pl.BlockSpec((1, SSM_WIDTH), lambda bi, i: (0, 0)),
            pl.BlockSpec((128, SSM_WIDTH), lambda bi, i: (0, 0)),
        ],
        out_specs=pl.BlockSpec((1, c, SSM_WIDTH), lambda bi, i: (bi, i, 0)),
        scratch_shapes=[
            pltpu.VMEM((c + 8, nch), F32),
            pltpu.VMEM((c, nch), F32),
            pltpu.VMEM((SSM_HEADS // 2, SSM_STATE, 2 * SSM_HEADDIM), F32),
            pltpu.VMEM((c, SSM_WIDTH), F32),
        ],
        compiler_params=_cparams(("parallel", "arbitrary")),
        name="ssd",
    )(proj, proj, proj, small, conv_w, conv_b.reshape(1, nch), pad16(dt_bias), pad16(a_log),
      d_exp, norm_w.reshape(1, SSM_WIDTH), jnp.asarray(hexp, dtype=BF16))


def _rope(x, cos, sin_signed):
    return x * cos + pltpu.roll(x, NSA_DH // 2, axis=1) * sin_signed


def _nsa_prep_kernel(q_ref, kc_ref, ks_ref, vs_ref, kw_ref, vw_ref, cos_ref, sin_ref,
                     qo_ref, kco_ref, kso_ref, vso_ref, kwo_ref, vwo_ref):
    cos = cos_ref[...]
    sin = sin_ref[...]
    dh = NSA_DH
    scale = dh ** -0.5
    for h in range(NSA_HEADS):
        sl = slice(h * dh, (h + 1) * dh)
        qo_ref[0, :, sl] = (_rope(q_ref[0, :, sl], cos, sin) * scale).astype(qo_ref.dtype)
    for g in range(NSA_GROUPS):
        sl = slice(g * dh, (g + 1) * dh)
        kco_ref[0, :, sl] = _rope(kc_ref[0, :, sl], cos, sin)
        kso_ref[0, :, sl] = _rope(ks_ref[0, :, sl], cos, sin).astype(kso_ref.dtype)
        kwo_ref[0, :, sl] = _rope(kw_ref[0, :, sl], cos, sin).astype(kwo_ref.dtype)
    tk = NSA_TILE
    for g in range(NSA_GROUPS):
        for r in range(q_ref.shape[1] // tk):
            rows = slice(r * tk, (r + 1) * tk)
            sl = slice(g * dh, (g + 1) * dh)
            vso_ref[0, g, r, 0:dh, :] = vs_ref[0, rows, sl].T.astype(vso_ref.dtype)
            vwo_ref[0, g, r, 0:dh, :] = vw_ref[0, rows, sl].T.astype(vwo_ref.dtype)
            ones = jnp.ones((NSA_ONES, tk), vso_ref.dtype)
            vso_ref[0, g, r, dh:dh + NSA_ONES, :] = ones
            vwo_ref[0, g, r, dh:dh + NSA_ONES, :] = ones


def _nsa_prep(proj):
    b, s, _ = proj.shape
    ts = min(512, s)
    tk = NSA_TILE
    half = NSA_DH // 2
    inv_freq = ROPE_THETA ** (-np.arange(half, dtype=np.float64) / half)
    ang = np.arange(s, dtype=np.float64)[:, None] * inv_freq[None, :]
    cos_full = jnp.asarray(np.concatenate([np.cos(ang), np.cos(ang)], axis=-1), dtype=F32)
    sin_signed = jnp.asarray(np.concatenate([-np.sin(ang), np.sin(ang)], axis=-1), dtype=F32)
    kvw = NSA_GROUPS * NSA_DH
    col = lambda idx: (lambda bi, i: (bi, i, idx))
    kv_spec = lambda idx: pl.BlockSpec((1, ts, kvw), col(idx))
    kv_out = pl.BlockSpec((1, ts, kvw), lambda bi, i: (bi, i, 0))
    vrows = NSA_DH + NSA_ONES
    vt_shape = jax.ShapeDtypeStruct((b, NSA_GROUPS, s // tk, vrows, tk), BF16)
    vt_out = pl.BlockSpec((1, NSA_GROUPS, ts // tk, vrows, tk), lambda bi, i: (bi, 0, i, 0, 0))
    return pl.pallas_call(
        _nsa_prep_kernel,
        out_shape=(jax.ShapeDtypeStruct((b, s, NSA_HEADS * NSA_DH), BF16),
                   jax.ShapeDtypeStruct((b, s, kvw), F32),
                   jax.ShapeDtypeStruct((b, s, kvw), BF16),
                   vt_shape,
                   jax.ShapeDtypeStruct((b, s, kvw), BF16),
                   vt_shape),
        grid=(b, s // ts),
        in_specs=[
            pl.BlockSpec((1, ts, NSA_HEADS * NSA_DH), col(3)),
            kv_spec(16), kv_spec(18), kv_spec(19), kv_spec(20), kv_spec(21),
            pl.BlockSpec((ts, NSA_DH), lambda bi, i: (i, 0)),
            pl.BlockSpec((ts, NSA_DH), lambda bi, i: (i, 0)),
        ],
        out_specs=(pl.BlockSpec((1, ts, NSA_HEADS * NSA_DH), lambda bi, i: (bi, i, 0)),
                   kv_out, kv_out, vt_out, kv_out, vt_out),
        compiler_params=_cparams(("parallel", "parallel")),
        name="nsa_prep",
    )(proj, proj, proj, proj, proj, proj, cos_full, sin_signed)


def _compress_kernel(t_ref, pe_ref, w1_ref, w2_ref, o_ref, sh_scr, *, transposed):
    n = t_ref.shape[1] // CMP_STRIDE
    dh = NSA_DH
    half = CMP_LEN // 2
    acc_a = jnp.zeros((n, CMP_HIDDEN), F32)
    acc_b = jnp.zeros((n, CMP_HIDDEN), F32)
    for l in range(half):
        t = t_ref[0, pl.ds(l, n, stride=CMP_STRIDE), :]
        acc_a = acc_a + _dot((t + pe_ref[l:l + 1, :]).astype(BF16), w1_ref[l * dh:(l + 1) * dh, :])
        acc_b = acc_b + _dot((t + pe_ref[half + l:half + l + 1, :]).astype(BF16),
                             w1_ref[(half + l) * dh:(half + l + 1) * dh, :])
    sh_scr[pl.ds(0, n), :] = acc_b
    sh_scr[pl.ds(n, 8), :] = jnp.zeros((8, CMP_HIDDEN), F32)
    hid = acc_a + sh_scr[pl.ds(1, n), :]
    out = _dot(_silu(hid).astype(BF16), w2_ref[...])
    if transposed:
        tk = NSA_TILE
        for r in range(n // tk):
            o_ref[0, 0, r] = out[r * tk:(r + 1) * tk, :].T.astype(o_ref.dtype)
    else:
        o_ref[0, 0] = out.astype(o_ref.dtype)


def _nsa_compress(src, col0, pe, w1, w2, transposed):
    b, s, _ = src.shape
    n = s // CMP_STRIDE
    base = col0 // NSA_DH
    tk = NSA_TILE
    if transposed:
        out_shape = jax.ShapeDtypeStruct((b, NSA_GROUPS, n // tk, NSA_DH, tk), BF16)
        out_spec = pl.BlockSpec((1, 1, n // tk, NSA_DH, tk), lambda bi, g: (bi, g, 0, 0, 0))
    else:
        out_shape = jax.ShapeDtypeStruct((b, NSA_GROUPS, n, NSA_DH), BF16)
        out_spec = pl.BlockSpec((1, 1, n, NSA_DH), lambda bi, g: (bi, g, 0, 0))
    return pl.pallas_call(
        functools.partial(_compress_kernel, transposed=transposed),
        out_shape=out_shape,
        grid=(b, NSA_GROUPS),
        in_specs=[
            pl.BlockSpec((1, s, NSA_DH), lambda bi, g: (bi, 0, base + g)),
            pl.BlockSpec((CMP_LEN, NSA_DH), lambda bi, g: (0, 0)),
            pl.BlockSpec((CMP_LEN * NSA_DH, CMP_HIDDEN), lambda bi, g: (0, 0)),
            pl.BlockSpec((CMP_HIDDEN, NSA_DH), lambda bi, g: (0, 0)),
        ],
        out_specs=out_spec,
        scratch_shapes=[pltpu.VMEM((n + 8, CMP_HIDDEN), F32)],
        compiler_params=_cparams(("parallel", "parallel")),
        name="nsa_compress",
    )(src, pe, w1.astype(BF16), w2.astype(BF16))


def _weights_bf16(s_tiles, m):
    return [jnp.exp((s - m).astype(BF16)) for s in s_tiles]


def _flash_update(s_tiles, vt_tiles, m_ref, acc_ref):
    m_old = m_ref[...]
    m_new = m_old
    for s in s_tiles:
        m_new = jnp.maximum(m_new, jnp.max(s, axis=0, keepdims=True))
    alpha = jnp.exp(m_old - m_new)
    acc_ref[...] = alpha * acc_ref[...] + _pv(vt_tiles, _weights_bf16(s_tiles, m_new))
    m_ref[...] = m_new


def _pv(vt_tiles, p_tiles):
    vt = jnp.concatenate(vt_tiles, axis=1)
    p = jnp.concatenate([p.astype(BF16) for p in p_tiles], axis=0)
    return _dot(vt, p)


def _softmax_tiles(s_tiles, vt_tiles):
    m = jnp.max(s_tiles[0], axis=0, keepdims=True)
    for s in s_tiles[1:]:
        m = jnp.maximum(m, jnp.max(s, axis=0, keepdims=True))
    p_tiles = [jnp.exp(s - m) for s in s_tiles]
    l = jnp.sum(p_tiles[0], axis=0, keepdims=True)
    for p in p_tiles[1:]:
        l = l + jnp.sum(p, axis=0, keepdims=True)
    return p_tiles, l, _pv(vt_tiles, p_tiles)


def _softmax_tiles_aug(s_tiles, vt_tiles):
    m = jnp.max(s_tiles[0], axis=0, keepdims=True)
    for s in s_tiles[1:]:
        m = jnp.maximum(m, jnp.max(s, axis=0, keepdims=True))
    acc = _pv(vt_tiles, _weights_bf16(s_tiles, m))
    return acc[0:NSA_DH] * (1.0 / acc[NSA_DH:NSA_DH + 1])


def _nsa_kernel(q_ref, kc_ref, vct_ref, ovt_ref, ks_ref, vst_ref, kw_ref, vwt_ref, gate_ref, nz_ref,
                o_ref, m_scr, accs_scr, acc_scr, cap_scr, capd_scr, capl_scr, sa_scr, sb_scr):
    tq = q_ref.shape[1]
    tk = NSA_TILE
    dh = NSA_DH
    hpg = NSA_HPG
    lanes = hpg * tq
    ns = ovt_ref.shape[0]
    qtiles = tq // tk
    qi = pl.program_id(2)
    t0 = qi * tq
    kt0 = qi * qtiles

    qt = jnp.concatenate([q_ref[0, :, j * dh:(j + 1) * dh].astype(F32).T for j in range(hpg)],
                         axis=1).astype(BF16)
    rowi = lax.broadcasted_iota(jnp.int32, (tk, lanes), 0)
    qpos = lax.broadcasted_iota(jnp.int32, (tk, lanes), 1) & (tq - 1)
    for d in range(qtiles):
        capd_scr[d] = jnp.where(rowi + d * tk <= qpos, FORCE, NEG)
        capl_scr[d] = jnp.where(rowi + d * tk > qpos, FORCE, NEG)

    n_ct = kc_ref.shape[2] // tk
    s_all = _dot(kc_ref[0, 0], qt)
    s_tiles = []
    for c in range(n_ct):
        first_end = c * (tk * CMP_STRIDE) + (CMP_LEN - 1) - t0
        cap = jnp.where(rowi * CMP_STRIDE + first_end <= qpos, FORCE, NEG)
        s_tiles.append(jnp.minimum(s_all[c * tk:(c + 1) * tk, :], cap))
    p_tiles, l_c, acc_c = _softmax_tiles(s_tiles, [vct_ref[0, 0, c] for c in range(n_ct)])
    p_hi = [p.astype(BF16) for p in p_tiles]
    p_lo = [(p - h.astype(F32)).astype(BF16) for p, h in zip(p_tiles, p_hi)]
    imp_2 = _dot(ovt_ref[...], jnp.concatenate([jnp.concatenate(p_hi, axis=0),
                                                 jnp.concatenate(p_lo, axis=0)], axis=1))
    imp_un = imp_2[:, 0:lanes] + imp_2[:, lanes:2 * lanes]
    tl = t0 + (lax.broadcasted_iota(jnp.int32, (1, lanes), 1) & (tq - 1))
    inv_l = jnp.where(tl >= CMP_LEN - 1, 1.0 / l_c, 0.0)
    acc_scr[0] = acc_c * inv_l
    impn = imp_un * inv_l
    imp = impn[:, 0:tq]
    for j in range(1, hpg):
        imp = imp + impn[:, j * tq:(j + 1) * tq]

    blk = lax.broadcasted_iota(jnp.int32, (ns, tq), 0)
    blk_t = jnp.right_shift(t0 + lax.broadcasted_iota(jnp.int32, (ns, tq), 1), SLC_SHIFT)
    score = jnp.where(blk == 0, FORCE,
                      jnp.where(blk == blk_t, FORCE,
                                jnp.where(blk == blk_t - 1, FORCE,
                                          jnp.where(blk <= blk_t, imp, NEG))))
    blk_f = blk.astype(F32)
    cap_sel = jnp.full((ns, tq), NEG, F32)
    for _ in range(min(SLC_TOPK, ns)):
        mx = jnp.max(score, axis=0, keepdims=True)
        first = jnp.min(jnp.where(score == mx, blk_f, float(ns)), axis=0, keepdims=True)
        pick = blk_f == first
        cap_sel = jnp.where(pick, FORCE, cap_sel)
        score = jnp.where(pick, -jnp.inf, score)
    for j in range(hpg):
        cap_scr[:, j * tq:(j + 1) * tq] = cap_sel

    bpt = tk // SLC_BLOCK

    def slc_scores(step):
        k0 = pl.multiple_of(step * (nb * tk), nb * tk)
        s = _dot(ks_ref[0, pl.ds(k0, nb * tk), :], qt)
        out = []
        for r in range(nb):
            caps = [jnp.broadcast_to(cap_scr[pl.ds((step * nb + r) * bpt + i, 1), :], (SLC_BLOCK, lanes))
                    for i in range(bpt)]
            out.append(jnp.minimum(s[r * tk:(r + 1) * tk, :], jnp.concatenate(caps, axis=0)))
        return out

    wt = WINDOW // tk
    s_tiles, vt_tiles = [], []
    for d in range(-wt, qtiles):
        kt = kt0 + d
        ktc = jnp.maximum(kt, 0)
        s = _dot(kw_ref[0, pl.ds(pl.multiple_of(ktc * tk, tk), tk), :], qt)
        if d + wt < qtiles:
            s = jnp.minimum(s, capl_scr[d + wt])
        if d >= 0:
            s = jnp.minimum(s, capd_scr[d])
        else:
            s = jnp.minimum(s, jnp.where(kt >= 0, FORCE, NEG))
        s_tiles.append(s)
        vt_tiles.append(vwt_ref[0, 0, ktc])
    acc_scr[1] = _softmax_tiles_aug(s_tiles, vt_tiles)

    nb = NSA_TILES_PER_STEP
    m_scr[...] = jnp.full((1, lanes), NEG, F32)
    accs_scr[...] = jnp.zeros_like(accs_scr)
    state = (m_scr, accs_scr)
    n_full = kt0 // nb

    def score_step(step, dst):
        for r, s in enumerate(slc_scores(step)):
            dst[r * tk:(r + 1) * tk, :] = s

    def value_step(step, src):
        _flash_update([src[r * tk:(r + 1) * tk, :] for r in range(nb)],
                      [vst_ref[0, 0, step * nb + r] for r in range(nb)], *state)

    @pl.when(n_full > 0)
    def _():
        score_step(0, sa_scr)

    def slc_pair(j, carry):
        score_step(2 * j + 1, sb_scr)
        value_step(2 * j, sa_scr)
        score_step(jnp.minimum(2 * j + 2, n_full - 1), sa_scr)
        value_step(2 * j + 1, sb_scr)
        return carry

    lax.fori_loop(0, n_full // 2, slc_pair, 0)

    @pl.when(n_full % 2 == 1)
    def _():
        value_step(n_full - 1, sa_scr)

    s_tiles, vt_tiles = [], []
    for r, s in enumerate(slc_scores(n_full)):
        kt = n_full * nb + r
        cap = jnp.where(kt < kt0 + qtiles, FORCE, NEG)
        for d in range(qtiles):
            cap = jnp.minimum(cap, jnp.maximum(capd_scr[d], jnp.where(kt == kt0 + d, NEG, FORCE)))
        s_tiles.append(jnp.minimum(s, cap))
        vt_tiles.append(vst_ref[0, 0, kt])
    _flash_update(s_tiles, vt_tiles, *state)

    o_c = acc_scr[0]
    o_s = accs_scr[0:dh, :] * (1.0 / accs_scr[dh:dh + 1, :])
    o_w = acc_scr[1]
    sgt = _sigmoid(gate_ref[0]).T
    for j in range(hpg):
        ls = slice(j * tq, (j + 1) * tq)
        mix_t = (sgt[3 * j:3 * j + 1, :] * o_c[:, ls] + sgt[3 * j + 1:3 * j + 2, :] * o_s[:, ls]
                 + sgt[3 * j + 2:3 * j + 3, :] * o_w[:, ls])
        sl = slice(j * dh, (j + 1) * dh)
        o_ref[0, :, sl] = (mix_t.T * _silu(nz_ref[0, :, sl])).astype(o_ref.dtype)


def _nsa_attention(qr, kc, vct, ksr, vst, kwr, vwt, small, proj):
    b, s, _ = qr.shape
    tk = NSA_TILE
    tq = min(NSA_QUERY_TILE, s)
    qtiles = tq // tk
    assert s % tq == 0 and tq % tk == 0 and WINDOW % tk == 0 and tk % SLC_BLOCK == 0
    assert NSA_TILES_PER_STEP % qtiles == 0 and WINDOW // tk >= qtiles
    assert (s // tk) % NSA_TILES_PER_STEP == 0
    ncp = kc.shape[2]
    assert ncp % tk == 0
    ns = s // SLC_BLOCK
    gw = NSA_HPG * NSA_DH
    lanes = NSA_HPG * tq
    cs = np.arange(ncp)[None, :] * CMP_STRIDE
    ss = np.arange(ns)[:, None] * SLC_BLOCK
    ovt = ((np.minimum(cs + CMP_LEN, ss + SLC_BLOCK) - np.maximum(cs, ss)) > 0).astype(np.float32)
    ovt[:, ncp - 1] = 0.0
    kv_spec = pl.BlockSpec((1, s, NSA_DH), lambda bi, g, i: (bi, 0, g))
    vrows = NSA_DH + NSA_ONES
    vt_spec = pl.BlockSpec((1, 1, s // tk, vrows, tk), lambda bi, g, i: (bi, g, 0, 0, 0))
    return pl.pallas_call(
        _nsa_kernel,
        out_shape=jax.ShapeDtypeStruct((b, s, NSA_HEADS * NSA_DH), BF16),
        grid=(b, NSA_GROUPS, s // tq),
        in_specs=[
            pl.BlockSpec((1, tq, gw), lambda bi, g, i: (bi, i, g)),
            pl.BlockSpec((1, 1, ncp, NSA_DH), lambda bi, g, i: (bi, g, 0, 0)),
            pl.BlockSpec((1, 1, ncp // tk, NSA_DH, tk), lambda bi, g, i: (bi, g, 0, 0, 0)),
            pl.BlockSpec((ns, ncp), lambda bi, g, i: (0, 0)),
            kv_spec, vt_spec, kv_spec, vt_spec,
            pl.BlockSpec((1, tq, 128), lambda bi, g, i: (bi, i, 1 + g)),
            pl.BlockSpec((1, tq, gw), lambda bi, g, i: (bi, i, 11 + g)),
        ],
        out_specs=pl.BlockSpec((1, tq, gw), lambda bi, g, i: (bi, i, g)),
        scratch_shapes=[
            pltpu.VMEM((1, lanes), F32),
            pltpu.VMEM((vrows, lanes), F32),
            pltpu.VMEM((2, NSA_DH, lanes), F32),
            pltpu.VMEM((ns, lanes), F32),
            pltpu.VMEM((qtiles, tk, lanes), F32),
            pltpu.VMEM((qtiles, tk, lanes), F32),
            pltpu.VMEM((NSA_TILES_PER_STEP * tk, lanes), F32),
            pltpu.VMEM((NSA_TILES_PER_STEP * tk, lanes), F32),
        ],
        compiler_params=_cparams(("parallel", "parallel", "arbitrary")),
        name="nsa_attention",
    )(qr, kc, vct, jnp.asarray(ovt, dtype=BF16), ksr, vst, kwr, vwt, small, proj)


def _nsa_branch(proj, small, pe_k, w1_k, w2_k, pe_v, w1_v, w2_v):
    qr, kcr, ksr, vst, kwr, vwt = _nsa_prep(proj)
    kc = _nsa_compress(kcr, 0, pe_k, w1_k, w2_k, transposed=False)
    vct = _nsa_compress(proj, 4352, pe_v, w1_v, w2_v, transposed=True)
    return _nsa_attention(qr, kc, vct, ksr, vst, kwr, vwt, small, proj)


def _pad_rows(w, rows):
    return jnp.pad(w, ((0, rows - w.shape[0]), (0, 0)))


def _cast_rows_kernel(w_ref, o_ref):
    o_ref[...] = w_ref[...].astype(o_ref.dtype)


def _even_weights(w_in):
    wt = w_in.T
    d = wt.shape[1]
    per = NSA_HPG * 3
    tile = 512
    t1, t2 = 2048 // tile, 5632 // tile

    def src_row(j):
        shift8 = jnp.where(j >= t2, (GLA_RANK + NSA_GROUPS * per) // 8, jnp.where(j >= t1, GLA_RANK // 8, 0))
        return 8 * (j * (tile // 8) + shift8)

    main = pl.pallas_call(
        _cast_rows_kernel,
        out_shape=jax.ShapeDtypeStruct((MAIN_COLS, d), BF16),
        grid=(MAIN_COLS // tile,),
        in_specs=[pl.BlockSpec((pl.Element(tile), pl.Element(d)), lambda j: (src_row(j), 0))],
        out_specs=pl.BlockSpec((tile, d), lambda j: (j, 0)),
        compiler_params=_cparams(("parallel",)),
        name="weight_rows",
    )(wt)

    def small_kernel(decay_ref, gates_ref, o_ref):
        row = lax.broadcasted_iota(jnp.int32, (128, d), 0)
        zeros = jnp.zeros((128, d), F32)
        o_ref[0:128, :] = jnp.where(row < GLA_RANK, decay_ref[...], 0.0).astype(o_ref.dtype)
        gates = gates_ref[...]
        for g in range(NSA_GROUPS):
            rows_g = jnp.concatenate([gates[g * per:(g + 1) * per, :], zeros[0:128 - per, :]], axis=0)
            o_ref[(1 + g) * 128:(2 + g) * 128, :] = rows_g.astype(o_ref.dtype)

    window = lambda r0: pl.BlockSpec((pl.Element(128), pl.Element(d)), lambda i: (r0, 0))
    small = pl.pallas_call(
        small_kernel,
        out_shape=jax.ShapeDtypeStruct(((1 + NSA_GROUPS) * 128, d), BF16),
        grid=(1,),
        in_specs=[window(2048), window(5648)],
        out_specs=pl.BlockSpec(((1 + NSA_GROUPS) * 128, d), lambda i: (0, 0)),
        compiler_params=_cparams(("arbitrary",)),
        name="weight_rows_small",
    )(wt, wt)
    return main, small


def _odd_weights(w_in):
    wt = w_in.T.astype(BF16)
    return wt, _pad_rows(wt[MAIN_COLS:MAIN_COLS + SSM_HEADS], 128)


def kernel(x, c, ada_w, ada_b, pre_norm_w, post_norm_w, even_w_in, even_w_out, gla_w_up, gla_b_up,
           gla_norm_w, nsa_pe_k, nsa_w1_k, nsa_w2_k, nsa_pe_v, nsa_w1_v, nsa_w2_v, odd_w_in,
           odd_w_out, hgrn_lb_logits, hgrn_norm_w, ssm_conv_w, ssm_conv_b, ssm_dt_bias, ssm_a_log,
           ssm_d, ssm_norm_w):
    depth = ada_w.shape[0]
    mod = _adaln_mod(c, ada_w, ada_b)
    for l in range(depth):
        if l % 2 == 0:
            e = l // 2
            w_main, w_small = _even_weights(even_w_in[e])
            proj, small = _in_proj(x, mod[l], pre_norm_w[l], w_main, w_small)
            o_a = _gla_branch(proj, small, gla_w_up[e], gla_b_up[e], gla_norm_w[e])
            o_b = _nsa_branch(proj, small, nsa_pe_k[e], nsa_w1_k[e], nsa_w2_k[e],
                              nsa_pe_v[e], nsa_w1_v[e], nsa_w2_v[e])
            w_out = even_w_out[e]
        else:
            o = l // 2
            w_main, w_small = _odd_weights(odd_w_in[o])
            proj, small = _in_proj(x, mod[l], pre_norm_w[l], w_main, w_small)
            o_a = _hgrn_branch(proj, hgrn_lb_logits, hgrn_norm_w[o], l)
            o_b = _ssd_branch(proj, small, ssm_conv_w[o], ssm_conv_b[o], ssm_dt_bias[o],
                              ssm_a_log[o], ssm_d[o], ssm_norm_w[o])
            w_out = odd_w_out[o]
        x = _out_proj(o_a, o_b, w_out.astype(BF16), x, mod[l], post_norm_w[l])
    return x
```

```python
import functools
import math

import jax
import jax.numpy as jnp
import numpy as np
from jax import lax
from jax.experimental import pallas as pl
from jax.experimental.pallas import tpu as pltpu

F32 = jnp.float32
BF16 = jnp.bfloat16

D_MODEL = 2048
EPS = 1e-6
NEG = -1e30
FORCE = 1e30
ROPE_THETA = 10000.0

GLA_HEADS = 4
GLA_DK = 128
GLA_DV = 256
GLA_RANK = 16
GLA_TAU = 16.0

NSA_DH = 128
NSA_HEADS = 8
NSA_GROUPS = 2
NSA_HPG = 4
CMP_LEN = 32
CMP_STRIDE = 16
CMP_HIDDEN = 256
SLC_BLOCK = 64
SLC_SHIFT = 6
SLC_TOPK = 16
WINDOW = 512
NSA_TILE = 128
NSA_QUERY_TILE = 256
NSA_ONES = 16
NSA_TILES_PER_STEP = 4

HGRN_HEADS = 8
HGRN_DK = 128
HGRN_DV = 128

SSM_HEADDIM = 64
SSM_HEADS = 16
SSM_GROUPS = 2
SSM_STATE = 128
SSM_CONV = 4
SSM_CHUNK = 256
SSM_WIDTH = 1024
SSM_CONV_CH = SSM_WIDTH + 2 * SSM_GROUPS * SSM_STATE

LIN_CHUNK = 128
MAIN_COLS = 6656
VMEM_LIMIT = 56 * 1024 * 1024


def _cparams(sem):
    return pltpu.CompilerParams(dimension_semantics=sem, vmem_limit_bytes=VMEM_LIMIT)


def _dot(a, b):
    return jnp.dot(a, b, preferred_element_type=F32)


def _dot_nt(a, b):
    return lax.dot_general(a, b, (((1,), (1,)), ((), ())), preferred_element_type=F32)


def _dot_tn(a, b):
    return lax.dot_general(a, b, (((0,), (0,)), ((), ())), preferred_element_type=F32)


def _split3(x):
    hi = x.astype(BF16)
    r = x - hi.astype(F32)
    mid = r.astype(BF16)
    lo = (r - mid.astype(F32)).astype(BF16)
    return hi, mid, lo


def _sel_dot_left(sel, x):
    n = x.shape[1]
    y = _dot(sel.astype(BF16), jnp.concatenate(_split3(x), axis=1))
    return y[:, 0:n] + y[:, n:2 * n] + y[:, 2 * n:3 * n]


def _sel_dot_right(x, sel):
    m = x.shape[0]
    y = _dot(jnp.concatenate(_split3(x), axis=0), sel.astype(BF16))
    return y[0:m] + y[m:2 * m] + y[2 * m:3 * m]


def _sigmoid(x):
    return 1.0 / (1.0 + jnp.exp(-x))


def _silu(x):
    return x * _sigmoid(x)


def _log1p_exp_neg_abs(x):
    return jnp.log(1.0 + jnp.exp(-jnp.abs(x)))


def _log_sigmoid(x):
    return jnp.minimum(x, 0.0) - _log1p_exp_neg_abs(x)


def _softplus(x):
    return jnp.maximum(x, 0.0) + _log1p_exp_neg_abs(x)


def _logaddexp(a, b):
    return jnp.maximum(a, b) + _log1p_exp_neg_abs(a - b)


def _mod_kernel(c_ref, w_ref, b_ref, o_ref):
    ca = _silu(c_ref[...])
    o_ref[0] = _dot(ca, w_ref[0]) + b_ref[0]


def _adaln_mod(c, ada_w, ada_b):
    depth, d, n3 = ada_w.shape
    b = c.shape[0]
    rows = 8
    c_pad = jnp.pad(c, ((0, rows - b), (0, 0)))
    tn = 768
    out = pl.pallas_call(
        _mod_kernel,
        out_shape=jax.ShapeDtypeStruct((depth, rows, n3), F32),
        grid=(depth, n3 // tn),
        in_specs=[
            pl.BlockSpec((rows, d), lambda l, j: (0, 0)),
            pl.BlockSpec((1, d, tn), lambda l, j: (l, 0, j)),
            pl.BlockSpec((1, 1, tn), lambda l, j: (l, 0, j)),
        ],
        out_specs=pl.BlockSpec((1, rows, tn), lambda l, j: (l, 0, j)),
        compiler_params=_cparams(("parallel", "parallel")),
        name="adaln_mod",
    )(c_pad, ada_w, ada_b.reshape(depth, 1, n3))
    return out[:, :b]


def _inproj_kernel(x_ref, mod_ref, nw_ref, w_ref, ws_ref, o_ref, os_ref, h_scr):
    d = x_ref.shape[-1]

    @pl.when(pl.program_id(2) == 0)
    def _():
        x = x_ref[0]
        var = jnp.mean(x * x, axis=-1, keepdims=True)
        y = x * lax.rsqrt(var + EPS) * nw_ref[...]
        shift = mod_ref[0, :, 0:d]
        scale = mod_ref[0, :, d:2 * d]
        hb = (y * (1.0 + scale) + shift).astype(BF16)
        h_scr[...] = hb
        os_ref[0] = _dot_nt(hb, ws_ref[...])

    o_ref[0] = _dot_nt(h_scr[...], w_ref[...])


def _in_proj(x, mod_l, norm_w, w_main, w_small):
    b, s, d = x.shape
    n = MAIN_COLS
    ns = w_small.shape[0]
    tm = min(1024, s)
    tn = 512
    return pl.pallas_call(
        _inproj_kernel,
        out_shape=(jax.ShapeDtypeStruct((b, s, n), F32),
                   jax.ShapeDtypeStruct((b, s, ns), F32)),
        grid=(b, s // tm, n // tn),
        in_specs=[
            pl.BlockSpec((1, tm, d), lambda bi, i, j: (bi, i, 0)),
            pl.BlockSpec((1, 1, 3 * d), lambda bi, i, j: (bi, 0, 0)),
            pl.BlockSpec((1, d), lambda bi, i, j: (0, 0)),
            pl.BlockSpec((tn, d), lambda bi, i, j: (j, 0)),
            pl.BlockSpec((ns, d), lambda bi, i, j: (0, 0)),
        ],
        out_specs=(pl.BlockSpec((1, tm, tn), lambda bi, i, j: (bi, i, j)),
                   pl.BlockSpec((1, tm, ns), lambda bi, i, j: (bi, i, 0))),
        scratch_shapes=[pltpu.VMEM((tm, d), BF16)],
        compiler_params=_cparams(("parallel", "parallel", "arbitrary")),
        name="in_proj",
    )(x, mod_l.reshape(b, 1, 3 * d), norm_w.reshape(1, d), w_main, w_small)


def _outproj_kernel(a1_ref, a2_ref, w_ref, x_ref, mod_ref, nw_ref, o_ref):
    d = x_ref.shape[-1]
    half = a1_ref.shape[-1]
    y = _dot(a1_ref[0], w_ref[0:half, :]) + _dot(a2_ref[0], w_ref[half:2 * half, :])
    var = jnp.mean(y * y, axis=-1, keepdims=True)
    yn = y * lax.rsqrt(var + EPS) * nw_ref[...]
    gate = mod_ref[0, :, 2 * d:3 * d]
    o_ref[0] = x_ref[0] + gate * yn


def _out_proj(a1, a2, w_out, x, mod_l, norm_w):
    b, s, d = x.shape
    half = a1.shape[-1]
    tm = min(512, s)
    return pl.pallas_call(
        _outproj_kernel,
        out_shape=jax.ShapeDtypeStruct((b, s, d), F32),
        grid=(b, s // tm),
        in_specs=[
            pl.BlockSpec((1, tm, half), lambda bi, i: (bi, i, 0)),
            pl.BlockSpec((1, tm, half), lambda bi, i: (bi, i, 0)),
            pl.BlockSpec((2 * half, d), lambda bi, i: (0, 0)),
            pl.BlockSpec((1, tm, d), lambda bi, i: (bi, i, 0)),
            pl.BlockSpec((1, 1, 3 * d), lambda bi, i: (bi, 0, 0)),
            pl.BlockSpec((1, d), lambda bi, i: (0, 0)),
        ],
        out_specs=pl.BlockSpec((1, tm, d), lambda bi, i: (bi, i, 0)),
        compiler_params=_cparams(("parallel", "parallel")),
        name="out_proj",
    )(a1, a2, w_out, x, mod_l.reshape(b, 1, 3 * d), norm_w.reshape(1, d))


def _cumsum_chains(tri, gs):
    n = gs[0].shape[1]
    terms = []
    for g in gs:
        terms.extend(_split3(g))
    y = _dot(tri, jnp.concatenate(terms, axis=1))
    return [y[:, (3 * i) * n:(3 * i + 1) * n] + y[:, (3 * i + 1) * n:(3 * i + 2) * n]
            + y[:, (3 * i + 2) * n:(3 * i + 3) * n] for i in range(len(gs))]


def _lin_attn_chunk(q, k, v, g, b, st_scr, b_scr, g_scr, consts):
    c, dk = q.shape
    row, eye, tri, pairs = consts
    b_scr[...] = b
    g_scr[pl.ds(8, c), :] = g
    b_last = b_scr[pl.ds(c - 1, 1), :]

    def level_operand(s, expo):
        upper = (row & s) != 0
        return (jnp.where(upper, q, k) * jnp.exp(expo)).astype(BF16)

    items = []
    for s in [c >> i for i in range(1, c.bit_length() - 2)]:
        pieces = [jnp.broadcast_to(b_scr[pl.ds(p * 2 * s + s - 1, 1), :], (2 * s, dk))
                  for p in range(c // (2 * s))]
        d = b - jnp.concatenate(pieces, axis=0)
        x = level_operand(s, jnp.where((row & s) != 0, d, -d))
        items.append((x, x, pairs[s]))
    g_dn = g_scr[pl.ds(7, c), :]
    g_up = g_scr[pl.ds(9, c), :]
    r4 = row & 3
    x = level_operand(2, jnp.where(r4 == 2, g, jnp.where(r4 == 3, g + g_dn, jnp.where(r4 == 0, g_up, 0.0))))
    items.append((x, x, pairs[2]))
    x = level_operand(1, jnp.where((row & 1) != 0, g, 0.0))
    items.append((x, x, pairs[1]))
    items.append((q.astype(BF16), k.astype(BF16), eye))

    attn = jnp.zeros((c, c), F32)
    for i in range(0, len(items), 2):
        grp = items[i:i + 2]
        prod = _dot_nt(jnp.concatenate([it[0] for it in grp], axis=0),
                       jnp.concatenate([it[1] for it in grp], axis=0))
        for n, it in enumerate(grp):
            attn = jnp.where(it[2], prod[n * c:(n + 1) * c, n * c:(n + 1) * c], attn)

    v16 = v.astype(BF16)
    st = st_scr[...]
    o = _dot(jnp.concatenate([attn.astype(BF16), (q * jnp.exp(b)).astype(BF16)], axis=1),
             jnp.concatenate([v16, st.astype(BF16)], axis=0))
    k_dec = (k * jnp.exp(b_last - b)).astype(BF16)
    e_col = jnp.broadcast_to(jnp.exp(b_last), (dk, dk)).T
    e_col = jnp.concatenate([e_col] * (v.shape[1] // dk), axis=1)
    st_scr[...] = st * e_col + _dot_tn(k_dec, v16)
    return o


def _lin_attn_consts(c, dk):
    row = lax.broadcasted_iota(jnp.int32, (c, dk), 0)
    ri = lax.broadcasted_iota(jnp.int32, (c, c), 0)
    ci = lax.broadcasted_iota(jnp.int32, (c, c), 1)
    eye = ri == ci
    tri = jnp.where(ri >= ci, 1.0, 0.0).astype(BF16)
    pairs = {s: jnp.where((ri & -(2 * s)) == (ci & -(2 * s)), (ri & s) - (ci & s), 0) == s
             for s in [c >> i for i in range(1, c.bit_length())]}
    return row, eye, tri, pairs


def _lin_attn_finish(o, gz, nw):
    var = jnp.mean(o * o, axis=-1, keepdims=True)
    return (o * lax.rsqrt(var + EPS) * nw) * _silu(gz)


LIN_HEADS_PER_STEP = 4


def _gla_kernel(q_ref, k_ref, v_ref, gz_ref, glr_ref, wup_ref, bup_ref, nw_ref, o_ref,
                st_scr, b_scr, g_scr):
    c = LIN_CHUNK
    dk, dv = GLA_DK, GLA_DV
    nb, ts = q_ref.shape[0], q_ref.shape[1]
    hp = q_ref.shape[2] // dk

    @pl.when(pl.program_id(1) == 0)
    def _():
        st_scr[...] = jnp.zeros_like(st_scr)

    g_scr[...] = jnp.zeros_like(g_scr)
    consts = _lin_attn_consts(c, dk)
    nw = nw_ref[...]
    q_scale = dk ** -0.5

    def body(ci, carry):
        r0 = pl.multiple_of(ci * c, c)
        rows = pl.ds(r0, c)
        gs = []
        for bb in range(nb):
            z = _dot(glr_ref[bb, rows, :], wup_ref[...]) + bup_ref[...]
            g_all = _log_sigmoid(z) * (1.0 / GLA_TAU)
            gs.extend(g_all[:, hh * dk:(hh + 1) * dk] for hh in range(hp))
        bs = _cumsum_chains(consts[2], gs)
        for bb in range(nb):
            for hh in range(hp):
                ks = slice(hh * dk, (hh + 1) * dk)
                vs = slice(hh * dv, (hh + 1) * dv)
                ch = bb * hp + hh
                q = q_ref[bb, rows, ks] * q_scale
                o = _lin_attn_chunk(q, k_ref[bb, rows, ks], v_ref[bb, rows, vs], gs[ch], bs[ch],
                                    st_scr.at[ch], b_scr.at[ch], g_scr.at[ch], consts)
                out = _lin_attn_finish(o, gz_ref[bb, rows, vs], nw)
                o_ref[bb, rows, vs] = out.astype(o_ref.dtype)
        return carry

    lax.fori_loop(0, ts // c, body, 0)


def _hgrn_kernel(q_ref, f_ref, v_ref, gz_ref, lbl_ref, nw_ref, o_ref, st_scr, b_scr, g_scr,
                 *, layer):
    c = LIN_CHUNK
    dk, dv = HGRN_DK, HGRN_DV
    nb, ts = q_ref.shape[0], q_ref.shape[1]
    hp = q_ref.shape[2] // dk

    @pl.when(pl.program_id(1) == 0)
    def _():
        st_scr[...] = jnp.zeros_like(st_scr)

    g_scr[...] = jnp.zeros_like(g_scr)
    consts = _lin_attn_consts(c, dk)
    nw = nw_ref[...]

    logits = lbl_ref[...]
    depth = logits.shape[0]
    mx = logits[0:1, :]
    for r in range(1, depth):
        mx = jnp.maximum(mx, logits[r:r + 1, :])
    ex = [jnp.exp(logits[r:r + 1, :] - mx) for r in range(depth)]
    den = ex[0]
    for r in range(1, depth):
        den = den + ex[r]
    sm = [e / den for e in ex]
    lb_all = sm[0]
    for r in range(1, layer + 1):
        lb_all = lb_all + sm[r]
    lb_all = lb_all - sm[0]
    log_lb_all = jnp.log(lb_all)
    log_1mlb_all = jnp.log1p(-lb_all)

    def body(ci, carry):
        r0 = pl.multiple_of(ci * c, c)
        rows = pl.ds(r0, c)
        gs, kk = [], []
        for bb in range(nb):
            for hh in range(hp):
                ks = slice(hh * dk, (hh + 1) * dk)
                z = f_ref[bb, rows, ks]
                y = jnp.exp(-jnp.abs(z))
                r = 1.0 / (1.0 + y)
                log_sig = jnp.minimum(z, 0.0) - jnp.log(1.0 + y)
                gs.append(_logaddexp(log_lb_all[:, ks], log_1mlb_all[:, ks] + log_sig))
                kk.append((1.0 - lb_all[:, ks]) * jnp.where(z > 0.0, y * r, r))
        bs = _cumsum_chains(consts[2], gs)
        for bb in range(nb):
            for hh in range(hp):
                ks = slice(hh * dk, (hh + 1) * dk)
                vs = slice(hh * dv, (hh + 1) * dv)
                ch = bb * hp + hh
                o = _lin_attn_chunk(q_ref[bb, rows, ks], kk[ch], v_ref[bb, rows, vs], gs[ch], bs[ch],
                                    st_scr.at[ch], b_scr.at[ch], g_scr.at[ch], consts)
                out = _lin_attn_finish(o, gz_ref[bb, rows, vs], nw)
                o_ref[bb, rows, vs] = out.astype(o_ref.dtype)
        return carry

    lax.fori_loop(0, ts // c, body, 0)


def _lin_scratch(chains, dv, dk):
    return [pltpu.VMEM((chains, dk, dv), F32),
            pltpu.VMEM((chains, LIN_CHUNK, dk), F32),
            pltpu.VMEM((chains, LIN_CHUNK + 16, dk), F32)]


def _gla_branch(proj, small, w_up, b_up, norm_w):
    b, s, _ = proj.shape
    ts = min(512, s)
    dk, dv, h = GLA_DK, GLA_DV, GLA_HEADS
    hp = LIN_HEADS_PER_STEP
    kw, vw = hp * dk, hp * dv
    w_up_pad = jnp.pad(w_up, ((0, 128 - GLA_RANK), (0, 0)))
    return pl.pallas_call(
        _gla_kernel,
        out_shape=jax.ShapeDtypeStruct((b, s, h * dv), BF16),
        grid=(h // hp, s // ts),
        in_specs=[
            pl.BlockSpec((b, ts, kw), lambda hi, i: (0, i, hi)),
            pl.BlockSpec((b, ts, kw), lambda hi, i: (0, i, h // hp + hi)),
            pl.BlockSpec((b, ts, vw), lambda hi, i: (0, i, h // hp + hi)),
            pl.BlockSpec((b, ts, vw), lambda hi, i: (0, i, 2 * (h // hp) + hi)),
            pl.BlockSpec((b, ts, 128), lambda hi, i: (0, i, 0)),
            pl.BlockSpec((128, kw), lambda hi, i: (0, hi)),
            pl.BlockSpec((1, kw), lambda hi, i: (0, hi)),
            pl.BlockSpec((1, dv), lambda hi, i: (0, 0)),
        ],
        out_specs=pl.BlockSpec((b, ts, vw), lambda hi, i: (0, i, hi)),
        scratch_shapes=_lin_scratch(b * hp, dv, dk),
        compiler_params=_cparams(("parallel", "arbitrary")),
        name="gla",
    )(proj, proj, proj, proj, small, w_up_pad, b_up.reshape(1, -1), norm_w.reshape(1, dv))


def _hgrn_branch(proj, lb_logits, norm_w, layer):
    b, s, _ = proj.shape
    ts = min(1024, s)
    dk, dv, h = HGRN_DK, HGRN_DV, HGRN_HEADS
    hp = LIN_HEADS_PER_STEP
    kw, vw = hp * dk, hp * dv
    ng = h // hp
    depth = lb_logits.shape[0]
    return pl.pallas_call(
        functools.partial(_hgrn_kernel, layer=layer),
        out_shape=jax.ShapeDtypeStruct((b, s, h * dv), BF16),
        grid=(ng, s // ts),
        in_specs=[
            pl.BlockSpec((b, ts, kw), lambda hi, i: (0, i, hi)),
            pl.BlockSpec((b, ts, kw), lambda hi, i: (0, i, ng + hi)),
            pl.BlockSpec((b, ts, vw), lambda hi, i: (0, i, 2 * ng + hi)),
            pl.BlockSpec((b, ts, vw), lambda hi, i: (0, i, 3 * ng + hi)),
            pl.BlockSpec((depth, kw), lambda hi, i: (0, hi)),
            pl.BlockSpec((1, dv), lambda hi, i: (0, 0)),
        ],
        out_specs=pl.BlockSpec((b, ts, vw), lambda hi, i: (0, i, hi)),
        scratch_shapes=_lin_scratch(b * hp, dv, dk),
        compiler_params=_cparams(("parallel", "arbitrary")),
        name="hgrn2",
    )(proj, proj, proj, proj, lb_logits, norm_w.reshape(1, dv))


def _ssd_kernel(z_ref, x_ref, bc_ref, dt_ref, cw_ref, cb_ref, dtb_ref, alog_ref, dexp_ref, nw_ref,
                hexp_ref, o_ref, stage_scr, xbc_scr, st_scr, y_scr):
    c = x_ref.shape[1]
    nch = SSM_CONV_CH
    width = SSM_WIDTH
    n = SSM_STATE
    pairs = SSM_HEADS // 2

    @pl.when(pl.program_id(1) == 0)
    def _():
        st_scr[...] = jnp.zeros_like(st_scr)
        stage_scr[pl.ds(0, 8), :] = jnp.zeros((8, nch), F32)

    stage_scr[pl.ds(8, c), 0:width] = x_ref[0]
    stage_scr[pl.ds(8, c), width:nch] = bc_ref[0]
    staged = stage_scr[...]
    acc = cb_ref[...] + cw_ref[SSM_CONV - 1:SSM_CONV, :] * staged[8:8 + c, :]
    for kk in range(SSM_CONV - 1):
        shifted = pltpu.roll(staged, SSM_CONV - 1 - kk, axis=0)
        acc = acc + cw_ref[kk:kk + 1, :] * shifted[8:8 + c, :]
    stage_scr[pl.ds(0, 8), :] = stage_scr[pl.ds(c, 8), :]
    xbc_scr[...] = _silu(acc)

    dt = _softplus(dt_ref[0] + dtb_ref[...])
    a = -jnp.exp(alog_ref[...])
    da = dt * a
    ri = lax.broadcasted_iota(jnp.int32, (c, c), 0)
    ci = lax.broadcasted_iota(jnp.int32, (c, c), 1)
    causal = ri >= ci
    tri = jnp.where(causal, 1.0, 0.0).astype(BF16)
    a_cs = _sel_dot_left(tri, da)
    a_cs_t = a_cs.T
    hexp = hexp_ref[...]
    dt_x = _sel_dot_right(dt, hexp)
    acs_x = _sel_dot_right(a_cs, hexp)
    a_last_x = acs_x[c - 1:c, :]
    lane = lax.broadcasted_iota(jnp.int32, (c, 2 * SSM_HEADDIM), 1)
    first = lane < SSM_HEADDIM

    for grp in range(SSM_GROUPS):
        bm = xbc_scr[:, width + grp * n: width + (grp + 1) * n]
        cm = xbc_scr[:, width + (SSM_GROUPS + grp) * n: width + (SSM_GROUPS + grp + 1) * n]
        bm16 = bm.astype(BF16)
        cm16 = cm.astype(BF16)
        cb = _dot_nt(cm16, bm16)
        for pp in range(pairs // SSM_GROUPS):
            p = grp * (pairs // SSM_GROUPS) + pp
            lo = p * 2 * SSM_HEADDIM
            hi = lo + 2 * SSM_HEADDIM
            xs = xbc_scr[:, lo:hi]
            xdt = xs * dt_x[:, lo:hi]
            acs = acs_x[:, lo:hi]
            y = jnp.zeros((c, 2 * SSM_HEADDIM), F32)
            for hh in range(2):
                h = 2 * p + hh
                col = jnp.broadcast_to(a_cs[:, h:h + 1], (c, c))
                rw = jnp.broadcast_to(a_cs_t[h:h + 1, :], (c, c))
                lmat = jnp.exp(jnp.where(causal, col - rw, NEG))
                keep = first if hh == 0 else jnp.logical_not(first)
                xh = jnp.where(keep, xdt, 0.0).astype(BF16)
                y = y + _dot((cb * lmat).astype(BF16), xh)
            st = st_scr[p]
            y = y + _dot(cm16, st.astype(BF16)) * jnp.exp(acs)
            decay = jnp.exp(a_last_x[:, lo:hi] - acs)
            st_scr[p] = st * jnp.exp(a_last_x[:, lo:hi]) + _dot_tn(bm16, (xdt * decay).astype(BF16))
            y_scr[:, lo:hi] = y + dexp_ref[:, lo:hi] * xs

    yz = y_scr[...] * _silu(z_ref[0])
    var = jnp.mean(yz * yz, axis=-1, keepdims=True)
    o_ref[0] = (yz * lax.rsqrt(var + EPS) * nw_ref[...]).astype(o_ref.dtype)


def _ssd_branch(proj, small, conv_w, conv_b, dt_bias, a_log, d_skip, norm_w):
    b, s, _ = proj.shape
    c = math.gcd(SSM_CHUNK, s)
    nch = SSM_CONV_CH
    pad16 = lambda v: jnp.pad(v.reshape(1, -1), ((0, 0), (0, 128 - SSM_HEADS)))
    hexp = np.zeros((128, SSM_WIDTH), np.float32)
    for h in range(SSM_HEADS):
        hexp[h, h * SSM_HEADDIM:(h + 1) * SSM_HEADDIM] = 1.0
    d_exp = jnp.repeat(d_skip, SSM_HEADDIM).reshape(1, SSM_WIDTH)
    return pl.pallas_call(
        _ssd_kernel,
        out_shape=jax.ShapeDtypeStruct((b, s, SSM_WIDTH), BF16),
        grid=(b, s // c),
        in_specs=[
            pl.BlockSpec((1, c, SSM_WIDTH), lambda bi, i: (bi, i, 4)),
            pl.BlockSpec((1, c, SSM_WIDTH), lambda bi, i: (bi, i, 5)),
            pl.BlockSpec((1, c, 512), lambda bi, i: (bi, i, 12)),
            pl.BlockSpec((1, c, 128), lambda bi, i: (bi, i, 0)),
            pl.BlockSpec((SSM_CONV, nch), lambda bi, i: (0, 0)),
            pl.BlockSpec((1, nch), lambda bi, i: (0, 0)),
            pl.BlockSpec((1, 128), lambda bi, i: (0, 0)),
            pl.BlockSpec((1, 128), lambda bi, i: (0, 0)),
            pl.BlockSpec((1, SSM_WIDTH), lambda bi, i: (0, 0)),
            pl.BlockSpec((1, SSM_WIDTH), lambda bi, i: (0, 0)),
            pl.BlockSpec((128, SSM_WIDTH), lambda bi, i: (0, 0)),
        ],
        out_specs=pl.BlockSpec((1, c, SSM_WIDTH), lambda bi, i: (bi, i, 0)),
        scratch_shapes=[
            pltpu.VMEM((c + 8, nch), F32),
            pltpu.VMEM((c, nch), F32),
            pltpu.VMEM((SSM_HEADS // 2, SSM_STATE, 2 * SSM_HEADDIM), F32),
            pltpu.VMEM((c, SSM_WIDTH), F32),
        ],
        compiler_params=_cparams(("parallel", "arbitrary")),
        name="ssd",
    )(proj, proj, proj, small, conv_w, conv_b.reshape(1, nch), pad16(dt_bias), pad16(a_log),
      d_exp, norm_w.reshape(1, SSM_WIDTH), jnp.asarray(hexp, dtype=BF16))


def _rope(x, cos, sin_signed):
    return x * cos + pltpu.roll(x, NSA_DH // 2, axis=1) * sin_signed


def _nsa_prep_kernel(q_ref, kc_ref, ks_ref, vs_ref, kw_ref, vw_ref, cos_ref, sin_ref,
                     qo_ref, kco_ref, kso_ref, vso_ref, kwo_ref, vwo_ref):
    cos = cos_ref[...]
    sin = sin_ref[...]
    dh = NSA_DH
    scale = dh ** -0.5
    for h in range(NSA_HEADS):
        sl = slice(h * dh, (h + 1) * dh)
        qo_ref[0, :, sl] = (_rope(q_ref[0, :, sl], cos, sin) * scale).astype(qo_ref.dtype)
    for g in range(NSA_GROUPS):
        sl = slice(g * dh, (g + 1) * dh)
        kco_ref[0, :, sl] = _rope(kc_ref[0, :, sl], cos, sin)
        kso_ref[0, :, sl] = _rope(ks_ref[0, :, sl], cos, sin).astype(kso_ref.dtype)
        kwo_ref[0, :, sl] = _rope(kw_ref[0, :, sl], cos, sin).astype(kwo_ref.dtype)
    tk = NSA_TILE
    for g in range(NSA_GROUPS):
        for r in range(q_ref.shape[1] // tk):
            rows = slice(r * tk, (r + 1) * tk)
            sl = slice(g * dh, (g + 1) * dh)
            vso_ref[0, g, r, 0:dh, :] = vs_ref[0, rows, sl].T.astype(vso_ref.dtype)
            vwo_ref[0, g, r, 0:dh, :] = vw_ref[0, rows, sl].T.astype(vwo_ref.dtype)
            ones = jnp.ones((NSA_ONES, tk), vso_ref.dtype)
            vso_ref[0, g, r, dh:dh + NSA_ONES, :] = ones
            vwo_ref[0, g, r, dh:dh + NSA_ONES, :] = ones


def _nsa_prep(proj):
    b, s, _ = proj.shape
    ts = min(512, s)
    tk = NSA_TILE
    half = NSA_DH // 2
    inv_freq = ROPE_THETA ** (-np.arange(half, dtype=np.float64) / half)
    ang = np.arange(s, dtype=np.float64)[:, None] * inv_freq[None, :]
    cos_full = jnp.asarray(np.concatenate([np.cos(ang), np.cos(ang)], axis=-1), dtype=F32)
    sin_signed = jnp.asarray(np.concatenate([-np.sin(ang), np.sin(ang)], axis=-1), dtype=F32)
    kvw = NSA_GROUPS * NSA_DH
    col = lambda idx: (lambda bi, i: (bi, i, idx))
    kv_spec = lambda idx: pl.BlockSpec((1, ts, kvw), col(idx))
    kv_out = pl.BlockSpec((1, ts, kvw), lambda bi, i: (bi, i, 0))
    vrows = NSA_DH + NSA_ONES
    vt_shape = jax.ShapeDtypeStruct((b, NSA_GROUPS, s // tk, vrows, tk), BF16)
    vt_out = pl.BlockSpec((1, NSA_GROUPS, ts // tk, vrows, tk), lambda bi, i: (bi, 0, i, 0, 0))
    return pl.pallas_call(
        _nsa_prep_kernel,
        out_shape=(jax.ShapeDtypeStruct((b, s, NSA_HEADS * NSA_DH), BF16),
                   jax.ShapeDtypeStruct((b, s, kvw), F32),
                   jax.ShapeDtypeStruct((b, s, kvw), BF16),
                   vt_shape,
                   jax.ShapeDtypeStruct((b, s, kvw), BF16),
                   vt_shape),
        grid=(b, s // ts),
        in_specs=[
            pl.BlockSpec((1, ts, NSA_HEADS * NSA_DH), col(3)),
            kv_spec(16), kv_spec(18), kv_spec(19), kv_spec(20), kv_spec(21),
            pl.BlockSpec((ts, NSA_DH), lambda bi, i: (i, 0)),
            pl.BlockSpec((ts, NSA_DH), lambda bi, i: (i, 0)),
        ],
        out_specs=(pl.BlockSpec((1, ts, NSA_HEADS * NSA_DH), lambda bi, i: (bi, i, 0)),
                   kv_out, kv_out, vt_out, kv_out, vt_out),
        compiler_params=_cparams(("parallel", "parallel")),
        name="nsa_prep",
    )(proj, proj, proj, proj, proj, proj, cos_full, sin_signed)


def _compress_kernel(t_ref, pe_ref, w1_ref, w2_ref, o_ref, sh_scr, *, transposed):
    n = t_ref.shape[1] // CMP_STRIDE
    dh = NSA_DH
    half = CMP_LEN // 2
    acc_a = jnp.zeros((n, CMP_HIDDEN), F32)
    acc_b = jnp.zeros((n, CMP_HIDDEN), F32)
    for l in range(half):
        t = t_ref[0, pl.ds(l, n, stride=CMP_STRIDE), :]
        acc_a = acc_a + _dot((t + pe_ref[l:l + 1, :]).astype(BF16), w1_ref[l * dh:(l + 1) * dh, :])
        acc_b = acc_b + _dot((t + pe_ref[half + l:half + l + 1, :]).astype(BF16),
                             w1_ref[(half + l) * dh:(half + l + 1) * dh, :])
    sh_scr[pl.ds(0, n), :] = acc_b
    sh_scr[pl.ds(n, 8), :] = jnp.zeros((8, CMP_HIDDEN), F32)
    hid = acc_a + sh_scr[pl.ds(1, n), :]
    out = _dot(_silu(hid).astype(BF16), w2_ref[...])
    if transposed:
        tk = NSA_TILE
        for r in range(n // tk):
            o_ref[0, 0, r] = out[r * tk:(r + 1) * tk, :].T.astype(o_ref.dtype)
    else:
        o_ref[0, 0] = out.astype(o_ref.dtype)


def _nsa_compress(src, col0, pe, w1, w2, transposed):
    b, s, _ = src.shape
    n = s // CMP_STRIDE
    base = col0 // NSA_DH
    tk = NSA_TILE
    if transposed:
        out_shape = jax.ShapeDtypeStruct((b, NSA_GROUPS, n // tk, NSA_DH, tk), BF16)
        out_spec = pl.BlockSpec((1, 1, n // tk, NSA_DH, tk), lambda bi, g: (bi, g, 0, 0, 0))
    else:
        out_shape = jax.ShapeDtypeStruct((b, NSA_GROUPS, n, NSA_DH), BF16)
        out_spec = pl.BlockSpec((1, 1, n, NSA_DH), lambda bi, g: (bi, g, 0, 0))
    return pl.pallas_call(
        functools.partial(_compress_kernel, transposed=transposed),
        out_shape=out_shape,
        grid=(b, NSA_GROUPS),
        in_specs=[
            pl.BlockSpec((1, s, NSA_DH), lambda bi, g: (bi, 0, base + g)),
            pl.BlockSpec((CMP_LEN, NSA_DH), lambda bi, g: (0, 0)),
            pl.BlockSpec((CMP_LEN * NSA_DH, CMP_HIDDEN), lambda bi, g: (0, 0)),
            pl.BlockSpec((CMP_HIDDEN, NSA_DH), lambda bi, g: (0, 0)),
        ],
        out_specs=out_spec,
        scratch_shapes=[pltpu.VMEM((n + 8, CMP_HIDDEN), F32)],
        compiler_params=_cparams(("parallel", "parallel")),
        name="nsa_compress",
    )(src, pe, w1.astype(BF16), w2.astype(BF16))


def _weights_bf16(s_tiles, m):
    return [jnp.exp((s - m).astype(BF16)) for s in s_tiles]


def _flash_update(s_tiles, vt_tiles, m_ref, acc_ref):
    m_old = m_ref[...]
    m_new = m_old
    for s in s_tiles:
        m_new = jnp.maximum(m_new, jnp.max(s, axis=0, keepdims=True))
    alpha = jnp.exp(m_old - m_new)
    acc_ref[...] = alpha * acc_ref[...] + _pv(vt_tiles, _weights_bf16(s_tiles, m_new))
    m_ref[...] = m_new


def _pv(vt_tiles, p_tiles):
    vt = jnp.concatenate(vt_tiles, axis=1)
    p = jnp.concatenate([p.astype(BF16) for p in p_tiles], axis=0)
    return _dot(vt, p)


def _softmax_tiles(s_tiles, vt_tiles):
    m = jnp.max(s_tiles[0], axis=0, keepdims=True)
    for s in s_tiles[1:]:
        m = jnp.maximum(m, jnp.max(s, axis=0, keepdims=True))
    p_tiles = [jnp.exp(s - m) for s in s_tiles]
    l = jnp.sum(p_tiles[0], axis=0, keepdims=True)
    for p in p_tiles[1:]:
        l = l + jnp.sum(p, axis=0, keepdims=True)
    return p_tiles, l, _pv(vt_tiles, p_tiles)


def _softmax_tiles_aug(s_tiles, vt_tiles):
    m = jnp.max(s_tiles[0], axis=0, keepdims=True)
    for s in s_tiles[1:]:
        m = jnp.maximum(m, jnp.max(s, axis=0, keepdims=True))
    acc = _pv(vt_tiles, _weights_bf16(s_tiles, m))
    return acc[0:NSA_DH] * (1.0 / acc[NSA_DH:NSA_DH + 1])


def _nsa_kernel(q_ref, kc_ref, vct_ref, ovt_ref, ks_ref, vst_ref, kw_ref, vwt_ref, gate_ref, nz_ref,
                o_ref, m_scr, accs_scr, acc_scr, cap_scr, capd_scr, capl_scr, sa_scr, sb_scr):
    tq = q_ref.shape[1]
    tk = NSA_TILE
    dh = NSA_DH
    hpg = NSA_HPG
    lanes = hpg * tq
    ns = ovt_ref.shape[0]
    qtiles = tq // tk
    qi = pl.program_id(2)
    t0 = qi * tq
    kt0 = qi * qtiles

    qt = jnp.concatenate([q_ref[0, :, j * dh:(j + 1) * dh].astype(F32).T for j in range(hpg)],
                         axis=1).astype(BF16)
    rowi = lax.broadcasted_iota(jnp.int32, (tk, lanes), 0)
    qpos = lax.broadcasted_iota(jnp.int32, (tk, lanes), 1) & (tq - 1)
    for d in range(qtiles):
        capd_scr[d] = jnp.where(rowi + d * tk <= qpos, FORCE, NEG)
        capl_scr[d] = jnp.where(rowi + d * tk > qpos, FORCE, NEG)

    n_ct = kc_ref.shape[2] // tk
    s_all = _dot(kc_ref[0, 0], qt)
    s_tiles = []
    for c in range(n_ct):
        first_end = c * (tk * CMP_STRIDE) + (CMP_LEN - 1) - t0
        cap = jnp.where(rowi * CMP_STRIDE + first_end <= qpos, FORCE, NEG)
        s_tiles.append(jnp.minimum(s_all[c * tk:(c + 1) * tk, :], cap))
    p_tiles, l_c, acc_c = _softmax_tiles(s_tiles, [vct_ref[0, 0, c] for c in range(n_ct)])
    p_hi = [p.astype(BF16) for p in p_tiles]
    p_lo = [(p - h.astype(F32)).astype(BF16) for p, h in zip(p_tiles, p_hi)]
    imp_2 = _dot(ovt_ref[...], jnp.concatenate([jnp.concatenate(p_hi, axis=0),
                                                 jnp.concatenate(p_lo, axis=0)], axis=1))
    imp_un = imp_2[:, 0:lanes] + imp_2[:, lanes:2 * lanes]
    tl = t0 + (lax.broadcasted_iota(jnp.int32, (1, lanes), 1) & (tq - 1))
    inv_l = jnp.where(tl >= CMP_LEN - 1, 1.0 / l_c, 0.0)
    acc_scr[0] = acc_c * inv_l
    impn = imp_un * inv_l
    imp = impn[:, 0:tq]
    for j in range(1, hpg):
        imp = imp + impn[:, j * tq:(j + 1) * tq]

    blk = lax.broadcasted_iota(jnp.int32, (ns, tq), 0)
    blk_t = jnp.right_shift(t0 + lax.broadcasted_iota(jnp.int32, (ns, tq), 1), SLC_SHIFT)
    score = jnp.where(blk == 0, FORCE,
                      jnp.where(blk == blk_t, FORCE,
                                jnp.where(blk == blk_t - 1, FORCE,
                                          jnp.where(blk <= blk_t, imp, NEG))))
    blk_f = blk.astype(F32)
    cap_sel = jnp.full((ns, tq), NEG, F32)
    for _ in range(min(SLC_TOPK, ns)):
        mx = jnp.max(score, axis=0, keepdims=True)
        first = jnp.min(jnp.where(score == mx, blk_f, float(ns)), axis=0, keepdims=True)
        pick = blk_f == first
        cap_sel = jnp.where(pick, FORCE, cap_sel)
        score = jnp.where(pick, -jnp.inf, score)
    for j in range(hpg):
        cap_scr[:, j * tq:(j + 1) * tq] = cap_sel

    bpt = tk // SLC_BLOCK

    def slc_scores(step):
        k0 = pl.multiple_of(step * (nb * tk), nb * tk)
        s = _dot(ks_ref[0, pl.ds(k0, nb * tk), :], qt)
        out = []
        for r in range(nb):
            caps = [jnp.broadcast_to(cap_scr[pl.ds((step * nb + r) * bpt + i, 1), :], (SLC_BLOCK, lanes))
                    for i in range(bpt)]
            out.append(jnp.minimum(s[r * tk:(r + 1) * tk, :], jnp.concatenate(caps, axis=0)))
        return out

    wt = WINDOW // tk
    s_tiles, vt_tiles = [], []
    for d in range(-wt, qtiles):
        kt = kt0 + d
        ktc = jnp.maximum(kt, 0)
        s = _dot(kw_ref[0, pl.ds(pl.multiple_of(ktc * tk, tk), tk), :], qt)
        if d + wt < qtiles:
            s = jnp.minimum(s, capl_scr[d + wt])
        if d >= 0:
            s = jnp.minimum(s, capd_scr[d])
        else:
            s = jnp.minimum(s, jnp.where(kt >= 0, FORCE, NEG))
        s_tiles.append(s)
        vt_tiles.append(vwt_ref[0, 0, ktc])
    acc_scr[1] = _softmax_tiles_aug(s_tiles, vt_tiles)

    nb = NSA_TILES_PER_STEP
    m_scr[...] = jnp.full((1, lanes), NEG, F32)
    accs_scr[...] = jnp.zeros_like(accs_scr)
    state = (m_scr, accs_scr)
    n_full = kt0 // nb

    def score_step(step, dst):
        for r, s in enumerate(slc_scores(step)):
            dst[r * tk:(r + 1) * tk, :] = s

    def value_step(step, src):
        _flash_update([src[r * tk:(r + 1) * tk, :] for r in range(nb)],
                      [vst_ref[0, 0, step * nb + r] for r in range(nb)], *state)

    @pl.when(n_full > 0)
    def _():
        score_step(0, sa_scr)

    def slc_pair(j, carry):
        score_step(2 * j + 1, sb_scr)
        value_step(2 * j, sa_scr)
        score_step(jnp.minimum(2 * j + 2, n_full - 1), sa_scr)
        value_step(2 * j + 1, sb_scr)
        return carry

    lax.fori_loop(0, n_full // 2, slc_pair, 0)

    @pl.when(n_full % 2 == 1)
    def _():
        value_step(n_full - 1, sa_scr)

    s_tiles, vt_tiles = [], []
    for r, s in enumerate(slc_scores(n_full)):
        kt = n_full * nb + r
        cap = jnp.where(kt < kt0 + qtiles, FORCE, NEG)
        for d in range(qtiles):
            cap = jnp.minimum(cap, jnp.maximum(capd_scr[d], jnp.where(kt == kt0 + d, NEG, FORCE)))
        s_tiles.append(jnp.minimum(s, cap))
        vt_tiles.append(vst_ref[0, 0, kt])
    _flash_update(s_tiles, vt_tiles, *state)

    o_c = acc_scr[0]
    o_s = accs_scr[0:dh, :] * (1.0 / accs_scr[dh:dh + 1, :])
    o_w = acc_scr[1]
    sgt = _sigmoid(gate_ref[0]).T
    for j in range(hpg):
        ls = slice(j * tq, (j + 1) * tq)
        mix_t = (sgt[3 * j:3 * j + 1, :] * o_c[:, ls] + sgt[3 * j + 1:3 * j + 2, :] * o_s[:, ls]
                 + sgt[3 * j + 2:3 * j + 3, :] * o_w[:, ls])
        sl = slice(j * dh, (j + 1) * dh)
        o_ref[0, :, sl] = (mix_t.T * _silu(nz_ref[0, :, sl])).astype(o_ref.dtype)


def _nsa_attention(qr, kc, vct, ksr, vst, kwr, vwt, small, proj):
    b, s, _ = qr.shape
    tk = NSA_TILE
    tq = min(NSA_QUERY_TILE, s)
    qtiles = tq // tk
    assert s % tq == 0 and tq % tk == 0 and WINDOW % tk == 0 and tk % SLC_BLOCK == 0
    assert NSA_TILES_PER_STEP % qtiles == 0 and WINDOW // tk >= qtiles
    assert (s // tk) % NSA_TILES_PER_STEP == 0
    ncp = kc.shape[2]
    assert ncp % tk == 0
    ns = s // SLC_BLOCK
    gw = NSA_HPG * NSA_DH
    lanes = NSA_HPG * tq
    cs = np.arange(ncp)[None, :] * CMP_STRIDE
    ss = np.arange(ns)[:, None] * SLC_BLOCK
    ovt = ((np.minimum(cs + CMP_LEN, ss + SLC_BLOCK) - np.maximum(cs, ss)) > 0).astype(np.float32)
    ovt[:, ncp - 1] = 0.0
    kv_spec = pl.BlockSpec((1, s, NSA_DH), lambda bi, g, i: (bi, 0, g))
    vrows = NSA_DH + NSA_ONES
    vt_spec = pl.BlockSpec((1, 1, s // tk, vrows, tk), lambda bi, g, i: (bi, g, 0, 0, 0))
    return pl.pallas_call(
        _nsa_kernel,
        out_shape=jax.ShapeDtypeStruct((b, s, NSA_HEADS * NSA_DH), BF16),
        grid=(b, NSA_GROUPS, s // tq),
        in_specs=[
            pl.BlockSpec((1, tq, gw), lambda bi, g, i: (bi, i, g)),
            pl.BlockSpec((1, 1, ncp, NSA_DH), lambda bi, g, i: (bi, g, 0, 0)),
            pl.BlockSpec((1, 1, ncp // tk, NSA_DH, tk), lambda bi, g, i: (bi, g, 0, 0, 0)),
            pl.BlockSpec((ns, ncp), lambda bi, g, i: (0, 0)),
            kv_spec, vt_spec, kv_spec, vt_spec,
            pl.BlockSpec((1, tq, 128), lambda bi, g, i: (bi, i, 1 + g)),
            pl.BlockSpec((1, tq, gw), lambda bi, g, i: (bi, i, 11 + g)),
        ],
        out_specs=pl.BlockSpec((1, tq, gw), lambda bi, g, i: (bi, i, g)),
        scratch_shapes=[
            pltpu.VMEM((1, lanes), F32),
            pltpu.VMEM((vrows, lanes), F32),
            pltpu.VMEM((2, NSA_DH, lanes), F32),
            pltpu.VMEM((ns, lanes), F32),
            pltpu.VMEM((qtiles, tk, lanes), F32),
            pltpu.VMEM((qtiles, tk, lanes), F32),
            pltpu.VMEM((NSA_TILES_PER_STEP * tk, lanes), F32),
            pltpu.VMEM((NSA_TILES_PER_STEP * tk, lanes), F32),
        ],
        compiler_params=_cparams(("parallel", "parallel", "arbitrary")),
        name="nsa_attention",
    )(qr, kc, vct, jnp.asarray(ovt, dtype=BF16), ksr, vst, kwr, vwt, small, proj)


def _nsa_branch(proj, small, pe_k, w1_k, w2_k, pe_v, w1_v, w2_v):
    qr, kcr, ksr, vst, kwr, vwt = _nsa_prep(proj)
    kc = _nsa_compress(kcr, 0, pe_k, w1_k, w2_k, transposed=False)
    vct = _nsa_compress(proj, 4352, pe_v, w1_v, w2_v, transposed=True)
    return _nsa_attention(qr, kc, vct, ksr, vst, kwr, vwt, small, proj)


def _pad_rows(w, rows):
    return jnp.pad(w, ((0, rows - w.shape[0]), (0, 0)))


def _cast_rows_kernel(w_ref, o_ref):
    o_ref[...] = w_ref[...].astype(o_ref.dtype)


def _even_weights(w_in):
    wt = w_in.T
    d = wt.shape[1]
    per = NSA_HPG * 3
    tile = 512
    t1, t2 = 2048 // tile, 5632 // tile

    def src_row(j):
        shift8 = jnp.where(j >= t2, (GLA_RANK + NSA_GROUPS * per) // 8, jnp.where(j >= t1, GLA_RANK // 8, 0))
        return 8 * (j * (tile // 8) + shift8)

    main = pl.pallas_call(
        _cast_rows_kernel,
        out_shape=jax.ShapeDtypeStruct((MAIN_COLS, d), BF16),
        grid=(MAIN_COLS // tile,),
        in_specs=[pl.BlockSpec((pl.Element(tile), pl.Element(d)), lambda j: (src_row(j), 0))],
        out_specs=pl.BlockSpec((tile, d), lambda j: (j, 0)),
        compiler_params=_cparams(("parallel",)),
        name="weight_rows",
    )(wt)

    def small_kernel(decay_ref, gates_ref, o_ref):
        row = lax.broadcasted_iota(jnp.int32, (128, d), 0)
        zeros = jnp.zeros((128, d), F32)
        o_ref[0:128, :] = jnp.where(row < GLA_RANK, decay_ref[...], 0.0).astype(o_ref.dtype)
        gates = gates_ref[...]
        for g in range(NSA_GROUPS):
            rows_g = jnp.concatenate([gates[g * per:(g + 1) * per, :], zeros[0:128 - per, :]], axis=0)
            o_ref[(1 + g) * 128:(2 + g) * 128, :] = rows_g.astype(o_ref.dtype)

    window = lambda r0: pl.BlockSpec((pl.Element(128), pl.Element(d)), lambda i: (r0, 0))
    small = pl.pallas_call(
        small_kernel,
        out_shape=jax.ShapeDtypeStruct(((1 + NSA_GROUPS) * 128, d), BF16),
        grid=(1,),
        in_specs=[window(2048), window(5648)],
        out_specs=pl.BlockSpec(((1 + NSA_GROUPS) * 128, d), lambda i: (0, 0)),
        compiler_params=_cparams(("arbitrary",)),
        name="weight_rows_small",
    )(wt, wt)
    return main, small


def _odd_weights(w_in):
    wt = w_in.T.astype(BF16)
    return wt, _pad_rows(wt[MAIN_COLS:MAIN_COLS + SSM_HEADS], 128)


def kernel(x, c, ada_w, ada_b, pre_norm_w, post_norm_w, even_w_in, even_w_out, gla_w_up, gla_b_up,
           gla_norm_w, nsa_pe_k, nsa_w1_k, nsa_w2_k, nsa_pe_v, nsa_w1_v, nsa_w2_v, odd_w_in,
           odd_w_out, hgrn_lb_logits, hgrn_norm_w, ssm_conv_w, ssm_conv_b, ssm_dt_bias, ssm_a_log,
           ssm_d, ssm_norm_w):
    depth = ada_w.shape[0]
    mod = _adaln_mod(c, ada_w, ada_b)
    for l in range(depth):
        if l % 2 == 0:
            e = l // 2
            w_main, w_small = _even_weights(even_w_in[e])
            proj, small = _in_proj(x, mod[l], pre_norm_w[l], w_main, w_small)
            o_a = _gla_branch(proj, small, gla_w_up[e], gla_b_up[e], gla_norm_w[e])
            o_b = _nsa_branch(proj, small, nsa_pe_k[e], nsa_w1_k[e], nsa_w2_k[e],
                              nsa_pe_v[e], nsa_w1_v[e], nsa_w2_v[e])
            w_out = even_w_out[e]
        else:
            o = l // 2
            w_main, w_small = _odd_weights(odd_w_in[o])
            proj, small = _in_proj(x, mod[l], pre_norm_w[l], w_main, w_small)
            o_a = _hgrn_branch(proj, hgrn_lb_logits, hgrn_norm_w[o], l)
            o_b = _ssd_branch(proj, small, ssm_conv_w[o], ssm_conv_b[o], ssm_dt_bias[o],
                              ssm_a_log[o], ssm_d[o], ssm_norm_w[o])
            w_out = odd_w_out[o]
        x = _out_proj(o_a, o_b, w_out.astype(BF16), x, mod[l], post_norm_w[l])
    return x
```

```python
import functools
import math

import jax
import jax.numpy as jnp
import numpy as np
from jax import lax
from jax.experimental import pallas as pl
from jax.experimental.pallas import tpu as pltpu

F32 = jnp.float32
BF16 = jnp.bfloat16

D_MODEL = 2048
EPS = 1e-6
NEG = -1e30
FORCE = 1e30
ROPE_THETA = 10000.0

GLA_HEADS = 4
GLA_DK = 128
GLA_DV = 256
GLA_RANK = 16
GLA_TAU = 16.0

NSA_DH = 128
NSA_HEADS = 8
NSA_GROUPS = 2
NSA_HPG = 4
CMP_LEN = 32
CMP_STRIDE = 16
CMP_HIDDEN = 256
SLC_BLOCK = 64
SLC_SHIFT = 6
SLC_TOPK = 16
WINDOW = 512
NSA_TILE = 128
NSA_QUERY_TILE = 256
NSA_ONES = 16
NSA_TILES_PER_STEP = 4

HGRN_HEADS = 8
HGRN_DK = 128
HGRN_DV = 128

SSM_HEADDIM = 64
SSM_HEADS = 16
SSM_GROUPS = 2
SSM_STATE = 128
SSM_CONV = 4
SSM_CHUNK = 256
SSM_WIDTH = 1024
SSM_CONV_CH = SSM_WIDTH + 2 * SSM_GROUPS * SSM_STATE

LIN_CHUNK = 128
MAIN_COLS = 6656
VMEM_LIMIT = 56 * 1024 * 1024


def _cparams(sem):
    return pltpu.CompilerParams(dimension_semantics=sem, vmem_limit_bytes=VMEM_LIMIT)


def _dot(a, b):
    return jnp.dot(a, b, preferred_element_type=F32)


def _dot_nt(a, b):
    return lax.dot_general(a, b, (((1,), (1,)), ((), ())), preferred_element_type=F32)


def _dot_tn(a, b):
    return lax.dot_general(a, b, (((0,), (0,)), ((), ())), preferred_element_type=F32)


def _split3(x):
    hi = x.astype(BF16)
    r = x - hi.astype(F32)
    mid = r.astype(BF16)
    lo = (r - mid.astype(F32)).astype(BF16)
    return hi, mid, lo


def _sel_dot_left(sel, x):
    n = x.shape[1]
    y = _dot(sel.astype(BF16), jnp.concatenate(_split3(x), axis=1))
    return y[:, 0:n] + y[:, n:2 * n] + y[:, 2 * n:3 * n]


def _sel_dot_right(x, sel):
    m = x.shape[0]
    y = _dot(jnp.concatenate(_split3(x), axis=0), sel.astype(BF16))
    return y[0:m] + y[m:2 * m] + y[2 * m:3 * m]


def _sigmoid(x):
    return 1.0 / (1.0 + jnp.exp(-x))


def _silu(x):
    return x * _sigmoid(x)


def _log1p_exp_neg_abs(x):
    return jnp.log(1.0 + jnp.exp(-jnp.abs(x)))


def _log_sigmoid(x):
    return jnp.minimum(x, 0.0) - _log1p_exp_neg_abs(x)


def _softplus(x):
    return jnp.maximum(x, 0.0) + _log1p_exp_neg_abs(x)


def _logaddexp(a, b):
    return jnp.maximum(a, b) + _log1p_exp_neg_abs(a - b)


def _mod_kernel(c_ref, w_ref, b_ref, o_ref):
    ca = _silu(c_ref[...])
    o_ref[0] = _dot(ca, w_ref[0]) + b_ref[0]


def _adaln_mod(c, ada_w, ada_b):
    depth, d, n3 = ada_w.shape
    b = c.shape[0]
    rows = 8
    c_pad = jnp.pad(c, ((0, rows - b), (0, 0)))
    tn = 768
    out = pl.pallas_call(
        _mod_kernel,
        out_shape=jax.ShapeDtypeStruct((depth, rows, n3), F32),
        grid=(depth, n3 // tn),
        in_specs=[
            pl.BlockSpec((rows, d), lambda l, j: (0, 0)),
            pl.BlockSpec((1, d, tn), lambda l, j: (l, 0, j)),
            pl.BlockSpec((1, 1, tn), lambda l, j: (l, 0, j)),
        ],
        out_specs=pl.BlockSpec((1, rows, tn), lambda l, j: (l, 0, j)),
        compiler_params=_cparams(("parallel", "parallel")),
        name="adaln_mod",
    )(c_pad, ada_w, ada_b.reshape(depth, 1, n3))
    return out[:, :b]


def _inproj_kernel(x_ref, mod_ref, nw_ref, w_ref, ws_ref, o_ref, os_ref, h_scr):
    d = x_ref.shape[-1]

    @pl.when(pl.program_id(2) == 0)
    def _():
        x = x_ref[0]
        var = jnp.mean(x * x, axis=-1, keepdims=True)
        y = x * lax.rsqrt(var + EPS) * nw_ref[...]
        shift = mod_ref[0, :, 0:d]
        scale = mod_ref[0, :, d:2 * d]
        hb = (y * (1.0 + scale) + shift).astype(BF16)
        h_scr[...] = hb
        os_ref[0] = _dot_nt(hb, ws_ref[...])

    o_ref[0] = _dot_nt(h_scr[...], w_ref[...])


def _in_proj(x, mod_l, norm_w, w_main, w_small):
    b, s, d = x.shape
    n = MAIN_COLS
    ns = w_small.shape[0]
    tm = min(1024, s)
    tn = 512
    return pl.pallas_call(
        _inproj_kernel,
        out_shape=(jax.ShapeDtypeStruct((b, s, n), F32),
                   jax.ShapeDtypeStruct((b, s, ns), F32)),
        grid=(b, s // tm, n // tn),
        in_specs=[
            pl.BlockSpec((1, tm, d), lambda bi, i, j: (bi, i, 0)),
            pl.BlockSpec((1, 1, 3 * d), lambda bi, i, j: (bi, 0, 0)),
            pl.BlockSpec((1, d), lambda bi, i, j: (0, 0)),
            pl.BlockSpec((tn, d), lambda bi, i, j: (j, 0)),
            pl.BlockSpec((ns, d), lambda bi, i, j: (0, 0)),
        ],
        out_specs=(pl.BlockSpec((1, tm, tn), lambda bi, i, j: (bi, i, j)),
                   pl.BlockSpec((1, tm, ns), lambda bi, i, j: (bi, i, 0))),
        scratch_shapes=[pltpu.VMEM((tm, d), BF16)],
        compiler_params=_cparams(("parallel", "parallel", "arbitrary")),
        name="in_proj",
    )(x, mod_l.reshape(b, 1, 3 * d), norm_w.reshape(1, d), w_main, w_small)


def _outproj_kernel(a1_ref, a2_ref, w_ref, x_ref, mod_ref, nw_ref, o_ref):
    d = x_ref.shape[-1]
    half = a1_ref.shape[-1]
    y = _dot(a1_ref[0], w_ref[0:half, :]) + _dot(a2_ref[0], w_ref[half:2 * half, :])
    var = jnp.mean(y * y, axis=-1, keepdims=True)
    yn = y * lax.rsqrt(var + EPS) * nw_ref[...]
    gate = mod_ref[0, :, 2 * d:3 * d]
    o_ref[0] = x_ref[0] + gate * yn


def _out_proj(a1, a2, w_out, x, mod_l, norm_w):
    b, s, d = x.shape
    half = a1.shape[-1]
    tm = min(512, s)
    return pl.pallas_call(
        _outproj_kernel,
        out_shape=jax.ShapeDtypeStruct((b, s, d), F32),
        grid=(b, s // tm),
        in_specs=[
            pl.BlockSpec((1, tm, half), lambda bi, i: (bi, i, 0)),
            pl.BlockSpec((1, tm, half), lambda bi, i: (bi, i, 0)),
            pl.BlockSpec((2 * half, d), lambda bi, i: (0, 0)),
            pl.BlockSpec((1, tm, d), lambda bi, i: (bi, i, 0)),
            pl.BlockSpec((1, 1, 3 * d), lambda bi, i: (bi, 0, 0)),
            pl.BlockSpec((1, d), lambda bi, i: (0, 0)),
        ],
        out_specs=pl.BlockSpec((1, tm, d), lambda bi, i: (bi, i, 0)),
        compiler_params=_cparams(("parallel", "parallel")),
        name="out_proj",
    )(a1, a2, w_out, x, mod_l.reshape(b, 1, 3 * d), norm_w.reshape(1, d))


def _cumsum_chains(tri, gs):
    n = gs[0].shape[1]
    terms = []
    for g in gs:
        terms.extend(_split3(g))
    y = _dot(tri, jnp.concatenate(terms, axis=1))
    return [y[:, (3 * i) * n:(3 * i + 1) * n] + y[:, (3 * i + 1) * n:(3 * i + 2) * n]
            + y[:, (3 * i + 2) * n:(3 * i + 3) * n] for i in range(len(gs))]


def _lin_attn_chunk(q, k, v, g, b, st_scr, b_scr, g_scr, consts):
    c, dk = q.shape
    row, eye, tri, pairs = consts
    b_scr[...] = b
    g_scr[pl.ds(8, c), :] = g
    b_last = b_scr[pl.ds(c - 1, 1), :]

    def level_operand(s, expo):
        upper = (row & s) != 0
        return (jnp.where(upper, q, k) * jnp.exp(expo)).astype(BF16)

    items = []
    for s in [c >> i for i in range(1, c.bit_length() - 2)]:
        pieces = [jnp.broadcast_to(b_scr[pl.ds(p * 2 * s + s - 1, 1), :], (2 * s, dk))
                  for p in range(c // (2 * s))]
        d = b - jnp.concatenate(pieces, axis=0)
        x = level_operand(s, jnp.where((row & s) != 0, d, -d))
        items.append((x, x, pairs[s]))
    g_dn = g_scr[pl.ds(7, c), :]
    g_up = g_scr[pl.ds(9, c), :]
    r4 = row & 3
    x = level_operand(2, jnp.where(r4 == 2, g, jnp.where(r4 == 3, g + g_dn, jnp.where(r4 == 0, g_up, 0.0))))
    items.append((x, x, pairs[2]))
    x = level_operand(1, jnp.where((row & 1) != 0, g, 0.0))
    items.append((x, x, pairs[1]))
    items.append((q.astype(BF16), k.astype(BF16), eye))

    attn = jnp.zeros((c, c), F32)
    for i in range(0, len(items), 2):
        grp = items[i:i + 2]
        prod = _dot_nt(jnp.concatenate([it[0] for it in grp], axis=0),
                       jnp.concatenate([it[1] for it in grp], axis=0))
        for n, it in enumerate(grp):
            attn = jnp.where(it[2], prod[n * c:(n + 1) * c, n * c:(n + 1) * c], attn)

    v16 = v.astype(BF16)
    st = st_scr[...]
    o = _dot(jnp.concatenate([attn.astype(BF16), (q * jnp.exp(b)).astype(BF16)], axis=1),
             jnp.concatenate([v16, st.astype(BF16)], axis=0))
    k_dec = (k * jnp.exp(b_last - b)).astype(BF16)
    e_col = jnp.broadcast_to(jnp.exp(b_last), (dk, dk)).T
    e_col = jnp.concatenate([e_col] * (v.shape[1] // dk), axis=1)
    st_scr[...] = st * e_col + _dot_tn(k_dec, v16)
    return o


def _lin_attn_consts(c, dk):
    row = lax.broadcasted_iota(jnp.int32, (c, dk), 0)
    ri = lax.broadcasted_iota(jnp.int32, (c, c), 0)
    ci = lax.broadcasted_iota(jnp.int32, (c, c), 1)
    eye = ri == ci
    tri = jnp.where(ri >= ci, 1.0, 0.0).astype(BF16)
    pairs = {s: jnp.where((ri & -(2 * s)) == (ci & -(2 * s)), (ri & s) - (ci & s), 0) == s
             for s in [c >> i for i in range(1, c.bit_length())]}
    return row, eye, tri, pairs


def _lin_attn_finish(o, gz, nw):
    var = jnp.mean(o * o, axis=-1, keepdims=True)
    return (o * lax.rsqrt(var + EPS) * nw) * _silu(gz)


LIN_HEADS_PER_STEP = 4


def _gla_kernel(q_ref, k_ref, v_ref, gz_ref, glr_ref, wup_ref, bup_ref, nw_ref, o_ref,
                st_scr, b_scr, g_scr):
    c = LIN_CHUNK
    dk, dv = GLA_DK, GLA_DV
    nb, ts = q_ref.shape[0], q_ref.shape[1]
    hp = q_ref.shape[2] // dk

    @pl.when(pl.program_id(1) == 0)
    def _():
        st_scr[...] = jnp.zeros_like(st_scr)

    g_scr[...] = jnp.zeros_like(g_scr)
    consts = _lin_attn_consts(c, dk)
    nw = nw_ref[...]
    q_scale = dk ** -0.5

    def body(ci, carry):
        r0 = pl.multiple_of(ci * c, c)
        rows = pl.ds(r0, c)
        gs = []
        for bb in range(nb):
            z = _dot(glr_ref[bb, rows, :], wup_ref[...]) + bup_ref[...]
            g_all = _log_sigmoid(z) * (1.0 / GLA_TAU)
            gs.extend(g_all[:, hh * dk:(hh + 1) * dk] for hh in range(hp))
        bs = _cumsum_chains(consts[2], gs)
        for bb in range(nb):
            for hh in range(hp):
                ks = slice(hh * dk, (hh + 1) * dk)
                vs = slice(hh * dv, (hh + 1) * dv)
                ch = bb * hp + hh
                q = q_ref[bb, rows, ks] * q_scale
                o = _lin_attn_chunk(q, k_ref[bb, rows, ks], v_ref[bb, rows, vs], gs[ch], bs[ch],
                                    st_scr.at[ch], b_scr.at[ch], g_scr.at[ch], consts)
                out = _lin_attn_finish(o, gz_ref[bb, rows, vs], nw)
                o_ref[bb, rows, vs] = out.astype(o_ref.dtype)
        return carry

    lax.fori_loop(0, ts // c, body, 0)


def _hgrn_kernel(q_ref, f_ref, v_ref, gz_ref, lbl_ref, nw_ref, o_ref, st_scr, b_scr, g_scr,
                 *, layer):
    c = LIN_CHUNK
    dk, dv = HGRN_DK, HGRN_DV
    nb, ts = q_ref.shape[0], q_ref.shape[1]
    hp = q_ref.shape[2] // dk

    @pl.when(pl.program_id(1) == 0)
    def _():
        st_scr[...] = jnp.zeros_like(st_scr)

    g_scr[...] = jnp.zeros_like(g_scr)
    consts = _lin_attn_consts(c, dk)
    nw = nw_ref[...]

    logits = lbl_ref[...]
    depth = logits.shape[0]
    mx = logits[0:1, :]
    for r in range(1, depth):
        mx = jnp.maximum(mx, logits[r:r + 1, :])
    ex = [jnp.exp(logits[r:r + 1, :] - mx) for r in range(depth)]
    den = ex[0]
    for r in range(1, depth):
        den = den + ex[r]
    sm = [e / den for e in ex]
    lb_all = sm[0]
    for r in range(1, layer + 1):
        lb_all = lb_all + sm[r]
    lb_all = lb_all - sm[0]
    log_lb_all = jnp.log(lb_all)
    log_1mlb_all = jnp.log1p(-lb_all)

    def body(ci, carry):
        r0 = pl.multiple_of(ci * c, c)
        rows = pl.ds(r0, c)
        gs, kk = [], []
        for bb in range(nb):
            for hh in range(hp):
                ks = slice(hh * dk, (hh + 1) * dk)
                z = f_ref[bb, rows, ks]
                y = jnp.exp(-jnp.abs(z))
                r = 1.0 / (1.0 + y)
                log_sig = jnp.minimum(z, 0.0) - jnp.log(1.0 + y)
                gs.append(_logaddexp(log_lb_all[:, ks], log_1mlb_all[:, ks] + log_sig))
                kk.append((1.0 - lb_all[:, ks]) * jnp.where(z > 0.0, y * r, r))
        bs = _cumsum_chains(consts[2], gs)
        for bb in range(nb):
            for hh in range(hp):
                ks = slice(hh * dk, (hh + 1) * dk)
                vs = slice(hh * dv, (hh + 1) * dv)
                ch = bb * hp + hh
                o = _lin_attn_chunk(q_ref[bb, rows, ks], kk[ch], v_ref[bb, rows, vs], gs[ch], bs[ch],
                                    st_scr.at[ch], b_scr.at[ch], g_scr.at[ch], consts)
                out = _lin_attn_finish(o, gz_ref[bb, rows, vs], nw)
                o_ref[bb, rows, vs] = out.astype(o_ref.dtype)
        return carry

    lax.fori_loop(0, ts // c, body, 0)


def _lin_scratch(chains, dv, dk):
    return [pltpu.VMEM((chains, dk, dv), F32),
            pltpu.VMEM((chains, LIN_CHUNK, dk), F32),
            pltpu.VMEM((chains, LIN_CHUNK + 16, dk), F32)]


def _gla_branch(proj, small, w_up, b_up, norm_w):
    b, s, _ = proj.shape
    ts = min(512, s)
    dk, dv, h = GLA_DK, GLA_DV, GLA_HEADS
    hp = LIN_HEADS_PER_STEP
    kw, vw = hp * dk, hp * dv
    w_up_pad = jnp.pad(w_up, ((0, 128 - GLA_RANK), (0, 0)))
    return pl.pallas_call(
        _gla_kernel,
        out_shape=jax.ShapeDtypeStruct((b, s, h * dv), BF16),
        grid=(h // hp, s // ts),
        in_specs=[
            pl.BlockSpec((b, ts, kw), lambda hi, i: (0, i, hi)),
            pl.BlockSpec((b, ts, kw), lambda hi, i: (0, i, h // hp + hi)),
            pl.BlockSpec((b, ts, vw), lambda hi, i: (0, i, h // hp + hi)),
            pl.BlockSpec((b, ts, vw), lambda hi, i: (0, i, 2 * (h // hp) + hi)),
            pl.BlockSpec((b, ts, 128), lambda hi, i: (0, i, 0)),
            pl.BlockSpec((128, kw), lambda hi, i: (0, hi)),
            pl.BlockSpec((1, kw), lambda hi, i: (0, hi)),
            pl.BlockSpec((1, dv), lambda hi, i: (0, 0)),
        ],
        out_specs=pl.BlockSpec((b, ts, vw), lambda hi, i: (0, i, hi)),
        scratch_shapes=_lin_scratch(b * hp, dv, dk),
        compiler_params=_cparams(("parallel", "arbitrary")),
        name="gla",
    )(proj, proj, proj, proj, small, w_up_pad, b_up.reshape(1, -1), norm_w.reshape(1, dv))


def _hgrn_branch(proj, lb_logits, norm_w, layer):
    b, s, _ = proj.shape
    ts = min(1024, s)
    dk, dv, h = HGRN_DK, HGRN_DV, HGRN_HEADS
    hp = LIN_HEADS_PER_STEP
    kw, vw = hp * dk, hp * dv
    ng = h // hp
    depth = lb_logits.shape[0]
    return pl.pallas_call(
        functools.partial(_hgrn_kernel, layer=layer),
        out_shape=jax.ShapeDtypeStruct((b, s, h * dv), BF16),
        grid=(ng, s // ts),
        in_specs=[
            pl.BlockSpec((b, ts, kw), lambda hi, i: (0, i, hi)),
            pl.BlockSpec((b, ts, kw), lambda hi, i: (0, i, ng + hi)),
            pl.BlockSpec((b, ts, vw), lambda hi, i: (0, i, 2 * ng + hi)),
            pl.BlockSpec((b, ts, vw), lambda hi, i: (0, i, 3 * ng + hi)),
            pl.BlockSpec((depth, kw), lambda hi, i: (0, hi)),
            pl.BlockSpec((1, dv), lambda hi, i: (0, 0)),
        ],
        out_specs=pl.BlockSpec((b, ts, vw), lambda hi, i: (0, i, hi)),
        scratch_shapes=_lin_scratch(b * hp, dv, dk),
        compiler_params=_cparams(("parallel", "arbitrary")),
        name="hgrn2",
    )(proj, proj, proj, proj, lb_logits, norm_w.reshape(1, dv))


def _ssd_kernel(z_ref, x_ref, bc_ref, dt_ref, cw_ref, cb_ref, dtb_ref, alog_ref, dexp_ref, nw_ref,
                hexp_ref, o_ref, stage_scr, xbc_scr, st_scr, y_scr):
    c = x_ref.shape[1]
    nch = SSM_CONV_CH
    width = SSM_WIDTH
    n = SSM_STATE
    pairs = SSM_HEADS // 2

    @pl.when(pl.program_id(1) == 0)
    def _():
        st_scr[...] = jnp.zeros_like(st_scr)
        stage_scr[pl.ds(0, 8), :] = jnp.zeros((8, nch), F32)

    stage_scr[pl.ds(8, c), 0:width] = x_ref[0]
    stage_scr[pl.ds(8, c), width:nch] = bc_ref[0]
    staged = stage_scr[...]
    acc = cb_ref[...] + cw_ref[SSM_CONV - 1:SSM_CONV, :] * staged[8:8 + c, :]
    for kk in range(SSM_CONV - 1):
        shifted = pltpu.roll(staged, SSM_CONV - 1 - kk, axis=0)
        acc = acc + cw_ref[kk:kk + 1, :] * shifted[8:8 + c, :]
    stage_scr[pl.ds(0, 8), :] = stage_scr[pl.ds(c, 8), :]
    xbc_scr[...] = _silu(acc)

    dt = _softplus(dt_ref[0] + dtb_ref[...])
    a = -jnp.exp(alog_ref[...])
    da = dt * a
    ri = lax.broadcasted_iota(jnp.int32, (c, c), 0)
    ci = lax.broadcasted_iota(jnp.int32, (c, c), 1)
    causal = ri >= ci
    tri = jnp.where(causal, 1.0, 0.0).astype(BF16)
    a_cs = _sel_dot_left(tri, da)
    a_cs_t = a_cs.T
    hexp = hexp_ref[...]
    dt_x = _sel_dot_right(dt, hexp)
    acs_x = _sel_dot_right(a_cs, hexp)
    a_last_x = acs_x[c - 1:c, :]
    lane = lax.broadcasted_iota(jnp.int32, (c, 2 * SSM_HEADDIM), 1)
    first = lane < SSM_HEADDIM

    for grp in range(SSM_GROUPS):
        bm = xbc_scr[:, width + grp * n: width + (grp + 1) * n]
        cm = xbc_scr[:, width + (SSM_GROUPS + grp) * n: width + (SSM_GROUPS + grp + 1) * n]
        bm16 = bm.astype(BF16)
        cm16 = cm.astype(BF16)
        cb = _dot_nt(cm16, bm16)
        for pp in range(pairs // SSM_GROUPS):
            p = grp * (pairs // SSM_GROUPS) + pp
            lo = p * 2 * SSM_HEADDIM
            hi = lo + 2 * SSM_HEADDIM
            xs = xbc_scr[:, lo:hi]
            xdt = xs * dt_x[:, lo:hi]
            acs = acs_x[:, lo:hi]
            y = jnp.zeros((c, 2 * SSM_HEADDIM), F32)
            for hh in range(2):
                h = 2 * p + hh
                col = jnp.broadcast_to(a_cs[:, h:h + 1], (c, c))
                rw = jnp.broadcast_to(a_cs_t[h:h + 1, :], (c, c))
                lmat = jnp.exp(jnp.where(causal, col - rw, NEG))
                keep = first if hh == 0 else jnp.logical_not(first)
                xh = jnp.where(keep, xdt, 0.0).astype(BF16)
                y = y + _dot((cb * lmat).astype(BF16), xh)
            st = st_scr[p]
            y = y + _dot(cm16, st.astype(BF16)) * jnp.exp(acs)
            decay = jnp.exp(a_last_x[:, lo:hi] - acs)
            st_scr[p] = st * jnp.exp(a_last_x[:, lo:hi]) + _dot_tn(bm16, (xdt * decay).astype(BF16))
            y_scr[:, lo:hi] = y + dexp_ref[:, lo:hi] * xs

    yz = y_scr[...] * _silu(z_ref[0])
    var = jnp.mean(yz * yz, axis=-1, keepdims=True)
    o_ref[0] = (yz * lax.rsqrt(var + EPS) * nw_ref[...]).astype(o_ref.dtype)


def _ssd_branch(proj, small, conv_w, conv_b, dt_bias, a_log, d_skip, norm_w):
    b, s, _ = proj.shape
    c = math.gcd(SSM_CHUNK, s)
    nch = SSM_CONV_CH
    pad16 = lambda v: jnp.pad(v.reshape(1, -1), ((0, 0), (0, 128 - SSM_HEADS)))
    hexp = np.zeros((128, SSM_WIDTH), np.float32)
    for h in range(SSM_HEADS):
        hexp[h, h * SSM_HEADDIM:(h + 1) * SSM_HEADDIM] = 1.0
    d_exp = jnp.repeat(d_skip, SSM_HEADDIM).reshape(1, SSM_WIDTH)
    return pl.pallas_call(
        _ssd_kernel,
        out_shape=jax.ShapeDtypeStruct((b, s, SSM_WIDTH), BF16),
        grid=(b, s // c),
        in_specs=[
            pl.BlockSpec((1, c, SSM_WIDTH), lambda bi, i: (bi, i, 4)),
            pl.BlockSpec((1, c, SSM_WIDTH), lambda bi, i: (bi, i, 5)),
            pl.BlockSpec((1, c, 512), lambda bi, i: (bi, i, 12)),
            pl.BlockSpec((1, c, 128), lambda bi, i: (bi, i, 0)),
            pl.BlockSpec((SSM_CONV, nch), lambda bi, i: (0, 0)),
            pl.BlockSpec((1, nch), lambda bi, i: (0, 0)),
            pl.BlockSpec((1, 128), lambda bi, i: (0, 0)),
            pl.BlockSpec((1, 128), lambda bi, i: (0, 0)),
            pl.BlockSpec((1, SSM_WIDTH), lambda bi, i: (0, 0)),
            pl.BlockSpec((1, SSM_WIDTH), lambda bi, i: (0, 0)),
            pl.BlockSpec((128, SSM_WIDTH), lambda bi, i: (0, 0)),
        ],
        out_specs=pl.BlockSpec((1, c, SSM_WIDTH), lambda bi, i: (bi, i, 0)),
        scratch_shapes=[
            pltpu.VMEM((c + 8, nch), F32),
            pltpu.VMEM((c, nch), F32),
            pltpu.VMEM((SSM_HEADS // 2, SSM_STATE, 2 * SSM_HEADDIM), F32),
            pltpu.VMEM((c, SSM_WIDTH), F32),
        ],
        compiler_params=_cparams(("parallel", "arbitrary")),
        name="ssd",
    )(proj, proj, proj, small, conv_w, conv_b.reshape(1, nch), pad16(dt_bias), pad16(a_log),
      d_exp, norm_w.reshape(1, SSM_WIDTH), jnp.asarray(hexp, dtype=BF16))


def _rope(x, cos, sin_signed):
    return x * cos + pltpu.roll(x, NSA_DH // 2, axis=1) * sin_signed


def _nsa_prep_kernel(q_ref, kc_ref, ks_ref, vs_ref, kw_ref, vw_ref, cos_ref, sin_ref,
                     qo_ref, kco_ref, kso_ref, vso_ref, kwo_ref, vwo_ref):
    cos = cos_ref[...]
    sin = sin_ref[...]
    dh = NSA_DH
    scale = dh ** -0.5
    for h in range(NSA_HEADS):
        sl = slice(h * dh, (h + 1) * dh)
        qo_ref[0, :, sl] = (_rope(q_ref[0, :, sl], cos, sin) * scale).astype(qo_ref.dtype)
    for g in range(NSA_GROUPS):
        sl = slice(g * dh, (g + 1) * dh)
        kco_ref[0, :, sl] = _rope(kc_ref[0, :, sl], cos, sin)
        kso_ref[0, :, sl] = _rope(ks_ref[0, :, sl], cos, sin).astype(kso_ref.dtype)
        kwo_ref[0, :, sl] = _rope(kw_ref[0, :, sl], cos, sin).astype(kwo_ref.dtype)
    tk = NSA_TILE
    for g in range(NSA_GROUPS):
        for r in range(q_ref.shape[1] // tk):
            rows = slice(r * tk, (r + 1) * tk)
            sl = slice(g * dh, (g + 1) * dh)
            vso_ref[0, g, r, 0:dh, :] = vs_ref[0, rows, sl].T.astype(vso_ref.dtype)
            vwo_ref[0, g, r, 0:dh, :] = vw_ref[0, rows, sl].T.astype(vwo_ref.dtype)
            ones = jnp.ones((NSA_ONES, tk), vso_ref.dtype)
            vso_ref[0, g, r, dh:dh + NSA_ONES, :] = ones
            vwo_ref[0, g, r, dh:dh + NSA_ONES, :] = ones


def _nsa_prep(proj):
    b, s, _ = proj.shape
    ts = min(512, s)
    tk = NSA_TILE
    half = NSA_DH // 2
    inv_freq = ROPE_THETA ** (-np.arange(half, dtype=np.float64) / half)
    ang = np.arange(s, dtype=np.float64)[:, None] * inv_freq[None, :]
    cos_full = jnp.asarray(np.concatenate([np.cos(ang), np.cos(ang)], axis=-1), dtype=F32)
    sin_signed = jnp.asarray(np.concatenate([-np.sin(ang), np.sin(ang)], axis=-1), dtype=F32)
    kvw = NSA_GROUPS * NSA_DH
    col = lambda idx: (lambda bi, i: (bi, i, idx))
    kv_spec = lambda idx: pl.BlockSpec((1, ts, kvw), col(idx))
    kv_out = pl.BlockSpec((1, ts, kvw), lambda bi, i: (bi, i, 0))
    vrows = NSA_DH + NSA_ONES
    vt_shape = jax.ShapeDtypeStruct((b, NSA_GROUPS, s // tk, vrows, tk), BF16)
    vt_out = pl.BlockSpec((1, NSA_GROUPS, ts // tk, vrows, tk), lambda bi, i: (bi, 0, i, 0, 0))
    return pl.pallas_call(
        _nsa_prep_kernel,
        out_shape=(jax.ShapeDtypeStruct((b, s, NSA_HEADS * NSA_DH), BF16),
                   jax.ShapeDtypeStruct((b, s, kvw), F32),
                   jax.ShapeDtypeStruct((b, s, kvw), BF16),
                   vt_shape,
                   jax.ShapeDtypeStruct((b, s, kvw), BF16),
                   vt_shape),
        grid=(b, s // ts),
        in_specs=[
            pl.BlockSpec((1, ts, NSA_HEADS * NSA_DH), col(3)),
            kv_spec(16), kv_spec(18), kv_spec(19), kv_spec(20), kv_spec(21),
            pl.BlockSpec((ts, NSA_DH), lambda bi, i: (i, 0)),
            pl.BlockSpec((ts, NSA_DH), lambda bi, i: (i, 0)),
        ],
        out_specs=(pl.BlockSpec((1, ts, NSA_HEADS * NSA_DH), lambda bi, i: (bi, i, 0)),
                   kv_out, kv_out, vt_out, kv_out, vt_out),
        compiler_params=_cparams(("parallel", "parallel")),
        name="nsa_prep",
    )(proj, proj, proj, proj, proj, proj, cos_full, sin_signed)


def _compress_kernel(t_ref, pe_ref, w1_ref, w2_ref, o_ref, sh_scr, *, transposed):
    n = t_ref.shape[1] // CMP_STRIDE
    dh = NSA_DH
    half = CMP_LEN // 2
    acc_a = jnp.zeros((n, CMP_HIDDEN), F32)
    acc_b = jnp.zeros((n, CMP_HIDDEN), F32)
    for l in range(half):
        t = t_ref[0, pl.ds(l, n, stride=CMP_STRIDE), :]
        acc_a = acc_a + _dot((t + pe_ref[l:l + 1, :]).astype(BF16), w1_ref[l * dh:(l + 1) * dh, :])
        acc_b = acc_b + _dot((t + pe_ref[half + l:half + l + 1, :]).astype(BF16),
                             w1_ref[(half + l) * dh:(half + l + 1) * dh, :])
    sh_scr[pl.ds(0, n), :] = acc_b
    sh_scr[pl.ds(n, 8), :] = jnp.zeros((8, CMP_HIDDEN), F32)
    hid = acc_a + sh_scr[pl.ds(1, n), :]
    out = _dot(_silu(hid).astype(BF16), w2_ref[...])
    if transposed:
        tk = NSA_TILE
        for r in range(n // tk):
            o_ref[0, 0, r] = out[r * tk:(r + 1) * tk, :].T.astype(o_ref.dtype)
    else:
        o_ref[0, 0] = out.astype(o_ref.dtype)


def _nsa_compress(src, col0, pe, w1, w2, transposed):
    b, s, _ = src.shape
    n = s // CMP_STRIDE
    base = col0 // NSA_DH
    tk = NSA_TILE
    if transposed:
        out_shape = jax.ShapeDtypeStruct((b, NSA_GROUPS, n // tk, NSA_DH, tk), BF16)
        out_spec = pl.BlockSpec((1, 1, n // tk, NSA_DH, tk), lambda bi, g: (bi, g, 0, 0, 0))
    else:
        out_shape = jax.ShapeDtypeStruct((b, NSA_GROUPS, n, NSA_DH), BF16)
        out_spec = pl.BlockSpec((1, 1, n, NSA_DH), lambda bi, g: (bi, g, 0, 0))
    return pl.pallas_call(
        functools.partial(_compress_kernel, transposed=transposed),
        out_shape=out_shape,
        grid=(b, NSA_GROUPS),
        in_specs=[
            pl.BlockSpec((1, s, NSA_DH), lambda bi, g: (bi, 0, base + g)),
            pl.BlockSpec((CMP_LEN, NSA_DH), lambda bi, g: (0, 0)),
            pl.BlockSpec((CMP_LEN * NSA_DH, CMP_HIDDEN), lambda bi, g: (0, 0)),
            pl.BlockSpec((CMP_HIDDEN, NSA_DH), lambda bi, g: (0, 0)),
        ],
        out_specs=out_spec,
        scratch_shapes=[pltpu.VMEM((n + 8, CMP_HIDDEN), F32)],
        compiler_params=_cparams(("parallel", "parallel")),
        name="nsa_compress",
    )(src, pe, w1.astype(BF16), w2.astype(BF16))


def _weights_bf16(s_tiles, m):
    return [jnp.exp((s - m).astype(BF16)) for s in s_tiles]


def _flash_update(s_tiles, vt_tiles, m_ref, acc_ref):
    m_old = m_ref[...]
    m_new = m_old
    for s in s_tiles:
        m_new = jnp.maximum(m_new, jnp.max(s, axis=0, keepdims=True))
    alpha = jnp.exp(m_old - m_new)
    acc_ref[...] = alpha * acc_ref[...] + _pv(vt_tiles, _weights_bf16(s_tiles, m_new))
    m_ref[...] = m_new


def _pv(vt_tiles, p_tiles):
    vt = jnp.concatenate(vt_tiles, axis=1)
    p = jnp.concatenate([p.astype(BF16) for p in p_tiles], axis=0)
    return _dot(vt, p)


def _softmax_tiles(s_tiles, vt_tiles):
    m = jnp.max(s_tiles[0], axis=0, keepdims=True)
    for s in s_tiles[1:]:
        m = jnp.maximum(m, jnp.max(s, axis=0, keepdims=True))
    p_tiles = [jnp.exp(s - m) for s in s_tiles]
    l = jnp.sum(p_tiles[0], axis=0, keepdims=True)
    for p in p_tiles[1:]:
        l = l + jnp.sum(p, axis=0, keepdims=True)
    return p_tiles, l, _pv(vt_tiles, p_tiles)


def _softmax_tiles_aug(s_tiles, vt_tiles):
    m = jnp.max(s_tiles[0], axis=0, keepdims=True)
    for s in s_tiles[1:]:
        m = jnp.maximum(m, jnp.max(s, axis=0, keepdims=True))
    acc = _pv(vt_tiles, _weights_bf16(s_tiles, m))
    return acc[0:NSA_DH] * (1.0 / acc[NSA_DH:NSA_DH + 1])


def _nsa_kernel(q_ref, kc_ref, vct_ref, ovt_ref, ks_ref, vst_ref, kw_ref, vwt_ref, gate_ref, nz_ref,
                o_ref, m_scr, accs_scr, acc_scr, cap_scr, capd_scr, capl_scr, sa_scr, sb_scr):
    tq = q_ref.shape[1]
    tk = NSA_TILE
    dh = NSA_DH
    hpg = NSA_HPG
    lanes = hpg * tq
    ns = ovt_ref.shape[0]
    qtiles = tq // tk
    qi = pl.program_id(2)
    t0 = qi * tq
    kt0 = qi * qtiles

    qt = jnp.concatenate([q_ref[0, :, j * dh:(j + 1) * dh].astype(F32).T for j in range(hpg)],
                         axis=1).astype(BF16)
    rowi = lax.broadcasted_iota(jnp.int32, (tk, lanes), 0)
    qpos = lax.broadcasted_iota(jnp.int32, (tk, lanes), 1) & (tq - 1)
    for d in range(qtiles):
        capd_scr[d] = jnp.where(rowi + d * tk <= qpos, FORCE, NEG)
        capl_scr[d] = jnp.where(rowi + d * tk > qpos, FORCE, NEG)

    n_ct = kc_ref.shape[2] // tk
    s_all = _dot(kc_ref[0, 0], qt)
    s_tiles = []
    for c in range(n_ct):
        first_end = c * (tk * CMP_STRIDE) + (CMP_LEN - 1) - t0
        cap = jnp.where(rowi * CMP_STRIDE + first_end <= qpos, FORCE, NEG)
        s_tiles.append(jnp.minimum(s_all[c * tk:(c + 1) * tk, :], cap))
    p_tiles, l_c, acc_c = _softmax_tiles(s_tiles, [vct_ref[0, 0, c] for c in range(n_ct)])
    p_hi = [p.astype(BF16) for p in p_tiles]
    p_lo = [(p - h.astype(F32)).astype(BF16) for p, h in zip(p_tiles, p_hi)]
    imp_2 = _dot(ovt_ref[...], jnp.concatenate([jnp.concatenate(p_hi, axis=0),
                                                 jnp.concatenate(p_lo, axis=0)], axis=1))
    imp_un = imp_2[:, 0:lanes] + imp_2[:, lanes:2 * lanes]
    tl = t0 + (lax.broadcasted_iota(jnp.int32, (1, lanes), 1) & (tq - 1))
    inv_l = jnp.where(tl >= CMP_LEN - 1, 1.0 / l_c, 0.0)
    acc_scr[0] = acc_c * inv_l
    impn = imp_un * inv_l
    imp = impn[:, 0:tq]
    for j in range(1, hpg):
        imp = imp + impn[:, j * tq:(j + 1) * tq]

    blk = lax.broadcasted_iota(jnp.int32, (ns, tq), 0)
    blk_t = jnp.right_shift(t0 + lax.broadcasted_iota(jnp.int32, (ns, tq), 1), SLC_SHIFT)
    score = jnp.where(blk == 0, FORCE,
                      jnp.where(blk == blk_t, FORCE,
                                jnp.where(blk == blk_t - 1, FORCE,
                                          jnp.where(blk <= blk_t, imp, NEG))))
    blk_f = blk.astype(F32)
    cap_sel = jnp.full((ns, tq), NEG, F32)
    for _ in range(min(SLC_TOPK, ns)):
        mx = jnp.max(score, axis=0, keepdims=True)
        first = jnp.min(jnp.where(score == mx, blk_f, float(ns)), axis=0, keepdims=True)
        pick = blk_f == first
        cap_sel = jnp.where(pick, FORCE, cap_sel)
        score = jnp.where(pick, -jnp.inf, score)
    for j in range(hpg):
        cap_scr[:, j * tq:(j + 1) * tq] = cap_sel

    bpt = tk // SLC_BLOCK

    def slc_scores(step, ntiles=None):
        ntiles = nb if ntiles is None else ntiles
        k0 = pl.multiple_of(step * (nb * tk), nb * tk)
        s = _dot(ks_ref[0, pl.ds(k0, ntiles * tk), :], qt)
        out = []
        for r in range(ntiles):
            caps = [jnp.broadcast_to(cap_scr[pl.ds((step * nb + r) * bpt + i, 1), :], (SLC_BLOCK, lanes))
                    for i in range(bpt)]
            out.append(jnp.minimum(s[r * tk:(r + 1) * tk, :], jnp.concatenate(caps, axis=0)))
        return out

    wt = WINDOW // tk
    s_tiles, vt_tiles = [], []
    for d in range(-wt, qtiles):
        kt = kt0 + d
        ktc = jnp.maximum(kt, 0)
        s = _dot(kw_ref[0, pl.ds(pl.multiple_of(ktc * tk, tk), tk), :], qt)
        if d + wt < qtiles:
            s = jnp.minimum(s, capl_scr[d + wt])
        if d >= 0:
            s = jnp.minimum(s, capd_scr[d])
        else:
            s = jnp.minimum(s, jnp.where(kt >= 0, FORCE, NEG))
        s_tiles.append(s)
        vt_tiles.append(vwt_ref[0, 0, ktc])
    acc_scr[1] = _softmax_tiles_aug(s_tiles, vt_tiles)

    nb = NSA_TILES_PER_STEP
    m_scr[...] = jnp.full((1, lanes), NEG, F32)
    accs_scr[...] = jnp.zeros_like(accs_scr)
    state = (m_scr, accs_scr)
    n_full = kt0 // nb

    def score_step(step, dst):
        for r, s in enumerate(slc_scores(step)):
            dst[r * tk:(r + 1) * tk, :] = s

    def value_step(step, src):
        _flash_update([src[r * tk:(r + 1) * tk, :] for r in range(nb)],
                      [vst_ref[0, 0, step * nb + r] for r in range(nb)], *state)

    @pl.when(n_full > 0)
    def _():
        score_step(0, sa_scr)

    def slc_pair(j, carry):
        score_step(2 * j + 1, sb_scr)
        value_step(2 * j, sa_scr)
        score_step(jnp.minimum(2 * j + 2, n_full - 1), sa_scr)
        value_step(2 * j + 1, sb_scr)
        return carry

    lax.fori_loop(0, n_full // 2, slc_pair, 0)

    @pl.when(n_full % 2 == 1)
    def _():
        value_step(n_full - 1, sa_scr)

    for below in range(0, nb, qtiles):
        @pl.when(kt0 - n_full * nb == below)
        def _(below=below):
            s_tiles = slc_scores(n_full, below + qtiles)
            for d in range(qtiles):
                s_tiles[below + d] = jnp.minimum(s_tiles[below + d], capd_scr[d])
            _flash_update(s_tiles, [vst_ref[0, 0, n_full * nb + r] for r in range(below + qtiles)], *state)

    o_c = acc_scr[0]
    o_s = accs_scr[0:dh, :] * (1.0 / accs_scr[dh:dh + 1, :])
    o_w = acc_scr[1]
    sgt = _sigmoid(gate_ref[0]).T
    for j in range(hpg):
        ls = slice(j * tq, (j + 1) * tq)
        mix_t = (sgt[3 * j:3 * j + 1, :] * o_c[:, ls] + sgt[3 * j + 1:3 * j + 2, :] * o_s[:, ls]
                 + sgt[3 * j + 2:3 * j + 3, :] * o_w[:, ls])
        sl = slice(j * dh, (j + 1) * dh)
        o_ref[0, :, sl] = (mix_t.T * _silu(nz_ref[0, :, sl])).astype(o_ref.dtype)


def _nsa_attention(qr, kc, vct, ksr, vst, kwr, vwt, small, proj):
    b, s, _ = qr.shape
    tk = NSA_TILE
    tq = min(NSA_QUERY_TILE, s)
    qtiles = tq // tk
    assert s % tq == 0 and tq % tk == 0 and WINDOW % tk == 0 and tk % SLC_BLOCK == 0
    assert NSA_TILES_PER_STEP % qtiles == 0 and WINDOW // tk >= qtiles
    assert (s // tk) % NSA_TILES_PER_STEP == 0
    ncp = kc.shape[2]
    assert ncp % tk == 0
    ns = s // SLC_BLOCK
    gw = NSA_HPG * NSA_DH
    lanes = NSA_HPG * tq
    cs = np.arange(ncp)[None, :] * CMP_STRIDE
    ss = np.arange(ns)[:, None] * SLC_BLOCK
    ovt = ((np.minimum(cs + CMP_LEN, ss + SLC_BLOCK) - np.maximum(cs, ss)) > 0).astype(np.float32)
    ovt[:, ncp - 1] = 0.0
    kv_spec = pl.BlockSpec((1, s, NSA_DH), lambda bi, g, i: (bi, 0, g))
    vrows = NSA_DH + NSA_ONES
    vt_spec = pl.BlockSpec((1, 1, s // tk, vrows, tk), lambda bi, g, i: (bi, g, 0, 0, 0))
    return pl.pallas_call(
        _nsa_kernel,
        out_shape=jax.ShapeDtypeStruct((b, s, NSA_HEADS * NSA_DH), BF16),
        grid=(b, NSA_GROUPS, s // tq),
        in_specs=[
            pl.BlockSpec((1, tq, gw), lambda bi, g, i: (bi, i, g)),
            pl.BlockSpec((1, 1, ncp, NSA_DH), lambda bi, g, i: (bi, g, 0, 0)),
            pl.BlockSpec((1, 1, ncp // tk, NSA_DH, tk), lambda bi, g, i: (bi, g, 0, 0, 0)),
            pl.BlockSpec((ns, ncp), lambda bi, g, i: (0, 0)),
            kv_spec, vt_spec, kv_spec, vt_spec,
            pl.BlockSpec((1, tq, 128), lambda bi, g, i: (bi, i, 1 + g)),
            pl.BlockSpec((1, tq, gw), lambda bi, g, i: (bi, i, 11 + g)),
        ],
        out_specs=pl.BlockSpec((1, tq, gw), lambda bi, g, i: (bi, i, g)),
        scratch_shapes=[
            pltpu.VMEM((1, lanes), F32),
            pltpu.VMEM((vrows, lanes), F32),
            pltpu.VMEM((2, NSA_DH, lanes), F32),
            pltpu.VMEM((ns, lanes), F32),
            pltpu.VMEM((qtiles, tk, lanes), F32),
            pltpu.VMEM((qtiles, tk, lanes), F32),
            pltpu.VMEM((NSA_TILES_PER_STEP * tk, lanes), F32),
            pltpu.VMEM((NSA_TILES_PER_STEP * tk, lanes), F32),
        ],
        compiler_params=_cparams(("parallel", "parallel", "arbitrary")),
        name="nsa_attention",
    )(qr, kc, vct, jnp.asarray(ovt, dtype=BF16), ksr, vst, kwr, vwt, small, proj)


def _nsa_branch(proj, small, pe_k, w1_k, w2_k, pe_v, w1_v, w2_v):
    qr, kcr, ksr, vst, kwr, vwt = _nsa_prep(proj)
    kc = _nsa_compress(kcr, 0, pe_k, w1_k, w2_k, transposed=False)
    vct = _nsa_compress(proj, 4352, pe_v, w1_v, w2_v, transposed=True)
    return _nsa_attention(qr, kc, vct, ksr, vst, kwr, vwt, small, proj)


def _pad_rows(w, rows):
    return jnp.pad(w, ((0, rows - w.shape[0]), (0, 0)))


def _cast_rows_kernel(w_ref, o_ref):
    o_ref[...] = w_ref[...].astype(o_ref.dtype)


def _even_weights(w_in):
    wt = w_in.T
    d = wt.shape[1]
    per = NSA_HPG * 3
    tile = 512
    t1, t2 = 2048 // tile, 5632 // tile

    def src_row(j):
        shift8 = jnp.where(j >= t2, (GLA_RANK + NSA_GROUPS * per) // 8, jnp.where(j >= t1, GLA_RANK // 8, 0))
        return 8 * (j * (tile // 8) + shift8)

    main = pl.pallas_call(
        _cast_rows_kernel,
        out_shape=jax.ShapeDtypeStruct((MAIN_COLS, d), BF16),
        grid=(MAIN_COLS // tile,),
        in_specs=[pl.BlockSpec((pl.Element(tile), pl.Element(d)), lambda j: (src_row(j), 0))],
        out_specs=pl.BlockSpec((tile, d), lambda j: (j, 0)),
        compiler_params=_cparams(("parallel",)),
        name="weight_rows",
    )(wt)

    def small_kernel(decay_ref, gates_ref, o_ref):
        row = lax.broadcasted_iota(jnp.int32, (128, d), 0)
        zeros = jnp.zeros((128, d), F32)
        o_ref[0:128, :] = jnp.where(row < GLA_RANK, decay_ref[...], 0.0).astype(o_ref.dtype)
        gates = gates_ref[...]
        for g in range(NSA_GROUPS):
            rows_g = jnp.concatenate([gates[g * per:(g + 1) * per, :], zeros[0:128 - per, :]], axis=0)
            o_ref[(1 + g) * 128:(2 + g) * 128, :] = rows_g.astype(o_ref.dtype)

    window = lambda r0: pl.BlockSpec((pl.Element(128), pl.Element(d)), lambda i: (r0, 0))
    small = pl.pallas_call(
        small_kernel,
        out_shape=jax.ShapeDtypeStruct(((1 + NSA_GROUPS) * 128, d), BF16),
        grid=(1,),
        in_specs=[window(2048), window(5648)],
        out_specs=pl.BlockSpec(((1 + NSA_GROUPS) * 128, d), lambda i: (0, 0)),
        compiler_params=_cparams(("arbitrary",)),
        name="weight_rows_small",
    )(wt, wt)
    return main, small


def _odd_weights(w_in):
    wt = w_in.T.astype(BF16)
    return wt, _pad_rows(wt[MAIN_COLS:MAIN_COLS + SSM_HEADS], 128)


def kernel(x, c, ada_w, ada_b, pre_norm_w, post_norm_w, even_w_in, even_w_out, gla_w_up, gla_b_up,
           gla_norm_w, nsa_pe_k, nsa_w1_k, nsa_w2_k, nsa_pe_v, nsa_w1_v, nsa_w2_v, odd_w_in,
           odd_w_out, hgrn_lb_logits, hgrn_norm_w, ssm_conv_w, ssm_conv_b, ssm_dt_bias, ssm_a_log,
           ssm_d, ssm_norm_w):
    depth = ada_w.shape[0]
    mod = _adaln_mod(c, ada_w, ada_b)
    for l in range(depth):
        if l % 2 == 0:
            e = l // 2
            w_main, w_small = _even_weights(even_w_in[e])
            proj, small = _in_proj(x, mod[l], pre_norm_w[l], w_main, w_small)
            o_a = _gla_branch(proj, small, gla_w_up[e], gla_b_up[e], gla_norm_w[e])
            o_b = _nsa_branch(proj, small, nsa_pe_k[e], nsa_w1_k[e], nsa_w2_k[e],
                              nsa_pe_v[e], nsa_w1_v[e], nsa_w2_v[e])
            w_out = even_w_out[e]
        else:
            o = l // 2
            w_main, w_small = _odd_weights(odd_w_in[o])
            proj, small = _in_proj(x, mod[l], pre_norm_w[l], w_main, w_small)
            o_a = _hgrn_branch(proj, hgrn_lb_logits, hgrn_norm_w[o], l)
            o_b = _ssd_branch(proj, small, ssm_conv_w[o], ssm_conv_b[o], ssm_dt_bias[o],
                              ssm_a_log[o], ssm_d[o], ssm_norm_w[o])
            w_out = odd_w_out[o]
        x = _out_proj(o_a, o_b, w_out.astype(BF16), x, mod[l], post_norm_w[l])
    return x
```
